```python
import math
import jax, jax.numpy as jnp
from jax import lax
import numpy as np

D_MODEL = 1024
BATCH = 32
SEQ = 2048
DEPTH = 2

GRID_W = 64
CTX_LEN = 256
N_EVEN = (DEPTH + 1) // 2
N_ODD = DEPTH // 2
RMS_EPS = 1e-6
N_MOD = 6

A_HEADS = 4
A_DQK = 64
A_DV = 128
Q_BLOCK = 128
ROPE_BASE = 10000.0
ROPE_AXIS_PAIRS = A_DQK // 4

B_HEADS = 4
B_DH = 128
B_WIDTH = B_HEADS * B_DH
CHUNK = 64
SHORT_CONV = 3

A_QW = A_HEADS * 2 * A_DQK
A_VW = A_HEADS * A_DV
GATE_W = 4 * B_HEADS
Q_SIDE = A_QW + 2 * B_WIDTH
KV_SIDE = A_QW + A_VW + 2 * B_WIDTH + GATE_W
P_AB = Q_SIDE + KV_SIDE
MIX_W = A_VW + B_WIDTH

HY_ORDER = 2
POS_EMB_DIM = 33
FILTER_HIDDEN = 64
FILTER_SIN_W = 1.0
DECAY_FAST_PCT = 0.3
DECAY_SLOW_PCT = 1.5
DECAY_TARGET = 1e-2

N_EXPERTS = 64
TOP_K = 8
N_GROUPS = 8
TOPK_GROUPS = 4
D_EXPERT = 256
D_SHARED = 256
ROUTED_SCALE = 2.5

kernel_name = 'hybrid_diffattn_mlstm_hyena_moe_dit'

f32 = jnp.float32


def _rms(x, g):
    xf = x.astype(f32)
    y = xf * lax.rsqrt(jnp.mean(xf * xf, axis=-1, keepdims=True) + RMS_EPS)
    return (y * g.astype(f32)).astype(x.dtype)


def _rms_heads(o, n_heads, g):
    B, L, W = o.shape
    of = o.astype(f32).reshape(B, L, n_heads, W // n_heads)
    of = of * lax.rsqrt(jnp.mean(of * of, axis=-1, keepdims=True) + RMS_EPS)
    return (of.reshape(B, L, W) * g.astype(f32)).astype(o.dtype)


def _modulate(h, shift, scale):
    return h * (1 + scale[:, None, :]) + shift[:, None, :]


def _split(t, widths):
    idx, acc = [], 0
    for w in widths[:-1]:
        acc += w
        idx.append(acc)
    return jnp.split(t, idx, axis=-1)


def _dwconv_centred(x, w, b):
    K = w.shape[0]
    L = x.shape[1]
    xp = jnp.pad(x, ((0, 0), (K // 2, K // 2), (0, 0)))
    return sum(xp[:, j:j + L] * w[j] for j in range(K)) + b


def _axial_rope(L):
    rows = L // GRID_W
    row = jnp.repeat(jnp.arange(rows), GRID_W)
    col = jnp.tile(jnp.arange(GRID_W), rows)
    inv = ROPE_BASE ** (-jnp.arange(ROPE_AXIS_PAIRS, dtype=f32) / ROPE_AXIS_PAIRS)
    ang = jnp.stack([row, col], axis=-1).astype(f32)[..., None] * inv
    ang = jnp.broadcast_to(ang[:, :, None, :], (L, 2, 2, ROPE_AXIS_PAIRS)).reshape(L, A_DQK)
    return jnp.cos(ang), jnp.sin(ang)


def _rope_2d(x, cos, sin):
    xa = x.reshape(*x.shape[:-1], 2, 2, ROPE_AXIS_PAIRS)
    rot = jnp.concatenate([-xa[..., 1:, :], xa[..., :1, :]], axis=-2).reshape(x.shape)
    c = cos[None, :, None, None, :].astype(x.dtype)
    s = sin[None, :, None, None, :].astype(x.dtype)
    return x * c + rot * s


def _diff_attention(q, k, v, lam):
    B, S, H = q.shape[:3]
    nb = S // Q_BLOCK
    qb = jnp.moveaxis(q.reshape(B, nb, Q_BLOCK, H, 2, A_DQK), 1, 0)

    def one_block(qi):
        s = jnp.einsum('bqhcd,bkhcd->bhcqk', qi, k).astype(f32) * (A_DQK ** -0.5)
        p = jax.nn.softmax(s, axis=-1)
        a = (p[:, :, 0] - lam * p[:, :, 1]).astype(v.dtype)
        return jnp.einsum('bhqk,bkhd->bqhd', a, v)

    o = lax.map(one_block, qb)
    return jnp.moveaxis(o, 0, 1).reshape(B, S, H * A_DV)


def _to_heads(t, n_heads):
    B, L, W = t.shape
    return t.astype(f32).reshape(B, L, n_heads, W // n_heads).transpose(0, 2, 1, 3)


def _gate_logs(g):
    B, L, _ = g.shape
    g = g.astype(f32).reshape(B, L, 2, 2, B_HEADS).transpose(0, 2, 3, 4, 1)
    return g[:, :, 0], jax.nn.log_sigmoid(g[:, :, 1])


def _mlstm_absorb(state, k, v, li, lf):
    C, n, m = state
    b = jnp.cumsum(lf, axis=-1)
    bL = b[..., -1]
    g = bL[..., None] - b + li
    m_new = jnp.maximum(bL + m, jnp.max(g, axis=-1))
    w = jnp.exp(g - m_new[..., None])
    decay = jnp.exp(bL + m - m_new)
    C = decay[..., None, None] * C + jnp.einsum('bhsd,bhse->bhde', k * w[..., None], v)
    n = decay[..., None] * n + jnp.einsum('bhs,bhsd->bhd', w, k)
    return (C, n, m_new)


def _mlstm_chunkwise(q, k, v, li, lf, state):
    B, H, L, dk = q.shape
    nc = L // CHUNK

    def chunks(a):
        return jnp.moveaxis(a.reshape(B, H, nc, CHUNK, *a.shape[3:]), 2, 0)

    causal = jnp.tril(jnp.ones((CHUNK, CHUNK), dtype=bool))

    def step(carry, inp):
        C, n, m = carry
        qc, kc, vc, ic, fc = inp
        b = jnp.cumsum(fc, axis=-1)
        d = jnp.where(causal, b[..., :, None] - b[..., None, :] + ic[..., None, :], -jnp.inf)
        inter = b + m[..., None]
        m_t = jnp.maximum(inter, jnp.max(d, axis=-1))
        s = jnp.einsum('bhtd,bhsd->bhts', qc, kc) * jnp.exp(d - m_t[..., None])
        sc = jnp.exp(inter - m_t)
        num = sc[..., None] * jnp.einsum('bhtd,bhde->bhte', qc, C) + jnp.einsum('bhts,bhse->bhte', s, vc)
        den = sc * jnp.einsum('bhtd,bhd->bht', qc, n) + jnp.sum(s, axis=-1)
        h = num / jnp.maximum(jnp.abs(den), jnp.exp(-m_t))[..., None]
        return _mlstm_absorb((C, n, m), kc, vc, ic, fc), h

    _, h = lax.scan(step, state, (chunks(q), chunks(k), chunks(v), chunks(li), chunks(lf)))
    return jnp.moveaxis(h, 0, 2).reshape(B, H, L, v.shape[-1])


def _flip(a, rev):
    return jnp.flip(a, axis=2) if rev else a


def _mlstm_bidirectional(q, k, v, gts, ck, cv, cgts):
    B, S, _ = q.shape
    q = _to_heads(q, B_HEADS) * (B_DH ** -0.5)
    k, v = _to_heads(k, B_HEADS), _to_heads(v, B_HEADS)
    ck, cv = _to_heads(ck, B_HEADS), _to_heads(cv, B_HEADS)
    li, lf = _gate_logs(gts)
    cli, clf = _gate_logs(cgts)
    zero = (jnp.zeros((B, B_HEADS, B_DH, B_DH), f32), jnp.zeros((B, B_HEADS, B_DH), f32),
            jnp.zeros((B, B_HEADS), f32))
    out = 0.0
    for d in range(2):
        rev = d == 1
        st = _mlstm_absorb(zero, _flip(ck, rev), _flip(cv, rev), _flip(cli[:, d], rev), _flip(clf[:, d], rev))
        hd = _mlstm_chunkwise(_flip(q, rev), _flip(k, rev), _flip(v, rev),
                              _flip(li[:, d], rev), _flip(lf[:, d], rev), st)
        out = out + _flip(hd, rev)
    return out.transpose(0, 2, 1, 3).reshape(B, S, B_WIDTH)


def _ab_mixer(h, hc, w_in, conv_w, conv_b, gate_b, lam_vecs, g_a, g_b, w_out, lam_init):
    B, S, _ = h.shape
    Lc = hc.shape[1]
    a_q, b_q, b_o, a_k, a_v, b_k, b_v, b_g = _split(
        h @ w_in, (A_QW, B_WIDTH, B_WIDTH, A_QW, A_VW, B_WIDTH, B_WIDTH, GATE_W))
    ca_k, ca_v, cb_k, cb_v, cb_g = _split(hc @ w_in[:, Q_SIDE:], (A_QW, A_VW, B_WIDTH, B_WIDTH, GATE_W))

    cos, sin = _axial_rope(S)
    q = _rope_2d(a_q.reshape(B, S, A_HEADS, 2, A_DQK), cos, sin)
    k_lat = _rope_2d(a_k.reshape(B, S, A_HEADS, 2, A_DQK), cos, sin)
    keys = jnp.concatenate([ca_k.reshape(B, Lc, A_HEADS, 2, A_DQK), k_lat], axis=1)
    vals = jnp.concatenate([ca_v.reshape(B, Lc, A_HEADS, A_DV), a_v.reshape(B, S, A_HEADS, A_DV)], axis=1)
    lv = lam_vecs.astype(f32)
    lam = jnp.exp(jnp.sum(lv[0] * lv[1])) - jnp.exp(jnp.sum(lv[2] * lv[3])) + lam_init
    out_a = _rms_heads(_diff_attention(q, keys, vals, lam), A_HEADS, g_a) * (1.0 - lam_init)

    qk = jax.nn.silu(_dwconv_centred(jnp.concatenate([b_q, b_k], axis=-1), conv_w, conv_b))
    mq, mk = jnp.split(qk, 2, axis=-1)
    cmk = jax.nn.silu(_dwconv_centred(cb_k, conv_w[:, B_WIDTH:], conv_b[B_WIDTH:]))
    hb = _mlstm_bidirectional(mq, mk, b_v, b_g + gate_b, cmk, cb_v, cb_g + gate_b)
    out_b = _rms_heads(hb.astype(h.dtype), B_HEADS, g_b) * jax.nn.sigmoid(b_o)

    return jnp.concatenate([out_a, out_b], axis=-1) @ w_out


def _hyena_filters(L, fw1, fb1, fw2, fb2, fw3):
    j = jnp.arange(L, dtype=f32)
    bands = (POS_EMB_DIM - 1) // 2
    freqs = jnp.linspace(1e-4, bands - 1, bands, dtype=f32)
    ang = (2.0 * math.pi / L) * j[:, None] * freqs[None, :]
    z = jnp.concatenate([(j / (L - 1))[:, None], jnp.cos(ang), -jnp.sin(ang)], axis=-1)
    hid = jnp.sin(FILTER_SIN_W * (z @ fw1.astype(f32) + fb1.astype(f32)))
    hid = jnp.sin(FILTER_SIN_W * (hid @ fw2.astype(f32) + fb2.astype(f32)))
    filt = (hid @ fw3.astype(f32)).reshape(L, HY_ORDER, D_MODEL)
    dist = jnp.abs(j - L // 2) / (L // 2)
    max_decay = math.log(DECAY_TARGET) / DECAY_FAST_PCT
    min_decay = math.log(DECAY_TARGET) / DECAY_SLOW_PCT
    deltas = jnp.abs(jnp.linspace(min_decay, max_decay, D_MODEL, dtype=f32))
    window = jnp.exp(-dist[:, None] * deltas[None, :])
    return filt * window[:, None, :]


def _fft_conv_centred(u, filt):
    L = u.shape[1]
    n = 2 * L
    U = jnp.fft.rfft(u.astype(f32), n=n, axis=1)
    K = jnp.fft.rfft(filt.astype(f32), n=n, axis=0)
    y = jnp.fft.irfft(U * K[None], n=n, axis=1)[:, L // 2:L // 2 + L]
    return y.astype(u.dtype)


def _hyena_mixer(h, w_in, conv_w, conv_b, fw1, fb1, fw2, fb2, fw3, fbias, w_out):
    L = h.shape[1]
    u = _dwconv_centred(h @ w_in, conv_w, conv_b)
    v, x1, x2 = jnp.split(u, 3, axis=-1)
    filt = _hyena_filters(L, fw1, fb1, fw2, fb2, fw3)
    z = x1 * (_fft_conv_centred(v, filt[:, 0]) + v * fbias[0])
    y = x2 * (_fft_conv_centred(z, filt[:, 1]) + z * fbias[1])
    return y @ w_out


def _swiglu(t, w_gu, w_dn):
    gate, up = jnp.split(t @ w_gu, 2, axis=-1)
    return (jax.nn.silu(gate) * up) @ w_dn


def _moe(h, router_w, router_b, exp_gu, exp_down, sh_gu, sh_down):
    B, L, D = h.shape
    t = h.reshape(B * L, D)
    T = t.shape[0]
    scores = jax.nn.sigmoid((t @ router_w).astype(f32))
    biased = scores + router_b.astype(f32)
    grp = lax.top_k(biased.reshape(T, N_GROUPS, N_EXPERTS // N_GROUPS), 2)[0].sum(-1)
    _, gidx = lax.top_k(grp, TOPK_GROUPS)
    gmask = jax.nn.one_hot(gidx, N_GROUPS, dtype=f32).sum(-2) > 0
    emask = jnp.repeat(gmask, N_EXPERTS // N_GROUPS, axis=-1)
    _, eidx = lax.top_k(jnp.where(emask, biased, -jnp.inf), TOP_K)
    sel = jnp.take_along_axis(scores, eidx, axis=-1)
    wts = ROUTED_SCALE * sel / jnp.sum(sel, axis=-1, keepdims=True)
    gates = jnp.zeros((T, N_EXPERTS), f32).at[jnp.arange(T)[:, None], eidx].set(wts)

    def expert_step(acc, e):
        gu_e, dn_e, g_e = e
        return acc + g_e[:, None] * _swiglu(t, gu_e, dn_e), None

    routed, _ = lax.scan(expert_step, jnp.zeros_like(t), (exp_gu, exp_down, gates.T.astype(t.dtype)))
    return (_swiglu(t, sh_gu, sh_down) + routed).reshape(B, L, D)


def setup_inputs(seed: int = 0) -> dict:
    key = jax.random.key(seed)
    ks = iter(jax.random.split(key, 40))

    def nrm(shape, scale):
        return jax.random.normal(next(ks), shape, f32) * scale

    D = D_MODEL
    i_b = nrm((N_EVEN, 2, 1, B_HEADS), 0.1)
    f_b = jnp.linspace(3.0, 6.0, B_HEADS, dtype=f32)[None, None, None, :] + nrm((N_EVEN, 2, 1, B_HEADS), 0.1)
    return {
        'x': nrm((BATCH, SEQ, D), 1.0),
        'c': nrm((BATCH, D), 1.0),
        'ctx': nrm((BATCH, CTX_LEN, D), 1.0),
        'c_ctx': nrm((D,), 1.0),
        'w_mod': nrm((DEPTH, D, N_MOD * D), 0.5 * D ** -0.5),
        'b_mod': nrm((DEPTH, N_MOD * D), 0.02),
        'norm_g': 1.0 + nrm((DEPTH, 4, D), 0.05),
        'w_in_ab': nrm((N_EVEN, D, P_AB), D ** -0.5),
        'conv_ab_w': nrm((N_EVEN, SHORT_CONV, 2 * B_WIDTH), SHORT_CONV ** -0.5),
        'conv_ab_b': nrm((N_EVEN, 2 * B_WIDTH), 0.02),
        'gate_b_ab': jnp.concatenate([i_b, f_b], axis=2).reshape(N_EVEN, GATE_W),
        'diff_lambda': nrm((N_EVEN, 4, A_DQK), 0.1),
        'head_g_a': 1.0 + nrm((N_EVEN, A_VW), 0.05),
        'head_g_b': 1.0 + nrm((N_EVEN, B_WIDTH), 0.05),
        'w_out_ab': nrm((N_EVEN, MIX_W, D), MIX_W ** -0.5),
        'w_in_hy': nrm((N_ODD, D, 3 * D), D ** -0.5),
        'conv_hy_w': nrm((N_ODD, SHORT_CONV, 3 * D), SHORT_CONV ** -0.5),
        'conv_hy_b': nrm((N_ODD, 3 * D), 0.02),
        'filt_w1': nrm((N_ODD, POS_EMB_DIM, FILTER_HIDDEN), POS_EMB_DIM ** -0.5),
        'filt_b1': nrm((N_ODD, FILTER_HIDDEN), 0.1),
        'filt_w2': nrm((N_ODD, FILTER_HIDDEN, FILTER_HIDDEN), FILTER_HIDDEN ** -0.5),
        'filt_b2': nrm((N_ODD, FILTER_HIDDEN), 0.1),
        'filt_w3': nrm((N_ODD, FILTER_HIDDEN, HY_ORDER * D), 0.1 * FILTER_HIDDEN ** -0.5),
        'filt_bias': nrm((N_ODD, HY_ORDER, D), 0.1),
        'w_out_hy': nrm((N_ODD, D, D), D ** -0.5),
        'router_w': nrm((DEPTH, D, N_EXPERTS), D ** -0.5),
        'router_b': nrm((DEPTH, N_EXPERTS), 0.01),
        'exp_gu': nrm((DEPTH, N_EXPERTS, D, 2 * D_EXPERT), D ** -0.5),
        'exp_down': nrm((DEPTH, N_EXPERTS, D_EXPERT, D), D_EXPERT ** -0.5),
        'sh_gu': nrm((DEPTH, D, 2 * D_SHARED), D ** -0.5),
        'sh_down': nrm((DEPTH, D_SHARED, D), D_SHARED ** -0.5),
    }


def reference(x, c, ctx, c_ctx, w_mod, b_mod, norm_g, w_in_ab, conv_ab_w, conv_ab_b, gate_b_ab,
              diff_lambda, head_g_a, head_g_b, w_out_ab, w_in_hy, conv_hy_w, conv_hy_b,
              filt_w1, filt_b1, filt_w2, filt_b2, filt_w3, filt_bias, w_out_hy,
              router_w, router_b, exp_gu, exp_down, sh_gu, sh_down):
    silu_c = jax.nn.silu(c)
    silu_cc = jax.nn.silu(c_ctx)
    for l in range(DEPTH):
        mod = silu_c @ w_mod[l] + b_mod[l]
        sh_m, sc_m, g_m, sh_f, sc_f, g_f = jnp.split(mod, N_MOD, axis=-1)
        h = _modulate(_rms(x, norm_g[l, 0]), sh_m, sc_m)
        if l % 2 == 0:
            e = l // 2
            lam_init = 0.8 - 0.6 * math.exp(-0.3 * l)
            mod_c = silu_cc @ w_mod[l, :, :2 * D_MODEL] + b_mod[l, :2 * D_MODEL]
            shift_c, scale_c = jnp.split(mod_c, 2)
            hc = _rms(ctx, norm_g[l, 0]) * (1 + scale_c) + shift_c
            mix = _ab_mixer(h, hc, w_in_ab[e], conv_ab_w[e], conv_ab_b[e], gate_b_ab[e], diff_lambda[e],
                            head_g_a[e], head_g_b[e], w_out_ab[e], lam_init)
        else:
            o = l // 2
            mix = _hyena_mixer(h, w_in_hy[o], conv_hy_w[o], conv_hy_b[o], filt_w1[o], filt_b1[o],
                               filt_w2[o], filt_b2[o], filt_w3[o], filt_bias[o], w_out_hy[o])
        x = x + g_m[:, None, :] * _rms(mix, norm_g[l, 1])
        h = _modulate(_rms(x, norm_g[l, 2]), sh_f, sc_f)
        x = x + g_f[:, None, :] * _rms(_moe(h, router_w[l], router_b[l], exp_gu[l], exp_down[l],
                                            sh_gu[l], sh_down[l]), norm_g[l, 3])
    return x
```

```python
import functools
import math

import numpy as np
import jax
import jax.numpy as jnp
from jax import lax
from jax.experimental import pallas as pl
from jax.experimental.pallas import tpu as pltpu

f32 = jnp.float32
bf16 = jnp.bfloat16

RMS_EPS = 1e-6
A_HEADS = 4
A_DQK = 64
A_DV = 128
B_HEADS = 4
B_DH = 128
B_WIDTH = B_HEADS * B_DH
A_QW = A_HEADS * 2 * A_DQK
A_VW = A_HEADS * A_DV
GRID_W = 64
ROPE_BASE = 10000.0
ROPE_AXIS_PAIRS = A_DQK // 4
N_EXPERTS = 64
TOP_K = 8
N_GROUPS = 8
TOPK_GROUPS = 4
D_EXPERT = 256
ROUTED_SCALE = 2.5
HY_ORDER = 2
POS_EMB_DIM = 33
FILTER_SIN_W = 1.0
DECAY_FAST_PCT = 0.3
DECAY_SLOW_PCT = 1.5
DECAY_TARGET = 1e-2

LANES = 128
VMEM_LIMIT = 56 * 1024 * 1024
MLSTM_CHUNK = 256
FFT_PAD = 8


def _cp(*sem):
    return pltpu.CompilerParams(dimension_semantics=sem, vmem_limit_bytes=VMEM_LIMIT)


def _split_bf16(a):
    hi = a.astype(bf16)
    lo = (a - hi.astype(f32)).astype(bf16)
    return hi, lo


def _dot(a, b, dims=(((1,), (0,)), ((), ()))):
    return lax.dot_general(a, b, dims, preferred_element_type=f32)


_NT = (((1,), (1,)), ((), ()))
_TN = (((0,), (0,)), ((), ()))


def _dot3(a, b, dims=(((1,), (0,)), ((), ()))):
    ah, al = _split_bf16(a)
    bh, bl = _split_bf16(b)
    return _dot(ah, bh, dims) + (_dot(ah, bl, dims) + _dot(al, bh, dims))


def _silu(v):
    return v / (1.0 + jnp.exp(-v))


def _sigmoid(v):
    return 1.0 / (1.0 + jnp.exp(-v))


def _log_sigmoid(v):
    return jnp.minimum(v, 0.0) - jnp.log(1.0 + jnp.exp(-jnp.abs(v)))


def _mod_kernel(c_ref, w_ref, b_ref, o_ref):
    o_ref[...] = _dot3(_silu(c_ref[...]), w_ref[...]) + b_ref[...]


def _mod(cc, w, b):
    rows, d = cc.shape
    n = w.shape[1]
    tn = d
    return pl.pallas_call(
        _mod_kernel,
        grid=(n // tn,),
        in_specs=[pl.BlockSpec((rows, d), lambda j: (0, 0)),
                  pl.BlockSpec((d, tn), lambda j: (0, j)),
                  pl.BlockSpec((1, tn), lambda j: (0, j))],
        out_specs=pl.BlockSpec((rows, tn), lambda j: (0, j)),
        out_shape=jax.ShapeDtypeStruct((rows, n), f32),
        compiler_params=_cp("arbitrary"),
        name="mod",
    )(cc, w, b.reshape(1, n))


def _norm_mod(xv, g, shift, scale):
    y = xv * lax.rsqrt(jnp.mean(xv * xv, axis=-1, keepdims=True) + RMS_EPS)
    return (y * g) * (1.0 + scale) + shift


def _norm_kernel(x_ref, g_ref, sh_ref, sc_ref, o_ref):
    o_ref[...] = _norm_mod(x_ref[...], g_ref[...], sh_ref[...], sc_ref[...]).astype(o_ref.dtype)


def _bidx(arr):
    if arr.shape[0] == 1:
        return lambda b, *_: (0, 0, 0)
    return lambda b, *_: (b, 0, 0)


def _norm(x, g, shift, scale, tl=512):
    bsz, l, d = x.shape
    tl = min(tl, l)
    return pl.pallas_call(
        _norm_kernel,
        grid=(bsz, l // tl),
        in_specs=[pl.BlockSpec((None, tl, d), lambda b, i: (b, i, 0)),
                  pl.BlockSpec((1, d), lambda b, i: (0, 0)),
                  pl.BlockSpec((None, 1, d), _bidx(shift)),
                  pl.BlockSpec((None, 1, d), _bidx(scale))],
        out_specs=pl.BlockSpec((None, tl, d), lambda b, i: (b, i, 0)),
        out_shape=jax.ShapeDtypeStruct((bsz, l, d), bf16),
        compiler_params=_cp("parallel", "parallel"),
        name="norm",
    )(x, g.reshape(1, d), shift, scale)


def _mm_plain_kernel(h_ref, w_ref, b_ref, o_ref):
    o_ref[...] = (_dot(h_ref[...], w_ref[...]) + b_ref[...]).astype(o_ref.dtype)


def _mm_rope_kernel(h_ref, w_ref, cos_ref, sin_ref, o_ref, *, scale):
    p = _dot(h_ref[...], w_ref[...])
    n = o_ref.shape[-1]
    o_ref[...] = ((p[:, :n] * cos_ref[...] + p[:, n:] * sin_ref[...]) * scale).astype(o_ref.dtype)


def _mm_conv_kernel(h_ref, w_ref, cw_ref, cb_ref, o_ref, *, act):
    p = _dot(h_ref[...], w_ref[...])
    l = p.shape[0]
    row = lax.broadcasted_iota(jnp.int32, p.shape, 0)
    prev = jnp.where(row == 0, 0.0, pltpu.roll(p, 1, 0))
    nxt = jnp.where(row == l - 1, 0.0, pltpu.roll(p, l - 1, 0))
    y = prev * cw_ref[0:1, :] + p * cw_ref[1:2, :] + nxt * cw_ref[2:3, :] + cb_ref[...]
    if act:
        y = _silu(y)
    o_ref[...] = y.astype(o_ref.dtype)


def _mm(h, w, *, out_dtype=bf16, bias=None, rope=None, conv=None, tl=512, tn=512):
    bsz, l, k = h.shape
    n = w.shape[1]
    if rope is not None:
        cos, sin, scale = rope
        n_out = n // 2
        tl = min(tl, l)
        return pl.pallas_call(
            functools.partial(_mm_rope_kernel, scale=scale),
            grid=(bsz, l // tl),
            in_specs=[pl.BlockSpec((None, tl, k), lambda b, i: (b, i, 0)),
                      pl.BlockSpec((k, n), lambda b, i: (0, 0)),
                      pl.BlockSpec((tl, n_out), lambda b, i: (i, 0)),
                      pl.BlockSpec((tl, n_out), lambda b, i: (i, 0))],
            out_specs=pl.BlockSpec((None, tl, n_out), lambda b, i: (b, i, 0)),
            out_shape=jax.ShapeDtypeStruct((bsz, l, n_out), out_dtype),
            compiler_params=_cp("parallel", "parallel"),
            name="mm_rope",
        )(h, w, cos, sin)
    tn = min(tn, n)
    if conv is not None:
        cw, cb, act = conv
        return pl.pallas_call(
            functools.partial(_mm_conv_kernel, act=act),
            grid=(bsz, n // tn),
            in_specs=[pl.BlockSpec((None, l, k), lambda b, j: (b, 0, 0)),
                      pl.BlockSpec((k, tn), lambda b, j: (0, j)),
                      pl.BlockSpec((3, tn), lambda b, j: (0, j)),
                      pl.BlockSpec((1, tn), lambda b, j: (0, j))],
            out_specs=pl.BlockSpec((None, l, tn), lambda b, j: (b, 0, j)),
            out_shape=jax.ShapeDtypeStruct((bsz, l, n), out_dtype),
            compiler_params=_cp("parallel", "arbitrary"),
            name="mm_conv",
        )(h, w, cw, cb.reshape(1, n))
    if bias is None:
        bias = jnp.zeros((n,), f32)
    tl = min(tl, l)
    return pl.pallas_call(
        _mm_plain_kernel,
        grid=(bsz, l // tl, n // tn),
        in_specs=[pl.BlockSpec((None, tl, k), lambda b, i, j: (b, i, 0)),
                  pl.BlockSpec((k, tn), lambda b, i, j: (0, j)),
                  pl.BlockSpec((1, tn), lambda b, i, j: (0, j))],
        out_specs=pl.BlockSpec((None, tl, tn), lambda b, i, j: (b, i, j)),
        out_shape=jax.ShapeDtypeStruct((bsz, l, n), out_dtype),
        compiler_params=_cp("parallel", "parallel", "arbitrary"),
        name="mm_plain",
    )(h, w, bias.reshape(1, n))


def _attn_kernel(lv_ref, q_ref, kc_ref, k_ref, vc_ref, v_ref, g_ref, o_ref, *, lam_init):
    tq = q_ref.shape[0]
    lv = lv_ref[...]
    lam = (jnp.exp(jnp.sum(lv[0:1] * lv[1:2], axis=1, keepdims=True))
           - jnp.exp(jnp.sum(lv[2:3] * lv[3:4], axis=1, keepdims=True)) + lam_init)
    first = lax.broadcasted_iota(jnp.int32, (tq, A_DV), 1) < A_DQK
    for hd in range(A_HEADS):
        cs = slice(hd * A_DV, (hd + 1) * A_DV)
        qh = q_ref[:, cs]
        zero = jnp.zeros_like(qh)
        q2 = jnp.concatenate([jnp.where(first, qh, zero), jnp.where(first, zero, qh)], axis=0)
        s_c = _dot(q2, kc_ref[:, cs], _NT)
        s_l = _dot(q2, k_ref[:, cs], _NT)
        m = jnp.maximum(jnp.max(s_c, axis=1, keepdims=True), jnp.max(s_l, axis=1, keepdims=True))
        p_c = jnp.exp(s_c - m)
        p_l = jnp.exp(s_l - m)
        inv = 1.0 / (jnp.sum(p_c, axis=1, keepdims=True) + jnp.sum(p_l, axis=1, keepdims=True))
        w0 = inv[:tq]
        w1 = inv[tq:] * lam
        a_c = (p_c[:tq] * w0 - p_c[tq:] * w1).astype(bf16)
        a_l = (p_l[:tq] * w0 - p_l[tq:] * w1).astype(bf16)
        o = _dot(a_c, vc_ref[:, cs]) + _dot(a_l, v_ref[:, cs])
        o = o * lax.rsqrt(jnp.mean(o * o, axis=1, keepdims=True) + RMS_EPS)
        o_ref[:, cs] = (o * g_ref[:, cs] * (1.0 - lam_init)).astype(o_ref.dtype)


def _attn(lv, q, k, vvo, ckv, g_a, lam_init, tq=256):
    bsz, s, _ = q.shape
    lc = ckv.shape[1]
    tq = min(tq, s)
    w = A_QW
    return pl.pallas_call(
        functools.partial(_attn_kernel, lam_init=lam_init),
        grid=(bsz, s // tq),
        in_specs=[pl.BlockSpec(lv.shape, lambda b, i: (0, 0)),
                  pl.BlockSpec((None, tq, w), lambda b, i: (b, i, 0)),
                  pl.BlockSpec((None, lc, w), lambda b, i: (b, 0, 0)),
                  pl.BlockSpec((None, s, w), lambda b, i: (b, 0, 0)),
                  pl.BlockSpec((None, lc, w), lambda b, i: (b, 0, 1)),
                  pl.BlockSpec((None, s, w), lambda b, i: (b, 0, 0)),
                  pl.BlockSpec((1, w), lambda b, i: (0, 0))],
        out_specs=pl.BlockSpec((None, tq, w), lambda b, i: (b, i, 0)),
        out_shape=jax.ShapeDtypeStruct((bsz, s, w), bf16),
        compiler_params=_cp("parallel", "arbitrary"),
        name="diff_attn",
    )(lv, q, ckv, k, ckv, vvo, g_a.reshape(1, w))


_LN_QSCALE = math.log(B_DH ** -0.5)
_NCHAIN = 2 * B_HEADS


def _chunk_gate_sums(gi, gf, tri):
    lf = _log_sigmoid(gf)
    hi, lo = _split_bf16(lf)
    cum = _dot(tri, hi) + _dot(tri, lo)
    t = gf.shape[0]
    tot = cum[t - 1:t, :]
    rcum = tot - cum + lf
    fwd = lax.broadcasted_iota(jnp.int32, gf.shape, 1) < B_HEADS
    bd = jnp.where(fwd, cum, rcum)
    return bd, tot, (bd - gi).T


def _lower_tri(t):
    r = lax.broadcasted_iota(jnp.int32, (t, t), 0)
    c = lax.broadcasted_iota(jnp.int32, (t, t), 1)
    return r, c


def _absorb(c_ref, n_ref, m_ref, ch, bcol, tot_c, gi_c, kf, vb):
    m_prev = m_ref[ch][:, 0:1]
    g = tot_c - bcol + gi_c
    m_new = jnp.maximum(tot_c + m_prev, jnp.max(g, axis=0, keepdims=True))
    wgt = jnp.exp(g - m_new)
    decay = jnp.exp(tot_c + m_prev - m_new)
    kw = kf * wgt
    c_ref[ch] = decay * c_ref[ch] + _dot(kw.astype(bf16), vb, _TN)
    n_ref[ch] = decay * n_ref[ch] + jnp.sum(kw, axis=0, keepdims=True)
    m_ref[ch] = jnp.broadcast_to(m_new, m_ref.shape[1:])


def _mlstm_kernel(qk_ref, vvo_ref, g_ref, ck_ref, ckv_ref, cg_ref, gb_ref, o_ref,
                  hf_ref, hb_ref, c_ref, n_ref, m_ref, *, tc):
    s = o_ref.shape[0]
    lc = ck_ref.shape[0]
    nc = s // tc
    w = B_WIDTH
    dh = B_DH

    c_ref[...] = jnp.zeros_like(c_ref)
    n_ref[...] = jnp.zeros_like(n_ref)
    m_ref[...] = jnp.zeros_like(m_ref)

    r, cidx = _lower_tri(lc)
    tri_c = jnp.where(cidx <= r, 1.0, 0.0).astype(bf16)
    cg = cg_ref[...]
    bd, tot, _ = _chunk_gate_sums(cg[:, :LANES], cg[:, LANES:], tri_c)
    gi = cg[:, :LANES]
    for ch in range(_NCHAIN):
        hs = slice((ch % B_HEADS) * dh, (ch % B_HEADS + 1) * dh)
        vs = slice(2 * w + (ch % B_HEADS) * dh, 2 * w + (ch % B_HEADS + 1) * dh)
        _absorb(c_ref, n_ref, m_ref, ch, bd[:, ch:ch + 1], tot[:, ch:ch + 1], gi[:, ch:ch + 1],
                ck_ref[:, hs].astype(f32), ckv_ref[:, vs])

    r, cidx = _lower_tri(tc)
    tri = jnp.where(cidx <= r, 1.0, 0.0).astype(bf16)
    causal = cidx <= r
    anti = cidx >= r

    def step(i, carry):
        for d in range(2):
            row0 = pl.multiple_of((i if d == 0 else nc - 1 - i) * tc, tc)
            rows = pl.ds(row0, tc)
            gch = g_ref[rows, :]
            gi = gch[:, :LANES]
            bd, tot, xt = _chunk_gate_sums(gi, gch[:, LANES:], tri)
            mask = causal if d == 0 else anti
            dst = hf_ref if d == 0 else hb_ref
            for hd in range(B_HEADS):
                ch = d * B_HEADS + hd
                hs = slice(hd * dh, (hd + 1) * dh)
                qb = qk_ref[rows, hs]
                kb = qk_ref[rows, slice(w + hd * dh, w + (hd + 1) * dh)]
                vb = vvo_ref[rows, slice(w + hd * dh, w + (hd + 1) * dh)]
                bcol = bd[:, ch:ch + 1]
                dmat = jnp.where(mask, bcol - xt[ch:ch + 1, :], -jnp.inf)
                m_prev = m_ref[ch][:, 0:1]
                inter = bcol + m_prev
                m_t = jnp.maximum(inter, jnp.max(dmat, axis=1, keepdims=True))
                e = jnp.exp(dmat - m_t + _LN_QSCALE)
                smat = _dot(qb, kb, _NT) * e
                sc = jnp.exp(inter - m_t + _LN_QSCALE)
                num = sc * _dot(qb, c_ref[ch].astype(bf16)) + _dot(smat.astype(bf16), vb)
                qn = jnp.sum(qb.astype(f32) * n_ref[ch], axis=1, keepdims=True)
                den = sc * qn + jnp.sum(smat, axis=1, keepdims=True)
                hout = num * (1.0 / jnp.maximum(jnp.abs(den), jnp.exp(-m_t)))
                dst[rows, hs] = hout
                _absorb(c_ref, n_ref, m_ref, ch, bcol, tot[:, ch:ch + 1], gi[:, ch:ch + 1],
                        kb.astype(f32), vb)
        return carry

    lax.fori_loop(0, nc, step, 0)

    for hd in range(B_HEADS):
        hs = slice(hd * dh, (hd + 1) * dh)
        hsum = hf_ref[:, hs] + hb_ref[:, hs]
        hn = hsum * lax.rsqrt(jnp.mean(hsum * hsum, axis=1, keepdims=True) + RMS_EPS)
        og = _sigmoid(vvo_ref[:, slice(2 * w + hd * dh, 2 * w + (hd + 1) * dh)].astype(f32))
        o_ref[:, hs] = (hn * gb_ref[:, hs] * og).astype(o_ref.dtype)


def _mlstm(qk, vvo, gates, cbk, ckv, cg, g_b):
    bsz, s, _ = qk.shape
    lc = cbk.shape[1]
    w = B_WIDTH
    tc = min(MLSTM_CHUNK, s)
    return pl.pallas_call(
        functools.partial(_mlstm_kernel, tc=tc),
        grid=(bsz,),
        in_specs=[pl.BlockSpec((None, s, 2 * w), lambda b: (b, 0, 0)),
                  pl.BlockSpec((None, s, 3 * w), lambda b: (b, 0, 0)),
                  pl.BlockSpec((None, s, 2 * LANES), lambda b: (b, 0, 0)),
                  pl.BlockSpec((None, lc, w), lambda b: (b, 0, 0)),
                  pl.BlockSpec((None, lc, 3 * w), lambda b: (b, 0, 0)),
                  pl.BlockSpec((None, lc, 2 * LANES), lambda b: (b, 0, 0)),
                  pl.BlockSpec((1, w), lambda b: (0, 0))],
        out_specs=pl.BlockSpec((None, s, w), lambda b: (b, 0, 0)),
        out_shape=jax.ShapeDtypeStruct((bsz, s, w), bf16),
        scratch_shapes=[pltpu.VMEM((s, w), f32), pltpu.VMEM((s, w), f32),
                        pltpu.VMEM((_NCHAIN, B_DH, B_DH), f32),
                        pltpu.VMEM((_NCHAIN, 1, B_DH), f32),
                        pltpu.VMEM((_NCHAIN, 1, LANES), f32)],
        compiler_params=_cp("arbitrary"),
        name="mlstm",
    )(qk, vvo, gates, cbk, ckv, cg, g_b.reshape(1, w))


def _out_kernel(*refs, n_act):
    acts = refs[:n_act]
    ws = refs[n_act:2 * n_act]
    x_ref, g_ref, gate_ref, o_ref = refs[2 * n_act:]
    mix = _dot(acts[0][...], ws[0][...])
    for a, wr in zip(acts[1:], ws[1:]):
        mix = mix + _dot(a[...], wr[...])
    y = mix * lax.rsqrt(jnp.mean(mix * mix, axis=-1, keepdims=True) + RMS_EPS) * g_ref[...]
    o_ref[...] = x_ref[...] + gate_ref[...] * y


def _out_proj(acts, ws, x, g, gate, tl=512):
    bsz, l, d = x.shape
    tl = min(tl, l)
    n_act = len(acts)
    in_specs = [pl.BlockSpec((None, tl, a.shape[2]), lambda b, i: (b, i, 0)) for a in acts]
    in_specs += [pl.BlockSpec(wm.shape, lambda b, i: (0, 0)) for wm in ws]
    in_specs += [pl.BlockSpec((None, tl, d), lambda b, i: (b, i, 0)),
                 pl.BlockSpec((1, d), lambda b, i: (0, 0)),
                 pl.BlockSpec((None, 1, d), _bidx(gate))]
    return pl.pallas_call(
        functools.partial(_out_kernel, n_act=n_act),
        grid=(bsz, l // tl),
        in_specs=in_specs,
        out_specs=pl.BlockSpec((None, tl, d), lambda b, i: (b, i, 0)),
        out_shape=jax.ShapeDtypeStruct((bsz, l, d), f32),
        compiler_params=_cp("parallel", "parallel"),
        name="out_proj",
    )(*acts, *ws, x, g.reshape(1, d), gate)


def _router_kernel(x_ref, g_ref, sh_ref, sc_ref, rw_ref, rb_ref, h_ref, gt_ref):
    hf = _norm_mod(x_ref[...], g_ref[...], sh_ref[...], sc_ref[...])
    h_ref[...] = hf.astype(h_ref.dtype)
    tl = hf.shape[0]
    per = N_EXPERTS // N_GROUPS
    logits = _dot3(rw_ref[...], hf, _NT)
    s3 = _sigmoid(logits).reshape(N_GROUPS, per, tl)
    b3 = s3 + rb_ref[...].reshape(N_GROUPS, per, 1)
    neg = -jnp.inf
    jdx = lax.broadcasted_iota(jnp.int32, b3.shape, 1)
    gdx = lax.broadcasted_iota(jnp.int32, b3.shape, 0)
    m1 = jnp.max(b3, axis=1, keepdims=True)
    f1 = jnp.min(jnp.where(b3 == m1, jdx, per), axis=1, keepdims=True)
    m2 = jnp.max(jnp.where(jdx == f1, neg, b3), axis=1, keepdims=True)
    grp = m1 + m2
    g1 = lax.broadcasted_iota(jnp.int32, grp.shape, 0)
    cnt = jnp.zeros(grp.shape, jnp.int32)
    for gp in range(N_GROUPS):
        rv = grp[gp:gp + 1]
        ahead = jnp.where(rv > grp, 1, jnp.where(rv == grp, jnp.where(g1 > gp, 1, 0), 0))
        cnt = cnt + ahead
    v = jnp.where(cnt < TOPK_GROUPS, b3, neg)
    eidx = gdx * per + jdx
    sel = jnp.zeros(b3.shape, f32)
    for _ in range(TOP_K):
        m = jnp.max(jnp.max(v, axis=1, keepdims=True), axis=0, keepdims=True)
        cand = jnp.where(v == m, eidx, N_EXPERTS)
        fi = jnp.min(jnp.min(cand, axis=1, keepdims=True), axis=0, keepdims=True)
        hit = eidx == fi
        sel = jnp.where(hit, 1.0, sel)
        v = jnp.where(hit, neg, v)
    ssel = sel * s3
    den = jnp.sum(jnp.sum(ssel, axis=1, keepdims=True), axis=0, keepdims=True)
    gates = ((ROUTED_SCALE * ssel) / den).reshape(N_EXPERTS, tl)
    gates = jnp.concatenate([gates, jnp.zeros((LANES - N_EXPERTS, tl), f32)], axis=0)
    gt_ref[...] = gates.T


def _router(x, g, shift, scale, rw_t, rb, tl=512):
    bsz, l, d = x.shape
    tl = min(tl, l)
    return pl.pallas_call(
        _router_kernel,
        grid=(bsz, l // tl),
        in_specs=[pl.BlockSpec((None, tl, d), lambda b, i: (b, i, 0)),
                  pl.BlockSpec((1, d), lambda b, i: (0, 0)),
                  pl.BlockSpec((None, 1, d), _bidx(shift)),
                  pl.BlockSpec((None, 1, d), _bidx(scale)),
                  pl.BlockSpec((N_EXPERTS, d), lambda b, i: (0, 0)),
                  pl.BlockSpec((N_EXPERTS, 1), lambda b, i: (0, 0))],
        out_specs=[pl.BlockSpec((None, tl, d), lambda b, i: (b, i, 0)),
                   pl.BlockSpec((None, tl, LANES), lambda b, i: (b, i, 0))],
        out_shape=[jax.ShapeDtypeStruct((bsz, l, d), bf16),
                   jax.ShapeDtypeStruct((bsz, l, LANES), f32)],
        compiler_params=_cp("parallel", "parallel"),
        name="router",
    )(x, g.reshape(1, d), shift, scale, rw_t, rb.reshape(N_EXPERTS, 1))


def _swiglu_act(hh):
    half = hh.shape[1] // 2
    return _silu(hh[:, :half]) * hh[:, half:]


def _moe_kernel(h_ref, gt_ref, gu_ref, dn_ref, sgu_ref, sdn_ref, x_ref, g_ref, gate_ref,
                o_ref, acc_ref):
    e = pl.program_id(2)
    hb = h_ref[...]

    @pl.when(e == 0)
    def _():
        act = _swiglu_act(_dot(hb, sgu_ref[...]))
        acc_ref[...] = _dot(act.astype(bf16), sdn_ref[...])

    gts = gt_ref[...]
    lane = lax.broadcasted_iota(jnp.int32, gts.shape, 1)
    gcol = jnp.sum(jnp.where(lane == e, gts, 0.0), axis=1, keepdims=True)
    act = _swiglu_act(_dot(hb, gu_ref[...])) * gcol
    acc_ref[...] += _dot(act.astype(bf16), dn_ref[...])

    @pl.when(e == pl.num_programs(2) - 1)
    def _():
        mo = acc_ref[...]
        y = mo * lax.rsqrt(jnp.mean(mo * mo, axis=-1, keepdims=True) + RMS_EPS) * g_ref[...]
        o_ref[...] = x_ref[...] + gate_ref[...] * y


def _moe(h2, gates, gu, dn, sgu, sdn, x, g, gate, tm=1024):
    bsz, l, d = x.shape
    tm = min(tm, l)
    ne = gu.shape[0]
    return pl.pallas_call(
        _moe_kernel,
        grid=(bsz, l // tm, ne),
        in_specs=[pl.BlockSpec((None, tm, d), lambda b, i, e: (b, i, 0)),
                  pl.BlockSpec((None, tm, LANES), lambda b, i, e: (b, i, 0)),
                  pl.BlockSpec((None,) + gu.shape[1:], lambda b, i, e: (e, 0, 0)),
                  pl.BlockSpec((None,) + dn.shape[1:], lambda b, i, e: (e, 0, 0)),
                  pl.BlockSpec(sgu.shape, lambda b, i, e: (0, 0)),
                  pl.BlockSpec(sdn.shape, lambda b, i, e: (0, 0)),
                  pl.BlockSpec((None, tm, d), lambda b, i, e: (b, i, 0)),
                  pl.BlockSpec((1, d), lambda b, i, e: (0, 0)),
                  pl.BlockSpec((None, 1, d), _bidx(gate))],
        out_specs=pl.BlockSpec((None, tm, d), lambda b, i, e: (b, i, 0)),
        out_shape=jax.ShapeDtypeStruct((bsz, l, d), f32),
        scratch_shapes=[pltpu.VMEM((tm, d), f32)],
        compiler_params=_cp("parallel", "parallel", "arbitrary"),
        name="moe",
    )(h2, gates, gu, dn, sgu, sdn, x, g.reshape(1, d), gate)


def _filter_kernel(z_ref, w1_ref, b1_ref, w2_ref, b2_ref, w3_ref, win_ref, o_ref):
    hid = jnp.sin(FILTER_SIN_W * (_dot3(z_ref[...], w1_ref[...]) + b1_ref[...]))
    hid = jnp.sin(FILTER_SIN_W * (_dot3(hid, w2_ref[...]) + b2_ref[...]))
    o_ref[...] = _dot3(hid, w3_ref[...]) * win_ref[...]


def _filters(z, w1, b1, w2, b2, w3, window, tn=512):
    l, p = z.shape
    hdim = w1.shape[1]
    n = w3.shape[1]
    d = window.shape[1]
    nd = d // tn
    return pl.pallas_call(
        _filter_kernel,
        grid=(n // tn,),
        in_specs=[pl.BlockSpec((l, p), lambda j: (0, 0)),
                  pl.BlockSpec((p, hdim), lambda j: (0, 0)),
                  pl.BlockSpec((1, hdim), lambda j: (0, 0)),
                  pl.BlockSpec((hdim, hdim), lambda j: (0, 0)),
                  pl.BlockSpec((1, hdim), lambda j: (0, 0)),
                  pl.BlockSpec((hdim, tn), lambda j: (0, j)),
                  pl.BlockSpec((l, tn), lambda j: (0, j % nd))],
        out_specs=pl.BlockSpec((l, tn), lambda j: (0, j)),
        out_shape=jax.ShapeDtypeStruct((l, n), f32),
        compiler_params=_cp("arbitrary"),
        name="hyena_filter",
    )(z, w1, b1.reshape(1, hdim), w2, b2.reshape(1, hdim), w3, window)


def _dft_tables(l):
    n = 2 * l
    n1 = math.isqrt(n)
    assert n == n1 * n1 and n1 % 16 == 0
    na = l // n1
    a = np.arange(na)
    b = np.arange(n1)
    c = np.arange(n1)
    th = 2.0 * np.pi * ((n1 * a[None, None, :] + b[:, None, None]) * c[None, :, None]) / n
    t1 = np.concatenate([np.cos(th), -np.sin(th)], axis=1)
    ph = 2.0 * np.pi * (b[:, None] * b[None, :]) / n1
    cs, sn = np.cos(ph), np.sin(ph)
    a3 = np.block([[cs, sn], [-sn, cs]])
    a3i = np.block([[cs, -sn], [sn, cs]])
    a2 = np.arange(na) + na // 2
    th2 = 2.0 * np.pi * ((n1 * a2[None, :, None] + b[:, None, None]) * c[None, None, :]) / n
    t2 = np.concatenate([np.cos(th2), -np.sin(th2)], axis=2)
    return [jnp.asarray(t, f32).astype(bf16) for t in (t1, a3, a3i, t2)]


def _fft_dims(t1):
    n1, _, na = t1.shape
    return n1, na, 2 * n1 + FFT_PAD, n1 + FFT_PAD


def _dft_forward(uf_ref, t1_ref, zs_ref):
    n1, na, sb, su = _fft_dims(t1_ref)
    for b in range(n1):
        ub = uf_ref[pl.ds(b, na, stride=su), :].astype(bf16)
        zb = _dot(t1_ref[b], ub)
        zs_ref[pl.ds(b, n1, stride=sb), :] = zb[:n1]
        zs_ref[pl.ds(n1 + b, n1, stride=sb), :] = zb[n1:]


def _spectrum_kernel(f_ref, t1_ref, a3_ref, o_ref, uf_ref, zs_ref, *, scale):
    n1, na, sb, su = _fft_dims(t1_ref)
    for a in range(na):
        uf_ref[pl.ds(a * su, n1), :] = f_ref[pl.ds(a * n1, n1), :]
    _dft_forward(uf_ref, t1_ref, zs_ref)
    a3 = a3_ref[...]
    for c in range(n1):
        zc = zs_ref[pl.ds(c * sb, 2 * n1), :].astype(bf16)
        o_ref[c] = _dot(a3, zc) * scale


def _spectrum(filt, tabs, dt=128):
    l, n = filt.shape
    t1, a3, _, _ = tabs
    n1, na, sb, su = _fft_dims(t1)
    return pl.pallas_call(
        functools.partial(_spectrum_kernel, scale=1.0 / (2 * l)),
        grid=(n // dt,),
        in_specs=[pl.BlockSpec((l, dt), lambda j: (0, j)),
                  pl.BlockSpec(t1.shape, lambda j: (0, 0, 0)),
                  pl.BlockSpec(a3.shape, lambda j: (0, 0))],
        out_specs=pl.BlockSpec((n1, 2 * n1, dt), lambda j: (0, 0, j)),
        out_shape=jax.ShapeDtypeStruct((n1, 2 * n1, n), f32),
        scratch_shapes=[pltpu.VMEM((na * su, dt), f32),
                        pltpu.VMEM((n1 * sb, dt), f32)],
        compiler_params=_cp("arbitrary"),
        name="hyena_spectrum",
    )(filt, t1, a3)


def _fftconv_kernel(u_ref, xg_ref, kf_ref, fb_ref, t1_ref, a3_ref, a3i_ref, t2_ref, o_ref,
                    uf_ref, zs_ref, qs_ref, y_ref):
    n1, na, sb, su = _fft_dims(t1_ref)
    for a in range(na):
        uf_ref[pl.ds(a * su, n1), :] = u_ref[pl.ds(a * n1, n1), :].astype(f32)
    _dft_forward(uf_ref, t1_ref, zs_ref)
    a3 = a3_ref[...]
    a3i = a3i_ref[...]
    for c in range(n1):
        zc = zs_ref[pl.ds(c * sb, 2 * n1), :].astype(bf16)
        xc = _dot(a3, zc)
        kc = kf_ref[c]
        xr, xi = xc[:n1], xc[n1:]
        kr, ki = kc[:n1], kc[n1:]
        pc = jnp.concatenate([xr * kr - xi * ki, xr * ki + xi * kr], axis=0).astype(bf16)
        qc = _dot(a3i, pc)
        qs_ref[pl.ds(c, n1, stride=sb), :] = qc[:n1]
        qs_ref[pl.ds(n1 + c, n1, stride=sb), :] = qc[n1:]
    for b in range(n1):
        qb = qs_ref[pl.ds(b * sb, 2 * n1), :].astype(bf16)
        y_ref[pl.ds(b, na, stride=su), :] = _dot(t2_ref[b], qb)
    fb = fb_ref[...]
    for a in range(na):
        rows = pl.ds(a * n1, n1)
        uv = uf_ref[pl.ds(a * su, n1), :]
        yv = y_ref[pl.ds(a * su, n1), :]
        o_ref[rows, :] = (xg_ref[rows, :].astype(f32) * (yv + uv * fb)).astype(o_ref.dtype)


def _fftconv(u, u_col, xg, xg_col, kf, kf_col, fbias, tabs, d, dt=128):
    bsz, l, _ = u.shape
    t1, a3, a3i, t2 = tabs
    n1, na, sb, su = _fft_dims(t1)
    nd = d // dt
    uo, go, ko = u_col // dt, xg_col // dt, kf_col // dt
    return pl.pallas_call(
        _fftconv_kernel,
        grid=(nd, bsz),
        in_specs=[pl.BlockSpec((None, l, dt), lambda j, b: (b, 0, j + uo)),
                  pl.BlockSpec((None, l, dt), lambda j, b: (b, 0, j + go)),
                  pl.BlockSpec((n1, 2 * n1, dt), lambda j, b: (0, 0, j + ko)),
                  pl.BlockSpec((1, dt), lambda j, b: (0, j)),
                  pl.BlockSpec(t1.shape, lambda j, b: (0, 0, 0)),
                  pl.BlockSpec(a3.shape, lambda j, b: (0, 0)),
                  pl.BlockSpec(a3i.shape, lambda j, b: (0, 0)),
                  pl.BlockSpec(t2.shape, lambda j, b: (0, 0, 0))],
        out_specs=pl.BlockSpec((None, l, dt), lambda j, b: (b, 0, j)),
        out_shape=jax.ShapeDtypeStruct((bsz, l, d), bf16),
        scratch_shapes=[pltpu.VMEM((na * su, dt), f32),
                        pltpu.VMEM((n1 * sb, dt), f32),
                        pltpu.VMEM((n1 * sb, dt), f32),
                        pltpu.VMEM((na * su, dt), f32)],
        compiler_params=_cp("parallel", "arbitrary"),
        name="hyena_fftconv",
    )(u, xg, kf, fbias.reshape(1, d), t1, a3, a3i, t2)


def _rope_tables(l):
    rows = l // GRID_W
    row = jnp.repeat(jnp.arange(rows), GRID_W)
    col = jnp.tile(jnp.arange(GRID_W), rows)
    inv = ROPE_BASE ** (-jnp.arange(ROPE_AXIS_PAIRS, dtype=f32) / ROPE_AXIS_PAIRS)
    ang = jnp.stack([row, col], axis=-1).astype(f32)[..., None] * inv
    ang = jnp.broadcast_to(ang[:, :, None, :], (l, 2, 2, ROPE_AXIS_PAIRS)).reshape(l, A_DQK)
    reps = A_QW // A_DQK
    return jnp.tile(jnp.cos(ang), (1, reps)), jnp.tile(jnp.sin(ang), (1, reps))


def _rotate_cols(w):
    j = np.arange(w.shape[1])
    lo = (j % (2 * ROPE_AXIS_PAIRS)) < ROPE_AXIS_PAIRS
    perm = np.where(lo, j + ROPE_AXIS_PAIRS, j - ROPE_AXIS_PAIRS)
    sign = np.where(lo, -1.0, 1.0).astype(np.float32)
    return w[:, perm] * sign


def _gate_cols(w_g, b_g):
    idx_i = np.array([d * 2 * B_HEADS + hd for d in range(2) for hd in range(B_HEADS)])
    idx_f = idx_i + B_HEADS
    pad = LANES - _NCHAIN
    k = w_g.shape[0]
    w = jnp.concatenate([w_g[:, idx_i], jnp.zeros((k, pad), f32),
                         w_g[:, idx_f], jnp.zeros((k, pad), f32)], axis=1)
    b = jnp.concatenate([b_g[idx_i], jnp.zeros((pad,), f32), b_g[idx_f], jnp.zeros((pad,), f32)])
    return w, b


def _hyena_consts(l, d):
    j = jnp.arange(l, dtype=f32)
    bands = (POS_EMB_DIM - 1) // 2
    freqs = jnp.linspace(1e-4, bands - 1, bands, dtype=f32)
    ang = (2.0 * math.pi / l) * j[:, None] * freqs[None, :]
    z = jnp.concatenate([(j / (l - 1))[:, None], jnp.cos(ang), -jnp.sin(ang)], axis=-1)
    dist = jnp.abs(j - l // 2) / (l // 2)
    max_decay = math.log(DECAY_TARGET) / DECAY_FAST_PCT
    min_decay = math.log(DECAY_TARGET) / DECAY_SLOW_PCT
    deltas = jnp.abs(jnp.linspace(min_decay, max_decay, d, dtype=f32))
    window = jnp.exp(-dist[:, None] * deltas[None, :])
    return z, window


def _ab_layer(x, ctx, mod_vecs, mod_ctx, norm_g, w_in, conv_w, conv_b, gate_b, lam_vecs,
              g_a, g_b, w_out, lam_init):
    sh_m, sc_m, g_m = mod_vecs
    bsz, s, d = x.shape
    h = _norm(x, norm_g[0], sh_m, sc_m)
    hc = _norm(ctx, norm_g[0], mod_ctx[0], mod_ctx[1])
    w = B_WIDTH
    o = 0
    cols = {}
    for name, width in (("aq", A_QW), ("bq", w), ("bo", w), ("ak", A_QW), ("av", A_VW),
                        ("bk", w), ("bv", w), ("g", 4 * B_HEADS)):
        cols[name] = w_in[:, o:o + width]
        o += width
    cos, sin = _rope_tables(s)
    cat = lambda *ws: jnp.concatenate(ws, axis=1).astype(bf16)
    q = _mm(h, cat(cols["aq"], _rotate_cols(cols["aq"])), rope=(cos, sin, A_DQK ** -0.5))
    k = _mm(h, cat(cols["ak"], _rotate_cols(cols["ak"])), rope=(cos, sin, 1.0))
    qk = _mm(h, cat(cols["bq"], cols["bk"]), conv=(conv_w, conv_b, True))
    vvo = _mm(h, cat(cols["av"], cols["bv"], cols["bo"]))
    wg, bg = _gate_cols(cols["g"], gate_b)
    gates = _mm(h, wg.astype(bf16), out_dtype=f32, bias=bg, tn=2 * LANES)
    ckv = _mm(hc, cat(cols["ak"], cols["av"], cols["bv"]))
    cbk = _mm(hc, cols["bk"].astype(bf16), conv=(conv_w[:, w:], conv_b[w:], True))
    cg = _mm(hc, wg.astype(bf16), out_dtype=f32, bias=bg, tn=2 * LANES)
    out_a = _attn(lam_vecs, q, k, vvo, ckv, g_a, lam_init)
    out_b = _mlstm(qk, vvo, gates, cbk, ckv, cg, g_b)
    wo = w_out.astype(bf16)
    return _out_proj([out_a, out_b], [wo[:A_VW], wo[A_VW:]], x, norm_g[1], g_m)


def _hyena_layer(x, mod_vecs, norm_g, w_in, conv_w, conv_b, fw1, fb1, fw2, fb2, fw3, fbias, w_out):
    sh_m, sc_m, g_m = mod_vecs
    bsz, l, d = x.shape
    h = _norm(x, norm_g[0], sh_m, sc_m)
    u = _mm(h, w_in.astype(bf16), conv=(conv_w, conv_b, False))
    z, window = _hyena_consts(l, d)
    pz, ph = LANES - z.shape[1], LANES - fw1.shape[1]
    filt = _filters(jnp.pad(z, ((0, 0), (0, pz))), jnp.pad(fw1, ((0, pz), (0, ph))),
                    jnp.pad(fb1, (0, ph)), jnp.pad(fw2, ((0, ph), (0, ph))), jnp.pad(fb2, (0, ph)),
                    jnp.pad(fw3, ((0, ph), (0, 0))), window)
    tabs = _dft_tables(l)
    kf = _spectrum(filt, tabs)
    zz = _fftconv(u, 0, u, d, kf, 0, fbias[0], tabs, d)
    y = _fftconv(zz, 0, u, 2 * d, kf, d, fbias[1], tabs, d)
    return _out_proj([y], [w_out.astype(bf16)], x, norm_g[1], g_m)


def kernel(x, c, ctx, c_ctx, w_mod, b_mod, norm_g, w_in_ab, conv_ab_w, conv_ab_b, gate_b_ab, diff_lambda, head_g_a, head_g_b, w_out_ab, w_in_hy, conv_hy_w, conv_hy_b, filt_w1, filt_b1, filt_w2, filt_b2, filt_w3, filt_bias, w_out_hy, router_w, router_b, exp_gu, exp_down, sh_gu, sh_down):
    bsz, s, d = x.shape
    depth = w_mod.shape[0]
    rows = -(-(bsz + 1) // 8) * 8
    cc = jnp.concatenate([c, c_ctx[None, :], jnp.zeros((rows - bsz - 1, d), f32)], axis=0)
    for l in range(depth):
        mod = _mod(cc, w_mod[l], b_mod[l])
        vec = lambda i: mod[:bsz, i * d:(i + 1) * d].reshape(bsz, 1, d)
        sh_m, sc_m, g_m, sh_f, sc_f, g_f = [vec(i) for i in range(6)]
        if l % 2 == 0:
            e = l // 2
            lam_init = 0.8 - 0.6 * math.exp(-0.3 * l)
            mod_ctx = (mod[bsz:bsz + 1, 0:d].reshape(1, 1, d), mod[bsz:bsz + 1, d:2 * d].reshape(1, 1, d))
            x = _ab_layer(x, ctx, (sh_m, sc_m, g_m), mod_ctx, norm_g[l], w_in_ab[e], conv_ab_w[e],
                          conv_ab_b[e], gate_b_ab[e], diff_lambda[e], head_g_a[e], head_g_b[e],
                          w_out_ab[e], lam_init)
        else:
            o = l // 2
            x = _hyena_layer(x, (sh_m, sc_m, g_m), norm_g[l], w_in_hy[o], conv_hy_w[o], conv_hy_b[o],
                             filt_w1[o], filt_b1[o], filt_w2[o], filt_b2[o], filt_w3[o], filt_bias[o],
                             w_out_hy[o])
        h2, gates = _router(x, norm_g[l, 2], sh_f, sc_f, router_w[l].T, router_b[l])
        x = _moe(h2, gates, exp_gu[l].astype(bf16), exp_down[l].astype(bf16), sh_gu[l].astype(bf16),
                 sh_down[l].astype(bf16), x, norm_g[l, 3], g_f)
    return x
```

```python
import functools
import math

import numpy as np
import jax
import jax.numpy as jnp
from jax import lax
from jax.experimental import pallas as pl
from jax.experimental.pallas import tpu as pltpu

f32 = jnp.float32
bf16 = jnp.bfloat16

RMS_EPS = 1e-6
A_HEADS = 4
A_DQK = 64
A_DV = 128
B_HEADS = 4
B_DH = 128
B_WIDTH = B_HEADS * B_DH
A_QW = A_HEADS * 2 * A_DQK
A_VW = A_HEADS * A_DV
GRID_W = 64
ROPE_BASE = 10000.0
ROPE_AXIS_PAIRS = A_DQK // 4
N_EXPERTS = 64
TOP_K = 8
N_GROUPS = 8
TOPK_GROUPS = 4
D_EXPERT = 256
ROUTED_SCALE = 2.5
HY_ORDER = 2
POS_EMB_DIM = 33
FILTER_SIN_W = 1.0
DECAY_FAST_PCT = 0.3
DECAY_SLOW_PCT = 1.5
DECAY_TARGET = 1e-2

LANES = 128
VMEM_LIMIT = 56 * 1024 * 1024
MLSTM_CHUNK = 256
FFT_PAD = 8
MOE_SUB = 256
MOE_WIN = 64
MOE_GROUP = 4


def _cp(*sem):
    return pltpu.CompilerParams(dimension_semantics=sem, vmem_limit_bytes=VMEM_LIMIT)


def _split_bf16(a):
    hi = a.astype(bf16)
    lo = (a - hi.astype(f32)).astype(bf16)
    return hi, lo


def _dot(a, b, dims=(((1,), (0,)), ((), ()))):
    return lax.dot_general(a, b, dims, preferred_element_type=f32)


_NT = (((1,), (1,)), ((), ()))
_TN = (((0,), (0,)), ((), ()))


def _dot3(a, b, dims=(((1,), (0,)), ((), ()))):
    ah, al = _split_bf16(a)
    bh, bl = _split_bf16(b)
    return _dot(ah, bh, dims) + (_dot(ah, bl, dims) + _dot(al, bh, dims))


def _silu(v):
    return v / (1.0 + jnp.exp(-v))


def _sigmoid(v):
    return 1.0 / (1.0 + jnp.exp(-v))


def _log_sigmoid(v):
    return jnp.minimum(v, 0.0) - jnp.log(1.0 + jnp.exp(-jnp.abs(v)))


def _mod_kernel(c_ref, w_ref, b_ref, o_ref):
    o_ref[...] = _dot3(_silu(c_ref[...]), w_ref[...]) + b_ref[...]


def _mod(cc, w, b):
    rows, d = cc.shape
    n = w.shape[1]
    tn = d
    return pl.pallas_call(
        _mod_kernel,
        grid=(n // tn,),
        in_specs=[pl.BlockSpec((rows, d), lambda j: (0, 0)),
                  pl.BlockSpec((d, tn), lambda j: (0, j)),
                  pl.BlockSpec((1, tn), lambda j: (0, j))],
        out_specs=pl.BlockSpec((rows, tn), lambda j: (0, j)),
        out_shape=jax.ShapeDtypeStruct((rows, n), f32),
        compiler_params=_cp("arbitrary"),
        name="mod",
    )(cc, w, b.reshape(1, n))


def _norm_mod(xv, g, shift, scale):
    y = xv * lax.rsqrt(jnp.mean(xv * xv, axis=-1, keepdims=True) + RMS_EPS)
    return (y * g) * (1.0 + scale) + shift


def _norm_kernel(x_ref, g_ref, sh_ref, sc_ref, o_ref):
    o_ref[...] = _norm_mod(x_ref[...], g_ref[...], sh_ref[...], sc_ref[...]).astype(o_ref.dtype)


def _bidx(arr):
    if arr.shape[0] == 1:
        return lambda b, *_: (0, 0, 0)
    return lambda b, *_: (b, 0, 0)


def _norm(x, g, shift, scale, tl=512):
    bsz, l, d = x.shape
    tl = min(tl, l)
    return pl.pallas_call(
        _norm_kernel,
        grid=(bsz, l // tl),
        in_specs=[pl.BlockSpec((None, tl, d), lambda b, i: (b, i, 0)),
                  pl.BlockSpec((1, d), lambda b, i: (0, 0)),
                  pl.BlockSpec((None, 1, d), _bidx(shift)),
                  pl.BlockSpec((None, 1, d), _bidx(scale))],
        out_specs=pl.BlockSpec((None, tl, d), lambda b, i: (b, i, 0)),
        out_shape=jax.ShapeDtypeStruct((bsz, l, d), bf16),
        compiler_params=_cp("parallel", "parallel"),
        name="norm",
    )(x, g.reshape(1, d), shift, scale)


def _mm_plain_kernel(h_ref, w_ref, b_ref, o_ref):
    o_ref[...] = (_dot(h_ref[...], w_ref[...]) + b_ref[...]).astype(o_ref.dtype)


def _mm_rope_kernel(h_ref, w_ref, cos_ref, sin_ref, o_ref, *, scale):
    p = _dot(h_ref[...], w_ref[...])
    n = o_ref.shape[-1]
    o_ref[...] = ((p[:, :n] * cos_ref[...] + p[:, n:] * sin_ref[...]) * scale).astype(o_ref.dtype)


def _mm_conv_kernel(h_ref, w_ref, cw_ref, cb_ref, o_ref, *, act):
    p = _dot(h_ref[...], w_ref[...])
    l = p.shape[0]
    row = lax.broadcasted_iota(jnp.int32, p.shape, 0)
    prev = jnp.where(row == 0, 0.0, pltpu.roll(p, 1, 0))
    nxt = jnp.where(row == l - 1, 0.0, pltpu.roll(p, l - 1, 0))
    y = prev * cw_ref[0:1, :] + p * cw_ref[1:2, :] + nxt * cw_ref[2:3, :] + cb_ref[...]
    if act:
        y = _silu(y)
    o_ref[...] = y.astype(o_ref.dtype)


def _mm(h, w, *, out_dtype=bf16, bias=None, rope=None, conv=None, tl=512, tn=512):
    bsz, l, k = h.shape
    n = w.shape[1]
    if rope is not None:
        cos, sin, scale = rope
        n_out = n // 2
        tl = min(tl, l)
        return pl.pallas_call(
            functools.partial(_mm_rope_kernel, scale=scale),
            grid=(bsz, l // tl),
            in_specs=[pl.BlockSpec((None, tl, k), lambda b, i: (b, i, 0)),
                      pl.BlockSpec((k, n), lambda b, i: (0, 0)),
                      pl.BlockSpec((tl, n_out), lambda b, i: (i, 0)),
                      pl.BlockSpec((tl, n_out), lambda b, i: (i, 0))],
            out_specs=pl.BlockSpec((None, tl, n_out), lambda b, i: (b, i, 0)),
            out_shape=jax.ShapeDtypeStruct((bsz, l, n_out), out_dtype),
            compiler_params=_cp("parallel", "parallel"),
            name="mm_rope",
        )(h, w, cos, sin)
    tn = min(tn, n)
    if conv is not None:
        cw, cb, act = conv
        return pl.pallas_call(
            functools.partial(_mm_conv_kernel, act=act),
            grid=(bsz, n // tn),
            in_specs=[pl.BlockSpec((None, l, k), lambda b, j: (b, 0, 0)),
                      pl.BlockSpec((k, tn), lambda b, j: (0, j)),
                      pl.BlockSpec((3, tn), lambda b, j: (0, j)),
                      pl.BlockSpec((1, tn), lambda b, j: (0, j))],
            out_specs=pl.BlockSpec((None, l, tn), lambda b, j: (b, 0, j)),
            out_shape=jax.ShapeDtypeStruct((bsz, l, n), out_dtype),
            compiler_params=_cp("parallel", "arbitrary"),
            name="mm_conv",
        )(h, w, cw, cb.reshape(1, n))
    if bias is None:
        bias = jnp.zeros((n,), f32)
    tl = min(tl, l)
    return pl.pallas_call(
        _mm_plain_kernel,
        grid=(bsz, l // tl, n // tn),
        in_specs=[pl.BlockSpec((None, tl, k), lambda b, i, j: (b, i, 0)),
                  pl.BlockSpec((k, tn), lambda b, i, j: (0, j)),
                  pl.BlockSpec((1, tn), lambda b, i, j: (0, j))],
        out_specs=pl.BlockSpec((None, tl, tn), lambda b, i, j: (b, i, j)),
        out_shape=jax.ShapeDtypeStruct((bsz, l, n), out_dtype),
        compiler_params=_cp("parallel", "parallel", "arbitrary"),
        name="mm_plain",
    )(h, w, bias.reshape(1, n))


def _attn_kernel(lv_ref, q_ref, kc_ref, k_ref, vc_ref, v_ref, g_ref, o_ref, *, lam_init):
    tq = q_ref.shape[0]
    lv = lv_ref[...]
    lam = (jnp.exp(jnp.sum(lv[0:1] * lv[1:2], axis=1, keepdims=True))
           - jnp.exp(jnp.sum(lv[2:3] * lv[3:4], axis=1, keepdims=True)) + lam_init)
    first = lax.broadcasted_iota(jnp.int32, (tq, A_DV), 1) < A_DQK
    for hd in range(A_HEADS):
        cs = slice(hd * A_DV, (hd + 1) * A_DV)
        qh = q_ref[:, cs]
        zero = jnp.zeros_like(qh)
        q2 = jnp.concatenate([jnp.where(first, qh, zero), jnp.where(first, zero, qh)], axis=0)
        s_c = _dot(q2, kc_ref[:, cs], _NT)
        s_l = _dot(q2, k_ref[:, cs], _NT)
        m = jnp.maximum(jnp.max(s_c, axis=1, keepdims=True), jnp.max(s_l, axis=1, keepdims=True))
        p_c = jnp.exp(s_c - m)
        p_l = jnp.exp(s_l - m)
        inv = 1.0 / (jnp.sum(p_c, axis=1, keepdims=True) + jnp.sum(p_l, axis=1, keepdims=True))
        w0 = inv[:tq]
        w1 = inv[tq:] * lam
        a_c = (p_c[:tq] * w0 - p_c[tq:] * w1).astype(bf16)
        a_l = (p_l[:tq] * w0 - p_l[tq:] * w1).astype(bf16)
        o = _dot(a_c, vc_ref[:, cs]) + _dot(a_l, v_ref[:, cs])
        o = o * lax.rsqrt(jnp.mean(o * o, axis=1, keepdims=True) + RMS_EPS)
        o_ref[:, cs] = (o * g_ref[:, cs] * (1.0 - lam_init)).astype(o_ref.dtype)


def _attn(lv, q, k, vvo, ckv, g_a, lam_init, tq=256):
    bsz, s, _ = q.shape
    lc = ckv.shape[1]
    tq = min(tq, s)
    w = A_QW
    return pl.pallas_call(
        functools.partial(_attn_kernel, lam_init=lam_init),
        grid=(bsz, s // tq),
        in_specs=[pl.BlockSpec(lv.shape, lambda b, i: (0, 0)),
                  pl.BlockSpec((None, tq, w), lambda b, i: (b, i, 0)),
                  pl.BlockSpec((None, lc, w), lambda b, i: (b, 0, 0)),
                  pl.BlockSpec((None, s, w), lambda b, i: (b, 0, 0)),
                  pl.BlockSpec((None, lc, w), lambda b, i: (b, 0, 1)),
                  pl.BlockSpec((None, s, w), lambda b, i: (b, 0, 0)),
                  pl.BlockSpec((1, w), lambda b, i: (0, 0))],
        out_specs=pl.BlockSpec((None, tq, w), lambda b, i: (b, i, 0)),
        out_shape=jax.ShapeDtypeStruct((bsz, s, w), bf16),
        compiler_params=_cp("parallel", "arbitrary"),
        name="diff_attn",
    )(lv, q, ckv, k, ckv, vvo, g_a.reshape(1, w))


_LN_QSCALE = math.log(B_DH ** -0.5)
_NCHAIN = 2 * B_HEADS


def _chunk_gate_sums(gi, gf, tri):
    lf = _log_sigmoid(gf)
    hi, lo = _split_bf16(lf)
    cum = _dot(tri, hi) + _dot(tri, lo)
    t = gf.shape[0]
    tot = cum[t - 1:t, :]
    rcum = tot - cum + lf
    fwd = lax.broadcasted_iota(jnp.int32, gf.shape, 1) < B_HEADS
    bd = jnp.where(fwd, cum, rcum)
    return bd, tot, (bd - gi).T


def _lower_tri(t):
    r = lax.broadcasted_iota(jnp.int32, (t, t), 0)
    c = lax.broadcasted_iota(jnp.int32, (t, t), 1)
    return r, c


def _absorb(c_ref, n_ref, m_ref, ch, bcol, tot_c, gi_c, kf, vb):
    m_prev = m_ref[ch][:, 0:1]
    g = tot_c - bcol + gi_c
    m_new = jnp.maximum(tot_c + m_prev, jnp.max(g, axis=0, keepdims=True))
    wgt = jnp.exp(g - m_new)
    decay = jnp.exp(tot_c + m_prev - m_new)
    kw = kf * wgt
    c_ref[ch] = decay * c_ref[ch] + _dot(kw.astype(bf16), vb, _TN)
    n_ref[ch] = decay * n_ref[ch] + jnp.sum(kw, axis=0, keepdims=True)
    m_ref[ch] = jnp.broadcast_to(m_new, m_ref.shape[1:])


def _mlstm_kernel(qk_ref, vvo_ref, g_ref, ck_ref, ckv_ref, cg_ref, gb_ref, o_ref,
                  hf_ref, hb_ref, c_ref, n_ref, m_ref, *, tc):
    s = o_ref.shape[0]
    lc = ck_ref.shape[0]
    nc = s // tc
    w = B_WIDTH
    dh = B_DH

    c_ref[...] = jnp.zeros_like(c_ref)
    n_ref[...] = jnp.zeros_like(n_ref)
    m_ref[...] = jnp.zeros_like(m_ref)

    r, cidx = _lower_tri(lc)
    tri_c = jnp.where(cidx <= r, 1.0, 0.0).astype(bf16)
    cg = cg_ref[...]
    bd, tot, _ = _chunk_gate_sums(cg[:, :LANES], cg[:, LANES:], tri_c)
    gi = cg[:, :LANES]
    for ch in range(_NCHAIN):
        hs = slice((ch % B_HEADS) * dh, (ch % B_HEADS + 1) * dh)
        vs = slice(2 * w + (ch % B_HEADS) * dh, 2 * w + (ch % B_HEADS + 1) * dh)
        _absorb(c_ref, n_ref, m_ref, ch, bd[:, ch:ch + 1], tot[:, ch:ch + 1], gi[:, ch:ch + 1],
                ck_ref[:, hs].astype(f32), ckv_ref[:, vs])

    r, cidx = _lower_tri(tc)
    tri = jnp.where(cidx <= r, 1.0, 0.0).astype(bf16)
    causal = cidx <= r
    anti = cidx >= r

    def step(i, carry):
        for d in range(2):
            row0 = pl.multiple_of((i if d == 0 else nc - 1 - i) * tc, tc)
            rows = pl.ds(row0, tc)
            gch = g_ref[rows, :]
            gi = gch[:, :LANES]
            bd, tot, xt = _chunk_gate_sums(gi, gch[:, LANES:], tri)
            mask = causal if d == 0 else anti
            dst = hf_ref if d == 0 else hb_ref
            for hd in range(B_HEADS):
                ch = d * B_HEADS + hd
                hs = slice(hd * dh, (hd + 1) * dh)
                qb = qk_ref[rows, hs]
                kb = qk_ref[rows, slice(w + hd * dh, w + (hd + 1) * dh)]
                vb = vvo_ref[rows, slice(w + hd * dh, w + (hd + 1) * dh)]
                bcol = bd[:, ch:ch + 1]
                dmat = jnp.where(mask, bcol - xt[ch:ch + 1, :], -jnp.inf)
                m_prev = m_ref[ch][:, 0:1]
                inter = bcol + m_prev
                m_t = jnp.maximum(inter, jnp.max(dmat, axis=1, keepdims=True))
                e = jnp.exp(dmat - m_t + _LN_QSCALE)
                smat = _dot(qb, kb, _NT) * e
                sc = jnp.exp(inter - m_t + _LN_QSCALE)
                num = sc * _dot(qb, c_ref[ch].astype(bf16)) + _dot(smat.astype(bf16), vb)
                qn = jnp.sum(qb.astype(f32) * n_ref[ch], axis=1, keepdims=True)
                den = sc * qn + jnp.sum(smat, axis=1, keepdims=True)
                hout = num * (1.0 / jnp.maximum(jnp.abs(den), jnp.exp(-m_t)))
                dst[rows, hs] = hout
                _absorb(c_ref, n_ref, m_ref, ch, bcol, tot[:, ch:ch + 1], gi[:, ch:ch + 1],
                        kb.astype(f32), vb)
        return carry

    lax.fori_loop(0, nc, step, 0)

    for hd in range(B_HEADS):
        hs = slice(hd * dh, (hd + 1) * dh)
        hsum = hf_ref[:, hs] + hb_ref[:, hs]
        hn = hsum * lax.rsqrt(jnp.mean(hsum * hsum, axis=1, keepdims=True) + RMS_EPS)
        og = _sigmoid(vvo_ref[:, slice(2 * w + hd * dh, 2 * w + (hd + 1) * dh)].astype(f32))
        o_ref[:, hs] = (hn * gb_ref[:, hs] * og).astype(o_ref.dtype)


def _mlstm(qk, vvo, gates, cbk, ckv, cg, g_b):
    bsz, s, _ = qk.shape
    lc = cbk.shape[1]
    w = B_WIDTH
    tc = min(MLSTM_CHUNK, s)
    return pl.pallas_call(
        functools.partial(_mlstm_kernel, tc=tc),
        grid=(bsz,),
        in_specs=[pl.BlockSpec((None, s, 2 * w), lambda b: (b, 0, 0)),
                  pl.BlockSpec((None, s, 3 * w), lambda b: (b, 0, 0)),
                  pl.BlockSpec((None, s, 2 * LANES), lambda b: (b, 0, 0)),
                  pl.BlockSpec((None, lc, w), lambda b: (b, 0, 0)),
                  pl.BlockSpec((None, lc, 3 * w), lambda b: (b, 0, 0)),
                  pl.BlockSpec((None, lc, 2 * LANES), lambda b: (b, 0, 0)),
                  pl.BlockSpec((1, w), lambda b: (0, 0))],
        out_specs=pl.BlockSpec((None, s, w), lambda b: (b, 0, 0)),
        out_shape=jax.ShapeDtypeStruct((bsz, s, w), bf16),
        scratch_shapes=[pltpu.VMEM((s, w), f32), pltpu.VMEM((s, w), f32),
                        pltpu.VMEM((_NCHAIN, B_DH, B_DH), f32),
                        pltpu.VMEM((_NCHAIN, 1, B_DH), f32),
                        pltpu.VMEM((_NCHAIN, 1, LANES), f32)],
        compiler_params=_cp("arbitrary"),
        name="mlstm",
    )(qk, vvo, gates, cbk, ckv, cg, g_b.reshape(1, w))


def _out_kernel(*refs, n_act):
    acts = refs[:n_act]
    ws = refs[n_act:2 * n_act]
    x_ref, g_ref, gate_ref, o_ref = refs[2 * n_act:]
    mix = _dot(acts[0][...], ws[0][...])
    for a, wr in zip(acts[1:], ws[1:]):
        mix = mix + _dot(a[...], wr[...])
    y = mix * lax.rsqrt(jnp.mean(mix * mix, axis=-1, keepdims=True) + RMS_EPS) * g_ref[...]
    o_ref[...] = x_ref[...] + gate_ref[...] * y


def _out_proj(acts, ws, x, g, gate, tl=512):
    bsz, l, d = x.shape
    tl = min(tl, l)
    n_act = len(acts)
    in_specs = [pl.BlockSpec((None, tl, a.shape[2]), lambda b, i: (b, i, 0)) for a in acts]
    in_specs += [pl.BlockSpec(wm.shape, lambda b, i: (0, 0)) for wm in ws]
    in_specs += [pl.BlockSpec((None, tl, d), lambda b, i: (b, i, 0)),
                 pl.BlockSpec((1, d), lambda b, i: (0, 0)),
                 pl.BlockSpec((None, 1, d), _bidx(gate))]
    return pl.pallas_call(
        functools.partial(_out_kernel, n_act=n_act),
        grid=(bsz, l // tl),
        in_specs=in_specs,
        out_specs=pl.BlockSpec((None, tl, d), lambda b, i: (b, i, 0)),
        out_shape=jax.ShapeDtypeStruct((bsz, l, d), f32),
        compiler_params=_cp("parallel", "parallel"),
        name="out_proj",
    )(*acts, *ws, x, g.reshape(1, d), gate)


def _router_kernel(x_ref, g_ref, sh_ref, sc_ref, rw_ref, rb_ref, h_ref, rk_ref, cm_ref):
    hf = _norm_mod(x_ref[...], g_ref[...], sh_ref[...], sc_ref[...])
    tl, d = hf.shape
    h_ref[:, :d] = hf.astype(h_ref.dtype)
    per = N_EXPERTS // N_GROUPS
    logits = _dot3(rw_ref[...], hf, _NT)
    s3 = _sigmoid(logits).reshape(N_GROUPS, per, tl)
    b3 = s3 + rb_ref[...].reshape(N_GROUPS, per, 1)
    neg = -jnp.inf
    jdx = lax.broadcasted_iota(jnp.int32, b3.shape, 1)
    gdx = lax.broadcasted_iota(jnp.int32, b3.shape, 0)
    m1 = jnp.max(b3, axis=1, keepdims=True)
    f1 = jnp.min(jnp.where(b3 == m1, jdx, per), axis=1, keepdims=True)
    m2 = jnp.max(jnp.where(jdx == f1, neg, b3), axis=1, keepdims=True)
    grp = m1 + m2
    g1 = lax.broadcasted_iota(jnp.int32, grp.shape, 0)
    cnt = jnp.zeros(grp.shape, jnp.int32)
    for gp in range(N_GROUPS):
        rv = grp[gp:gp + 1]
        ahead = jnp.where(rv > grp, 1, jnp.where(rv == grp, jnp.where(g1 > gp, 1, 0), 0))
        cnt = cnt + ahead
    v = jnp.where(cnt < TOPK_GROUPS, b3, neg)
    eidx = gdx * per + jdx
    sel = jnp.zeros(b3.shape, f32)
    for _ in range(TOP_K):
        m = jnp.max(jnp.max(v, axis=1, keepdims=True), axis=0, keepdims=True)
        cand = jnp.where(v == m, eidx, N_EXPERTS)
        fi = jnp.min(jnp.min(cand, axis=1, keepdims=True), axis=0, keepdims=True)
        hit = eidx == fi
        sel = jnp.where(hit, 1.0, sel)
        v = jnp.where(hit, neg, v)
    ssel = sel * s3
    den = jnp.sum(jnp.sum(ssel, axis=1, keepdims=True), axis=0, keepdims=True)
    gates = ((ROUTED_SCALE * ssel) / den).reshape(N_EXPERTS, tl)
    gates = jnp.concatenate([gates, jnp.zeros((LANES - N_EXPERTS, tl), f32)], axis=0)
    g_hi, g_lo = _split_bf16(gates.T)
    h_ref[:, d:d + LANES] = g_hi
    h_ref[:, d + LANES:] = g_lo
    sel2 = sel.reshape(N_EXPERTS, tl)
    r = lax.broadcasted_iota(jnp.int32, (MOE_SUB, MOE_SUB), 0)
    c = lax.broadcasted_iota(jnp.int32, (MOE_SUB, MOE_SUB), 1)
    before = jnp.where(r < c, 1.0, 0.0).astype(bf16)
    cmax = jnp.zeros((1, 1), f32)
    for j in range(tl // MOE_SUB):
        sub = sel2[:, j * MOE_SUB:(j + 1) * MOE_SUB]
        rank = _dot(sub.astype(bf16), before)
        rk_ref[:, j * MOE_SUB:(j + 1) * MOE_SUB] = jnp.where(sub > 0.0, rank, -1.0)
        cnt = jnp.max(jnp.sum(sub, axis=1, keepdims=True), axis=0, keepdims=True)
        cmax = jnp.maximum(cmax, cnt)
    cm_ref[...] = jnp.broadcast_to(cmax, cm_ref.shape)


def _router(x, g, shift, scale, rw_t, rb, tl=512):
    bsz, l, d = x.shape
    tl = min(tl, l)
    nl = l // tl
    return pl.pallas_call(
        _router_kernel,
        grid=(bsz, nl),
        in_specs=[pl.BlockSpec((None, tl, d), lambda b, i: (b, i, 0)),
                  pl.BlockSpec((1, d), lambda b, i: (0, 0)),
                  pl.BlockSpec((None, 1, d), _bidx(shift)),
                  pl.BlockSpec((None, 1, d), _bidx(scale)),
                  pl.BlockSpec((N_EXPERTS, d), lambda b, i: (0, 0)),
                  pl.BlockSpec((N_EXPERTS, 1), lambda b, i: (0, 0))],
        out_specs=[pl.BlockSpec((None, tl, d + 2 * LANES), lambda b, i: (b, i, 0)),
                   pl.BlockSpec((N_EXPERTS, tl), lambda b, i: (0, b * nl + i)),
                   pl.BlockSpec((None, 8, LANES), lambda b, i: (b * nl + i, 0, 0))],
        out_shape=[jax.ShapeDtypeStruct((bsz, l, d + 2 * LANES), bf16),
                   jax.ShapeDtypeStruct((N_EXPERTS, bsz * l), f32),
                   jax.ShapeDtypeStruct((bsz * nl, 8, LANES), f32)],
        compiler_params=_cp("parallel", "parallel"),
        name="router",
    )(x, g.reshape(1, d), shift, scale, rw_t, rb.reshape(N_EXPERTS, 1))


def _swiglu_act(hh):
    half = hh.shape[1] // 2
    return _silu(hh[:, :half]) * hh[:, half:]


def _moe_kernel(np_ref, h_ref, rk_ref, gu_ref, dn_ref, sgu_ref, sdn_ref, x_ref, g_ref, gate_ref,
                o_ref, acc_ref, xg_ref, ys_ref, p_ref, gr_ref):
    tile = pl.program_id(0)
    grp = pl.program_id(1)
    tm, d = acc_ref.shape
    ns = tm // MOE_SUB
    win = MOE_WIN
    ng = MOE_GROUP

    @pl.when(grp == 0)
    def _():
        act = _swiglu_act(_dot(h_ref[:, :d], sgu_ref[...]))
        acc_ref[...] = _dot(act.astype(bf16), sdn_ref[...])

    riota = lax.broadcasted_iota(jnp.int32, (win, MOE_SUB), 0).astype(f32)
    lane = lax.broadcasted_iota(jnp.int32, (win, LANES), 1)

    def one_pass(p, carry):
        base = (p * win).astype(f32)
        for s in range(ns):
            cols = slice(s * MOE_SUB, (s + 1) * MOE_SUB)
            onehots = []
            for el in range(ng):
                rke = rk_ref[pl.ds(grp * ng + el, 1), cols] - base
                onehots.append(jnp.where(rke == riota, 1.0, 0.0).astype(bf16))
            pm = jnp.concatenate(onehots, axis=0)
            p_ref[s] = pm
            gx = _dot(pm, h_ref[cols, :])
            for el in range(ng):
                blk = gx[el * win:(el + 1) * win]
                xg_ref[el, s * win:(s + 1) * win, :] = blk[:, :d].astype(bf16)
                gw = blk[:, d:d + LANES] + blk[:, d + LANES:]
                gsel = jnp.sum(jnp.where(lane == grp * ng + el, gw, 0.0), axis=1, keepdims=True)
                gr_ref[el, s * win:(s + 1) * win, :] = jnp.broadcast_to(gsel, (win, LANES))
        for el in range(ng):
            hh = _dot(xg_ref[el], gu_ref[el])
            gr = gr_ref[el]
            act = _swiglu_act(hh) * jnp.concatenate([gr] * (hh.shape[1] // (2 * LANES)), axis=1)
            y = _dot(act.astype(bf16), dn_ref[el]).astype(bf16)
            for s in range(ns):
                ys_ref[s, el * win:(el + 1) * win, :] = y[s * win:(s + 1) * win]
        for s in range(ns):
            acc_ref[s * MOE_SUB:(s + 1) * MOE_SUB, :] += _dot(p_ref[s], ys_ref[s], _TN)
        return carry

    lax.fori_loop(0, np_ref[tile], one_pass, 0)

    @pl.when(grp == pl.num_programs(1) - 1)
    def _():
        mo = acc_ref[...]
        y = mo * lax.rsqrt(jnp.mean(mo * mo, axis=-1, keepdims=True) + RMS_EPS) * g_ref[...]
        o_ref[...] = x_ref[...] + gate_ref[...] * y


def _moe(hx, rk, cmax, gu, dn, sgu, sdn, x, g, gate, tm=1024):
    bsz, l, d = x.shape
    tm = min(tm, l)
    per_b = l // tm
    nt = bsz * per_b
    ne = gu.shape[0]
    ng = MOE_GROUP
    ns = tm // MOE_SUB
    most = jnp.max(cmax[:, 0, 0].reshape(nt, -1), axis=1)
    npass = jnp.ceil(most / MOE_WIN).astype(jnp.int32)
    grid_spec = pltpu.PrefetchScalarGridSpec(
        num_scalar_prefetch=1,
        grid=(nt, ne // ng),
        in_specs=[pl.BlockSpec((tm, hx.shape[2]), lambda t, e, n: (t, 0)),
                  pl.BlockSpec((ne, tm), lambda t, e, n: (0, t)),
                  pl.BlockSpec((ng,) + gu.shape[1:], lambda t, e, n: (e, 0, 0)),
                  pl.BlockSpec((ng,) + dn.shape[1:], lambda t, e, n: (e, 0, 0)),
                  pl.BlockSpec(sgu.shape, lambda t, e, n: (0, 0)),
                  pl.BlockSpec(sdn.shape, lambda t, e, n: (0, 0)),
                  pl.BlockSpec((tm, d), lambda t, e, n: (t, 0)),
                  pl.BlockSpec((1, d), lambda t, e, n: (0, 0)),
                  pl.BlockSpec((None, 1, d), lambda t, e, n: (t // per_b, 0, 0))],
        out_specs=pl.BlockSpec((tm, d), lambda t, e, n: (t, 0)),
        scratch_shapes=[pltpu.VMEM((tm, d), f32),
                        pltpu.VMEM((ng, ns * MOE_WIN, d), bf16),
                        pltpu.VMEM((ns, ng * MOE_WIN, d), bf16),
                        pltpu.VMEM((ns, ng * MOE_WIN, MOE_SUB), bf16),
                        pltpu.VMEM((ng, ns * MOE_WIN, LANES), f32)])
    out = pl.pallas_call(
        _moe_kernel,
        grid_spec=grid_spec,
        out_shape=jax.ShapeDtypeStruct((bsz * l, d), f32),
        compiler_params=_cp("parallel", "arbitrary"),
        name="moe",
    )(npass, hx.reshape(bsz * l, hx.shape[2]), rk, gu, dn, sgu, sdn, x.reshape(bsz * l, d),
      g.reshape(1, d), gate)
    return out.reshape(bsz, l, d)


def _filter_kernel(z_ref, w1_ref, b1_ref, w2_ref, b2_ref, w3_ref, win_ref, o_ref):
    hid = jnp.sin(FILTER_SIN_W * (_dot3(z_ref[...], w1_ref[...]) + b1_ref[...]))
    hid = jnp.sin(FILTER_SIN_W * (_dot3(hid, w2_ref[...]) + b2_ref[...]))
    o_ref[...] = _dot3(hid, w3_ref[...]) * win_ref[...]


def _filters(z, w1, b1, w2, b2, w3, window, tn=512):
    l, p = z.shape
    hdim = w1.shape[1]
    n = w3.shape[1]
    d = window.shape[1]
    nd = d // tn
    return pl.pallas_call(
        _filter_kernel,
        grid=(n // tn,),
        in_specs=[pl.BlockSpec((l, p), lambda j: (0, 0)),
                  pl.BlockSpec((p, hdim), lambda j: (0, 0)),
                  pl.BlockSpec((1, hdim), lambda j: (0, 0)),
                  pl.BlockSpec((hdim, hdim), lambda j: (0, 0)),
                  pl.BlockSpec((1, hdim), lambda j: (0, 0)),
                  pl.BlockSpec((hdim, tn), lambda j: (0, j)),
                  pl.BlockSpec((l, tn), lambda j: (0, j % nd))],
        out_specs=pl.BlockSpec((l, tn), lambda j: (0, j)),
        out_shape=jax.ShapeDtypeStruct((l, n), f32),
        compiler_params=_cp("arbitrary"),
        name="hyena_filter",
    )(z, w1, b1.reshape(1, hdim), w2, b2.reshape(1, hdim), w3, window)


def _dft_tables(l):
    n = 2 * l
    n1 = math.isqrt(n)
    assert n == n1 * n1 and n1 % 16 == 0
    na = l // n1
    a = np.arange(na)
    b = np.arange(n1)
    c = np.arange(n1)
    th = 2.0 * np.pi * ((n1 * a[None, None, :] + b[:, None, None]) * c[None, :, None]) / n
    t1 = np.concatenate([np.cos(th), -np.sin(th)], axis=1)
    ph = 2.0 * np.pi * (b[:, None] * b[None, :]) / n1
    cs, sn = np.cos(ph), np.sin(ph)
    a3 = np.block([[cs, sn], [-sn, cs]])
    a3i = np.block([[cs, -sn], [sn, cs]])
    a2 = np.arange(na) + na // 2
    th2 = 2.0 * np.pi * ((n1 * a2[None, :, None] + b[:, None, None]) * c[None, None, :]) / n
    t2 = np.concatenate([np.cos(th2), -np.sin(th2)], axis=2)
    return [jnp.asarray(t, f32).astype(bf16) for t in (t1, a3, a3i, t2)]


def _fft_dims(t1):
    n1, _, na = t1.shape
    return n1, na, 2 * n1 + FFT_PAD, n1 + FFT_PAD


def _dft_forward(uf_ref, t1_ref, zs_ref):
    n1, na, sb, su = _fft_dims(t1_ref)
    for b in range(n1):
        ub = uf_ref[pl.ds(b, na, stride=su), :].astype(bf16)
        zb = _dot(t1_ref[b], ub)
        zs_ref[pl.ds(b, n1, stride=sb), :] = zb[:n1]
        zs_ref[pl.ds(n1 + b, n1, stride=sb), :] = zb[n1:]


def _spectrum_kernel(f_ref, t1_ref, a3_ref, o_ref, uf_ref, zs_ref, *, scale):
    n1, na, sb, su = _fft_dims(t1_ref)
    for a in range(na):
        uf_ref[pl.ds(a * su, n1), :] = f_ref[pl.ds(a * n1, n1), :]
    _dft_forward(uf_ref, t1_ref, zs_ref)
    a3 = a3_ref[...]
    for c in range(n1):
        zc = zs_ref[pl.ds(c * sb, 2 * n1), :].astype(bf16)
        o_ref[c] = _dot(a3, zc) * scale


def _spectrum(filt, tabs, dt=128):
    l, n = filt.shape
    t1, a3, _, _ = tabs
    n1, na, sb, su = _fft_dims(t1)
    return pl.pallas_call(
        functools.partial(_spectrum_kernel, scale=1.0 / (2 * l)),
        grid=(n // dt,),
        in_specs=[pl.BlockSpec((l, dt), lambda j: (0, j)),
                  pl.BlockSpec(t1.shape, lambda j: (0, 0, 0)),
                  pl.BlockSpec(a3.shape, lambda j: (0, 0))],
        out_specs=pl.BlockSpec((n1, 2 * n1, dt), lambda j: (0, 0, j)),
        out_shape=jax.ShapeDtypeStruct((n1, 2 * n1, n), f32),
        scratch_shapes=[pltpu.VMEM((na * su, dt), f32),
                        pltpu.VMEM((n1 * sb, dt), f32)],
        compiler_params=_cp("arbitrary"),
        name="hyena_spectrum",
    )(filt, t1, a3)


def _fftconv_kernel(u_ref, xg_ref, kf_ref, fb_ref, t1_ref, a3_ref, a3i_ref, t2_ref, o_ref,
                    uf_ref, zs_ref, qs_ref, y_ref):
    n1, na, sb, su = _fft_dims(t1_ref)
    for a in range(na):
        uf_ref[pl.ds(a * su, n1), :] = u_ref[pl.ds(a * n1, n1), :].astype(f32)
    _dft_forward(uf_ref, t1_ref, zs_ref)
    a3 = a3_ref[...]
    a3i = a3i_ref[...]
    for c in range(n1):
        zc = zs_ref[pl.ds(c * sb, 2 * n1), :].astype(bf16)
        xc = _dot(a3, zc)
        kc = kf_ref[c]
        xr, xi = xc[:n1], xc[n1:]
        kr, ki = kc[:n1], kc[n1:]
        pc = jnp.concatenate([xr * kr - xi * ki, xr * ki + xi * kr], axis=0).astype(bf16)
        qc = _dot(a3i, pc)
        qs_ref[pl.ds(c, n1, stride=sb), :] = qc[:n1]
        qs_ref[pl.ds(n1 + c, n1, stride=sb), :] = qc[n1:]
    for b in range(n1):
        qb = qs_ref[pl.ds(b * sb, 2 * n1), :].astype(bf16)
        y_ref[pl.ds(b, na, stride=su), :] = _dot(t2_ref[b], qb)
    fb = fb_ref[...]
    for a in range(na):
        rows = pl.ds(a * n1, n1)
        uv = uf_ref[pl.ds(a * su, n1), :]
        yv = y_ref[pl.ds(a * su, n1), :]
        o_ref[rows, :] = (xg_ref[rows, :].astype(f32) * (yv + uv * fb)).astype(o_ref.dtype)


def _fftconv(u, u_col, xg, xg_col, kf, kf_col, fbias, tabs, d, dt=128):
    bsz, l, _ = u.shape
    t1, a3, a3i, t2 = tabs
    n1, na, sb, su = _fft_dims(t1)
    nd = d // dt
    uo, go, ko = u_col // dt, xg_col // dt, kf_col // dt
    return pl.pallas_call(
        _fftconv_kernel,
        grid=(nd, bsz),
        in_specs=[pl.BlockSpec((None, l, dt), lambda j, b: (b, 0, j + uo)),
                  pl.BlockSpec((None, l, dt), lambda j, b: (b, 0, j + go)),
                  pl.BlockSpec((n1, 2 * n1, dt), lambda j, b: (0, 0, j + ko)),
                  pl.BlockSpec((1, dt), lambda j, b: (0, j)),
                  pl.BlockSpec(t1.shape, lambda j, b: (0, 0, 0)),
                  pl.BlockSpec(a3.shape, lambda j, b: (0, 0)),
                  pl.BlockSpec(a3i.shape, lambda j, b: (0, 0)),
                  pl.BlockSpec(t2.shape, lambda j, b: (0, 0, 0))],
        out_specs=pl.BlockSpec((None, l, dt), lambda j, b: (b, 0, j)),
        out_shape=jax.ShapeDtypeStruct((bsz, l, d), bf16),
        scratch_shapes=[pltpu.VMEM((na * su, dt), f32),
                        pltpu.VMEM((n1 * sb, dt), f32),
                        pltpu.VMEM((n1 * sb, dt), f32),
                        pltpu.VMEM((na * su, dt), f32)],
        compiler_params=_cp("parallel", "arbitrary"),
        name="hyena_fftconv",
    )(u, xg, kf, fbias.reshape(1, d), t1, a3, a3i, t2)


def _rope_tables(l):
    rows = l // GRID_W
    row = jnp.repeat(jnp.arange(rows), GRID_W)
    col = jnp.tile(jnp.arange(GRID_W), rows)
    inv = ROPE_BASE ** (-jnp.arange(ROPE_AXIS_PAIRS, dtype=f32) / ROPE_AXIS_PAIRS)
    ang = jnp.stack([row, col], axis=-1).astype(f32)[..., None] * inv
    ang = jnp.broadcast_to(ang[:, :, None, :], (l, 2, 2, ROPE_AXIS_PAIRS)).reshape(l, A_DQK)
    reps = A_QW // A_DQK
    return jnp.tile(jnp.cos(ang), (1, reps)), jnp.tile(jnp.sin(ang), (1, reps))


def _rotate_cols(w):
    j = np.arange(w.shape[1])
    lo = (j % (2 * ROPE_AXIS_PAIRS)) < ROPE_AXIS_PAIRS
    perm = np.where(lo, j + ROPE_AXIS_PAIRS, j - ROPE_AXIS_PAIRS)
    sign = np.where(lo, -1.0, 1.0).astype(np.float32)
    return w[:, perm] * sign


def _gate_cols(w_g, b_g):
    idx_i = np.array([d * 2 * B_HEADS + hd for d in range(2) for hd in range(B_HEADS)])
    idx_f = idx_i + B_HEADS
    pad = LANES - _NCHAIN
    k = w_g.shape[0]
    w = jnp.concatenate([w_g[:, idx_i], jnp.zeros((k, pad), f32),
                         w_g[:, idx_f], jnp.zeros((k, pad), f32)], axis=1)
    b = jnp.concatenate([b_g[idx_i], jnp.zeros((pad,), f32), b_g[idx_f], jnp.zeros((pad,), f32)])
    return w, b


def _hyena_consts(l, d):
    j = jnp.arange(l, dtype=f32)
    bands = (POS_EMB_DIM - 1) // 2
    freqs = jnp.linspace(1e-4, bands - 1, bands, dtype=f32)
    ang = (2.0 * math.pi / l) * j[:, None] * freqs[None, :]
    z = jnp.concatenate([(j / (l - 1))[:, None], jnp.cos(ang), -jnp.sin(ang)], axis=-1)
    dist = jnp.abs(j - l // 2) / (l // 2)
    max_decay = math.log(DECAY_TARGET) / DECAY_FAST_PCT
    min_decay = math.log(DECAY_TARGET) / DECAY_SLOW_PCT
    deltas = jnp.abs(jnp.linspace(min_decay, max_decay, d, dtype=f32))
    window = jnp.exp(-dist[:, None] * deltas[None, :])
    return z, window


def _ab_layer(x, ctx, mod_vecs, mod_ctx, norm_g, w_in, conv_w, conv_b, gate_b, lam_vecs,
              g_a, g_b, w_out, lam_init):
    sh_m, sc_m, g_m = mod_vecs
    bsz, s, d = x.shape
    h = _norm(x, norm_g[0], sh_m, sc_m)
    hc = _norm(ctx, norm_g[0], mod_ctx[0], mod_ctx[1])
    w = B_WIDTH
    o = 0
    cols = {}
    for name, width in (("aq", A_QW), ("bq", w), ("bo", w), ("ak", A_QW), ("av", A_VW),
                        ("bk", w), ("bv", w), ("g", 4 * B_HEADS)):
        cols[name] = w_in[:, o:o + width]
        o += width
    cos, sin = _rope_tables(s)
    cat = lambda *ws: jnp.concatenate(ws, axis=1).astype(bf16)
    q = _mm(h, cat(cols["aq"], _rotate_cols(cols["aq"])), rope=(cos, sin, A_DQK ** -0.5))
    k = _mm(h, cat(cols["ak"], _rotate_cols(cols["ak"])), rope=(cos, sin, 1.0))
    qk = _mm(h, cat(cols["bq"], cols["bk"]), conv=(conv_w, conv_b, True))
    vvo = _mm(h, cat(cols["av"], cols["bv"], cols["bo"]))
    wg, bg = _gate_cols(cols["g"], gate_b)
    gates = _mm(h, wg.astype(bf16), out_dtype=f32, bias=bg, tn=2 * LANES)
    ckv = _mm(hc, cat(cols["ak"], cols["av"], cols["bv"]))
    cbk = _mm(hc, cols["bk"].astype(bf16), conv=(conv_w[:, w:], conv_b[w:], True))
    cg = _mm(hc, wg.astype(bf16), out_dtype=f32, bias=bg, tn=2 * LANES)
    out_a = _attn(lam_vecs, q, k, vvo, ckv, g_a, lam_init)
    out_b = _mlstm(qk, vvo, gates, cbk, ckv, cg, g_b)
    wo = w_out.astype(bf16)
    return _out_proj([out_a, out_b], [wo[:A_VW], wo[A_VW:]], x, norm_g[1], g_m)


def _hyena_layer(x, mod_vecs, norm_g, w_in, conv_w, conv_b, fw1, fb1, fw2, fb2, fw3, fbias, w_out):
    sh_m, sc_m, g_m = mod_vecs
    bsz, l, d = x.shape
    h = _norm(x, norm_g[0], sh_m, sc_m)
    u = _mm(h, w_in.astype(bf16), conv=(conv_w, conv_b, False))
    z, window = _hyena_consts(l, d)
    pz, ph = LANES - z.shape[1], LANES - fw1.shape[1]
    filt = _filters(jnp.pad(z, ((0, 0), (0, pz))), jnp.pad(fw1, ((0, pz), (0, ph))),
                    jnp.pad(fb1, (0, ph)), jnp.pad(fw2, ((0, ph), (0, ph))), jnp.pad(fb2, (0, ph)),
                    jnp.pad(fw3, ((0, ph), (0, 0))), window)
    tabs = _dft_tables(l)
    kf = _spectrum(filt, tabs)
    zz = _fftconv(u, 0, u, d, kf, 0, fbias[0], tabs, d)
    y = _fftconv(zz, 0, u, 2 * d, kf, d, fbias[1], tabs, d)
    return _out_proj([y], [w_out.astype(bf16)], x, norm_g[1], g_m)


def kernel(x, c, ctx, c_ctx, w_mod, b_mod, norm_g, w_in_ab, conv_ab_w, conv_ab_b, gate_b_ab, diff_lambda, head_g_a, head_g_b, w_out_ab, w_in_hy, conv_hy_w, conv_hy_b, filt_w1, filt_b1, filt_w2, filt_b2, filt_w3, filt_bias, w_out_hy, router_w, router_b, exp_gu, exp_down, sh_gu, sh_down):
    bsz, s, d = x.shape
    depth = w_mod.shape[0]
    rows = -(-(bsz + 1) // 8) * 8
    cc = jnp.concatenate([c, c_ctx[None, :], jnp.zeros((rows - bsz - 1, d), f32)], axis=0)
    for l in range(depth):
        mod = _mod(cc, w_mod[l], b_mod[l])
        vec = lambda i: mod[:bsz, i * d:(i + 1) * d].reshape(bsz, 1, d)
        sh_m, sc_m, g_m, sh_f, sc_f, g_f = [vec(i) for i in range(6)]
        if l % 2 == 0:
            e = l // 2
            lam_init = 0.8 - 0.6 * math.exp(-0.3 * l)
            mod_ctx = (mod[bsz:bsz + 1, 0:d].reshape(1, 1, d), mod[bsz:bsz + 1, d:2 * d].reshape(1, 1, d))
            x = _ab_layer(x, ctx, (sh_m, sc_m, g_m), mod_ctx, norm_g[l], w_in_ab[e], conv_ab_w[e],
                          conv_ab_b[e], gate_b_ab[e], diff_lambda[e], head_g_a[e], head_g_b[e],
                          w_out_ab[e], lam_init)
        else:
            o = l // 2
            x = _hyena_layer(x, (sh_m, sc_m, g_m), norm_g[l], w_in_hy[o], conv_hy_w[o], conv_hy_b[o],
                             filt_w1[o], filt_b1[o], filt_w2[o], filt_b2[o], filt_w3[o], filt_bias[o],
                             w_out_hy[o])
        hx, rk, cmax = _router(x, norm_g[l, 2], sh_f, sc_f, router_w[l].T, router_b[l])
        x = _moe(hx, rk, cmax, exp_gu[l].astype(bf16), exp_down[l].astype(bf16), sh_gu[l].astype(bf16),
                 sh_down[l].astype(bf16), x, norm_g[l, 3], g_f)
    return x
```

```python
import functools
import math

import numpy as np
import jax
import jax.numpy as jnp
from jax import lax
from jax.experimental import pallas as pl
from jax.experimental.pallas import tpu as pltpu

f32 = jnp.float32
bf16 = jnp.bfloat16

RMS_EPS = 1e-6
A_HEADS = 4
A_DQK = 64
A_DV = 128
B_HEADS = 4
B_DH = 128
B_WIDTH = B_HEADS * B_DH
A_QW = A_HEADS * 2 * A_DQK
A_VW = A_HEADS * A_DV
GRID_W = 64
ROPE_BASE = 10000.0
ROPE_AXIS_PAIRS = A_DQK // 4
N_EXPERTS = 64
TOP_K = 8
N_GROUPS = 8
TOPK_GROUPS = 4
D_EXPERT = 256
ROUTED_SCALE = 2.5
HY_ORDER = 2
POS_EMB_DIM = 33
FILTER_SIN_W = 1.0
DECAY_FAST_PCT = 0.3
DECAY_SLOW_PCT = 1.5
DECAY_TARGET = 1e-2

LANES = 128
VMEM_LIMIT = 56 * 1024 * 1024
MLSTM_CHUNK = 256
FFT_PAD = 8
MOE_SUB = 256
MOE_WIN = 64
MOE_GROUP = 4


def _cp(*sem):
    return pltpu.CompilerParams(dimension_semantics=sem, vmem_limit_bytes=VMEM_LIMIT)


def _split_bf16(a):
    hi = a.astype(bf16)
    lo = (a - hi.astype(f32)).astype(bf16)
    return hi, lo


def _dot(a, b, dims=(((1,), (0,)), ((), ()))):
    return lax.dot_general(a, b, dims, preferred_element_type=f32)


_NT = (((1,), (1,)), ((), ()))
_TN = (((0,), (0,)), ((), ()))


def _dot3(a, b, dims=(((1,), (0,)), ((), ()))):
    ah, al = _split_bf16(a)
    bh, bl = _split_bf16(b)
    return _dot(ah, bh, dims) + (_dot(ah, bl, dims) + _dot(al, bh, dims))


def _silu(v):
    return v / (1.0 + jnp.exp(-v))


def _sigmoid(v):
    return 1.0 / (1.0 + jnp.exp(-v))


def _log_sigmoid(v):
    return jnp.minimum(v, 0.0) - jnp.log(1.0 + jnp.exp(-jnp.abs(v)))


def _mod_kernel(c_ref, w_ref, b_ref, o_ref):
    o_ref[...] = _dot3(_silu(c_ref[...]), w_ref[...]) + b_ref[...]


def _mod(cc, w, b):
    rows, d = cc.shape
    n = w.shape[1]
    tn = d
    return pl.pallas_call(
        _mod_kernel,
        grid=(n // tn,),
        in_specs=[pl.BlockSpec((rows, d), lambda j: (0, 0)),
                  pl.BlockSpec((d, tn), lambda j: (0, j)),
                  pl.BlockSpec((1, tn), lambda j: (0, j))],
        out_specs=pl.BlockSpec((rows, tn), lambda j: (0, j)),
        out_shape=jax.ShapeDtypeStruct((rows, n), f32),
        compiler_params=_cp("arbitrary"),
        name="mod",
    )(cc, w, b.reshape(1, n))


def _norm_mod(xv, g, shift, scale):
    y = xv * lax.rsqrt(jnp.mean(xv * xv, axis=-1, keepdims=True) + RMS_EPS)
    return (y * g) * (1.0 + scale) + shift


def _norm_kernel(x_ref, g_ref, sh_ref, sc_ref, o_ref):
    o_ref[...] = _norm_mod(x_ref[...], g_ref[...], sh_ref[...], sc_ref[...]).astype(o_ref.dtype)


def _bidx(arr):
    if arr.shape[0] == 1:
        return lambda b, *_: (0, 0, 0)
    return lambda b, *_: (b, 0, 0)


def _norm(x, g, shift, scale, tl=512):
    bsz, l, d = x.shape
    tl = min(tl, l)
    return pl.pallas_call(
        _norm_kernel,
        grid=(bsz, l // tl),
        in_specs=[pl.BlockSpec((None, tl, d), lambda b, i: (b, i, 0)),
                  pl.BlockSpec((1, d), lambda b, i: (0, 0)),
                  pl.BlockSpec((None, 1, d), _bidx(shift)),
                  pl.BlockSpec((None, 1, d), _bidx(scale))],
        out_specs=pl.BlockSpec((None, tl, d), lambda b, i: (b, i, 0)),
        out_shape=jax.ShapeDtypeStruct((bsz, l, d), bf16),
        compiler_params=_cp("parallel", "parallel"),
        name="norm",
    )(x, g.reshape(1, d), shift, scale)


def _mm_plain_kernel(h_ref, w_ref, b_ref, o_ref):
    o_ref[...] = (_dot(h_ref[...], w_ref[...]) + b_ref[...]).astype(o_ref.dtype)


def _mm_rope_kernel(h_ref, w_ref, cos_ref, sin_ref, o_ref, *, scale):
    p = _dot(h_ref[...], w_ref[...])
    n = o_ref.shape[-1]
    o_ref[...] = ((p[:, :n] * cos_ref[...] + p[:, n:] * sin_ref[...]) * scale).astype(o_ref.dtype)


def _mm_conv_kernel(h_ref, w_ref, cw_ref, cb_ref, o_ref, *, act):
    p = _dot(h_ref[...], w_ref[...])
    l = p.shape[0]
    row = lax.broadcasted_iota(jnp.int32, p.shape, 0)
    prev = jnp.where(row == 0, 0.0, pltpu.roll(p, 1, 0))
    nxt = jnp.where(row == l - 1, 0.0, pltpu.roll(p, l - 1, 0))
    y = prev * cw_ref[0:1, :] + p * cw_ref[1:2, :] + nxt * cw_ref[2:3, :] + cb_ref[...]
    if act:
        y = _silu(y)
    o_ref[...] = y.astype(o_ref.dtype)


def _mm(h, w, *, out_dtype=bf16, bias=None, rope=None, conv=None, tl=512, tn=512):
    bsz, l, k = h.shape
    n = w.shape[1]
    if rope is not None:
        cos, sin, scale = rope
        n_out = n // 2
        tl = min(tl, l)
        return pl.pallas_call(
            functools.partial(_mm_rope_kernel, scale=scale),
            grid=(bsz, l // tl),
            in_specs=[pl.BlockSpec((None, tl, k), lambda b, i: (b, i, 0)),
                      pl.BlockSpec((k, n), lambda b, i: (0, 0)),
                      pl.BlockSpec((tl, n_out), lambda b, i: (i, 0)),
                      pl.BlockSpec((tl, n_out), lambda b, i: (i, 0))],
            out_specs=pl.BlockSpec((None, tl, n_out), lambda b, i: (b, i, 0)),
            out_shape=jax.ShapeDtypeStruct((bsz, l, n_out), out_dtype),
            compiler_params=_cp("parallel", "parallel"),
            name="mm_rope",
        )(h, w, cos, sin)
    tn = min(tn, n)
    if conv is not None:
        cw, cb, act = conv
        return pl.pallas_call(
            functools.partial(_mm_conv_kernel, act=act),
            grid=(bsz, n // tn),
            in_specs=[pl.BlockSpec((None, l, k), lambda b, j: (b, 0, 0)),
                      pl.BlockSpec((k, tn), lambda b, j: (0, j)),
                      pl.BlockSpec((3, tn), lambda b, j: (0, j)),
                      pl.BlockSpec((1, tn), lambda b, j: (0, j))],
            out_specs=pl.BlockSpec((None, l, tn), lambda b, j: (b, 0, j)),
            out_shape=jax.ShapeDtypeStruct((bsz, l, n), out_dtype),
            compiler_params=_cp("parallel", "arbitrary"),
            name="mm_conv",
        )(h, w, cw, cb.reshape(1, n))
    if bias is None:
        bias = jnp.zeros((n,), f32)
    tl = min(tl, l)
    return pl.pallas_call(
        _mm_plain_kernel,
        grid=(bsz, l // tl, n // tn),
        in_specs=[pl.BlockSpec((None, tl, k), lambda b, i, j: (b, i, 0)),
                  pl.BlockSpec((k, tn), lambda b, i, j: (0, j)),
                  pl.BlockSpec((1, tn), lambda b, i, j: (0, j))],
        out_specs=pl.BlockSpec((None, tl, tn), lambda b, i, j: (b, i, j)),
        out_shape=jax.ShapeDtypeStruct((bsz, l, n), out_dtype),
        compiler_params=_cp("parallel", "parallel", "arbitrary"),
        name="mm_plain",
    )(h, w, bias.reshape(1, n))


def _attn_kernel(lv_ref, q_ref, kc_ref, k_ref, vc_ref, v_ref, g_ref, o_ref, *, lam_init):
    tq = q_ref.shape[0]
    lv = lv_ref[...]
    lam = (jnp.exp(jnp.sum(lv[0:1] * lv[1:2], axis=1, keepdims=True))
           - jnp.exp(jnp.sum(lv[2:3] * lv[3:4], axis=1, keepdims=True)) + lam_init)
    first = lax.broadcasted_iota(jnp.int32, (tq, A_DV), 1) < A_DQK
    for hd in range(A_HEADS):
        cs = slice(hd * A_DV, (hd + 1) * A_DV)
        qh = q_ref[:, cs]
        zero = jnp.zeros_like(qh)
        q2 = jnp.concatenate([jnp.where(first, qh, zero), jnp.where(first, zero, qh)], axis=0)
        s_c = _dot(q2, kc_ref[:, cs], _NT)
        s_l = _dot(q2, k_ref[:, cs], _NT)
        m = jnp.maximum(jnp.max(s_c, axis=1, keepdims=True), jnp.max(s_l, axis=1, keepdims=True))
        p_c = jnp.exp(s_c - m)
        p_l = jnp.exp(s_l - m)
        inv = 1.0 / (jnp.sum(p_c, axis=1, keepdims=True) + jnp.sum(p_l, axis=1, keepdims=True))
        w0 = inv[:tq]
        w1 = inv[tq:] * lam
        a_c = (p_c[:tq] * w0 - p_c[tq:] * w1).astype(bf16)
        a_l = (p_l[:tq] * w0 - p_l[tq:] * w1).astype(bf16)
        o = _dot(a_c, vc_ref[:, cs]) + _dot(a_l, v_ref[:, cs])
        o = o * lax.rsqrt(jnp.mean(o * o, axis=1, keepdims=True) + RMS_EPS)
        o_ref[:, cs] = (o * g_ref[:, cs] * (1.0 - lam_init)).astype(o_ref.dtype)


def _attn(lv, q, k, vvo, ckv, g_a, lam_init, tq=256):
    bsz, s, _ = q.shape
    lc = ckv.shape[1]
    tq = min(tq, s)
    w = A_QW
    return pl.pallas_call(
        functools.partial(_attn_kernel, lam_init=lam_init),
        grid=(bsz, s // tq),
        in_specs=[pl.BlockSpec(lv.shape, lambda b, i: (0, 0)),
                  pl.BlockSpec((None, tq, w), lambda b, i: (b, i, 0)),
                  pl.BlockSpec((None, lc, w), lambda b, i: (b, 0, 0)),
                  pl.BlockSpec((None, s, w), lambda b, i: (b, 0, 0)),
                  pl.BlockSpec((None, lc, w), lambda b, i: (b, 0, 1)),
                  pl.BlockSpec((None, s, w), lambda b, i: (b, 0, 0)),
                  pl.BlockSpec((1, w), lambda b, i: (0, 0))],
        out_specs=pl.BlockSpec((None, tq, w), lambda b, i: (b, i, 0)),
        out_shape=jax.ShapeDtypeStruct((bsz, s, w), bf16),
        compiler_params=_cp("parallel", "arbitrary"),
        name="diff_attn",
    )(lv, q, ckv, k, ckv, vvo, g_a.reshape(1, w))


_LN_QSCALE = math.log(B_DH ** -0.5)
_NCHAIN = 2 * B_HEADS


def _chunk_gate_sums(gi, gf, tri):
    lf = _log_sigmoid(gf)
    hi, lo = _split_bf16(lf)
    cum = _dot(tri, hi) + _dot(tri, lo)
    t = gf.shape[0]
    tot = cum[t - 1:t, :]
    rcum = tot - cum + lf
    fwd = lax.broadcasted_iota(jnp.int32, gf.shape, 1) < B_HEADS
    bd = jnp.where(fwd, cum, rcum)
    return bd, tot, (bd - gi).T


def _lower_tri(t):
    r = lax.broadcasted_iota(jnp.int32, (t, t), 0)
    c = lax.broadcasted_iota(jnp.int32, (t, t), 1)
    return r, c


def _absorb(c_ref, n_ref, m_ref, ch, bcol, tot_c, gi_c, kf, vb):
    m_prev = m_ref[ch][:, 0:1]
    g = tot_c - bcol + gi_c
    m_new = jnp.maximum(tot_c + m_prev, jnp.max(g, axis=0, keepdims=True))
    wgt = jnp.exp(g - m_new)
    decay = jnp.exp(tot_c + m_prev - m_new)
    kw = kf * wgt
    c_ref[ch] = decay * c_ref[ch] + _dot(kw.astype(bf16), vb, _TN)
    n_ref[ch] = decay * n_ref[ch] + jnp.sum(kw, axis=0, keepdims=True)
    m_ref[ch] = jnp.broadcast_to(m_new, m_ref.shape[1:])


def _mlstm_kernel(qk_ref, vvo_ref, g_ref, ck_ref, ckv_ref, cg_ref, gb_ref, o_ref,
                  hf_ref, hb_ref, c_ref, n_ref, m_ref, *, tc):
    s = o_ref.shape[0]
    lc = ck_ref.shape[0]
    nc = s // tc
    w = B_WIDTH
    dh = B_DH

    c_ref[...] = jnp.zeros_like(c_ref)
    n_ref[...] = jnp.zeros_like(n_ref)
    m_ref[...] = jnp.zeros_like(m_ref)

    r, cidx = _lower_tri(lc)
    tri_c = jnp.where(cidx <= r, 1.0, 0.0).astype(bf16)
    cg = cg_ref[...]
    bd, tot, _ = _chunk_gate_sums(cg[:, :LANES], cg[:, LANES:], tri_c)
    gi = cg[:, :LANES]
    for ch in range(_NCHAIN):
        hs = slice((ch % B_HEADS) * dh, (ch % B_HEADS + 1) * dh)
        vs = slice(2 * w + (ch % B_HEADS) * dh, 2 * w + (ch % B_HEADS + 1) * dh)
        _absorb(c_ref, n_ref, m_ref, ch, bd[:, ch:ch + 1], tot[:, ch:ch + 1], gi[:, ch:ch + 1],
                ck_ref[:, hs].astype(f32), ckv_ref[:, vs])

    r, cidx = _lower_tri(tc)
    tri = jnp.where(cidx <= r, 1.0, 0.0).astype(bf16)
    causal = cidx <= r
    anti = cidx >= r

    def step(i, carry):
        for d in range(2):
            row0 = pl.multiple_of((i if d == 0 else nc - 1 - i) * tc, tc)
            rows = pl.ds(row0, tc)
            gch = g_ref[rows, :]
            gi = gch[:, :LANES]
            bd, tot, xt = _chunk_gate_sums(gi, gch[:, LANES:], tri)
            mask = causal if d == 0 else anti
            dst = hf_ref if d == 0 else hb_ref
            for hd in range(B_HEADS):
                ch = d * B_HEADS + hd
                hs = slice(hd * dh, (hd + 1) * dh)
                qb = qk_ref[rows, hs]
                kb = qk_ref[rows, slice(w + hd * dh, w + (hd + 1) * dh)]
                vb = vvo_ref[rows, slice(w + hd * dh, w + (hd + 1) * dh)]
                bcol = bd[:, ch:ch + 1]
                dmat = jnp.where(mask, bcol - xt[ch:ch + 1, :], -jnp.inf)
                m_prev = m_ref[ch][:, 0:1]
                inter = bcol + m_prev
                m_t = jnp.maximum(inter, jnp.max(dmat, axis=1, keepdims=True))
                e = jnp.exp(dmat - m_t + _LN_QSCALE)
                smat = _dot(qb, kb, _NT) * e
                sc = jnp.exp(inter - m_t + _LN_QSCALE)
                num = sc * _dot(qb, c_ref[ch].astype(bf16)) + _dot(smat.astype(bf16), vb)
                qn = jnp.sum(qb.astype(f32) * n_ref[ch], axis=1, keepdims=True)
                den = sc * qn + jnp.sum(smat, axis=1, keepdims=True)
                hout = num * (1.0 / jnp.maximum(jnp.abs(den), jnp.exp(-m_t)))
                dst[rows, hs] = hout
                _absorb(c_ref, n_ref, m_ref, ch, bcol, tot[:, ch:ch + 1], gi[:, ch:ch + 1],
                        kb.astype(f32), vb)
        return carry

    lax.fori_loop(0, nc, step, 0)

    for hd in range(B_HEADS):
        hs = slice(hd * dh, (hd + 1) * dh)
        hsum = hf_ref[:, hs] + hb_ref[:, hs]
        hn = hsum * lax.rsqrt(jnp.mean(hsum * hsum, axis=1, keepdims=True) + RMS_EPS)
        og = _sigmoid(vvo_ref[:, slice(2 * w + hd * dh, 2 * w + (hd + 1) * dh)].astype(f32))
        o_ref[:, hs] = (hn * gb_ref[:, hs] * og).astype(o_ref.dtype)


def _mlstm(qk, vvo, gates, cbk, ckv, cg, g_b):
    bsz, s, _ = qk.shape
    lc = cbk.shape[1]
    w = B_WIDTH
    tc = min(MLSTM_CHUNK, s)
    return pl.pallas_call(
        functools.partial(_mlstm_kernel, tc=tc),
        grid=(bsz,),
        in_specs=[pl.BlockSpec((None, s, 2 * w), lambda b: (b, 0, 0)),
                  pl.BlockSpec((None, s, 3 * w), lambda b: (b, 0, 0)),
                  pl.BlockSpec((None, s, 2 * LANES), lambda b: (b, 0, 0)),
                  pl.BlockSpec((None, lc, w), lambda b: (b, 0, 0)),
                  pl.BlockSpec((None, lc, 3 * w), lambda b: (b, 0, 0)),
                  pl.BlockSpec((None, lc, 2 * LANES), lambda b: (b, 0, 0)),
                  pl.BlockSpec((1, w), lambda b: (0, 0))],
        out_specs=pl.BlockSpec((None, s, w), lambda b: (b, 0, 0)),
        out_shape=jax.ShapeDtypeStruct((bsz, s, w), bf16),
        scratch_shapes=[pltpu.VMEM((s, w), f32), pltpu.VMEM((s, w), f32),
                        pltpu.VMEM((_NCHAIN, B_DH, B_DH), f32),
                        pltpu.VMEM((_NCHAIN, 1, B_DH), f32),
                        pltpu.VMEM((_NCHAIN, 1, LANES), f32)],
        compiler_params=_cp("arbitrary"),
        name="mlstm",
    )(qk, vvo, gates, cbk, ckv, cg, g_b.reshape(1, w))


def _out_kernel(*refs, n_act):
    acts = refs[:n_act]
    ws = refs[n_act:2 * n_act]
    x_ref, g_ref, gate_ref, o_ref = refs[2 * n_act:]
    mix = _dot(acts[0][...], ws[0][...])
    for a, wr in zip(acts[1:], ws[1:]):
        mix = mix + _dot(a[...], wr[...])
    y = mix * lax.rsqrt(jnp.mean(mix * mix, axis=-1, keepdims=True) + RMS_EPS) * g_ref[...]
    o_ref[...] = x_ref[...] + gate_ref[...] * y


def _out_proj(acts, ws, x, g, gate, tl=512):
    bsz, l, d = x.shape
    tl = min(tl, l)
    n_act = len(acts)
    in_specs = [pl.BlockSpec((None, tl, a.shape[2]), lambda b, i: (b, i, 0)) for a in acts]
    in_specs += [pl.BlockSpec(wm.shape, lambda b, i: (0, 0)) for wm in ws]
    in_specs += [pl.BlockSpec((None, tl, d), lambda b, i: (b, i, 0)),
                 pl.BlockSpec((1, d), lambda b, i: (0, 0)),
                 pl.BlockSpec((None, 1, d), _bidx(gate))]
    return pl.pallas_call(
        functools.partial(_out_kernel, n_act=n_act),
        grid=(bsz, l // tl),
        in_specs=in_specs,
        out_specs=pl.BlockSpec((None, tl, d), lambda b, i: (b, i, 0)),
        out_shape=jax.ShapeDtypeStruct((bsz, l, d), f32),
        compiler_params=_cp("parallel", "parallel"),
        name="out_proj",
    )(*acts, *ws, x, g.reshape(1, d), gate)


def _router_kernel(x_ref, g_ref, sh_ref, sc_ref, rw_ref, rb_ref, h_ref, rk_ref, cm_ref):
    hf = _norm_mod(x_ref[...], g_ref[...], sh_ref[...], sc_ref[...])
    tl, d = hf.shape
    h_ref[:, :d] = hf.astype(h_ref.dtype)
    per = N_EXPERTS // N_GROUPS
    logits = _dot3(rw_ref[...], hf, _NT)
    s3 = _sigmoid(logits).reshape(N_GROUPS, per, tl)
    b3 = s3 + rb_ref[...].reshape(N_GROUPS, per, 1)
    neg = -jnp.inf
    jdx = lax.broadcasted_iota(jnp.int32, b3.shape, 1)
    gdx = lax.broadcasted_iota(jnp.int32, b3.shape, 0)
    m1 = jnp.max(b3, axis=1, keepdims=True)
    f1 = jnp.min(jnp.where(b3 == m1, jdx, per), axis=1, keepdims=True)
    m2 = jnp.max(jnp.where(jdx == f1, neg, b3), axis=1, keepdims=True)
    grp = m1 + m2
    g1 = lax.broadcasted_iota(jnp.int32, grp.shape, 0)
    cnt = jnp.zeros(grp.shape, jnp.int32)
    for gp in range(N_GROUPS):
        rv = grp[gp:gp + 1]
        ahead = jnp.where(rv > grp, 1, jnp.where(rv == grp, jnp.where(g1 > gp, 1, 0), 0))
        cnt = cnt + ahead
    v = jnp.where(cnt < TOPK_GROUPS, b3, neg)
    eidx = gdx * per + jdx
    sel = jnp.zeros(b3.shape, f32)
    for _ in range(TOP_K):
        m = jnp.max(jnp.max(v, axis=1, keepdims=True), axis=0, keepdims=True)
        cand = jnp.where(v == m, eidx, N_EXPERTS)
        fi = jnp.min(jnp.min(cand, axis=1, keepdims=True), axis=0, keepdims=True)
        hit = eidx == fi
        sel = jnp.where(hit, 1.0, sel)
        v = jnp.where(hit, neg, v)
    ssel = sel * s3
    den = jnp.sum(jnp.sum(ssel, axis=1, keepdims=True), axis=0, keepdims=True)
    gates = ((ROUTED_SCALE * ssel) / den).reshape(N_EXPERTS, tl)
    gates = jnp.concatenate([gates, jnp.zeros((LANES - N_EXPERTS, tl), f32)], axis=0)
    g_hi, g_lo = _split_bf16(gates.T)
    h_ref[:, d:d + LANES] = g_hi
    h_ref[:, d + LANES:] = g_lo
    sel2 = sel.reshape(N_EXPERTS, tl)
    r = lax.broadcasted_iota(jnp.int32, (MOE_SUB, MOE_SUB), 0)
    c = lax.broadcasted_iota(jnp.int32, (MOE_SUB, MOE_SUB), 1)
    before = jnp.where(r < c, 1.0, 0.0).astype(bf16)
    cmax = jnp.zeros((N_EXPERTS, 1), f32)
    for j in range(tl // MOE_SUB):
        sub = sel2[:, j * MOE_SUB:(j + 1) * MOE_SUB]
        rank = _dot(sub.astype(bf16), before)
        rk_ref[:, j * MOE_SUB:(j + 1) * MOE_SUB] = jnp.where(sub > 0.0, rank, -1.0)
        cmax = jnp.maximum(cmax, jnp.sum(sub, axis=1, keepdims=True))
    cm_ref[...] = jnp.broadcast_to(cmax, cm_ref.shape)


def _router(x, g, shift, scale, rw_t, rb, tl=512):
    bsz, l, d = x.shape
    tl = min(tl, l)
    nl = l // tl
    return pl.pallas_call(
        _router_kernel,
        grid=(bsz, nl),
        in_specs=[pl.BlockSpec((None, tl, d), lambda b, i: (b, i, 0)),
                  pl.BlockSpec((1, d), lambda b, i: (0, 0)),
                  pl.BlockSpec((None, 1, d), _bidx(shift)),
                  pl.BlockSpec((None, 1, d), _bidx(scale)),
                  pl.BlockSpec((N_EXPERTS, d), lambda b, i: (0, 0)),
                  pl.BlockSpec((N_EXPERTS, 1), lambda b, i: (0, 0))],
        out_specs=[pl.BlockSpec((None, tl, d + 2 * LANES), lambda b, i: (b, i, 0)),
                   pl.BlockSpec((N_EXPERTS, tl), lambda b, i: (0, b * nl + i)),
                   pl.BlockSpec((None, N_EXPERTS, LANES), lambda b, i: (b * nl + i, 0, 0))],
        out_shape=[jax.ShapeDtypeStruct((bsz, l, d + 2 * LANES), bf16),
                   jax.ShapeDtypeStruct((N_EXPERTS, bsz * l), f32),
                   jax.ShapeDtypeStruct((bsz * nl, N_EXPERTS, LANES), f32)],
        compiler_params=_cp("parallel", "parallel"),
        name="router",
    )(x, g.reshape(1, d), shift, scale, rw_t, rb.reshape(N_EXPERTS, 1))


def _swiglu_act(hh):
    half = hh.shape[1] // 2
    return _silu(hh[:, :half]) * hh[:, half:]


def _moe_kernel(cnt_ref, h_ref, rk_ref, gu_ref, dn_ref, sgu_ref, sdn_ref, x_ref, g_ref, gate_ref,
                o_ref, acc_ref, xg_ref, ys_ref, p_ref, gr_ref):
    tile = pl.program_id(0)
    grp = pl.program_id(1)
    tm, d = acc_ref.shape
    ns = tm // MOE_SUB
    win = MOE_WIN
    ng = MOE_GROUP

    @pl.when(grp == 0)
    def _():
        act = _swiglu_act(_dot(h_ref[:, :d], sgu_ref[...]))
        acc_ref[...] = _dot(act.astype(bf16), sdn_ref[...])

    riota = lax.broadcasted_iota(jnp.int32, (win, MOE_SUB), 0).astype(f32)
    lane = lax.broadcasted_iota(jnp.int32, (win, LANES), 1)

    def one_pass(p, carry):
        base = (p * win).astype(f32)
        for s in range(ns):
            cols = slice(s * MOE_SUB, (s + 1) * MOE_SUB)
            onehots = []
            for el in range(ng):
                rke = rk_ref[pl.ds(grp * ng + el, 1), cols] - base
                onehots.append(jnp.where(rke == riota, 1.0, 0.0).astype(bf16))
            pm = jnp.concatenate(onehots, axis=0)
            p_ref[s] = pm
            gx = _dot(pm, h_ref[cols, :])
            for el in range(ng):
                blk = gx[el * win:(el + 1) * win]
                xg_ref[el, s * win:(s + 1) * win, :] = blk[:, :d].astype(bf16)
                gw = blk[:, d:d + LANES] + blk[:, d + LANES:]
                gsel = jnp.sum(jnp.where(lane == grp * ng + el, gw, 0.0), axis=1, keepdims=True)
                gr_ref[el, s * win:(s + 1) * win, :] = jnp.broadcast_to(gsel, (win, LANES))
        for el in range(ng):
            busy = cnt_ref[tile, grp * ng + el] > p * win

            @pl.when(busy)
            def _():
                hh = _dot(xg_ref[el], gu_ref[el])
                gr = gr_ref[el]
                act = _swiglu_act(hh) * jnp.concatenate([gr] * (hh.shape[1] // (2 * LANES)), axis=1)
                y = _dot(act.astype(bf16), dn_ref[el]).astype(bf16)
                for s in range(ns):
                    ys_ref[s, el * win:(el + 1) * win, :] = y[s * win:(s + 1) * win]

            @pl.when(jnp.logical_not(busy))
            def _():
                for s in range(ns):
                    ys_ref[s, el * win:(el + 1) * win, :] = jnp.zeros((win, d), bf16)
        for s in range(ns):
            acc_ref[s * MOE_SUB:(s + 1) * MOE_SUB, :] += _dot(p_ref[s], ys_ref[s], _TN)
        return carry

    most = cnt_ref[tile, grp * ng]
    for el in range(1, ng):
        most = jnp.maximum(most, cnt_ref[tile, grp * ng + el])
    lax.fori_loop(0, (most + win - 1) // win, one_pass, 0)

    @pl.when(grp == pl.num_programs(1) - 1)
    def _():
        mo = acc_ref[...]
        y = mo * lax.rsqrt(jnp.mean(mo * mo, axis=-1, keepdims=True) + RMS_EPS) * g_ref[...]
        o_ref[...] = x_ref[...] + gate_ref[...] * y


def _moe(hx, rk, cmax, gu, dn, sgu, sdn, x, g, gate, tm=1024):
    bsz, l, d = x.shape
    tm = min(tm, l)
    per_b = l // tm
    nt = bsz * per_b
    ne = gu.shape[0]
    ng = MOE_GROUP
    ns = tm // MOE_SUB
    counts = jnp.max(cmax[:, :, 0].reshape(nt, -1, ne), axis=1).astype(jnp.int32)
    grid_spec = pltpu.PrefetchScalarGridSpec(
        num_scalar_prefetch=1,
        grid=(nt, ne // ng),
        in_specs=[pl.BlockSpec((tm, hx.shape[2]), lambda t, e, n: (t, 0)),
                  pl.BlockSpec((ne, tm), lambda t, e, n: (0, t)),
                  pl.BlockSpec((ng,) + gu.shape[1:], lambda t, e, n: (e, 0, 0)),
                  pl.BlockSpec((ng,) + dn.shape[1:], lambda t, e, n: (e, 0, 0)),
                  pl.BlockSpec(sgu.shape, lambda t, e, n: (0, 0)),
                  pl.BlockSpec(sdn.shape, lambda t, e, n: (0, 0)),
                  pl.BlockSpec((tm, d), lambda t, e, n: (t, 0)),
                  pl.BlockSpec((1, d), lambda t, e, n: (0, 0)),
                  pl.BlockSpec((None, 1, d), lambda t, e, n: (t // per_b, 0, 0))],
        out_specs=pl.BlockSpec((tm, d), lambda t, e, n: (t, 0)),
        scratch_shapes=[pltpu.VMEM((tm, d), f32),
                        pltpu.VMEM((ng, ns * MOE_WIN, d), bf16),
                        pltpu.VMEM((ns, ng * MOE_WIN, d), bf16),
                        pltpu.VMEM((ns, ng * MOE_WIN, MOE_SUB), bf16),
                        pltpu.VMEM((ng, ns * MOE_WIN, LANES), f32)])
    out = pl.pallas_call(
        _moe_kernel,
        grid_spec=grid_spec,
        out_shape=jax.ShapeDtypeStruct((bsz * l, d), f32),
        compiler_params=_cp("parallel", "arbitrary"),
        name="moe",
    )(counts, hx.reshape(bsz * l, hx.shape[2]), rk, gu, dn, sgu, sdn, x.reshape(bsz * l, d),
      g.reshape(1, d), gate)
    return out.reshape(bsz, l, d)


def _filter_kernel(z_ref, w1_ref, b1_ref, w2_ref, b2_ref, w3_ref, win_ref, o_ref):
    hid = jnp.sin(FILTER_SIN_W * (_dot3(z_ref[...], w1_ref[...]) + b1_ref[...]))
    hid = jnp.sin(FILTER_SIN_W * (_dot3(hid, w2_ref[...]) + b2_ref[...]))
    o_ref[...] = _dot3(hid, w3_ref[...]) * win_ref[...]


def _filters(z, w1, b1, w2, b2, w3, window, tn=512):
    l, p = z.shape
    hdim = w1.shape[1]
    n = w3.shape[1]
    d = window.shape[1]
    nd = d // tn
    return pl.pallas_call(
        _filter_kernel,
        grid=(n // tn,),
        in_specs=[pl.BlockSpec((l, p), lambda j: (0, 0)),
                  pl.BlockSpec((p, hdim), lambda j: (0, 0)),
                  pl.BlockSpec((1, hdim), lambda j: (0, 0)),
                  pl.BlockSpec((hdim, hdim), lambda j: (0, 0)),
                  pl.BlockSpec((1, hdim), lambda j: (0, 0)),
                  pl.BlockSpec((hdim, tn), lambda j: (0, j)),
                  pl.BlockSpec((l, tn), lambda j: (0, j % nd))],
        out_specs=pl.BlockSpec((l, tn), lambda j: (0, j)),
        out_shape=jax.ShapeDtypeStruct((l, n), f32),
        compiler_params=_cp("arbitrary"),
        name="hyena_filter",
    )(z, w1, b1.reshape(1, hdim), w2, b2.reshape(1, hdim), w3, window)


def _dft_tables(l):
    n = 2 * l
    n1 = math.isqrt(n)
    assert n == n1 * n1 and n1 % 16 == 0
    na = l // n1
    a = np.arange(na)
    b = np.arange(n1)
    c = np.arange(n1)
    th = 2.0 * np.pi * ((n1 * a[None, None, :] + b[:, None, None]) * c[None, :, None]) / n
    t1 = np.concatenate([np.cos(th), -np.sin(th)], axis=1)
    ph = 2.0 * np.pi * (b[:, None] * b[None, :]) / n1
    cs, sn = np.cos(ph), np.sin(ph)
    a3 = np.block([[cs, sn], [-sn, cs]])
    a3i = np.block([[cs, -sn], [sn, cs]])
    a2 = np.arange(na) + na // 2
    th2 = 2.0 * np.pi * ((n1 * a2[None, :, None] + b[:, None, None]) * c[None, None, :]) / n
    t2 = np.concatenate([np.cos(th2), -np.sin(th2)], axis=2)
    return [jnp.asarray(t, f32).astype(bf16) for t in (t1, a3, a3i, t2)]


def _fft_dims(t1):
    n1, _, na = t1.shape
    return n1, na, 2 * n1 + FFT_PAD, n1 + FFT_PAD


def _dft_forward(uf_ref, t1_ref, zs_ref):
    n1, na, sb, su = _fft_dims(t1_ref)
    for b in range(n1):
        ub = uf_ref[pl.ds(b, na, stride=su), :].astype(bf16)
        zb = _dot(t1_ref[b], ub)
        zs_ref[pl.ds(b, n1, stride=sb), :] = zb[:n1]
        zs_ref[pl.ds(n1 + b, n1, stride=sb), :] = zb[n1:]


def _spectrum_kernel(f_ref, t1_ref, a3_ref, o_ref, uf_ref, zs_ref, *, scale):
    n1, na, sb, su = _fft_dims(t1_ref)
    for a in range(na):
        uf_ref[pl.ds(a * su, n1), :] = f_ref[pl.ds(a * n1, n1), :]
    _dft_forward(uf_ref, t1_ref, zs_ref)
    a3 = a3_ref[...]
    for c in range(n1):
        zc = zs_ref[pl.ds(c * sb, 2 * n1), :].astype(bf16)
        o_ref[c] = _dot(a3, zc) * scale


def _spectrum(filt, tabs, dt=128):
    l, n = filt.shape
    t1, a3, _, _ = tabs
    n1, na, sb, su = _fft_dims(t1)
    return pl.pallas_call(
        functools.partial(_spectrum_kernel, scale=1.0 / (2 * l)),
        grid=(n // dt,),
        in_specs=[pl.BlockSpec((l, dt), lambda j: (0, j)),
                  pl.BlockSpec(t1.shape, lambda j: (0, 0, 0)),
                  pl.BlockSpec(a3.shape, lambda j: (0, 0))],
        out_specs=pl.BlockSpec((n1, 2 * n1, dt), lambda j: (0, 0, j)),
        out_shape=jax.ShapeDtypeStruct((n1, 2 * n1, n), f32),
        scratch_shapes=[pltpu.VMEM((na * su, dt), f32),
                        pltpu.VMEM((n1 * sb, dt), f32)],
        compiler_params=_cp("arbitrary"),
        name="hyena_spectrum",
    )(filt, t1, a3)


def _fftconv_kernel(u_ref, xg_ref, kf_ref, fb_ref, t1_ref, a3_ref, a3i_ref, t2_ref, o_ref,
                    uf_ref, zs_ref, qs_ref, y_ref):
    n1, na, sb, su = _fft_dims(t1_ref)
    for a in range(na):
        uf_ref[pl.ds(a * su, n1), :] = u_ref[pl.ds(a * n1, n1), :].astype(f32)
    _dft_forward(uf_ref, t1_ref, zs_ref)
    a3 = a3_ref[...]
    a3i = a3i_ref[...]
    for c in range(n1):
        zc = zs_ref[pl.ds(c * sb, 2 * n1), :].astype(bf16)
        xc = _dot(a3, zc)
        kc = kf_ref[c]
        xr, xi = xc[:n1], xc[n1:]
        kr, ki = kc[:n1], kc[n1:]
        pc = jnp.concatenate([xr * kr - xi * ki, xr * ki + xi * kr], axis=0).astype(bf16)
        qc = _dot(a3i, pc)
        qs_ref[pl.ds(c, n1, stride=sb), :] = qc[:n1]
        qs_ref[pl.ds(n1 + c, n1, stride=sb), :] = qc[n1:]
    for b in range(n1):
        qb = qs_ref[pl.ds(b * sb, 2 * n1), :].astype(bf16)
        y_ref[pl.ds(b, na, stride=su), :] = _dot(t2_ref[b], qb)
    fb = fb_ref[...]
    for a in range(na):
        rows = pl.ds(a * n1, n1)
        uv = uf_ref[pl.ds(a * su, n1), :]
        yv = y_ref[pl.ds(a * su, n1), :]
        o_ref[rows, :] = (xg_ref[rows, :].astype(f32) * (yv + uv * fb)).astype(o_ref.dtype)


def _fftconv(u, u_col, xg, xg_col, kf, kf_col, fbias, tabs, d, dt=128):
    bsz, l, _ = u.shape
    t1, a3, a3i, t2 = tabs
    n1, na, sb, su = _fft_dims(t1)
    nd = d // dt
    uo, go, ko = u_col // dt, xg_col // dt, kf_col // dt
    return pl.pallas_call(
        _fftconv_kernel,
        grid=(nd, bsz),
        in_specs=[pl.BlockSpec((None, l, dt), lambda j, b: (b, 0, j + uo)),
                  pl.BlockSpec((None, l, dt), lambda j, b: (b, 0, j + go)),
                  pl.BlockSpec((n1, 2 * n1, dt), lambda j, b: (0, 0, j + ko)),
                  pl.BlockSpec((1, dt), lambda j, b: (0, j)),
                  pl.BlockSpec(t1.shape, lambda j, b: (0, 0, 0)),
                  pl.BlockSpec(a3.shape, lambda j, b: (0, 0)),
                  pl.BlockSpec(a3i.shape, lambda j, b: (0, 0)),
                  pl.BlockSpec(t2.shape, lambda j, b: (0, 0, 0))],
        out_specs=pl.BlockSpec((None, l, dt), lambda j, b: (b, 0, j)),
        out_shape=jax.ShapeDtypeStruct((bsz, l, d), bf16),
        scratch_shapes=[pltpu.VMEM((na * su, dt), f32),
                        pltpu.VMEM((n1 * sb, dt), f32),
                        pltpu.VMEM((n1 * sb, dt), f32),
                        pltpu.VMEM((na * su, dt), f32)],
        compiler_params=_cp("parallel", "arbitrary"),
        name="hyena_fftconv",
    )(u, xg, kf, fbias.reshape(1, d), t1, a3, a3i, t2)


def _rope_tables(l):
    rows = l // GRID_W
    row = jnp.repeat(jnp.arange(rows), GRID_W)
    col = jnp.tile(jnp.arange(GRID_W), rows)
    inv = ROPE_BASE ** (-jnp.arange(ROPE_AXIS_PAIRS, dtype=f32) / ROPE_AXIS_PAIRS)
    ang = jnp.stack([row, col], axis=-1).astype(f32)[..., None] * inv
    ang = jnp.broadcast_to(ang[:, :, None, :], (l, 2, 2, ROPE_AXIS_PAIRS)).reshape(l, A_DQK)
    reps = A_QW // A_DQK
    return jnp.tile(jnp.cos(ang), (1, reps)), jnp.tile(jnp.sin(ang), (1, reps))


def _rotate_cols(w):
    j = np.arange(w.shape[1])
    lo = (j % (2 * ROPE_AXIS_PAIRS)) < ROPE_AXIS_PAIRS
    perm = np.where(lo, j + ROPE_AXIS_PAIRS, j - ROPE_AXIS_PAIRS)
    sign = np.where(lo, -1.0, 1.0).astype(np.float32)
    return w[:, perm] * sign


def _gate_cols(w_g, b_g):
    idx_i = np.array([d * 2 * B_HEADS + hd for d in range(2) for hd in range(B_HEADS)])
    idx_f = idx_i + B_HEADS
    pad = LANES - _NCHAIN
    k = w_g.shape[0]
    w = jnp.concatenate([w_g[:, idx_i], jnp.zeros((k, pad), f32),
                         w_g[:, idx_f], jnp.zeros((k, pad), f32)], axis=1)
    b = jnp.concatenate([b_g[idx_i], jnp.zeros((pad,), f32), b_g[idx_f], jnp.zeros((pad,), f32)])
    return w, b


def _hyena_consts(l, d):
    j = jnp.arange(l, dtype=f32)
    bands = (POS_EMB_DIM - 1) // 2
    freqs = jnp.linspace(1e-4, bands - 1, bands, dtype=f32)
    ang = (2.0 * math.pi / l) * j[:, None] * freqs[None, :]
    z = jnp.concatenate([(j / (l - 1))[:, None], jnp.cos(ang), -jnp.sin(ang)], axis=-1)
    dist = jnp.abs(j - l // 2) / (l // 2)
    max_decay = math.log(DECAY_TARGET) / DECAY_FAST_PCT
    min_decay = math.log(DECAY_TARGET) / DECAY_SLOW_PCT
    deltas = jnp.abs(jnp.linspace(min_decay, max_decay, d, dtype=f32))
    window = jnp.exp(-dist[:, None] * deltas[None, :])
    return z, window


def _ab_layer(x, ctx, mod_vecs, mod_ctx, norm_g, w_in, conv_w, conv_b, gate_b, lam_vecs,
              g_a, g_b, w_out, lam_init):
    sh_m, sc_m, g_m = mod_vecs
    bsz, s, d = x.shape
    h = _norm(x, norm_g[0], sh_m, sc_m)
    hc = _norm(ctx, norm_g[0], mod_ctx[0], mod_ctx[1])
    w = B_WIDTH
    o = 0
    cols = {}
    for name, width in (("aq", A_QW), ("bq", w), ("bo", w), ("ak", A_QW), ("av", A_VW),
                        ("bk", w), ("bv", w), ("g", 4 * B_HEADS)):
        cols[name] = w_in[:, o:o + width]
        o += width
    cos, sin = _rope_tables(s)
    cat = lambda *ws: jnp.concatenate(ws, axis=1).astype(bf16)
    q = _mm(h, cat(cols["aq"], _rotate_cols(cols["aq"])), rope=(cos, sin, A_DQK ** -0.5))
    k = _mm(h, cat(cols["ak"], _rotate_cols(cols["ak"])), rope=(cos, sin, 1.0))
    qk = _mm(h, cat(cols["bq"], cols["bk"]), conv=(conv_w, conv_b, True))
    vvo = _mm(h, cat(cols["av"], cols["bv"], cols["bo"]))
    wg, bg = _gate_cols(cols["g"], gate_b)
    gates = _mm(h, wg.astype(bf16), out_dtype=f32, bias=bg, tn=2 * LANES)
    ckv = _mm(hc, cat(cols["ak"], cols["av"], cols["bv"]))
    cbk = _mm(hc, cols["bk"].astype(bf16), conv=(conv_w[:, w:], conv_b[w:], True))
    cg = _mm(hc, wg.astype(bf16), out_dtype=f32, bias=bg, tn=2 * LANES)
    out_a = _attn(lam_vecs, q, k, vvo, ckv, g_a, lam_init)
    out_b = _mlstm(qk, vvo, gates, cbk, ckv, cg, g_b)
    wo = w_out.astype(bf16)
    return _out_proj([out_a, out_b], [wo[:A_VW], wo[A_VW:]], x, norm_g[1], g_m)


def _hyena_layer(x, mod_vecs, norm_g, w_in, conv_w, conv_b, fw1, fb1, fw2, fb2, fw3, fbias, w_out):
    sh_m, sc_m, g_m = mod_vecs
    bsz, l, d = x.shape
    h = _norm(x, norm_g[0], sh_m, sc_m)
    u = _mm(h, w_in.astype(bf16), conv=(conv_w, conv_b, False))
    z, window = _hyena_consts(l, d)
    pz, ph = LANES - z.shape[1], LANES - fw1.shape[1]
    filt = _filters(jnp.pad(z, ((0, 0), (0, pz))), jnp.pad(fw1, ((0, pz), (0, ph))),
                    jnp.pad(fb1, (0, ph)), jnp.pad(fw2, ((0, ph), (0, ph))), jnp.pad(fb2, (0, ph)),
                    jnp.pad(fw3, ((0, ph), (0, 0))), window)
    tabs = _dft_tables(l)
    kf = _spectrum(filt, tabs)
    zz = _fftconv(u, 0, u, d, kf, 0, fbias[0], tabs, d)
    y = _fftconv(zz, 0, u, 2 * d, kf, d, fbias[1], tabs, d)
    return _out_proj([y], [w_out.astype(bf16)], x, norm_g[1], g_m)


def kernel(x, c, ctx, c_ctx, w_mod, b_mod, norm_g, w_in_ab, conv_ab_w, conv_ab_b, gate_b_ab, diff_lambda, head_g_a, head_g_b, w_out_ab, w_in_hy, conv_hy_w, conv_hy_b, filt_w1, filt_b1, filt_w2, filt_b2, filt_w3, filt_bias, w_out_hy, router_w, router_b, exp_gu, exp_down, sh_gu, sh_down):
    bsz, s, d = x.shape
    depth = w_mod.shape[0]
    rows = -(-(bsz + 1) // 8) * 8
    cc = jnp.concatenate([c, c_ctx[None, :], jnp.zeros((rows - bsz - 1, d), f32)], axis=0)
    for l in range(depth):
        mod = _mod(cc, w_mod[l], b_mod[l])
        vec = lambda i: mod[:bsz, i * d:(i + 1) * d].reshape(bsz, 1, d)
        sh_m, sc_m, g_m, sh_f, sc_f, g_f = [vec(i) for i in range(6)]
        if l % 2 == 0:
            e = l // 2
            lam_init = 0.8 - 0.6 * math.exp(-0.3 * l)
            mod_ctx = (mod[bsz:bsz + 1, 0:d].reshape(1, 1, d), mod[bsz:bsz + 1, d:2 * d].reshape(1, 1, d))
            x = _ab_layer(x, ctx, (sh_m, sc_m, g_m), mod_ctx, norm_g[l], w_in_ab[e], conv_ab_w[e],
                          conv_ab_b[e], gate_b_ab[e], diff_lambda[e], head_g_a[e], head_g_b[e],
                          w_out_ab[e], lam_init)
        else:
            o = l // 2
            x = _hyena_layer(x, (sh_m, sc_m, g_m), norm_g[l], w_in_hy[o], conv_hy_w[o], conv_hy_b[o],
                             filt_w1[o], filt_b1[o], filt_w2[o], filt_b2[o], filt_w3[o], filt_bias[o],
                             w_out_hy[o])
        hx, rk, cmax = _router(x, norm_g[l, 2], sh_f, sc_f, router_w[l].T, router_b[l])
        x = _moe(hx, rk, cmax, exp_gu[l].astype(bf16), exp_down[l].astype(bf16), sh_gu[l].astype(bf16),
                 sh_down[l].astype(bf16), x, norm_g[l, 3], g_f)
    return x
```

```python
import functools
import math

import numpy as np
import jax
import jax.numpy as jnp
from jax import lax
from jax.experimental import pallas as pl
from jax.experimental.pallas import tpu as pltpu

f32 = jnp.float32
bf16 = jnp.bfloat16

RMS_EPS = 1e-6
A_HEADS = 4
A_DQK = 64
A_DV = 128
B_HEADS = 4
B_DH = 128
B_WIDTH = B_HEADS * B_DH
A_QW = A_HEADS * 2 * A_DQK
A_VW = A_HEADS * A_DV
GRID_W = 64
ROPE_BASE = 10000.0
ROPE_AXIS_PAIRS = A_DQK // 4
N_EXPERTS = 64
TOP_K = 8
N_GROUPS = 8
TOPK_GROUPS = 4
D_EXPERT = 256
ROUTED_SCALE = 2.5
HY_ORDER = 2
POS_EMB_DIM = 33
FILTER_SIN_W = 1.0
DECAY_FAST_PCT = 0.3
DECAY_SLOW_PCT = 1.5
DECAY_TARGET = 1e-2

LANES = 128
VMEM_LIMIT = 56 * 1024 * 1024
MLSTM_CHUNK = 256
FFT_PAD = 8
MOE_SUB = 256
MOE_WIN = 64
MOE_GROUP = 4


def _cp(*sem):
    return pltpu.CompilerParams(dimension_semantics=sem, vmem_limit_bytes=VMEM_LIMIT)


def _split_bf16(a):
    hi = a.astype(bf16)
    lo = (a - hi.astype(f32)).astype(bf16)
    return hi, lo


def _dot(a, b, dims=(((1,), (0,)), ((), ()))):
    return lax.dot_general(a, b, dims, preferred_element_type=f32)


_NT = (((1,), (1,)), ((), ()))
_TN = (((0,), (0,)), ((), ()))


def _dot3(a, b, dims=(((1,), (0,)), ((), ()))):
    ah, al = _split_bf16(a)
    bh, bl = _split_bf16(b)
    return _dot(ah, bh, dims) + (_dot(ah, bl, dims) + _dot(al, bh, dims))


def _silu(v):
    return v / (1.0 + jnp.exp(-v))


def _sigmoid(v):
    return 1.0 / (1.0 + jnp.exp(-v))


def _log_sigmoid(v):
    return jnp.minimum(v, 0.0) - jnp.log(1.0 + jnp.exp(-jnp.abs(v)))


def _mod_kernel(c_ref, w_ref, b_ref, o_ref):
    o_ref[...] = _dot3(_silu(c_ref[...]), w_ref[...]) + b_ref[...]


def _mod(cc, w, b):
    rows, d = cc.shape
    n = w.shape[1]
    tn = d
    return pl.pallas_call(
        _mod_kernel,
        grid=(n // tn,),
        in_specs=[pl.BlockSpec((rows, d), lambda j: (0, 0)),
                  pl.BlockSpec((d, tn), lambda j: (0, j)),
                  pl.BlockSpec((1, tn), lambda j: (0, j))],
        out_specs=pl.BlockSpec((rows, tn), lambda j: (0, j)),
        out_shape=jax.ShapeDtypeStruct((rows, n), f32),
        compiler_params=_cp("arbitrary"),
        name="mod",
    )(cc, w, b.reshape(1, n))


def _norm_mod(xv, g, shift, scale):
    y = xv * lax.rsqrt(jnp.mean(xv * xv, axis=-1, keepdims=True) + RMS_EPS)
    return (y * g) * (1.0 + scale) + shift


def _norm_kernel(x_ref, g_ref, sh_ref, sc_ref, o_ref):
    o_ref[...] = _norm_mod(x_ref[...], g_ref[...], sh_ref[...], sc_ref[...]).astype(o_ref.dtype)


def _bidx(arr):
    if arr.shape[0] == 1:
        return lambda b, *_: (0, 0, 0)
    return lambda b, *_: (b, 0, 0)


def _norm(x, g, shift, scale, tl=512):
    bsz, l, d = x.shape
    tl = min(tl, l)
    return pl.pallas_call(
        _norm_kernel,
        grid=(bsz, l // tl),
        in_specs=[pl.BlockSpec((None, tl, d), lambda b, i: (b, i, 0)),
                  pl.BlockSpec((1, d), lambda b, i: (0, 0)),
                  pl.BlockSpec((None, 1, d), _bidx(shift)),
                  pl.BlockSpec((None, 1, d), _bidx(scale))],
        out_specs=pl.BlockSpec((None, tl, d), lambda b, i: (b, i, 0)),
        out_shape=jax.ShapeDtypeStruct((bsz, l, d), bf16),
        compiler_params=_cp("parallel", "parallel"),
        name="norm",
    )(x, g.reshape(1, d), shift, scale)


def _mm_plain_kernel(h_ref, w_ref, b_ref, o_ref):
    o_ref[...] = (_dot(h_ref[...], w_ref[...]) + b_ref[...]).astype(o_ref.dtype)


def _mm_rope_kernel(h_ref, w_ref, cos_ref, sin_ref, o_ref, *, scale):
    p = _dot(h_ref[...], w_ref[...])
    n = o_ref.shape[-1]
    o_ref[...] = ((p[:, :n] * cos_ref[...] + p[:, n:] * sin_ref[...]) * scale).astype(o_ref.dtype)


def _mm_conv_kernel(h_ref, w_ref, cw_ref, cb_ref, o_ref, *, act):
    p = _dot(h_ref[...], w_ref[...])
    l = p.shape[0]
    row = lax.broadcasted_iota(jnp.int32, p.shape, 0)
    prev = jnp.where(row == 0, 0.0, pltpu.roll(p, 1, 0))
    nxt = jnp.where(row == l - 1, 0.0, pltpu.roll(p, l - 1, 0))
    y = prev * cw_ref[0:1, :] + p * cw_ref[1:2, :] + nxt * cw_ref[2:3, :] + cb_ref[...]
    if act:
        y = _silu(y)
    o_ref[...] = y.astype(o_ref.dtype)


def _mm(h, w, *, out_dtype=bf16, bias=None, rope=None, conv=None, tl=512, tn=512):
    bsz, l, k = h.shape
    n = w.shape[1]
    if rope is not None:
        cos, sin, scale = rope
        n_out = n // 2
        tl = min(tl, l)
        return pl.pallas_call(
            functools.partial(_mm_rope_kernel, scale=scale),
            grid=(bsz, l // tl),
            in_specs=[pl.BlockSpec((None, tl, k), lambda b, i: (b, i, 0)),
                      pl.BlockSpec((k, n), lambda b, i: (0, 0)),
                      pl.BlockSpec((tl, n_out), lambda b, i: (i, 0)),
                      pl.BlockSpec((tl, n_out), lambda b, i: (i, 0))],
            out_specs=pl.BlockSpec((None, tl, n_out), lambda b, i: (b, i, 0)),
            out_shape=jax.ShapeDtypeStruct((bsz, l, n_out), out_dtype),
            compiler_params=_cp("parallel", "parallel"),
            name="mm_rope",
        )(h, w, cos, sin)
    tn = min(tn, n)
    if conv is not None:
        cw, cb, act = conv
        return pl.pallas_call(
            functools.partial(_mm_conv_kernel, act=act),
            grid=(bsz, n // tn),
            in_specs=[pl.BlockSpec((None, l, k), lambda b, j: (b, 0, 0)),
                      pl.BlockSpec((k, tn), lambda b, j: (0, j)),
                      pl.BlockSpec((3, tn), lambda b, j: (0, j)),
                      pl.BlockSpec((1, tn), lambda b, j: (0, j))],
            out_specs=pl.BlockSpec((None, l, tn), lambda b, j: (b, 0, j)),
            out_shape=jax.ShapeDtypeStruct((bsz, l, n), out_dtype),
            compiler_params=_cp("parallel", "arbitrary"),
            name="mm_conv",
        )(h, w, cw, cb.reshape(1, n))
    if bias is None:
        bias = jnp.zeros((n,), f32)
    tl = min(tl, l)
    return pl.pallas_call(
        _mm_plain_kernel,
        grid=(bsz, l // tl, n // tn),
        in_specs=[pl.BlockSpec((None, tl, k), lambda b, i, j: (b, i, 0)),
                  pl.BlockSpec((k, tn), lambda b, i, j: (0, j)),
                  pl.BlockSpec((1, tn), lambda b, i, j: (0, j))],
        out_specs=pl.BlockSpec((None, tl, tn), lambda b, i, j: (b, i, j)),
        out_shape=jax.ShapeDtypeStruct((bsz, l, n), out_dtype),
        compiler_params=_cp("parallel", "parallel", "arbitrary"),
        name="mm_plain",
    )(h, w, bias.reshape(1, n))


def _attn_kernel(lv_ref, q_ref, kc_ref, k_ref, vc_ref, v_ref, g_ref, o_ref, *, lam_init):
    tq = q_ref.shape[0]
    lv = lv_ref[...]
    lam = (jnp.exp(jnp.sum(lv[0:1] * lv[1:2], axis=1, keepdims=True))
           - jnp.exp(jnp.sum(lv[2:3] * lv[3:4], axis=1, keepdims=True)) + lam_init)
    first = lax.broadcasted_iota(jnp.int32, (tq, A_DV), 1) < A_DQK
    for hd in range(A_HEADS):
        cs = slice(hd * A_DV, (hd + 1) * A_DV)
        qh = q_ref[:, cs]
        zero = jnp.zeros_like(qh)
        q2 = jnp.concatenate([jnp.where(first, qh, zero), jnp.where(first, zero, qh)], axis=0)
        s_c = _dot(q2, kc_ref[:, cs], _NT)
        s_l = _dot(q2, k_ref[:, cs], _NT)
        m = jnp.maximum(jnp.max(s_c, axis=1, keepdims=True), jnp.max(s_l, axis=1, keepdims=True))
        p_c = jnp.exp(s_c - m)
        p_l = jnp.exp(s_l - m)
        inv = 1.0 / (jnp.sum(p_c, axis=1, keepdims=True) + jnp.sum(p_l, axis=1, keepdims=True))
        w0 = inv[:tq]
        w1 = inv[tq:] * lam
        a_c = (p_c[:tq] * w0 - p_c[tq:] * w1).astype(bf16)
        a_l = (p_l[:tq] * w0 - p_l[tq:] * w1).astype(bf16)
        o = _dot(a_c, vc_ref[:, cs]) + _dot(a_l, v_ref[:, cs])
        o = o * lax.rsqrt(jnp.mean(o * o, axis=1, keepdims=True) + RMS_EPS)
        o_ref[:, cs] = (o * g_ref[:, cs] * (1.0 - lam_init)).astype(o_ref.dtype)


def _attn(lv, q, k, vvo, ckv, g_a, lam_init, tq=256):
    bsz, s, _ = q.shape
    lc = ckv.shape[1]
    tq = min(tq, s)
    w = A_QW
    return pl.pallas_call(
        functools.partial(_attn_kernel, lam_init=lam_init),
        grid=(bsz, s // tq),
        in_specs=[pl.BlockSpec(lv.shape, lambda b, i: (0, 0)),
                  pl.BlockSpec((None, tq, w), lambda b, i: (b, i, 0)),
                  pl.BlockSpec((None, lc, w), lambda b, i: (b, 0, 0)),
                  pl.BlockSpec((None, s, w), lambda b, i: (b, 0, 0)),
                  pl.BlockSpec((None, lc, w), lambda b, i: (b, 0, 1)),
                  pl.BlockSpec((None, s, w), lambda b, i: (b, 0, 0)),
                  pl.BlockSpec((1, w), lambda b, i: (0, 0))],
        out_specs=pl.BlockSpec((None, tq, w), lambda b, i: (b, i, 0)),
        out_shape=jax.ShapeDtypeStruct((bsz, s, w), bf16),
        compiler_params=_cp("parallel", "arbitrary"),
        name="diff_attn",
    )(lv, q, ckv, k, ckv, vvo, g_a.reshape(1, w))


_LN_QSCALE = math.log(B_DH ** -0.5)
_NCHAIN = 2 * B_HEADS


def _chunk_gate_sums(gi, gf, tri):
    lf = _log_sigmoid(gf)
    hi, lo = _split_bf16(lf)
    cum = _dot(tri, hi) + _dot(tri, lo)
    t = gf.shape[0]
    tot = cum[t - 1:t, :]
    rcum = tot - cum + lf
    fwd = lax.broadcasted_iota(jnp.int32, gf.shape, 1) < B_HEADS
    bd = jnp.where(fwd, cum, rcum)
    return bd, tot, (bd - gi).T


def _lower_tri(t):
    r = lax.broadcasted_iota(jnp.int32, (t, t), 0)
    c = lax.broadcasted_iota(jnp.int32, (t, t), 1)
    return r, c


def _absorb(c_ref, n_ref, m_ref, ch, bcol, tot_c, gi_c, kf, vb):
    m_prev = m_ref[ch][:, 0:1]
    g = tot_c - bcol + gi_c
    m_new = jnp.maximum(tot_c + m_prev, jnp.max(g, axis=0, keepdims=True))
    wgt = jnp.exp(g - m_new)
    decay = jnp.exp(tot_c + m_prev - m_new)
    kw = kf * wgt
    c_ref[ch] = decay * c_ref[ch] + _dot(kw.astype(bf16), vb, _TN)
    n_ref[ch] = decay * n_ref[ch] + jnp.sum(kw, axis=0, keepdims=True)
    m_ref[ch] = jnp.broadcast_to(m_new, m_ref.shape[1:])


def _mlstm_kernel(qk_ref, vvo_ref, g_ref, ck_ref, ckv_ref, cg_ref, gb_ref, o_ref,
                  hf_ref, hb_ref, c_ref, n_ref, m_ref, *, tc):
    s = o_ref.shape[0]
    lc = ck_ref.shape[0]
    nc = s // tc
    w = B_WIDTH
    dh = B_DH

    c_ref[...] = jnp.zeros_like(c_ref)
    n_ref[...] = jnp.zeros_like(n_ref)
    m_ref[...] = jnp.zeros_like(m_ref)

    r, cidx = _lower_tri(lc)
    tri_c = jnp.where(cidx <= r, 1.0, 0.0).astype(bf16)
    cg = cg_ref[...]
    bd, tot, _ = _chunk_gate_sums(cg[:, :LANES], cg[:, LANES:], tri_c)
    gi = cg[:, :LANES]
    for ch in range(_NCHAIN):
        hs = slice((ch % B_HEADS) * dh, (ch % B_HEADS + 1) * dh)
        vs = slice(2 * w + (ch % B_HEADS) * dh, 2 * w + (ch % B_HEADS + 1) * dh)
        _absorb(c_ref, n_ref, m_ref, ch, bd[:, ch:ch + 1], tot[:, ch:ch + 1], gi[:, ch:ch + 1],
                ck_ref[:, hs].astype(f32), ckv_ref[:, vs])

    r, cidx = _lower_tri(tc)
    tri = jnp.where(cidx <= r, 1.0, 0.0).astype(bf16)
    causal = cidx <= r
    anti = cidx >= r

    def step(i, carry):
        for d in range(2):
            row0 = pl.multiple_of((i if d == 0 else nc - 1 - i) * tc, tc)
            rows = pl.ds(row0, tc)
            gch = g_ref[rows, :]
            gi = gch[:, :LANES]
            bd, tot, xt = _chunk_gate_sums(gi, gch[:, LANES:], tri)
            mask = causal if d == 0 else anti
            dst = hf_ref if d == 0 else hb_ref
            for hd in range(B_HEADS):
                ch = d * B_HEADS + hd
                hs = slice(hd * dh, (hd + 1) * dh)
                qb = qk_ref[rows, hs]
                kb = qk_ref[rows, slice(w + hd * dh, w + (hd + 1) * dh)]
                vb = vvo_ref[rows, slice(w + hd * dh, w + (hd + 1) * dh)]
                bcol = bd[:, ch:ch + 1]
                dmat = jnp.where(mask, bcol - xt[ch:ch + 1, :], -jnp.inf)
                m_prev = m_ref[ch][:, 0:1]
                inter = bcol + m_prev
                m_t = jnp.maximum(inter, jnp.max(dmat, axis=1, keepdims=True))
                e = jnp.exp(dmat - m_t + _LN_QSCALE)
                smat = _dot(qb, kb, _NT) * e
                sc = jnp.exp(inter - m_t + _LN_QSCALE)
                num = sc * _dot(qb, c_ref[ch].astype(bf16)) + _dot(smat.astype(bf16), vb)
                qn = jnp.sum(qb.astype(f32) * n_ref[ch], axis=1, keepdims=True)
                den = sc * qn + jnp.sum(smat, axis=1, keepdims=True)
                hout = num * (1.0 / jnp.maximum(jnp.abs(den), jnp.exp(-m_t)))
                dst[rows, hs] = hout
                _absorb(c_ref, n_ref, m_ref, ch, bcol, tot[:, ch:ch + 1], gi[:, ch:ch + 1],
                        kb.astype(f32), vb)
        return carry

    lax.fori_loop(0, nc, step, 0)

    for hd in range(B_HEADS):
        hs = slice(hd * dh, (hd + 1) * dh)
        hsum = hf_ref[:, hs] + hb_ref[:, hs]
        hn = hsum * lax.rsqrt(jnp.mean(hsum * hsum, axis=1, keepdims=True) + RMS_EPS)
        og = _sigmoid(vvo_ref[:, slice(2 * w + hd * dh, 2 * w + (hd + 1) * dh)].astype(f32))
        o_ref[:, hs] = (hn * gb_ref[:, hs] * og).astype(o_ref.dtype)


def _mlstm(qk, vvo, gates, cbk, ckv, cg, g_b):
    bsz, s, _ = qk.shape
    lc = cbk.shape[1]
    w = B_WIDTH
    tc = min(MLSTM_CHUNK, s)
    return pl.pallas_call(
        functools.partial(_mlstm_kernel, tc=tc),
        grid=(bsz,),
        in_specs=[pl.BlockSpec((None, s, 2 * w), lambda b: (b, 0, 0)),
                  pl.BlockSpec((None, s, 3 * w), lambda b: (b, 0, 0)),
                  pl.BlockSpec((None, s, 2 * LANES), lambda b: (b, 0, 0)),
                  pl.BlockSpec((None, lc, w), lambda b: (b, 0, 0)),
                  pl.BlockSpec((None, lc, 3 * w), lambda b: (b, 0, 0)),
                  pl.BlockSpec((None, lc, 2 * LANES), lambda b: (b, 0, 0)),
                  pl.BlockSpec((1, w), lambda b: (0, 0))],
        out_specs=pl.BlockSpec((None, s, w), lambda b: (b, 0, 0)),
        out_shape=jax.ShapeDtypeStruct((bsz, s, w), bf16),
        scratch_shapes=[pltpu.VMEM((s, w), f32), pltpu.VMEM((s, w), f32),
                        pltpu.VMEM((_NCHAIN, B_DH, B_DH), f32),
                        pltpu.VMEM((_NCHAIN, 1, B_DH), f32),
                        pltpu.VMEM((_NCHAIN, 1, LANES), f32)],
        compiler_params=_cp("arbitrary"),
        name="mlstm",
    )(qk, vvo, gates, cbk, ckv, cg, g_b.reshape(1, w))


def _out_kernel(*refs, n_act):
    acts = refs[:n_act]
    ws = refs[n_act:2 * n_act]
    x_ref, g_ref, gate_ref, o_ref = refs[2 * n_act:]
    mix = _dot(acts[0][...], ws[0][...])
    for a, wr in zip(acts[1:], ws[1:]):
        mix = mix + _dot(a[...], wr[...])
    y = mix * lax.rsqrt(jnp.mean(mix * mix, axis=-1, keepdims=True) + RMS_EPS) * g_ref[...]
    o_ref[...] = x_ref[...] + gate_ref[...] * y


def _out_proj(acts, ws, x, g, gate, tl=512):
    bsz, l, d = x.shape
    tl = min(tl, l)
    n_act = len(acts)
    in_specs = [pl.BlockSpec((None, tl, a.shape[2]), lambda b, i: (b, i, 0)) for a in acts]
    in_specs += [pl.BlockSpec(wm.shape, lambda b, i: (0, 0)) for wm in ws]
    in_specs += [pl.BlockSpec((None, tl, d), lambda b, i: (b, i, 0)),
                 pl.BlockSpec((1, d), lambda b, i: (0, 0)),
                 pl.BlockSpec((None, 1, d), _bidx(gate))]
    return pl.pallas_call(
        functools.partial(_out_kernel, n_act=n_act),
        grid=(bsz, l // tl),
        in_specs=in_specs,
        out_specs=pl.BlockSpec((None, tl, d), lambda b, i: (b, i, 0)),
        out_shape=jax.ShapeDtypeStruct((bsz, l, d), f32),
        compiler_params=_cp("parallel", "parallel"),
        name="out_proj",
    )(*acts, *ws, x, g.reshape(1, d), gate)


def _router_kernel(x_ref, g_ref, sh_ref, sc_ref, rw_ref, rb_ref, h_ref, rk_ref, gt_ref, cm_ref):
    hf = _norm_mod(x_ref[...], g_ref[...], sh_ref[...], sc_ref[...])
    tl = hf.shape[0]
    h_ref[...] = hf.astype(h_ref.dtype)
    per = N_EXPERTS // N_GROUPS
    logits = _dot3(rw_ref[...], hf, _NT)
    s3 = _sigmoid(logits).reshape(N_GROUPS, per, tl)
    b3 = s3 + rb_ref[...].reshape(N_GROUPS, per, 1)
    neg = -jnp.inf
    jdx = lax.broadcasted_iota(jnp.int32, b3.shape, 1)
    gdx = lax.broadcasted_iota(jnp.int32, b3.shape, 0)
    m1 = jnp.max(b3, axis=1, keepdims=True)
    f1 = jnp.min(jnp.where(b3 == m1, jdx, per), axis=1, keepdims=True)
    m2 = jnp.max(jnp.where(jdx == f1, neg, b3), axis=1, keepdims=True)
    grp = m1 + m2
    g1 = lax.broadcasted_iota(jnp.int32, grp.shape, 0)
    cnt = jnp.zeros(grp.shape, jnp.int32)
    for gp in range(N_GROUPS):
        rv = grp[gp:gp + 1]
        ahead = jnp.where(rv > grp, 1, jnp.where(rv == grp, jnp.where(g1 > gp, 1, 0), 0))
        cnt = cnt + ahead
    v = jnp.where(cnt < TOPK_GROUPS, b3, neg)
    eidx = gdx * per + jdx
    sel = jnp.zeros(b3.shape, f32)
    for _ in range(TOP_K):
        m = jnp.max(jnp.max(v, axis=1, keepdims=True), axis=0, keepdims=True)
        cand = jnp.where(v == m, eidx, N_EXPERTS)
        fi = jnp.min(jnp.min(cand, axis=1, keepdims=True), axis=0, keepdims=True)
        hit = eidx == fi
        sel = jnp.where(hit, 1.0, sel)
        v = jnp.where(hit, neg, v)
    ssel = sel * s3
    den = jnp.sum(jnp.sum(ssel, axis=1, keepdims=True), axis=0, keepdims=True)
    gt_ref[...] = ((ROUTED_SCALE * ssel) / den).reshape(N_EXPERTS, tl)
    sel2 = sel.reshape(N_EXPERTS, tl)
    r = lax.broadcasted_iota(jnp.int32, (MOE_SUB, MOE_SUB), 0)
    c = lax.broadcasted_iota(jnp.int32, (MOE_SUB, MOE_SUB), 1)
    before = jnp.where(r < c, 1.0, 0.0).astype(bf16)
    cmax = jnp.zeros((N_EXPERTS, 1), f32)
    for j in range(tl // MOE_SUB):
        sub = sel2[:, j * MOE_SUB:(j + 1) * MOE_SUB]
        rank = _dot(sub.astype(bf16), before)
        rk_ref[:, j * MOE_SUB:(j + 1) * MOE_SUB] = jnp.where(sub > 0.0, rank, -1.0)
        cmax = jnp.maximum(cmax, jnp.sum(sub, axis=1, keepdims=True))
    cm_ref[...] = jnp.broadcast_to(cmax, cm_ref.shape)


def _router(x, g, shift, scale, rw_t, rb, tl=512):
    bsz, l, d = x.shape
    tl = min(tl, l)
    nl = l // tl
    return pl.pallas_call(
        _router_kernel,
        grid=(bsz, nl),
        in_specs=[pl.BlockSpec((None, tl, d), lambda b, i: (b, i, 0)),
                  pl.BlockSpec((1, d), lambda b, i: (0, 0)),
                  pl.BlockSpec((None, 1, d), _bidx(shift)),
                  pl.BlockSpec((None, 1, d), _bidx(scale)),
                  pl.BlockSpec((N_EXPERTS, d), lambda b, i: (0, 0)),
                  pl.BlockSpec((N_EXPERTS, 1), lambda b, i: (0, 0))],
        out_specs=[pl.BlockSpec((None, tl, d), lambda b, i: (b, i, 0)),
                   pl.BlockSpec((N_EXPERTS, tl), lambda b, i: (0, b * nl + i)),
                   pl.BlockSpec((N_EXPERTS, tl), lambda b, i: (0, b * nl + i)),
                   pl.BlockSpec((None, N_EXPERTS, LANES), lambda b, i: (b * nl + i, 0, 0))],
        out_shape=[jax.ShapeDtypeStruct((bsz, l, d), bf16),
                   jax.ShapeDtypeStruct((N_EXPERTS, bsz * l), f32),
                   jax.ShapeDtypeStruct((N_EXPERTS, bsz * l), f32),
                   jax.ShapeDtypeStruct((bsz * nl, N_EXPERTS, LANES), f32)],
        compiler_params=_cp("parallel", "parallel"),
        name="router",
    )(x, g.reshape(1, d), shift, scale, rw_t, rb.reshape(N_EXPERTS, 1))


def _swiglu_act(hh):
    half = hh.shape[1] // 2
    return _silu(hh[:, :half]) * hh[:, half:]


def _moe_kernel(cnt_ref, h_ref, rk_ref, gt_ref, gu_ref, dn_ref, sgu_ref, sdn_ref, x_ref, g_ref, gate_ref,
                o_ref, acc_ref, xg_ref, ys_ref, p_ref, gr_ref):
    tile = pl.program_id(0)
    grp = pl.program_id(1)
    tm, d = acc_ref.shape
    ns = tm // MOE_SUB
    win = MOE_WIN
    ng = MOE_GROUP

    @pl.when(grp == 0)
    def _():
        act = _swiglu_act(_dot(h_ref[...], sgu_ref[...]))
        acc_ref[...] = _dot(act.astype(bf16), sdn_ref[...])

    riota = lax.broadcasted_iota(jnp.int32, (win, MOE_SUB), 0).astype(f32)

    def expert_ffn(el):
        hh = _dot(xg_ref[el], gu_ref[el])
        gr = gr_ref[el]
        act = _swiglu_act(hh) * jnp.concatenate([gr] * (hh.shape[1] // (2 * LANES)), axis=1)
        y = _dot(act.astype(bf16), dn_ref[el]).astype(bf16)
        for s in range(ns):
            ys_ref[s, el * win:(el + 1) * win, :] = y[s * win:(s + 1) * win]

    def one_pass(p, skip_idle):
        base = p * win
        for s in range(ns):
            cols = slice(s * MOE_SUB, (s + 1) * MOE_SUB)
            onehots = []
            for el in range(ng):
                row = pl.ds(grp * ng + el, 1)
                hit = (rk_ref[row, cols] - base) == riota
                onehots.append(jnp.where(hit, 1.0, 0.0).astype(bf16))
                gsel = jnp.sum(jnp.where(hit, gt_ref[row, cols], 0.0), axis=1, keepdims=True)
                gr_ref[el, s * win:(s + 1) * win, :] = jnp.broadcast_to(gsel, (win, LANES))
            pm = jnp.concatenate(onehots, axis=0)
            p_ref[s] = pm
            gx = _dot(pm, h_ref[cols, :])
            for el in range(ng):
                xg_ref[el, s * win:(s + 1) * win, :] = gx[el * win:(el + 1) * win].astype(bf16)
        for el in range(ng):
            if not skip_idle:
                expert_ffn(el)
                continue
            busy = cnt_ref[tile, grp * ng + el] > base
            pl.when(busy)(functools.partial(expert_ffn, el))

            @pl.when(jnp.logical_not(busy))
            def _():
                for s in range(ns):
                    ys_ref[s, el * win:(el + 1) * win, :] = jnp.zeros((win, d), bf16)
        for s in range(ns):
            acc_ref[s * MOE_SUB:(s + 1) * MOE_SUB, :] += _dot(p_ref[s], ys_ref[s], _TN)

    one_pass(0, False)
    most = cnt_ref[tile, grp * ng]
    for el in range(1, ng):
        most = jnp.maximum(most, cnt_ref[tile, grp * ng + el])

    def later_pass(p, carry):
        one_pass(p, True)
        return carry

    lax.fori_loop(1, (most + win - 1) // win, later_pass, 0)

    @pl.when(grp == pl.num_programs(1) - 1)
    def _():
        mo = acc_ref[...]
        y = mo * lax.rsqrt(jnp.mean(mo * mo, axis=-1, keepdims=True) + RMS_EPS) * g_ref[...]
        o_ref[...] = x_ref[...] + gate_ref[...] * y


def _moe(h2, rk, gt, cmax, gu, dn, sgu, sdn, x, g, gate, tm=1024):
    bsz, l, d = x.shape
    tm = min(tm, l)
    per_b = l // tm
    nt = bsz * per_b
    ne = gu.shape[0]
    ng = MOE_GROUP
    ns = tm // MOE_SUB
    counts = jnp.max(cmax[:, :, 0].reshape(nt, -1, ne), axis=1).astype(jnp.int32)
    grid_spec = pltpu.PrefetchScalarGridSpec(
        num_scalar_prefetch=1,
        grid=(nt, ne // ng),
        in_specs=[pl.BlockSpec((tm, d), lambda t, e, n: (t, 0)),
                  pl.BlockSpec((ne, tm), lambda t, e, n: (0, t)),
                  pl.BlockSpec((ne, tm), lambda t, e, n: (0, t)),
                  pl.BlockSpec((ng,) + gu.shape[1:], lambda t, e, n: (e, 0, 0)),
                  pl.BlockSpec((ng,) + dn.shape[1:], lambda t, e, n: (e, 0, 0)),
                  pl.BlockSpec(sgu.shape, lambda t, e, n: (0, 0)),
                  pl.BlockSpec(sdn.shape, lambda t, e, n: (0, 0)),
                  pl.BlockSpec((tm, d), lambda t, e, n: (t, 0)),
                  pl.BlockSpec((1, d), lambda t, e, n: (0, 0)),
                  pl.BlockSpec((None, 1, d), lambda t, e, n: (t // per_b, 0, 0))],
        out_specs=pl.BlockSpec((tm, d), lambda t, e, n: (t, 0)),
        scratch_shapes=[pltpu.VMEM((tm, d), f32),
                        pltpu.VMEM((ng, ns * MOE_WIN, d), bf16),
                        pltpu.VMEM((ns, ng * MOE_WIN, d), bf16),
                        pltpu.VMEM((ns, ng * MOE_WIN, MOE_SUB), bf16),
                        pltpu.VMEM((ng, ns * MOE_WIN, LANES), f32)])
    out = pl.pallas_call(
        _moe_kernel,
        grid_spec=grid_spec,
        out_shape=jax.ShapeDtypeStruct((bsz * l, d), f32),
        compiler_params=_cp("parallel", "arbitrary"),
        name="moe",
    )(counts, h2.reshape(bsz * l, d), rk, gt, gu, dn, sgu, sdn, x.reshape(bsz * l, d),
      g.reshape(1, d), gate)
    return out.reshape(bsz, l, d)


def _filter_kernel(z_ref, w1_ref, b1_ref, w2_ref, b2_ref, w3_ref, win_ref, o_ref):
    hid = jnp.sin(FILTER_SIN_W * (_dot3(z_ref[...], w1_ref[...]) + b1_ref[...]))
    hid = jnp.sin(FILTER_SIN_W * (_dot3(hid, w2_ref[...]) + b2_ref[...]))
    o_ref[...] = _dot3(hid, w3_ref[...]) * win_ref[...]


def _filters(z, w1, b1, w2, b2, w3, window, tn=512):
    l, p = z.shape
    hdim = w1.shape[1]
    n = w3.shape[1]
    d = window.shape[1]
    nd = d // tn
    return pl.pallas_call(
        _filter_kernel,
        grid=(n // tn,),
        in_specs=[pl.BlockSpec((l, p), lambda j: (0, 0)),
                  pl.BlockSpec((p, hdim), lambda j: (0, 0)),
                  pl.BlockSpec((1, hdim), lambda j: (0, 0)),
                  pl.BlockSpec((hdim, hdim), lambda j: (0, 0)),
                  pl.BlockSpec((1, hdim), lambda j: (0, 0)),
                  pl.BlockSpec((hdim, tn), lambda j: (0, j)),
                  pl.BlockSpec((l, tn), lambda j: (0, j % nd))],
        out_specs=pl.BlockSpec((l, tn), lambda j: (0, j)),
        out_shape=jax.ShapeDtypeStruct((l, n), f32),
        compiler_params=_cp("arbitrary"),
        name="hyena_filter",
    )(z, w1, b1.reshape(1, hdim), w2, b2.reshape(1, hdim), w3, window)


def _dft_tables(l):
    n = 2 * l
    n1 = math.isqrt(n)
    assert n == n1 * n1 and n1 % 16 == 0
    na = l // n1
    a = np.arange(na)
    b = np.arange(n1)
    c = np.arange(n1)
    th = 2.0 * np.pi * ((n1 * a[None, None, :] + b[:, None, None]) * c[None, :, None]) / n
    t1 = np.concatenate([np.cos(th), -np.sin(th)], axis=1)
    ph = 2.0 * np.pi * (b[:, None] * b[None, :]) / n1
    cs, sn = np.cos(ph), np.sin(ph)
    a3 = np.block([[cs, sn], [-sn, cs]])
    a3i = np.block([[cs, -sn], [sn, cs]])
    a2 = np.arange(na) + na // 2
    th2 = 2.0 * np.pi * ((n1 * a2[None, :, None] + b[:, None, None]) * c[None, None, :]) / n
    t2 = np.concatenate([np.cos(th2), -np.sin(th2)], axis=2)
    return [jnp.asarray(t, f32).astype(bf16) for t in (t1, a3, a3i, t2)]


def _fft_dims(t1):
    n1, _, na = t1.shape
    return n1, na, 2 * n1 + FFT_PAD, n1 + FFT_PAD


def _dft_forward(uf_ref, t1_ref, zs_ref):
    n1, na, sb, su = _fft_dims(t1_ref)
    for b in range(n1):
        ub = uf_ref[pl.ds(b, na, stride=su), :].astype(bf16)
        zb = _dot(t1_ref[b], ub)
        zs_ref[pl.ds(b, n1, stride=sb), :] = zb[:n1]
        zs_ref[pl.ds(n1 + b, n1, stride=sb), :] = zb[n1:]


def _spectrum_kernel(f_ref, t1_ref, a3_ref, o_ref, uf_ref, zs_ref, *, scale):
    n1, na, sb, su = _fft_dims(t1_ref)
    for a in range(na):
        uf_ref[pl.ds(a * su, n1), :] = f_ref[pl.ds(a * n1, n1), :]
    _dft_forward(uf_ref, t1_ref, zs_ref)
    a3 = a3_ref[...]
    for c in range(n1):
        zc = zs_ref[pl.ds(c * sb, 2 * n1), :].astype(bf16)
        o_ref[c] = _dot(a3, zc) * scale


def _spectrum(filt, tabs, dt=128):
    l, n = filt.shape
    t1, a3, _, _ = tabs
    n1, na, sb, su = _fft_dims(t1)
    return pl.pallas_call(
        functools.partial(_spectrum_kernel, scale=1.0 / (2 * l)),
        grid=(n // dt,),
        in_specs=[pl.BlockSpec((l, dt), lambda j: (0, j)),
                  pl.BlockSpec(t1.shape, lambda j: (0, 0, 0)),
                  pl.BlockSpec(a3.shape, lambda j: (0, 0))],
        out_specs=pl.BlockSpec((n1, 2 * n1, dt), lambda j: (0, 0, j)),
        out_shape=jax.ShapeDtypeStruct((n1, 2 * n1, n), f32),
        scratch_shapes=[pltpu.VMEM((na * su, dt), f32),
                        pltpu.VMEM((n1 * sb, dt), f32)],
        compiler_params=_cp("arbitrary"),
        name="hyena_spectrum",
    )(filt, t1, a3)


def _fftconv_kernel(u_ref, xg_ref, kf_ref, fb_ref, t1_ref, a3_ref, a3i_ref, t2_ref, o_ref,
                    uf_ref, zs_ref, qs_ref, y_ref):
    n1, na, sb, su = _fft_dims(t1_ref)
    for a in range(na):
        uf_ref[pl.ds(a * su, n1), :] = u_ref[pl.ds(a * n1, n1), :].astype(f32)
    _dft_forward(uf_ref, t1_ref, zs_ref)
    a3 = a3_ref[...]
    a3i = a3i_ref[...]
    for c in range(n1):
        zc = zs_ref[pl.ds(c * sb, 2 * n1), :].astype(bf16)
        xc = _dot(a3, zc)
        kc = kf_ref[c]
        xr, xi = xc[:n1], xc[n1:]
        kr, ki = kc[:n1], kc[n1:]
        pc = jnp.concatenate([xr * kr - xi * ki, xr * ki + xi * kr], axis=0).astype(bf16)
        qc = _dot(a3i, pc)
        qs_ref[pl.ds(c, n1, stride=sb), :] = qc[:n1]
        qs_ref[pl.ds(n1 + c, n1, stride=sb), :] = qc[n1:]
    for b in range(n1):
        qb = qs_ref[pl.ds(b * sb, 2 * n1), :].astype(bf16)
        y_ref[pl.ds(b, na, stride=su), :] = _dot(t2_ref[b], qb)
    fb = fb_ref[...]
    for a in range(na):
        rows = pl.ds(a * n1, n1)
        uv = uf_ref[pl.ds(a * su, n1), :]
        yv = y_ref[pl.ds(a * su, n1), :]
        o_ref[rows, :] = (xg_ref[rows, :].astype(f32) * (yv + uv * fb)).astype(o_ref.dtype)


def _fftconv(u, u_col, xg, xg_col, kf, kf_col, fbias, tabs, d, dt=128):
    bsz, l, _ = u.shape
    t1, a3, a3i, t2 = tabs
    n1, na, sb, su = _fft_dims(t1)
    nd = d // dt
    uo, go, ko = u_col // dt, xg_col // dt, kf_col // dt
    return pl.pallas_call(
        _fftconv_kernel,
        grid=(nd, bsz),
        in_specs=[pl.BlockSpec((None, l, dt), lambda j, b: (b, 0, j + uo)),
                  pl.BlockSpec((None, l, dt), lambda j, b: (b, 0, j + go)),
                  pl.BlockSpec((n1, 2 * n1, dt), lambda j, b: (0, 0, j + ko)),
                  pl.BlockSpec((1, dt), lambda j, b: (0, j)),
                  pl.BlockSpec(t1.shape, lambda j, b: (0, 0, 0)),
                  pl.BlockSpec(a3.shape, lambda j, b: (0, 0)),
                  pl.BlockSpec(a3i.shape, lambda j, b: (0, 0)),
                  pl.BlockSpec(t2.shape, lambda j, b: (0, 0, 0))],
        out_specs=pl.BlockSpec((None, l, dt), lambda j, b: (b, 0, j)),
        out_shape=jax.ShapeDtypeStruct((bsz, l, d), bf16),
        scratch_shapes=[pltpu.VMEM((na * su, dt), f32),
                        pltpu.VMEM((n1 * sb, dt), f32),
                        pltpu.VMEM((n1 * sb, dt), f32),
                        pltpu.VMEM((na * su, dt), f32)],
        compiler_params=_cp("parallel", "arbitrary"),
        name="hyena_fftconv",
    )(u, xg, kf, fbias.reshape(1, d), t1, a3, a3i, t2)


def _rope_tables(l):
    rows = l // GRID_W
    row = jnp.repeat(jnp.arange(rows), GRID_W)
    col = jnp.tile(jnp.arange(GRID_W), rows)
    inv = ROPE_BASE ** (-jnp.arange(ROPE_AXIS_PAIRS, dtype=f32) / ROPE_AXIS_PAIRS)
    ang = jnp.stack([row, col], axis=-1).astype(f32)[..., None] * inv
    ang = jnp.broadcast_to(ang[:, :, None, :], (l, 2, 2, ROPE_AXIS_PAIRS)).reshape(l, A_DQK)
    reps = A_QW // A_DQK
    return jnp.tile(jnp.cos(ang), (1, reps)), jnp.tile(jnp.sin(ang), (1, reps))


def _rotate_cols(w):
    j = np.arange(w.shape[1])
    lo = (j % (2 * ROPE_AXIS_PAIRS)) < ROPE_AXIS_PAIRS
    perm = np.where(lo, j + ROPE_AXIS_PAIRS, j - ROPE_AXIS_PAIRS)
    sign = np.where(lo, -1.0, 1.0).astype(np.float32)
    return w[:, perm] * sign


def _gate_cols(w_g, b_g):
    idx_i = np.array([d * 2 * B_HEADS + hd for d in range(2) for hd in range(B_HEADS)])
    idx_f = idx_i + B_HEADS
    pad = LANES - _NCHAIN
    k = w_g.shape[0]
    w = jnp.concatenate([w_g[:, idx_i], jnp.zeros((k, pad), f32),
                         w_g[:, idx_f], jnp.zeros((k, pad), f32)], axis=1)
    b = jnp.concatenate([b_g[idx_i], jnp.zeros((pad,), f32), b_g[idx_f], jnp.zeros((pad,), f32)])
    return w, b


def _hyena_consts(l, d):
    j = jnp.arange(l, dtype=f32)
    bands = (POS_EMB_DIM - 1) // 2
    freqs = jnp.linspace(1e-4, bands - 1, bands, dtype=f32)
    ang = (2.0 * math.pi / l) * j[:, None] * freqs[None, :]
    z = jnp.concatenate([(j / (l - 1))[:, None], jnp.cos(ang), -jnp.sin(ang)], axis=-1)
    dist = jnp.abs(j - l // 2) / (l // 2)
    max_decay = math.log(DECAY_TARGET) / DECAY_FAST_PCT
    min_decay = math.log(DECAY_TARGET) / DECAY_SLOW_PCT
    deltas = jnp.abs(jnp.linspace(min_decay, max_decay, d, dtype=f32))
    window = jnp.exp(-dist[:, None] * deltas[None, :])
    return z, window


def _ab_layer(x, ctx, mod_vecs, mod_ctx, norm_g, w_in, conv_w, conv_b, gate_b, lam_vecs,
              g_a, g_b, w_out, lam_init):
    sh_m, sc_m, g_m = mod_vecs
    bsz, s, d = x.shape
    h = _norm(x, norm_g[0], sh_m, sc_m)
    hc = _norm(ctx, norm_g[0], mod_ctx[0], mod_ctx[1])
    w = B_WIDTH
    o = 0
    cols = {}
    for name, width in (("aq", A_QW), ("bq", w), ("bo", w), ("ak", A_QW), ("av", A_VW),
                        ("bk", w), ("bv", w), ("g", 4 * B_HEADS)):
        cols[name] = w_in[:, o:o + width]
        o += width
    cos, sin = _rope_tables(s)
    cat = lambda *ws: jnp.concatenate(ws, axis=1).astype(bf16)
    q = _mm(h, cat(cols["aq"], _rotate_cols(cols["aq"])), rope=(cos, sin, A_DQK ** -0.5))
    k = _mm(h, cat(cols["ak"], _rotate_cols(cols["ak"])), rope=(cos, sin, 1.0))
    qk = _mm(h, cat(cols["bq"], cols["bk"]), conv=(conv_w, conv_b, True))
    vvo = _mm(h, cat(cols["av"], cols["bv"], cols["bo"]))
    wg, bg = _gate_cols(cols["g"], gate_b)
    gates = _mm(h, wg.astype(bf16), out_dtype=f32, bias=bg, tn=2 * LANES)
    ckv = _mm(hc, cat(cols["ak"], cols["av"], cols["bv"]))
    cbk = _mm(hc, cols["bk"].astype(bf16), conv=(conv_w[:, w:], conv_b[w:], True))
    cg = _mm(hc, wg.astype(bf16), out_dtype=f32, bias=bg, tn=2 * LANES)
    out_a = _attn(lam_vecs, q, k, vvo, ckv, g_a, lam_init)
    out_b = _mlstm(qk, vvo, gates, cbk, ckv, cg, g_b)
    wo = w_out.astype(bf16)
    return _out_proj([out_a, out_b], [wo[:A_VW], wo[A_VW:]], x, norm_g[1], g_m)


def _hyena_layer(x, mod_vecs, norm_g, w_in, conv_w, conv_b, fw1, fb1, fw2, fb2, fw3, fbias, w_out):
    sh_m, sc_m, g_m = mod_vecs
    bsz, l, d = x.shape
    h = _norm(x, norm_g[0], sh_m, sc_m)
    u = _mm(h, w_in.astype(bf16), conv=(conv_w, conv_b, False))
    z, window = _hyena_consts(l, d)
    pz, ph = LANES - z.shape[1], LANES - fw1.shape[1]
    filt = _filters(jnp.pad(z, ((0, 0), (0, pz))), jnp.pad(fw1, ((0, pz), (0, ph))),
                    jnp.pad(fb1, (0, ph)), jnp.pad(fw2, ((0, ph), (0, ph))), jnp.pad(fb2, (0, ph)),
                    jnp.pad(fw3, ((0, ph), (0, 0))), window)
    tabs = _dft_tables(l)
    kf = _spectrum(filt, tabs)
    zz = _fftconv(u, 0, u, d, kf, 0, fbias[0], tabs, d)
    y = _fftconv(zz, 0, u, 2 * d, kf, d, fbias[1], tabs, d)
    return _out_proj([y], [w_out.astype(bf16)], x, norm_g[1], g_m)


def kernel(x, c, ctx, c_ctx, w_mod, b_mod, norm_g, w_in_ab, conv_ab_w, conv_ab_b, gate_b_ab, diff_lambda, head_g_a, head_g_b, w_out_ab, w_in_hy, conv_hy_w, conv_hy_b, filt_w1, filt_b1, filt_w2, filt_b2, filt_w3, filt_bias, w_out_hy, router_w, router_b, exp_gu, exp_down, sh_gu, sh_down):
    bsz, s, d = x.shape
    depth = w_mod.shape[0]
    rows = -(-(bsz + 1) // 8) * 8
    cc = jnp.concatenate([c, c_ctx[None, :], jnp.zeros((rows - bsz - 1, d), f32)], axis=0)
    for l in range(depth):
        mod = _mod(cc, w_mod[l], b_mod[l])
        vec = lambda i: mod[:bsz, i * d:(i + 1) * d].reshape(bsz, 1, d)
        sh_m, sc_m, g_m, sh_f, sc_f, g_f = [vec(i) for i in range(6)]
        if l % 2 == 0:
            e = l // 2
            lam_init = 0.8 - 0.6 * math.exp(-0.3 * l)
            mod_ctx = (mod[bsz:bsz + 1, 0:d].reshape(1, 1, d), mod[bsz:bsz + 1, d:2 * d].reshape(1, 1, d))
            x = _ab_layer(x, ctx, (sh_m, sc_m, g_m), mod_ctx, norm_g[l], w_in_ab[e], conv_ab_w[e],
                          conv_ab_b[e], gate_b_ab[e], diff_lambda[e], head_g_a[e], head_g_b[e],
                          w_out_ab[e], lam_init)
        else:
            o = l // 2
            x = _hyena_layer(x, (sh_m, sc_m, g_m), norm_g[l], w_in_hy[o], conv_hy_w[o], conv_hy_b[o],
                             filt_w1[o], filt_b1[o], filt_w2[o], filt_b2[o], filt_w3[o], filt_bias[o],
                             w_out_hy[o])
        h2, rk, gt, cmax = _router(x, norm_g[l, 2], sh_f, sc_f, router_w[l].T, router_b[l])
        x = _moe(h2, rk, gt, cmax, exp_gu[l].astype(bf16), exp_down[l].astype(bf16), sh_gu[l].astype(bf16),
                 sh_down[l].astype(bf16), x, norm_g[l, 3], g_f)
    return x
```

```python
import functools
import math

import numpy as np
import jax
import jax.numpy as jnp
from jax import lax
from jax.experimental import pallas as pl
from jax.experimental.pallas import tpu as pltpu

f32 = jnp.float32
bf16 = jnp.bfloat16

RMS_EPS = 1e-6
A_HEADS = 4
A_DQK = 64
A_DV = 128
B_HEADS = 4
B_DH = 128
B_WIDTH = B_HEADS * B_DH
A_QW = A_HEADS * 2 * A_DQK
A_VW = A_HEADS * A_DV
GRID_W = 64
ROPE_BASE = 10000.0
ROPE_AXIS_PAIRS = A_DQK // 4
N_EXPERTS = 64
TOP_K = 8
N_GROUPS = 8
TOPK_GROUPS = 4
D_EXPERT = 256
ROUTED_SCALE = 2.5
HY_ORDER = 2
POS_EMB_DIM = 33
FILTER_SIN_W = 1.0
DECAY_FAST_PCT = 0.3
DECAY_SLOW_PCT = 1.5
DECAY_TARGET = 1e-2

LANES = 128
VMEM_LIMIT = 56 * 1024 * 1024
MLSTM_CHUNK = 256
FFT_PAD = 8
MOE_SUB = 256
MOE_WIN = 64
MOE_GROUP = 4


def _cp(*sem):
    return pltpu.CompilerParams(dimension_semantics=sem, vmem_limit_bytes=VMEM_LIMIT)


def _split_bf16(a):
    hi = a.astype(bf16)
    lo = (a - hi.astype(f32)).astype(bf16)
    return hi, lo


def _dot(a, b, dims=(((1,), (0,)), ((), ()))):
    return lax.dot_general(a, b, dims, preferred_element_type=f32)


_NT = (((1,), (1,)), ((), ()))
_TN = (((0,), (0,)), ((), ()))


def _dot3(a, b, dims=(((1,), (0,)), ((), ()))):
    ah, al = _split_bf16(a)
    bh, bl = _split_bf16(b)
    return _dot(ah, bh, dims) + (_dot(ah, bl, dims) + _dot(al, bh, dims))


def _silu(v):
    return v / (1.0 + jnp.exp(-v))


def _sigmoid(v):
    return 1.0 / (1.0 + jnp.exp(-v))


def _log_sigmoid(v):
    return jnp.minimum(v, 0.0) - jnp.log(1.0 + jnp.exp(-jnp.abs(v)))


def _mod_kernel(c_ref, w_ref, b_ref, o_ref):
    o_ref[...] = _dot3(_silu(c_ref[...]), w_ref[...]) + b_ref[...]


def _mod(cc, w, b):
    rows, d = cc.shape
    n = w.shape[1]
    tn = d
    return pl.pallas_call(
        _mod_kernel,
        grid=(n // tn,),
        in_specs=[pl.BlockSpec((rows, d), lambda j: (0, 0)),
                  pl.BlockSpec((d, tn), lambda j: (0, j)),
                  pl.BlockSpec((1, tn), lambda j: (0, j))],
        out_specs=pl.BlockSpec((rows, tn), lambda j: (0, j)),
        out_shape=jax.ShapeDtypeStruct((rows, n), f32),
        compiler_params=_cp("arbitrary"),
        name="mod",
    )(cc, w, b.reshape(1, n))


def _norm_mod(xv, g, shift, scale):
    y = xv * lax.rsqrt(jnp.mean(xv * xv, axis=-1, keepdims=True) + RMS_EPS)
    return (y * g) * (1.0 + scale) + shift


def _norm_kernel(x_ref, g_ref, sh_ref, sc_ref, o_ref):
    o_ref[...] = _norm_mod(x_ref[...], g_ref[...], sh_ref[...], sc_ref[...]).astype(o_ref.dtype)


def _bidx(arr):
    if arr.shape[0] == 1:
        return lambda b, *_: (0, 0, 0)
    return lambda b, *_: (b, 0, 0)


def _norm(x, g, shift, scale, tl=512):
    bsz, l, d = x.shape
    tl = min(tl, l)
    return pl.pallas_call(
        _norm_kernel,
        grid=(bsz, l // tl),
        in_specs=[pl.BlockSpec((None, tl, d), lambda b, i: (b, i, 0)),
                  pl.BlockSpec((1, d), lambda b, i: (0, 0)),
                  pl.BlockSpec((None, 1, d), _bidx(shift)),
                  pl.BlockSpec((None, 1, d), _bidx(scale))],
        out_specs=pl.BlockSpec((None, tl, d), lambda b, i: (b, i, 0)),
        out_shape=jax.ShapeDtypeStruct((bsz, l, d), bf16),
        compiler_params=_cp("parallel", "parallel"),
        name="norm",
    )(x, g.reshape(1, d), shift, scale)


def _mm_plain_kernel(h_ref, w_ref, b_ref, o_ref):
    o_ref[...] = (_dot(h_ref[...], w_ref[...]) + b_ref[...]).astype(o_ref.dtype)


def _mm_rope_kernel(h_ref, w_ref, cos_ref, sin_ref, o_ref, *, scale):
    p = _dot(h_ref[...], w_ref[...])
    n = o_ref.shape[-1]
    o_ref[...] = ((p[:, :n] * cos_ref[...] + p[:, n:] * sin_ref[...]) * scale).astype(o_ref.dtype)


def _mm_conv_kernel(h_ref, w_ref, cw_ref, cb_ref, o_ref, *, act):
    p = _dot(h_ref[...], w_ref[...])
    l = p.shape[0]
    w0, w1, w2, cb = cw_ref[0:1, :], cw_ref[1:2, :], cw_ref[2:3, :], cb_ref[...]

    def finish(v):
        return (_silu(v) if act else v).astype(o_ref.dtype)

    o_ref[...] = finish(pltpu.roll(p, 1, 0) * w0 + p * w1 + pltpu.roll(p, l - 1, 0) * w2 + cb)
    e = 16
    row = lax.broadcasted_iota(jnp.int32, (e, p.shape[1]), 0)
    top, bot = p[0:e], p[l - e:l]
    prev = jnp.where(row == 0, 0.0, pltpu.roll(top, 1, 0))
    o_ref[0:e, :] = finish(prev * w0 + top * w1 + pltpu.roll(p[0:2 * e], 2 * e - 1, 0)[0:e] * w2 + cb)
    nxt = jnp.where(row == e - 1, 0.0, pltpu.roll(bot, e - 1, 0))
    o_ref[l - e:l, :] = finish(pltpu.roll(p[l - 2 * e:l], 1, 0)[e:2 * e] * w0 + bot * w1 + nxt * w2 + cb)


def _mm(h, w, *, out_dtype=bf16, bias=None, rope=None, conv=None, tl=512, tn=512):
    bsz, l, k = h.shape
    n = w.shape[1]
    if rope is not None:
        cos, sin, scale = rope
        n_out = n // 2
        tl = min(tl, l)
        return pl.pallas_call(
            functools.partial(_mm_rope_kernel, scale=scale),
            grid=(bsz, l // tl),
            in_specs=[pl.BlockSpec((None, tl, k), lambda b, i: (b, i, 0)),
                      pl.BlockSpec((k, n), lambda b, i: (0, 0)),
                      pl.BlockSpec((tl, n_out), lambda b, i: (i, 0)),
                      pl.BlockSpec((tl, n_out), lambda b, i: (i, 0))],
            out_specs=pl.BlockSpec((None, tl, n_out), lambda b, i: (b, i, 0)),
            out_shape=jax.ShapeDtypeStruct((bsz, l, n_out), out_dtype),
            compiler_params=_cp("parallel", "parallel"),
            name="mm_rope",
        )(h, w, cos, sin)
    tn = min(tn, n)
    if conv is not None:
        cw, cb, act = conv
        return pl.pallas_call(
            functools.partial(_mm_conv_kernel, act=act),
            grid=(bsz, n // tn),
            in_specs=[pl.BlockSpec((None, l, k), lambda b, j: (b, 0, 0)),
                      pl.BlockSpec((k, tn), lambda b, j: (0, j)),
                      pl.BlockSpec((3, tn), lambda b, j: (0, j)),
                      pl.BlockSpec((1, tn), lambda b, j: (0, j))],
            out_specs=pl.BlockSpec((None, l, tn), lambda b, j: (b, 0, j)),
            out_shape=jax.ShapeDtypeStruct((bsz, l, n), out_dtype),
            compiler_params=_cp("parallel", "arbitrary"),
            name="mm_conv",
        )(h, w, cw, cb.reshape(1, n))
    if bias is None:
        bias = jnp.zeros((n,), f32)
    tl = min(tl, l)
    return pl.pallas_call(
        _mm_plain_kernel,
        grid=(bsz, l // tl, n // tn),
        in_specs=[pl.BlockSpec((None, tl, k), lambda b, i, j: (b, i, 0)),
                  pl.BlockSpec((k, tn), lambda b, i, j: (0, j)),
                  pl.BlockSpec((1, tn), lambda b, i, j: (0, j))],
        out_specs=pl.BlockSpec((None, tl, tn), lambda b, i, j: (b, i, j)),
        out_shape=jax.ShapeDtypeStruct((bsz, l, n), out_dtype),
        compiler_params=_cp("parallel", "parallel", "arbitrary"),
        name="mm_plain",
    )(h, w, bias.reshape(1, n))


def _attn_kernel(lv_ref, q_ref, kc_ref, k_ref, vc_ref, v_ref, g_ref, o_ref, *, lam_init):
    tq = q_ref.shape[0]
    lv = lv_ref[...]
    lam = (jnp.exp(jnp.sum(lv[0:1] * lv[1:2], axis=1, keepdims=True))
           - jnp.exp(jnp.sum(lv[2:3] * lv[3:4], axis=1, keepdims=True)) + lam_init)
    first = lax.broadcasted_iota(jnp.int32, (tq, A_DV), 1) < A_DQK
    for hd in range(A_HEADS):
        cs = slice(hd * A_DV, (hd + 1) * A_DV)
        qh = q_ref[:, cs]
        zero = jnp.zeros_like(qh)
        q2 = jnp.concatenate([jnp.where(first, qh, zero), jnp.where(first, zero, qh)], axis=0)
        s_c = _dot(q2, kc_ref[:, cs], _NT)
        s_l = _dot(q2, k_ref[:, cs], _NT)
        m = jnp.maximum(jnp.max(s_c, axis=1, keepdims=True), jnp.max(s_l, axis=1, keepdims=True))
        p_c = jnp.exp(s_c - m)
        p_l = jnp.exp(s_l - m)
        inv = 1.0 / (jnp.sum(p_c, axis=1, keepdims=True) + jnp.sum(p_l, axis=1, keepdims=True))
        w0 = inv[:tq]
        w1 = inv[tq:] * lam
        a_c = (p_c[:tq] * w0 - p_c[tq:] * w1).astype(bf16)
        a_l = (p_l[:tq] * w0 - p_l[tq:] * w1).astype(bf16)
        o = _dot(a_c, vc_ref[:, cs]) + _dot(a_l, v_ref[:, cs])
        o = o * lax.rsqrt(jnp.mean(o * o, axis=1, keepdims=True) + RMS_EPS)
        o_ref[:, cs] = (o * g_ref[:, cs] * (1.0 - lam_init)).astype(o_ref.dtype)


def _attn(lv, q, k, vvo, ckv, g_a, lam_init, tq=256):
    bsz, s, _ = q.shape
    lc = ckv.shape[1]
    tq = min(tq, s)
    w = A_QW
    return pl.pallas_call(
        functools.partial(_attn_kernel, lam_init=lam_init),
        grid=(bsz, s // tq),
        in_specs=[pl.BlockSpec(lv.shape, lambda b, i: (0, 0)),
                  pl.BlockSpec((None, tq, w), lambda b, i: (b, i, 0)),
                  pl.BlockSpec((None, lc, w), lambda b, i: (b, 0, 0)),
                  pl.BlockSpec((None, s, w), lambda b, i: (b, 0, 0)),
                  pl.BlockSpec((None, lc, w), lambda b, i: (b, 0, 1)),
                  pl.BlockSpec((None, s, w), lambda b, i: (b, 0, 0)),
                  pl.BlockSpec((1, w), lambda b, i: (0, 0))],
        out_specs=pl.BlockSpec((None, tq, w), lambda b, i: (b, i, 0)),
        out_shape=jax.ShapeDtypeStruct((bsz, s, w), bf16),
        compiler_params=_cp("parallel", "arbitrary"),
        name="diff_attn",
    )(lv, q, ckv, k, ckv, vvo, g_a.reshape(1, w))


_LN_QSCALE = math.log(B_DH ** -0.5)
_NCHAIN = 2 * B_HEADS


def _chunk_gate_sums(gi, gf, tri):
    lf = _log_sigmoid(gf)
    hi, lo = _split_bf16(lf)
    cum = _dot(tri, hi) + _dot(tri, lo)
    t = gf.shape[0]
    tot = cum[t - 1:t, :]
    rcum = tot - cum + lf
    fwd = lax.broadcasted_iota(jnp.int32, gf.shape, 1) < B_HEADS
    bd = jnp.where(fwd, cum, rcum)
    return bd, tot, (bd - gi).T


def _lower_tri(t):
    r = lax.broadcasted_iota(jnp.int32, (t, t), 0)
    c = lax.broadcasted_iota(jnp.int32, (t, t), 1)
    return r, c


def _absorb(c_ref, n_ref, m_ref, ch, bcol, tot_c, gi_c, kf, vb):
    m_prev = m_ref[ch][:, 0:1]
    g = tot_c - bcol + gi_c
    m_new = jnp.maximum(tot_c + m_prev, jnp.max(g, axis=0, keepdims=True))
    wgt = jnp.exp(g - m_new)
    decay = jnp.exp(tot_c + m_prev - m_new)
    kw = kf * wgt
    c_ref[ch] = decay * c_ref[ch] + _dot(kw.astype(bf16), vb, _TN)
    n_ref[ch] = decay * n_ref[ch] + jnp.sum(kw, axis=0, keepdims=True)
    m_ref[ch] = jnp.broadcast_to(m_new, m_ref.shape[1:])


def _mlstm_kernel(qk_ref, vvo_ref, g_ref, ck_ref, ckv_ref, cg_ref, gb_ref, o_ref,
                  hf_ref, hb_ref, c_ref, n_ref, m_ref, *, tc):
    s = o_ref.shape[0]
    lc = ck_ref.shape[0]
    nc = s // tc
    w = B_WIDTH
    dh = B_DH

    c_ref[...] = jnp.zeros_like(c_ref)
    n_ref[...] = jnp.zeros_like(n_ref)
    m_ref[...] = jnp.zeros_like(m_ref)

    r, cidx = _lower_tri(lc)
    tri_c = jnp.where(cidx <= r, 1.0, 0.0).astype(bf16)
    cg = cg_ref[...]
    bd, tot, _ = _chunk_gate_sums(cg[:, :LANES], cg[:, LANES:], tri_c)
    gi = cg[:, :LANES]
    for ch in range(_NCHAIN):
        hs = slice((ch % B_HEADS) * dh, (ch % B_HEADS + 1) * dh)
        vs = slice(2 * w + (ch % B_HEADS) * dh, 2 * w + (ch % B_HEADS + 1) * dh)
        _absorb(c_ref, n_ref, m_ref, ch, bd[:, ch:ch + 1], tot[:, ch:ch + 1], gi[:, ch:ch + 1],
                ck_ref[:, hs].astype(f32), ckv_ref[:, vs])

    r, cidx = _lower_tri(tc)
    tri = jnp.where(cidx <= r, 1.0, 0.0).astype(bf16)
    causal = cidx <= r
    anti = cidx >= r

    def step(i, carry):
        for d in range(2):
            row0 = pl.multiple_of((i if d == 0 else nc - 1 - i) * tc, tc)
            rows = pl.ds(row0, tc)
            gch = g_ref[rows, :]
            gi = gch[:, :LANES]
            bd, tot, xt = _chunk_gate_sums(gi, gch[:, LANES:], tri)
            mask = causal if d == 0 else anti
            dst = hf_ref if d == 0 else hb_ref
            for hd in range(B_HEADS):
                ch = d * B_HEADS + hd
                hs = slice(hd * dh, (hd + 1) * dh)
                qb = qk_ref[rows, hs]
                kb = qk_ref[rows, slice(w + hd * dh, w + (hd + 1) * dh)]
                vb = vvo_ref[rows, slice(w + hd * dh, w + (hd + 1) * dh)]
                bcol = bd[:, ch:ch + 1]
                dmat = jnp.where(mask, bcol - xt[ch:ch + 1, :], -jnp.inf)
                m_prev = m_ref[ch][:, 0:1]
                inter = bcol + m_prev
                m_t = jnp.maximum(inter, jnp.max(dmat, axis=1, keepdims=True))
                e = jnp.exp(dmat - m_t + _LN_QSCALE)
                smat = _dot(qb, kb, _NT) * e
                sc = jnp.exp(inter - m_t + _LN_QSCALE)
                num = sc * _dot(qb, c_ref[ch].astype(bf16)) + _dot(smat.astype(bf16), vb)
                qn = jnp.sum(qb.astype(f32) * n_ref[ch], axis=1, keepdims=True)
                den = sc * qn + jnp.sum(smat, axis=1, keepdims=True)
                hout = num * (1.0 / jnp.maximum(jnp.abs(den), jnp.exp(-m_t)))
                dst[rows, hs] = hout
                _absorb(c_ref, n_ref, m_ref, ch, bcol, tot[:, ch:ch + 1], gi[:, ch:ch + 1],
                        kb.astype(f32), vb)
        return carry

    lax.fori_loop(0, nc, step, 0)

    for hd in range(B_HEADS):
        hs = slice(hd * dh, (hd + 1) * dh)
        hsum = hf_ref[:, hs] + hb_ref[:, hs]
        hn = hsum * lax.rsqrt(jnp.mean(hsum * hsum, axis=1, keepdims=True) + RMS_EPS)
        og = _sigmoid(vvo_ref[:, slice(2 * w + hd * dh, 2 * w + (hd + 1) * dh)].astype(f32))
        o_ref[:, hs] = (hn * gb_ref[:, hs] * og).astype(o_ref.dtype)


def _mlstm(qk, vvo, gates, cbk, ckv, cg, g_b):
    bsz, s, _ = qk.shape
    lc = cbk.shape[1]
    w = B_WIDTH
    tc = min(MLSTM_CHUNK, s)
    return pl.pallas_call(
        functools.partial(_mlstm_kernel, tc=tc),
        grid=(bsz,),
        in_specs=[pl.BlockSpec((None, s, 2 * w), lambda b: (b, 0, 0)),
                  pl.BlockSpec((None, s, 3 * w), lambda b: (b, 0, 0)),
                  pl.BlockSpec((None, s, 2 * LANES), lambda b: (b, 0, 0)),
                  pl.BlockSpec((None, lc, w), lambda b: (b, 0, 0)),
                  pl.BlockSpec((None, lc, 3 * w), lambda b: (b, 0, 0)),
                  pl.BlockSpec((None, lc, 2 * LANES), lambda b: (b, 0, 0)),
                  pl.BlockSpec((1, w), lambda b: (0, 0))],
        out_specs=pl.BlockSpec((None, s, w), lambda b: (b, 0, 0)),
        out_shape=jax.ShapeDtypeStruct((bsz, s, w), bf16),
        scratch_shapes=[pltpu.VMEM((s, w), f32), pltpu.VMEM((s, w), f32),
                        pltpu.VMEM((_NCHAIN, B_DH, B_DH), f32),
                        pltpu.VMEM((_NCHAIN, 1, B_DH), f32),
                        pltpu.VMEM((_NCHAIN, 1, LANES), f32)],
        compiler_params=_cp("arbitrary"),
        name="mlstm",
    )(qk, vvo, gates, cbk, ckv, cg, g_b.reshape(1, w))


def _out_kernel(*refs, n_act):
    acts = refs[:n_act]
    ws = refs[n_act:2 * n_act]
    x_ref, g_ref, gate_ref, o_ref = refs[2 * n_act:]
    mix = _dot(acts[0][...], ws[0][...])
    for a, wr in zip(acts[1:], ws[1:]):
        mix = mix + _dot(a[...], wr[...])
    y = mix * lax.rsqrt(jnp.mean(mix * mix, axis=-1, keepdims=True) + RMS_EPS) * g_ref[...]
    o_ref[...] = x_ref[...] + gate_ref[...] * y


def _out_proj(acts, ws, x, g, gate, tl=512):
    bsz, l, d = x.shape
    tl = min(tl, l)
    n_act = len(acts)
    in_specs = [pl.BlockSpec((None, tl, a.shape[2]), lambda b, i: (b, i, 0)) for a in acts]
    in_specs += [pl.BlockSpec(wm.shape, lambda b, i: (0, 0)) for wm in ws]
    in_specs += [pl.BlockSpec((None, tl, d), lambda b, i: (b, i, 0)),
                 pl.BlockSpec((1, d), lambda b, i: (0, 0)),
                 pl.BlockSpec((None, 1, d), _bidx(gate))]
    return pl.pallas_call(
        functools.partial(_out_kernel, n_act=n_act),
        grid=(bsz, l // tl),
        in_specs=in_specs,
        out_specs=pl.BlockSpec((None, tl, d), lambda b, i: (b, i, 0)),
        out_shape=jax.ShapeDtypeStruct((bsz, l, d), f32),
        compiler_params=_cp("parallel", "parallel"),
        name="out_proj",
    )(*acts, *ws, x, g.reshape(1, d), gate)


def _router_kernel(x_ref, g_ref, sh_ref, sc_ref, rw_ref, rb_ref, h_ref, rk_ref, gt_ref, cm_ref):
    hf = _norm_mod(x_ref[...], g_ref[...], sh_ref[...], sc_ref[...])
    tl = hf.shape[0]
    h_ref[...] = hf.astype(h_ref.dtype)
    per = N_EXPERTS // N_GROUPS
    logits = _dot3(rw_ref[...], hf, _NT)
    s3 = _sigmoid(logits).reshape(N_GROUPS, per, tl)
    b3 = s3 + rb_ref[...].reshape(N_GROUPS, per, 1)
    neg = -jnp.inf
    jdx = lax.broadcasted_iota(jnp.int32, b3.shape, 1)
    gdx = lax.broadcasted_iota(jnp.int32, b3.shape, 0)
    m1 = jnp.max(b3, axis=1, keepdims=True)
    f1 = jnp.min(jnp.where(b3 == m1, jdx, per), axis=1, keepdims=True)
    m2 = jnp.max(jnp.where(jdx == f1, neg, b3), axis=1, keepdims=True)
    grp = m1 + m2
    g1 = lax.broadcasted_iota(jnp.int32, grp.shape, 0)
    cnt = jnp.zeros(grp.shape, jnp.int32)
    for gp in range(N_GROUPS):
        rv = grp[gp:gp + 1]
        ahead = jnp.where(rv > grp, 1, jnp.where(rv == grp, jnp.where(g1 > gp, 1, 0), 0))
        cnt = cnt + ahead
    v = jnp.where(cnt < TOPK_GROUPS, b3, neg)
    eidx = gdx * per + jdx
    sel = jnp.zeros(b3.shape, f32)
    for _ in range(TOP_K):
        m = jnp.max(jnp.max(v, axis=1, keepdims=True), axis=0, keepdims=True)
        cand = jnp.where(v == m, eidx, N_EXPERTS)
        fi = jnp.min(jnp.min(cand, axis=1, keepdims=True), axis=0, keepdims=True)
        hit = eidx == fi
        sel = jnp.where(hit, 1.0, sel)
        v = jnp.where(hit, neg, v)
    ssel = sel * s3
    den = jnp.sum(jnp.sum(ssel, axis=1, keepdims=True), axis=0, keepdims=True)
    gt_ref[...] = ((ROUTED_SCALE * ssel) / den).reshape(N_EXPERTS, tl)
    sel2 = sel.reshape(N_EXPERTS, tl)
    r = lax.broadcasted_iota(jnp.int32, (MOE_SUB, MOE_SUB), 0)
    c = lax.broadcasted_iota(jnp.int32, (MOE_SUB, MOE_SUB), 1)
    before = jnp.where(r < c, 1.0, 0.0).astype(bf16)
    cmax = jnp.zeros((N_EXPERTS, 1), f32)
    for j in range(tl // MOE_SUB):
        sub = sel2[:, j * MOE_SUB:(j + 1) * MOE_SUB]
        rank = _dot(sub.astype(bf16), before)
        rk_ref[:, j * MOE_SUB:(j + 1) * MOE_SUB] = jnp.where(sub > 0.0, rank, -1.0)
        cmax = jnp.maximum(cmax, jnp.sum(sub, axis=1, keepdims=True))
    cm_ref[...] = jnp.broadcast_to(cmax, cm_ref.shape)


def _router(x, g, shift, scale, rw_t, rb, tl=512):
    bsz, l, d = x.shape
    tl = min(tl, l)
    nl = l // tl
    return pl.pallas_call(
        _router_kernel,
        grid=(bsz, nl),
        in_specs=[pl.BlockSpec((None, tl, d), lambda b, i: (b, i, 0)),
                  pl.BlockSpec((1, d), lambda b, i: (0, 0)),
                  pl.BlockSpec((None, 1, d), _bidx(shift)),
                  pl.BlockSpec((None, 1, d), _bidx(scale)),
                  pl.BlockSpec((N_EXPERTS, d), lambda b, i: (0, 0)),
                  pl.BlockSpec((N_EXPERTS, 1), lambda b, i: (0, 0))],
        out_specs=[pl.BlockSpec((None, tl, d), lambda b, i: (b, i, 0)),
                   pl.BlockSpec((N_EXPERTS, tl), lambda b, i: (0, b * nl + i)),
                   pl.BlockSpec((N_EXPERTS, tl), lambda b, i: (0, b * nl + i)),
                   pl.BlockSpec((None, N_EXPERTS, LANES), lambda b, i: (b * nl + i, 0, 0))],
        out_shape=[jax.ShapeDtypeStruct((bsz, l, d), bf16),
                   jax.ShapeDtypeStruct((N_EXPERTS, bsz * l), f32),
                   jax.ShapeDtypeStruct((N_EXPERTS, bsz * l), f32),
                   jax.ShapeDtypeStruct((bsz * nl, N_EXPERTS, LANES), f32)],
        compiler_params=_cp("parallel", "parallel"),
        name="router",
    )(x, g.reshape(1, d), shift, scale, rw_t, rb.reshape(N_EXPERTS, 1))


def _swiglu_act(hh):
    half = hh.shape[1] // 2
    return _silu(hh[:, :half]) * hh[:, half:]


def _moe_kernel(cnt_ref, ord_ref, h_ref, rk_ref, gt_ref, *refs):
    ng = MOE_GROUP
    gu_refs, dn_refs = refs[:ng], refs[ng:2 * ng]
    (sgu_ref, sdn_ref, x_ref, g_ref, gate_ref, o_ref,
     acc_ref, xg_ref, ys_ref, p_ref, gr_ref) = refs[2 * ng:]
    tile = pl.program_id(0)
    grp = pl.program_id(1)
    tm, d = acc_ref.shape
    ns = tm // MOE_SUB
    win = MOE_WIN
    eids = [ord_ref[tile, grp * ng + el] for el in range(ng)]

    @pl.when(grp == 0)
    def _():
        act = _swiglu_act(_dot(h_ref[...], sgu_ref[...]))
        acc_ref[...] = _dot(act.astype(bf16), sdn_ref[...])

    riota = lax.broadcasted_iota(jnp.int32, (win, MOE_SUB), 0).astype(f32)

    def expert_ffn(el):
        hh = _dot(xg_ref[el], gu_refs[el][...])
        gr = gr_ref[el]
        act = _swiglu_act(hh) * jnp.concatenate([gr] * (hh.shape[1] // (2 * LANES)), axis=1)
        y = _dot(act.astype(bf16), dn_refs[el][...]).astype(bf16)
        for s in range(ns):
            ys_ref[s, el * win:(el + 1) * win, :] = y[s * win:(s + 1) * win]

    def one_pass(p, skip_idle):
        base = p * win
        for s in range(ns):
            cols = slice(s * MOE_SUB, (s + 1) * MOE_SUB)
            onehots = []
            for el in range(ng):
                row = pl.ds(eids[el], 1)
                hit = (rk_ref[row, cols] - base) == riota
                onehots.append(jnp.where(hit, 1.0, 0.0).astype(bf16))
                gsel = jnp.sum(jnp.where(hit, gt_ref[row, cols], 0.0), axis=1, keepdims=True)
                gr_ref[el, s * win:(s + 1) * win, :] = jnp.broadcast_to(gsel, (win, LANES))
            pm = jnp.concatenate(onehots, axis=0)
            p_ref[s] = pm
            gx = _dot(pm, h_ref[cols, :])
            for el in range(ng):
                xg_ref[el, s * win:(s + 1) * win, :] = gx[el * win:(el + 1) * win].astype(bf16)
        for el in range(ng):
            if not skip_idle:
                expert_ffn(el)
                continue
            busy = cnt_ref[tile, eids[el]] > base
            pl.when(busy)(functools.partial(expert_ffn, el))

            @pl.when(jnp.logical_not(busy))
            def _():
                for s in range(ns):
                    ys_ref[s, el * win:(el + 1) * win, :] = jnp.zeros((win, d), bf16)
        for s in range(ns):
            acc_ref[s * MOE_SUB:(s + 1) * MOE_SUB, :] += _dot(p_ref[s], ys_ref[s], _TN)

    one_pass(0, False)
    most = cnt_ref[tile, eids[0]]
    for el in range(1, ng):
        most = jnp.maximum(most, cnt_ref[tile, eids[el]])

    def later_pass(p, carry):
        one_pass(p, True)
        return carry

    lax.fori_loop(1, (most + win - 1) // win, later_pass, 0)

    @pl.when(grp == pl.num_programs(1) - 1)
    def _():
        mo = acc_ref[...]
        y = mo * lax.rsqrt(jnp.mean(mo * mo, axis=-1, keepdims=True) + RMS_EPS) * g_ref[...]
        o_ref[...] = x_ref[...] + gate_ref[...] * y


def _moe(h2, rk, gt, cmax, gu, dn, sgu, sdn, x, g, gate, tm=1024):
    bsz, l, d = x.shape
    tm = min(tm, l)
    per_b = l // tm
    nt = bsz * per_b
    ne = gu.shape[0]
    ng = MOE_GROUP
    ns = tm // MOE_SUB
    counts = jnp.max(cmax[:, :, 0].reshape(nt, -1, ne), axis=1).astype(jnp.int32)
    order = jnp.argsort(-counts, axis=1).astype(jnp.int32)

    def expert_spec(arr, k):
        return pl.BlockSpec((None,) + arr.shape[1:], lambda t, e, cnt, order_ref: (order_ref[t, e * ng + k], 0, 0))

    grid_spec = pltpu.PrefetchScalarGridSpec(
        num_scalar_prefetch=2,
        grid=(nt, ne // ng),
        in_specs=[pl.BlockSpec((tm, d), lambda t, e, *_: (t, 0)),
                  pl.BlockSpec((ne, tm), lambda t, e, *_: (0, t)),
                  pl.BlockSpec((ne, tm), lambda t, e, *_: (0, t))]
                 + [expert_spec(gu, k) for k in range(ng)]
                 + [expert_spec(dn, k) for k in range(ng)]
                 + [pl.BlockSpec(sgu.shape, lambda t, e, *_: (0, 0)),
                    pl.BlockSpec(sdn.shape, lambda t, e, *_: (0, 0)),
                    pl.BlockSpec((tm, d), lambda t, e, *_: (t, 0)),
                    pl.BlockSpec((1, d), lambda t, e, *_: (0, 0)),
                    pl.BlockSpec((None, 1, d), lambda t, e, *_: (t // per_b, 0, 0))],
        out_specs=pl.BlockSpec((tm, d), lambda t, e, *_: (t, 0)),
        scratch_shapes=[pltpu.VMEM((tm, d), f32),
                        pltpu.VMEM((ng, ns * MOE_WIN, d), bf16),
                        pltpu.VMEM((ns, ng * MOE_WIN, d), bf16),
                        pltpu.VMEM((ns, ng * MOE_WIN, MOE_SUB), bf16),
                        pltpu.VMEM((ng, ns * MOE_WIN, LANES), f32)])
    out = pl.pallas_call(
        _moe_kernel,
        grid_spec=grid_spec,
        out_shape=jax.ShapeDtypeStruct((bsz * l, d), f32),
        compiler_params=_cp("parallel", "arbitrary"),
        name="moe",
    )(counts, order, h2.reshape(bsz * l, d), rk, gt, *([gu] * ng), *([dn] * ng), sgu, sdn,
      x.reshape(bsz * l, d), g.reshape(1, d), gate)
    return out.reshape(bsz, l, d)


def _filter_kernel(z_ref, w1_ref, b1_ref, w2_ref, b2_ref, w3_ref, win_ref, o_ref):
    hid = jnp.sin(FILTER_SIN_W * (_dot3(z_ref[...], w1_ref[...]) + b1_ref[...]))
    hid = jnp.sin(FILTER_SIN_W * (_dot3(hid, w2_ref[...]) + b2_ref[...]))
    o_ref[...] = _dot3(hid, w3_ref[...]) * win_ref[...]


def _filters(z, w1, b1, w2, b2, w3, window, tn=512):
    l, p = z.shape
    hdim = w1.shape[1]
    n = w3.shape[1]
    d = window.shape[1]
    nd = d // tn
    return pl.pallas_call(
        _filter_kernel,
        grid=(n // tn,),
        in_specs=[pl.BlockSpec((l, p), lambda j: (0, 0)),
                  pl.BlockSpec((p, hdim), lambda j: (0, 0)),
                  pl.BlockSpec((1, hdim), lambda j: (0, 0)),
                  pl.BlockSpec((hdim, hdim), lambda j: (0, 0)),
                  pl.BlockSpec((1, hdim), lambda j: (0, 0)),
                  pl.BlockSpec((hdim, tn), lambda j: (0, j)),
                  pl.BlockSpec((l, tn), lambda j: (0, j % nd))],
        out_specs=pl.BlockSpec((l, tn), lambda j: (0, j)),
        out_shape=jax.ShapeDtypeStruct((l, n), f32),
        compiler_params=_cp("arbitrary"),
        name="hyena_filter",
    )(z, w1, b1.reshape(1, hdim), w2, b2.reshape(1, hdim), w3, window)


def _dft_tables(l):
    n = 2 * l
    n1 = math.isqrt(n)
    assert n == n1 * n1 and n1 % 16 == 0
    na = l // n1
    a = np.arange(na)
    b = np.arange(n1)
    c = np.arange(n1)
    th = 2.0 * np.pi * ((n1 * a[None, None, :] + b[:, None, None]) * c[None, :, None]) / n
    t1 = np.concatenate([np.cos(th), -np.sin(th)], axis=1)
    ph = 2.0 * np.pi * (b[:, None] * b[None, :]) / n1
    cs, sn = np.cos(ph), np.sin(ph)
    a3 = np.block([[cs, sn], [-sn, cs]])
    a3i = np.block([[cs, -sn], [sn, cs]])
    a2 = np.arange(na) + na // 2
    th2 = 2.0 * np.pi * ((n1 * a2[None, :, None] + b[:, None, None]) * c[None, None, :]) / n
    t2 = np.concatenate([np.cos(th2), -np.sin(th2)], axis=2)
    return [jnp.asarray(t, f32).astype(bf16) for t in (t1, a3, a3i, t2)]


def _fft_dims(t1):
    n1, _, na = t1.shape
    return n1, na, 2 * n1 + FFT_PAD, n1 + FFT_PAD


def _ld(ref, rows):
    return jnp.concatenate([ref[j, rows, :] for j in range(ref.shape[0])], axis=1)


def _st(ref, rows, val):
    for j in range(ref.shape[0]):
        ref[j, rows, :] = val[:, j * LANES:(j + 1) * LANES]


def _dft_forward(uf_ref, t1_ref, zs_ref):
    n1, na, sb, su = _fft_dims(t1_ref)
    for b in range(n1):
        ub = _ld(uf_ref, pl.ds(b, na, stride=su)).astype(bf16)
        zb = _dot(t1_ref[b], ub)
        _st(zs_ref, pl.ds(b, n1, stride=sb), zb[:n1])
        _st(zs_ref, pl.ds(n1 + b, n1, stride=sb), zb[n1:])


def _spectrum_kernel(f_ref, t1_ref, a3_ref, o_ref, uf_ref, zs_ref, *, scale):
    n1, na, sb, su = _fft_dims(t1_ref)
    for a in range(na):
        _st(uf_ref, pl.ds(a * su, n1), f_ref[pl.ds(a * n1, n1), :])
    _dft_forward(uf_ref, t1_ref, zs_ref)
    a3 = a3_ref[...]
    for c in range(n1):
        zc = _ld(zs_ref, pl.ds(c * sb, 2 * n1)).astype(bf16)
        o_ref[c] = (_dot(a3, zc) * scale).astype(o_ref.dtype)


def _spectrum(filt, tabs, dt=256):
    l, n = filt.shape
    t1, a3, _, _ = tabs
    n1, na, sb, su = _fft_dims(t1)
    nj = dt // LANES
    return pl.pallas_call(
        functools.partial(_spectrum_kernel, scale=1.0 / (2 * l)),
        grid=(n // dt,),
        in_specs=[pl.BlockSpec((l, dt), lambda j: (0, j)),
                  pl.BlockSpec(t1.shape, lambda j: (0, 0, 0)),
                  pl.BlockSpec(a3.shape, lambda j: (0, 0))],
        out_specs=pl.BlockSpec((n1, 2 * n1, dt), lambda j: (0, 0, j)),
        out_shape=jax.ShapeDtypeStruct((n1, 2 * n1, n), bf16),
        scratch_shapes=[pltpu.VMEM((nj, na * su, LANES), f32),
                        pltpu.VMEM((nj, n1 * sb, LANES), f32)],
        compiler_params=_cp("arbitrary"),
        name="hyena_spectrum",
    )(filt, t1, a3)


def _fftconv_kernel(u_ref, xg_ref, kf_ref, fb_ref, t1_ref, a3_ref, a3i_ref, t2_ref, o_ref,
                    uf_ref, zs_ref, qs_ref, y_ref):
    n1, na, sb, su = _fft_dims(t1_ref)
    for a in range(na):
        _st(uf_ref, pl.ds(a * su, n1), u_ref[pl.ds(a * n1, n1), :].astype(f32))
    _dft_forward(uf_ref, t1_ref, zs_ref)
    a3 = a3_ref[...]
    a3i = a3i_ref[...]
    for c in range(n1):
        zc = _ld(zs_ref, pl.ds(c * sb, 2 * n1)).astype(bf16)
        xc = _dot(a3, zc)
        kc = kf_ref[c].astype(f32)
        xr, xi = xc[:n1], xc[n1:]
        kr, ki = kc[:n1], kc[n1:]
        pc = jnp.concatenate([xr * kr - xi * ki, xr * ki + xi * kr], axis=0).astype(bf16)
        qc = _dot(a3i, pc)
        _st(qs_ref, pl.ds(c, n1, stride=sb), qc[:n1])
        _st(qs_ref, pl.ds(n1 + c, n1, stride=sb), qc[n1:])
    for b in range(n1):
        qb = _ld(qs_ref, pl.ds(b * sb, 2 * n1)).astype(bf16)
        _st(y_ref, pl.ds(b, na, stride=su), _dot(t2_ref[b], qb))
    fb = fb_ref[...]
    for a in range(na):
        rows = pl.ds(a * n1, n1)
        uv = _ld(uf_ref, pl.ds(a * su, n1))
        yv = _ld(y_ref, pl.ds(a * su, n1))
        o_ref[rows, :] = (xg_ref[rows, :].astype(f32) * (yv + uv * fb)).astype(o_ref.dtype)


def _fftconv(u, u_col, xg, xg_col, kf, kf_col, fbias, tabs, d, dt=256):
    bsz, l, _ = u.shape
    t1, a3, a3i, t2 = tabs
    n1, na, sb, su = _fft_dims(t1)
    nd = d // dt
    nj = dt // LANES
    uo, go, ko = u_col // dt, xg_col // dt, kf_col // dt
    return pl.pallas_call(
        _fftconv_kernel,
        grid=(nd, bsz),
        in_specs=[pl.BlockSpec((None, l, dt), lambda j, b: (b, 0, j + uo)),
                  pl.BlockSpec((None, l, dt), lambda j, b: (b, 0, j + go)),
                  pl.BlockSpec((n1, 2 * n1, dt), lambda j, b: (0, 0, j + ko)),
                  pl.BlockSpec((1, dt), lambda j, b: (0, j)),
                  pl.BlockSpec(t1.shape, lambda j, b: (0, 0, 0)),
                  pl.BlockSpec(a3.shape, lambda j, b: (0, 0)),
                  pl.BlockSpec(a3i.shape, lambda j, b: (0, 0)),
                  pl.BlockSpec(t2.shape, lambda j, b: (0, 0, 0))],
        out_specs=pl.BlockSpec((None, l, dt), lambda j, b: (b, 0, j)),
        out_shape=jax.ShapeDtypeStruct((bsz, l, d), bf16),
        scratch_shapes=[pltpu.VMEM((nj, na * su, LANES), f32),
                        pltpu.VMEM((nj, n1 * sb, LANES), f32),
                        pltpu.VMEM((nj, n1 * sb, LANES), f32),
                        pltpu.VMEM((nj, na * su, LANES), f32)],
        compiler_params=_cp("parallel", "arbitrary"),
        name="hyena_fftconv",
    )(u, xg, kf, fbias.reshape(1, d), t1, a3, a3i, t2)


def _rope_tables(l):
    rows = l // GRID_W
    row = jnp.repeat(jnp.arange(rows), GRID_W)
    col = jnp.tile(jnp.arange(GRID_W), rows)
    inv = ROPE_BASE ** (-jnp.arange(ROPE_AXIS_PAIRS, dtype=f32) / ROPE_AXIS_PAIRS)
    ang = jnp.stack([row, col], axis=-1).astype(f32)[..., None] * inv
    ang = jnp.broadcast_to(ang[:, :, None, :], (l, 2, 2, ROPE_AXIS_PAIRS)).reshape(l, A_DQK)
    reps = A_QW // A_DQK
    return jnp.tile(jnp.cos(ang), (1, reps)), jnp.tile(jnp.sin(ang), (1, reps))


def _rotate_cols(w):
    j = np.arange(w.shape[1])
    lo = (j % (2 * ROPE_AXIS_PAIRS)) < ROPE_AXIS_PAIRS
    perm = np.where(lo, j + ROPE_AXIS_PAIRS, j - ROPE_AXIS_PAIRS)
    sign = np.where(lo, -1.0, 1.0).astype(np.float32)
    return w[:, perm] * sign


def _gate_cols(w_g, b_g):
    idx_i = np.array([d * 2 * B_HEADS + hd for d in range(2) for hd in range(B_HEADS)])
    idx_f = idx_i + B_HEADS
    pad = LANES - _NCHAIN
    k = w_g.shape[0]
    w = jnp.concatenate([w_g[:, idx_i], jnp.zeros((k, pad), f32),
                         w_g[:, idx_f], jnp.zeros((k, pad), f32)], axis=1)
    b = jnp.concatenate([b_g[idx_i], jnp.zeros((pad,), f32), b_g[idx_f], jnp.zeros((pad,), f32)])
    return w, b


def _hyena_consts(l, d):
    j = jnp.arange(l, dtype=f32)
    bands = (POS_EMB_DIM - 1) // 2
    freqs = jnp.linspace(1e-4, bands - 1, bands, dtype=f32)
    ang = (2.0 * math.pi / l) * j[:, None] * freqs[None, :]
    z = jnp.concatenate([(j / (l - 1))[:, None], jnp.cos(ang), -jnp.sin(ang)], axis=-1)
    dist = jnp.abs(j - l // 2) / (l // 2)
    max_decay = math.log(DECAY_TARGET) / DECAY_FAST_PCT
    min_decay = math.log(DECAY_TARGET) / DECAY_SLOW_PCT
    deltas = jnp.abs(jnp.linspace(min_decay, max_decay, d, dtype=f32))
    window = jnp.exp(-dist[:, None] * deltas[None, :])
    return z, window


def _ab_layer(x, ctx, mod_vecs, mod_ctx, norm_g, w_in, conv_w, conv_b, gate_b, lam_vecs,
              g_a, g_b, w_out, lam_init):
    sh_m, sc_m, g_m = mod_vecs
    bsz, s, d = x.shape
    h = _norm(x, norm_g[0], sh_m, sc_m)
    hc = _norm(ctx, norm_g[0], mod_ctx[0], mod_ctx[1])
    w = B_WIDTH
    o = 0
    cols = {}
    for name, width in (("aq", A_QW), ("bq", w), ("bo", w), ("ak", A_QW), ("av", A_VW),
                        ("bk", w), ("bv", w), ("g", 4 * B_HEADS)):
        cols[name] = w_in[:, o:o + width]
        o += width
    cos, sin = _rope_tables(s)
    cat = lambda *ws: jnp.concatenate(ws, axis=1).astype(bf16)
    q = _mm(h, cat(cols["aq"], _rotate_cols(cols["aq"])), rope=(cos, sin, A_DQK ** -0.5))
    k = _mm(h, cat(cols["ak"], _rotate_cols(cols["ak"])), rope=(cos, sin, 1.0))
    qk = _mm(h, cat(cols["bq"], cols["bk"]), conv=(conv_w, conv_b, True))
    vvo = _mm(h, cat(cols["av"], cols["bv"], cols["bo"]))
    wg, bg = _gate_cols(cols["g"], gate_b)
    gates = _mm(h, wg.astype(bf16), out_dtype=f32, bias=bg, tn=2 * LANES)
    ckv = _mm(hc, cat(cols["ak"], cols["av"], cols["bv"]))
    cbk = _mm(hc, cols["bk"].astype(bf16), conv=(conv_w[:, w:], conv_b[w:], True))
    cg = _mm(hc, wg.astype(bf16), out_dtype=f32, bias=bg, tn=2 * LANES)
    out_a = _attn(lam_vecs, q, k, vvo, ckv, g_a, lam_init)
    out_b = _mlstm(qk, vvo, gates, cbk, ckv, cg, g_b)
    wo = w_out.astype(bf16)
    return _out_proj([out_a, out_b], [wo[:A_VW], wo[A_VW:]], x, norm_g[1], g_m)


def _hyena_layer(x, mod_vecs, norm_g, w_in, conv_w, conv_b, fw1, fb1, fw2, fb2, fw3, fbias, w_out):
    sh_m, sc_m, g_m = mod_vecs
    bsz, l, d = x.shape
    h = _norm(x, norm_g[0], sh_m, sc_m)
    u = _mm(h, w_in.astype(bf16), conv=(conv_w, conv_b, False))
    z, window = _hyena_consts(l, d)
    pz, ph = LANES - z.shape[1], LANES - fw1.shape[1]
    filt = _filters(jnp.pad(z, ((0, 0), (0, pz))), jnp.pad(fw1, ((0, pz), (0, ph))),
                    jnp.pad(fb1, (0, ph)), jnp.pad(fw2, ((0, ph), (0, ph))), jnp.pad(fb2, (0, ph)),
                    jnp.pad(fw3, ((0, ph), (0, 0))), window)
    tabs = _dft_tables(l)
    kf = _spectrum(filt, tabs)
    zz = _fftconv(u, 0, u, d, kf, 0, fbias[0], tabs, d)
    y = _fftconv(zz, 0, u, 2 * d, kf, d, fbias[1], tabs, d)
    return _out_proj([y], [w_out.astype(bf16)], x, norm_g[1], g_m)


def kernel(x, c, ctx, c_ctx, w_mod, b_mod, norm_g, w_in_ab, conv_ab_w, conv_ab_b, gate_b_ab, diff_lambda, head_g_a, head_g_b, w_out_ab, w_in_hy, conv_hy_w, conv_hy_b, filt_w1, filt_b1, filt_w2, filt_b2, filt_w3, filt_bias, w_out_hy, router_w, router_b, exp_gu, exp_down, sh_gu, sh_down):
    bsz, s, d = x.shape
    depth = w_mod.shape[0]
    rows = -(-(bsz + 1) // 8) * 8
    cc = jnp.concatenate([c, c_ctx[None, :], jnp.zeros((rows - bsz - 1, d), f32)], axis=0)
    for l in range(depth):
        mod = _mod(cc, w_mod[l], b_mod[l])
        vec = lambda i: mod[:bsz, i * d:(i + 1) * d].reshape(bsz, 1, d)
        sh_m, sc_m, g_m, sh_f, sc_f, g_f = [vec(i) for i in range(6)]
        if l % 2 == 0:
            e = l // 2
            lam_init = 0.8 - 0.6 * math.exp(-0.3 * l)
            mod_ctx = (mod[bsz:bsz + 1, 0:d].reshape(1, 1, d), mod[bsz:bsz + 1, d:2 * d].reshape(1, 1, d))
            x = _ab_layer(x, ctx, (sh_m, sc_m, g_m), mod_ctx, norm_g[l], w_in_ab[e], conv_ab_w[e],
                          conv_ab_b[e], gate_b_ab[e], diff_lambda[e], head_g_a[e], head_g_b[e],
                          w_out_ab[e], lam_init)
        else:
            o = l // 2
            x = _hyena_layer(x, (sh_m, sc_m, g_m), norm_g[l], w_in_hy[o], conv_hy_w[o], conv_hy_b[o],
                             filt_w1[o], filt_b1[o], filt_w2[o], filt_b2[o], filt_w3[o], filt_bias[o],
                             w_out_hy[o])
        h2, rk, gt, cmax = _router(x, norm_g[l, 2], sh_f, sc_f, router_w[l].T, router_b[l])
        x = _moe(h2, rk, gt, cmax, exp_gu[l].astype(bf16), exp_down[l].astype(bf16), sh_gu[l].astype(bf16),
                 sh_down[l].astype(bf16), x, norm_g[l, 3], g_f)
    return x
```

```python
import functools
import math

import numpy as np
import jax
import jax.numpy as jnp
from jax import lax
from jax.experimental import pallas as pl
from jax.experimental.pallas import tpu as pltpu

f32 = jnp.float32
bf16 = jnp.bfloat16

RMS_EPS = 1e-6
A_HEADS = 4
A_DQK = 64
A_DV = 128
B_HEADS = 4
B_DH = 128
B_WIDTH = B_HEADS * B_DH
A_QW = A_HEADS * 2 * A_DQK
A_VW = A_HEADS * A_DV
GRID_W = 64
ROPE_BASE = 10000.0
ROPE_AXIS_PAIRS = A_DQK // 4
N_EXPERTS = 64
TOP_K = 8
N_GROUPS = 8
TOPK_GROUPS = 4
D_EXPERT = 256
ROUTED_SCALE = 2.5
HY_ORDER = 2
POS_EMB_DIM = 33
FILTER_SIN_W = 1.0
DECAY_FAST_PCT = 0.3
DECAY_SLOW_PCT = 1.5
DECAY_TARGET = 1e-2

LANES = 128
VMEM_LIMIT = 56 * 1024 * 1024
MLSTM_CHUNK = 256
FFT_PAD = 8
MOE_SUB = 256
MOE_WIN = 48
MOE_GROUP = 4


def _cp(*sem):
    return pltpu.CompilerParams(dimension_semantics=sem, vmem_limit_bytes=VMEM_LIMIT)


def _split_bf16(a):
    hi = a.astype(bf16)
    lo = (a - hi.astype(f32)).astype(bf16)
    return hi, lo


def _dot(a, b, dims=(((1,), (0,)), ((), ()))):
    return lax.dot_general(a, b, dims, preferred_element_type=f32)


_NT = (((1,), (1,)), ((), ()))
_TN = (((0,), (0,)), ((), ()))


def _dot3(a, b, dims=(((1,), (0,)), ((), ()))):
    ah, al = _split_bf16(a)
    bh, bl = _split_bf16(b)
    return _dot(ah, bh, dims) + (_dot(ah, bl, dims) + _dot(al, bh, dims))


def _silu(v):
    return v / (1.0 + jnp.exp(-v))


def _sigmoid(v):
    return 1.0 / (1.0 + jnp.exp(-v))


def _log_sigmoid(v):
    return jnp.minimum(v, 0.0) - jnp.log(1.0 + jnp.exp(-jnp.abs(v)))


def _mod_kernel(c_ref, w_ref, b_ref, o_ref):
    o_ref[...] = _dot3(_silu(c_ref[...]), w_ref[...]) + b_ref[...]


def _mod(cc, w, b):
    rows, d = cc.shape
    n = w.shape[1]
    tn = d
    return pl.pallas_call(
        _mod_kernel,
        grid=(n // tn,),
        in_specs=[pl.BlockSpec((rows, d), lambda j: (0, 0)),
                  pl.BlockSpec((d, tn), lambda j: (0, j)),
                  pl.BlockSpec((1, tn), lambda j: (0, j))],
        out_specs=pl.BlockSpec((rows, tn), lambda j: (0, j)),
        out_shape=jax.ShapeDtypeStruct((rows, n), f32),
        compiler_params=_cp("arbitrary"),
        name="mod",
    )(cc, w, b.reshape(1, n))


def _norm_mod(xv, g, shift, scale):
    y = xv * lax.rsqrt(jnp.mean(xv * xv, axis=-1, keepdims=True) + RMS_EPS)
    return (y * g) * (1.0 + scale) + shift


def _norm_kernel(x_ref, g_ref, sh_ref, sc_ref, o_ref):
    o_ref[...] = _norm_mod(x_ref[...], g_ref[...], sh_ref[...], sc_ref[...]).astype(o_ref.dtype)


def _bidx(arr):
    if arr.shape[0] == 1:
        return lambda b, *_: (0, 0, 0)
    return lambda b, *_: (b, 0, 0)


def _norm(x, g, shift, scale, tl=512):
    bsz, l, d = x.shape
    tl = min(tl, l)
    return pl.pallas_call(
        _norm_kernel,
        grid=(bsz, l // tl),
        in_specs=[pl.BlockSpec((None, tl, d), lambda b, i: (b, i, 0)),
                  pl.BlockSpec((1, d), lambda b, i: (0, 0)),
                  pl.BlockSpec((None, 1, d), _bidx(shift)),
                  pl.BlockSpec((None, 1, d), _bidx(scale))],
        out_specs=pl.BlockSpec((None, tl, d), lambda b, i: (b, i, 0)),
        out_shape=jax.ShapeDtypeStruct((bsz, l, d), bf16),
        compiler_params=_cp("parallel", "parallel"),
        name="norm",
    )(x, g.reshape(1, d), shift, scale)


def _mm_plain_kernel(h_ref, w_ref, b_ref, o_ref):
    o_ref[...] = (_dot(h_ref[...], w_ref[...]) + b_ref[...]).astype(o_ref.dtype)


def _mm_rope_kernel(h_ref, w_ref, cos_ref, sin_ref, o_ref, *, scale):
    p = _dot(h_ref[...], w_ref[...])
    n = o_ref.shape[-1]
    o_ref[...] = ((p[:, :n] * cos_ref[...] + p[:, n:] * sin_ref[...]) * scale).astype(o_ref.dtype)


def _mm_conv_kernel(h_ref, w_ref, cw_ref, cb_ref, o_ref, *, act):
    p = _dot(h_ref[...], w_ref[...])
    l = p.shape[0]
    w0, w1, w2, cb = cw_ref[0:1, :], cw_ref[1:2, :], cw_ref[2:3, :], cb_ref[...]

    def finish(v):
        return (_silu(v) if act else v).astype(o_ref.dtype)

    o_ref[...] = finish(pltpu.roll(p, 1, 0) * w0 + p * w1 + pltpu.roll(p, l - 1, 0) * w2 + cb)
    e = 16
    row = lax.broadcasted_iota(jnp.int32, (e, p.shape[1]), 0)
    top, bot = p[0:e], p[l - e:l]
    prev = jnp.where(row == 0, 0.0, pltpu.roll(top, 1, 0))
    o_ref[0:e, :] = finish(prev * w0 + top * w1 + pltpu.roll(p[0:2 * e], 2 * e - 1, 0)[0:e] * w2 + cb)
    nxt = jnp.where(row == e - 1, 0.0, pltpu.roll(bot, e - 1, 0))
    o_ref[l - e:l, :] = finish(pltpu.roll(p[l - 2 * e:l], 1, 0)[e:2 * e] * w0 + bot * w1 + nxt * w2 + cb)


def _mm(h, w, *, out_dtype=bf16, bias=None, rope=None, conv=None, tl=512, tn=512):
    bsz, l, k = h.shape
    n = w.shape[1]
    if rope is not None:
        cos, sin, scale = rope
        n_out = n // 2
        tl = min(tl, l)
        return pl.pallas_call(
            functools.partial(_mm_rope_kernel, scale=scale),
            grid=(bsz, l // tl),
            in_specs=[pl.BlockSpec((None, tl, k), lambda b, i: (b, i, 0)),
                      pl.BlockSpec((k, n), lambda b, i: (0, 0)),
                      pl.BlockSpec((tl, n_out), lambda b, i: (i, 0)),
                      pl.BlockSpec((tl, n_out), lambda b, i: (i, 0))],
            out_specs=pl.BlockSpec((None, tl, n_out), lambda b, i: (b, i, 0)),
            out_shape=jax.ShapeDtypeStruct((bsz, l, n_out), out_dtype),
            compiler_params=_cp("parallel", "parallel"),
            name="mm_rope",
        )(h, w, cos, sin)
    tn = min(tn, n)
    if conv is not None:
        cw, cb, act = conv
        return pl.pallas_call(
            functools.partial(_mm_conv_kernel, act=act),
            grid=(bsz, n // tn),
            in_specs=[pl.BlockSpec((None, l, k), lambda b, j: (b, 0, 0)),
                      pl.BlockSpec((k, tn), lambda b, j: (0, j)),
                      pl.BlockSpec((3, tn), lambda b, j: (0, j)),
                      pl.BlockSpec((1, tn), lambda b, j: (0, j))],
            out_specs=pl.BlockSpec((None, l, tn), lambda b, j: (b, 0, j)),
            out_shape=jax.ShapeDtypeStruct((bsz, l, n), out_dtype),
            compiler_params=_cp("parallel", "arbitrary"),
            name="mm_conv",
        )(h, w, cw, cb.reshape(1, n))
    if bias is None:
        bias = jnp.zeros((n,), f32)
    tl = min(tl, l)
    return pl.pallas_call(
        _mm_plain_kernel,
        grid=(bsz, l // tl, n // tn),
        in_specs=[pl.BlockSpec((None, tl, k), lambda b, i, j: (b, i, 0)),
                  pl.BlockSpec((k, tn), lambda b, i, j: (0, j)),
                  pl.BlockSpec((1, tn), lambda b, i, j: (0, j))],
        out_specs=pl.BlockSpec((None, tl, tn), lambda b, i, j: (b, i, j)),
        out_shape=jax.ShapeDtypeStruct((bsz, l, n), out_dtype),
        compiler_params=_cp("parallel", "parallel", "arbitrary"),
        name="mm_plain",
    )(h, w, bias.reshape(1, n))


def _attn_kernel(lv_ref, q_ref, kc_ref, k_ref, vc_ref, v_ref, g_ref, o_ref, *, lam_init):
    tq = q_ref.shape[0]
    lv = lv_ref[...]
    lam = (jnp.exp(jnp.sum(lv[0:1] * lv[1:2], axis=1, keepdims=True))
           - jnp.exp(jnp.sum(lv[2:3] * lv[3:4], axis=1, keepdims=True)) + lam_init)
    first = lax.broadcasted_iota(jnp.int32, (tq, A_DV), 1) < A_DQK
    one0 = jnp.where(lax.broadcasted_iota(jnp.int32, (1, A_DV), 1) == 0, 1.0, 0.0).astype(bf16)
    ones_c = jnp.broadcast_to(one0, (kc_ref.shape[0], A_DV))
    ones_l = jnp.broadcast_to(one0, (k_ref.shape[0], A_DV))
    for hd in range(A_HEADS):
        cs = slice(hd * A_DV, (hd + 1) * A_DV)
        qh = q_ref[:, cs]
        zero = jnp.zeros_like(qh)
        q2 = jnp.concatenate([jnp.where(first, qh, zero), jnp.where(first, zero, qh)], axis=0)
        s_c = _dot(q2, kc_ref[:, cs], _NT)
        s_l = _dot(q2, k_ref[:, cs], _NT)
        m = jnp.maximum(jnp.max(s_c, axis=1, keepdims=True), jnp.max(s_l, axis=1, keepdims=True))
        p_c = jnp.exp((s_c - m).astype(bf16))
        p_l = jnp.exp((s_l - m).astype(bf16))
        oa = (_dot(p_c, jnp.concatenate([vc_ref[:, cs], ones_c], axis=1))
              + _dot(p_l, jnp.concatenate([v_ref[:, cs], ones_l], axis=1)))
        on = oa[:, :A_DV] * (1.0 / oa[:, A_DV:A_DV + 1])
        o = on[:tq] - lam * on[tq:]
        o = o * lax.rsqrt(jnp.mean(o * o, axis=1, keepdims=True) + RMS_EPS)
        o_ref[:, cs] = (o * g_ref[:, cs] * (1.0 - lam_init)).astype(o_ref.dtype)


def _attn(lv, q, k, vvo, ckv, g_a, lam_init, tq=256):
    bsz, s, _ = q.shape
    lc = ckv.shape[1]
    tq = min(tq, s)
    w = A_QW
    return pl.pallas_call(
        functools.partial(_attn_kernel, lam_init=lam_init),
        grid=(bsz, s // tq),
        in_specs=[pl.BlockSpec(lv.shape, lambda b, i: (0, 0)),
                  pl.BlockSpec((None, tq, w), lambda b, i: (b, i, 0)),
                  pl.BlockSpec((None, lc, w), lambda b, i: (b, 0, 0)),
                  pl.BlockSpec((None, s, w), lambda b, i: (b, 0, 0)),
                  pl.BlockSpec((None, lc, w), lambda b, i: (b, 0, 1)),
                  pl.BlockSpec((None, s, w), lambda b, i: (b, 0, 0)),
                  pl.BlockSpec((1, w), lambda b, i: (0, 0))],
        out_specs=pl.BlockSpec((None, tq, w), lambda b, i: (b, i, 0)),
        out_shape=jax.ShapeDtypeStruct((bsz, s, w), bf16),
        compiler_params=_cp("parallel", "arbitrary"),
        name="diff_attn",
    )(lv, q, ckv, k, ckv, vvo, g_a.reshape(1, w))


_LN_QSCALE = math.log(B_DH ** -0.5)
_NCHAIN = 2 * B_HEADS


def _chunk_gate_sums(gi, gf, tri):
    lf = _log_sigmoid(gf)
    hi, lo = _split_bf16(lf)
    cum = _dot(tri, hi) + _dot(tri, lo)
    t = gf.shape[0]
    tot = cum[t - 1:t, :]
    rcum = tot - cum + lf
    fwd = lax.broadcasted_iota(jnp.int32, gf.shape, 1) < B_HEADS
    bd = jnp.where(fwd, cum, rcum)
    return bd, tot, (bd - gi).T


def _lower_tri(t):
    r = lax.broadcasted_iota(jnp.int32, (t, t), 0)
    c = lax.broadcasted_iota(jnp.int32, (t, t), 1)
    return r, c


def _absorb(c_ref, n_ref, m_ref, ch, bcol, tot_c, gi_c, kf, vb):
    m_prev = m_ref[ch][:, 0:1]
    g = tot_c - bcol + gi_c
    m_new = jnp.maximum(tot_c + m_prev, jnp.max(g, axis=0, keepdims=True))
    wgt = jnp.exp(g - m_new)
    decay = jnp.exp(tot_c + m_prev - m_new)
    kw = kf * wgt
    c_ref[ch] = decay * c_ref[ch] + _dot(kw.astype(bf16), vb, _TN)
    n_ref[ch] = decay * n_ref[ch] + jnp.sum(kw, axis=0, keepdims=True)
    m_ref[ch] = jnp.broadcast_to(m_new, m_ref.shape[1:])


def _mlstm_kernel(qk_ref, vvo_ref, g_ref, ck_ref, ckv_ref, cg_ref, gb_ref, o_ref,
                  hf_ref, hb_ref, c_ref, n_ref, m_ref, *, tc):
    s = o_ref.shape[0]
    lc = ck_ref.shape[0]
    nc = s // tc
    w = B_WIDTH
    dh = B_DH

    c_ref[...] = jnp.zeros_like(c_ref)
    n_ref[...] = jnp.zeros_like(n_ref)
    m_ref[...] = jnp.zeros_like(m_ref)

    r, cidx = _lower_tri(lc)
    tri_c = jnp.where(cidx <= r, 1.0, 0.0).astype(bf16)
    cg = cg_ref[...]
    bd, tot, _ = _chunk_gate_sums(cg[:, :LANES], cg[:, LANES:], tri_c)
    gi = cg[:, :LANES]
    for ch in range(_NCHAIN):
        hs = slice((ch % B_HEADS) * dh, (ch % B_HEADS + 1) * dh)
        vs = slice(2 * w + (ch % B_HEADS) * dh, 2 * w + (ch % B_HEADS + 1) * dh)
        _absorb(c_ref, n_ref, m_ref, ch, bd[:, ch:ch + 1], tot[:, ch:ch + 1], gi[:, ch:ch + 1],
                ck_ref[:, hs].astype(f32), ckv_ref[:, vs])

    r, cidx = _lower_tri(tc)
    tri = jnp.where(cidx <= r, 1.0, 0.0).astype(bf16)
    causal = cidx <= r
    anti = cidx >= r

    def step(i, carry):
        for d in range(2):
            row0 = pl.multiple_of((i if d == 0 else nc - 1 - i) * tc, tc)
            rows = pl.ds(row0, tc)
            gch = g_ref[rows, :]
            gi = gch[:, :LANES]
            bd, tot, xt = _chunk_gate_sums(gi, gch[:, LANES:], tri)
            mask = causal if d == 0 else anti
            dst = hf_ref if d == 0 else hb_ref
            for hd in range(B_HEADS):
                ch = d * B_HEADS + hd
                hs = slice(hd * dh, (hd + 1) * dh)
                qb = qk_ref[rows, hs]
                kb = qk_ref[rows, slice(w + hd * dh, w + (hd + 1) * dh)]
                vb = vvo_ref[rows, slice(w + hd * dh, w + (hd + 1) * dh)]
                bcol = bd[:, ch:ch + 1]
                dmat = jnp.where(mask, bcol - xt[ch:ch + 1, :], -jnp.inf)
                m_prev = m_ref[ch][:, 0:1]
                inter = bcol + m_prev
                m_t = jnp.maximum(inter, jnp.max(dmat, axis=1, keepdims=True))
                e = jnp.exp(dmat - m_t + _LN_QSCALE)
                smat = _dot(qb, kb, _NT) * e
                sc = jnp.exp(inter - m_t + _LN_QSCALE)
                num = sc * _dot(qb, c_ref[ch].astype(bf16)) + _dot(smat.astype(bf16), vb)
                qn = jnp.sum(qb.astype(f32) * n_ref[ch], axis=1, keepdims=True)
                den = sc * qn + jnp.sum(smat, axis=1, keepdims=True)
                hout = num * (1.0 / jnp.maximum(jnp.abs(den), jnp.exp(-m_t)))
                dst[rows, hs] = hout
                _absorb(c_ref, n_ref, m_ref, ch, bcol, tot[:, ch:ch + 1], gi[:, ch:ch + 1],
                        kb.astype(f32), vb)
        return carry

    lax.fori_loop(0, nc, step, 0)

    for hd in range(B_HEADS):
        hs = slice(hd * dh, (hd + 1) * dh)
        hsum = hf_ref[:, hs] + hb_ref[:, hs]
        hn = hsum * lax.rsqrt(jnp.mean(hsum * hsum, axis=1, keepdims=True) + RMS_EPS)
        og = _sigmoid(vvo_ref[:, slice(2 * w + hd * dh, 2 * w + (hd + 1) * dh)].astype(f32))
        o_ref[:, hs] = (hn * gb_ref[:, hs] * og).astype(o_ref.dtype)


def _mlstm(qk, vvo, gates, cbk, ckv, cg, g_b):
    bsz, s, _ = qk.shape
    lc = cbk.shape[1]
    w = B_WIDTH
    tc = min(MLSTM_CHUNK, s)
    return pl.pallas_call(
        functools.partial(_mlstm_kernel, tc=tc),
        grid=(bsz,),
        in_specs=[pl.BlockSpec((None, s, 2 * w), lambda b: (b, 0, 0)),
                  pl.BlockSpec((None, s, 3 * w), lambda b: (b, 0, 0)),
                  pl.BlockSpec((None, s, 2 * LANES), lambda b: (b, 0, 0)),
                  pl.BlockSpec((None, lc, w), lambda b: (b, 0, 0)),
                  pl.BlockSpec((None, lc, 3 * w), lambda b: (b, 0, 0)),
                  pl.BlockSpec((None, lc, 2 * LANES), lambda b: (b, 0, 0)),
                  pl.BlockSpec((1, w), lambda b: (0, 0))],
        out_specs=pl.BlockSpec((None, s, w), lambda b: (b, 0, 0)),
        out_shape=jax.ShapeDtypeStruct((bsz, s, w), bf16),
        scratch_shapes=[pltpu.VMEM((s, w), f32), pltpu.VMEM((s, w), f32),
                        pltpu.VMEM((_NCHAIN, B_DH, B_DH), f32),
                        pltpu.VMEM((_NCHAIN, 1, B_DH), f32),
                        pltpu.VMEM((_NCHAIN, 1, LANES), f32)],
        compiler_params=_cp("arbitrary"),
        name="mlstm",
    )(qk, vvo, gates, cbk, ckv, cg, g_b.reshape(1, w))


def _out_kernel(*refs, n_act):
    acts = refs[:n_act]
    ws = refs[n_act:2 * n_act]
    x_ref, g_ref, gate_ref, o_ref = refs[2 * n_act:]
    mix = _dot(acts[0][...], ws[0][...])
    for a, wr in zip(acts[1:], ws[1:]):
        mix = mix + _dot(a[...], wr[...])
    y = mix * lax.rsqrt(jnp.mean(mix * mix, axis=-1, keepdims=True) + RMS_EPS) * g_ref[...]
    o_ref[...] = x_ref[...] + gate_ref[...] * y


def _out_proj(acts, ws, x, g, gate, tl=512):
    bsz, l, d = x.shape
    tl = min(tl, l)
    n_act = len(acts)
    in_specs = [pl.BlockSpec((None, tl, a.shape[2]), lambda b, i: (b, i, 0)) for a in acts]
    in_specs += [pl.BlockSpec(wm.shape, lambda b, i: (0, 0)) for wm in ws]
    in_specs += [pl.BlockSpec((None, tl, d), lambda b, i: (b, i, 0)),
                 pl.BlockSpec((1, d), lambda b, i: (0, 0)),
                 pl.BlockSpec((None, 1, d), _bidx(gate))]
    return pl.pallas_call(
        functools.partial(_out_kernel, n_act=n_act),
        grid=(bsz, l // tl),
        in_specs=in_specs,
        out_specs=pl.BlockSpec((None, tl, d), lambda b, i: (b, i, 0)),
        out_shape=jax.ShapeDtypeStruct((bsz, l, d), f32),
        compiler_params=_cp("parallel", "parallel"),
        name="out_proj",
    )(*acts, *ws, x, g.reshape(1, d), gate)


def _router_kernel(x_ref, g_ref, sh_ref, sc_ref, rw_ref, rb_ref, h_ref, rk_ref, gt_ref, cm_ref):
    hf = _norm_mod(x_ref[...], g_ref[...], sh_ref[...], sc_ref[...])
    tl = hf.shape[0]
    h_ref[...] = hf.astype(h_ref.dtype)
    per = N_EXPERTS // N_GROUPS
    logits = _dot3(rw_ref[...], hf, _NT)
    s3 = _sigmoid(logits).reshape(N_GROUPS, per, tl)
    b3 = s3 + rb_ref[...].reshape(N_GROUPS, per, 1)
    neg = -jnp.inf
    jdx = lax.broadcasted_iota(jnp.int32, b3.shape, 1)
    gdx = lax.broadcasted_iota(jnp.int32, b3.shape, 0)
    m1 = jnp.max(b3, axis=1, keepdims=True)
    f1 = jnp.min(jnp.where(b3 == m1, jdx, per), axis=1, keepdims=True)
    m2 = jnp.max(jnp.where(jdx == f1, neg, b3), axis=1, keepdims=True)
    grp = m1 + m2
    g1 = lax.broadcasted_iota(jnp.int32, grp.shape, 0)
    cnt = jnp.zeros(grp.shape, jnp.int32)
    for gp in range(N_GROUPS):
        rv = grp[gp:gp + 1]
        ahead = jnp.where(rv > grp, 1, jnp.where(rv == grp, jnp.where(g1 > gp, 1, 0), 0))
        cnt = cnt + ahead
    v = jnp.where(cnt < TOPK_GROUPS, b3, neg)
    eidx = gdx * per + jdx
    sel = jnp.zeros(b3.shape, f32)
    for _ in range(TOP_K):
        m = jnp.max(jnp.max(v, axis=1, keepdims=True), axis=0, keepdims=True)
        cand = jnp.where(v == m, eidx, N_EXPERTS)
        fi = jnp.min(jnp.min(cand, axis=1, keepdims=True), axis=0, keepdims=True)
        hit = eidx == fi
        sel = jnp.where(hit, 1.0, sel)
        v = jnp.where(hit, neg, v)
    ssel = sel * s3
    den = jnp.sum(jnp.sum(ssel, axis=1, keepdims=True), axis=0, keepdims=True)
    gt_ref[...] = ((ROUTED_SCALE * ssel) / den).reshape(N_EXPERTS, tl)
    sel2 = sel.reshape(N_EXPERTS, tl)
    r = lax.broadcasted_iota(jnp.int32, (MOE_SUB, MOE_SUB), 0)
    c = lax.broadcasted_iota(jnp.int32, (MOE_SUB, MOE_SUB), 1)
    before = jnp.where(r < c, 1.0, 0.0).astype(bf16)
    cmax = jnp.zeros((N_EXPERTS, 1), f32)
    for j in range(tl // MOE_SUB):
        sub = sel2[:, j * MOE_SUB:(j + 1) * MOE_SUB]
        rank = _dot(sub.astype(bf16), before)
        rk_ref[:, j * MOE_SUB:(j + 1) * MOE_SUB] = jnp.where(sub > 0.0, rank, -1.0)
        cmax = jnp.maximum(cmax, jnp.sum(sub, axis=1, keepdims=True))
    cm_ref[...] = jnp.broadcast_to(cmax, cm_ref.shape)


def _router(x, g, shift, scale, rw_t, rb, tl=512):
    bsz, l, d = x.shape
    tl = min(tl, l)
    nl = l // tl
    return pl.pallas_call(
        _router_kernel,
        grid=(bsz, nl),
        in_specs=[pl.BlockSpec((None, tl, d), lambda b, i: (b, i, 0)),
                  pl.BlockSpec((1, d), lambda b, i: (0, 0)),
                  pl.BlockSpec((None, 1, d), _bidx(shift)),
                  pl.BlockSpec((None, 1, d), _bidx(scale)),
                  pl.BlockSpec((N_EXPERTS, d), lambda b, i: (0, 0)),
                  pl.BlockSpec((N_EXPERTS, 1), lambda b, i: (0, 0))],
        out_specs=[pl.BlockSpec((None, tl, d), lambda b, i: (b, i, 0)),
                   pl.BlockSpec((N_EXPERTS, tl), lambda b, i: (0, b * nl + i)),
                   pl.BlockSpec((N_EXPERTS, tl), lambda b, i: (0, b * nl + i)),
                   pl.BlockSpec((None, N_EXPERTS, LANES), lambda b, i: (b * nl + i, 0, 0))],
        out_shape=[jax.ShapeDtypeStruct((bsz, l, d), bf16),
                   jax.ShapeDtypeStruct((N_EXPERTS, bsz * l), f32),
                   jax.ShapeDtypeStruct((N_EXPERTS, bsz * l), f32),
                   jax.ShapeDtypeStruct((bsz * nl, N_EXPERTS, LANES), f32)],
        compiler_params=_cp("parallel", "parallel"),
        name="router",
    )(x, g.reshape(1, d), shift, scale, rw_t, rb.reshape(N_EXPERTS, 1))


def _swiglu_act(hh):
    half = hh.shape[1] // 2
    return _silu(hh[:, :half]) * hh[:, half:]


def _moe_kernel(cnt_ref, ord_ref, h_ref, rk_ref, gt_ref, *refs):
    ng = MOE_GROUP
    gu_refs, dn_refs = refs[:ng], refs[ng:2 * ng]
    (sgu_ref, sdn_ref, x_ref, g_ref, gate_ref, o_ref,
     acc_ref, xg_ref, ys_ref, p_ref, gr_ref) = refs[2 * ng:]
    tile = pl.program_id(0)
    grp = pl.program_id(1)
    tm, d = acc_ref.shape
    ns = tm // MOE_SUB
    win = MOE_WIN
    eids = [ord_ref[tile, grp * ng + el] for el in range(ng)]

    @pl.when(grp == 0)
    def _():
        act = _swiglu_act(_dot(h_ref[...], sgu_ref[...]))
        acc_ref[...] = _dot(act.astype(bf16), sdn_ref[...])

    riota = lax.broadcasted_iota(jnp.int32, (win, MOE_SUB), 0).astype(f32)

    def expert_ffn(el):
        hh = _dot(xg_ref[el], gu_refs[el][...])
        gr = gr_ref[el]
        act = _swiglu_act(hh) * jnp.concatenate([gr] * (hh.shape[1] // (2 * LANES)), axis=1)
        y = _dot(act.astype(bf16), dn_refs[el][...]).astype(bf16)
        for s in range(ns):
            ys_ref[s, el * win:(el + 1) * win, :] = y[s * win:(s + 1) * win]

    def one_pass(p, skip_idle):
        base = p * win
        for s in range(ns):
            cols = slice(s * MOE_SUB, (s + 1) * MOE_SUB)
            onehots = []
            for el in range(ng):
                row = pl.ds(eids[el], 1)
                hit = (rk_ref[row, cols] - base) == riota
                onehots.append(jnp.where(hit, 1.0, 0.0).astype(bf16))
                gsel = jnp.sum(jnp.where(hit, gt_ref[row, cols], 0.0), axis=1, keepdims=True)
                gr_ref[el, s * win:(s + 1) * win, :] = jnp.broadcast_to(gsel, (win, LANES))
            pm = jnp.concatenate(onehots, axis=0)
            p_ref[s] = pm
            gx = _dot(pm, h_ref[cols, :])
            for el in range(ng):
                xg_ref[el, s * win:(s + 1) * win, :] = gx[el * win:(el + 1) * win].astype(bf16)
        for el in range(ng):
            if not skip_idle:
                expert_ffn(el)
                continue
            busy = cnt_ref[tile, eids[el]] > base
            pl.when(busy)(functools.partial(expert_ffn, el))

            @pl.when(jnp.logical_not(busy))
            def _():
                for s in range(ns):
                    ys_ref[s, el * win:(el + 1) * win, :] = jnp.zeros((win, d), bf16)
        for s in range(ns):
            acc_ref[s * MOE_SUB:(s + 1) * MOE_SUB, :] += _dot(p_ref[s], ys_ref[s], _TN)

    one_pass(0, False)
    most = cnt_ref[tile, eids[0]]
    for el in range(1, ng):
        most = jnp.maximum(most, cnt_ref[tile, eids[el]])

    def later_pass(p, carry):
        one_pass(p, True)
        return carry

    lax.fori_loop(1, (most + win - 1) // win, later_pass, 0)

    @pl.when(grp == pl.num_programs(1) - 1)
    def _():
        mo = acc_ref[...]
        y = mo * lax.rsqrt(jnp.mean(mo * mo, axis=-1, keepdims=True) + RMS_EPS) * g_ref[...]
        o_ref[...] = x_ref[...] + gate_ref[...] * y


def _moe(h2, rk, gt, cmax, gu, dn, sgu, sdn, x, g, gate, tm=1024):
    bsz, l, d = x.shape
    tm = min(tm, l)
    per_b = l // tm
    nt = bsz * per_b
    ne = gu.shape[0]
    ng = MOE_GROUP
    ns = tm // MOE_SUB
    counts = jnp.max(cmax[:, :, 0].reshape(nt, -1, ne), axis=1).astype(jnp.int32)
    order = jnp.argsort(-counts, axis=1).astype(jnp.int32)

    def expert_spec(arr, k):
        return pl.BlockSpec((None,) + arr.shape[1:], lambda t, e, cnt, order_ref: (order_ref[t, e * ng + k], 0, 0))

    grid_spec = pltpu.PrefetchScalarGridSpec(
        num_scalar_prefetch=2,
        grid=(nt, ne // ng),
        in_specs=[pl.BlockSpec((tm, d), lambda t, e, *_: (t, 0)),
                  pl.BlockSpec((ne, tm), lambda t, e, *_: (0, t)),
                  pl.BlockSpec((ne, tm), lambda t, e, *_: (0, t))]
                 + [expert_spec(gu, k) for k in range(ng)]
                 + [expert_spec(dn, k) for k in range(ng)]
                 + [pl.BlockSpec(sgu.shape, lambda t, e, *_: (0, 0)),
                    pl.BlockSpec(sdn.shape, lambda t, e, *_: (0, 0)),
                    pl.BlockSpec((tm, d), lambda t, e, *_: (t, 0)),
                    pl.BlockSpec((1, d), lambda t, e, *_: (0, 0)),
                    pl.BlockSpec((None, 1, d), lambda t, e, *_: (t // per_b, 0, 0))],
        out_specs=pl.BlockSpec((tm, d), lambda t, e, *_: (t, 0)),
        scratch_shapes=[pltpu.VMEM((tm, d), f32),
                        pltpu.VMEM((ng, ns * MOE_WIN, d), bf16),
                        pltpu.VMEM((ns, ng * MOE_WIN, d), bf16),
                        pltpu.VMEM((ns, ng * MOE_WIN, MOE_SUB), bf16),
                        pltpu.VMEM((ng, ns * MOE_WIN, LANES), f32)])
    out = pl.pallas_call(
        _moe_kernel,
        grid_spec=grid_spec,
        out_shape=jax.ShapeDtypeStruct((bsz * l, d), f32),
        compiler_params=_cp("parallel", "arbitrary"),
        name="moe",
    )(counts, order, h2.reshape(bsz * l, d), rk, gt, *([gu] * ng), *([dn] * ng), sgu, sdn,
      x.reshape(bsz * l, d), g.reshape(1, d), gate)
    return out.reshape(bsz, l, d)


def _filter_kernel(z_ref, w1_ref, b1_ref, w2_ref, b2_ref, w3_ref, win_ref, o_ref):
    hid = jnp.sin(FILTER_SIN_W * (_dot3(z_ref[...], w1_ref[...]) + b1_ref[...]))
    hid = jnp.sin(FILTER_SIN_W * (_dot3(hid, w2_ref[...]) + b2_ref[...]))
    o_ref[...] = _dot3(hid, w3_ref[...]) * win_ref[...]


def _filters(z, w1, b1, w2, b2, w3, window, tn=512):
    l, p = z.shape
    hdim = w1.shape[1]
    n = w3.shape[1]
    d = window.shape[1]
    nd = d // tn
    return pl.pallas_call(
        _filter_kernel,
        grid=(n // tn,),
        in_specs=[pl.BlockSpec((l, p), lambda j: (0, 0)),
                  pl.BlockSpec((p, hdim), lambda j: (0, 0)),
                  pl.BlockSpec((1, hdim), lambda j: (0, 0)),
                  pl.BlockSpec((hdim, hdim), lambda j: (0, 0)),
                  pl.BlockSpec((1, hdim), lambda j: (0, 0)),
                  pl.BlockSpec((hdim, tn), lambda j: (0, j)),
                  pl.BlockSpec((l, tn), lambda j: (0, j % nd))],
        out_specs=pl.BlockSpec((l, tn), lambda j: (0, j)),
        out_shape=jax.ShapeDtypeStruct((l, n), f32),
        compiler_params=_cp("arbitrary"),
        name="hyena_filter",
    )(z, w1, b1.reshape(1, hdim), w2, b2.reshape(1, hdim), w3, window)


def _dft_tables(l):
    n = 2 * l
    n1 = math.isqrt(n)
    assert n == n1 * n1 and n1 % 16 == 0
    na = l // n1
    ncp = -(-(n1 // 2 + 1) // 8) * 8
    a = np.arange(na)
    b = np.arange(n1)
    c = np.arange(ncp)
    th = 2.0 * np.pi * ((n1 * a[None, None, :] + b[:, None, None]) * c[None, :, None]) / n
    t1 = np.concatenate([np.cos(th), -np.sin(th)], axis=1)
    ph = 2.0 * np.pi * (b[:, None] * b[None, :]) / n1
    cs, sn = np.cos(ph), np.sin(ph)
    a3 = np.block([[cs, sn], [-sn, cs]])
    a3i = np.block([[cs, -sn], [sn, cs]])
    a2 = np.arange(na) + na // 2
    th2 = 2.0 * np.pi * ((n1 * a2[None, :, None] + b[:, None, None]) * c[None, None, :]) / n
    wc = np.where((c == 0) | (c == n1 // 2), 1.0, np.where(c < n1 // 2, 2.0, 0.0))[None, None, :]
    t2 = np.concatenate([wc * np.cos(th2), -wc * np.sin(th2)], axis=2)
    return [jnp.asarray(t, f32).astype(bf16) for t in (t1, a3, a3i, t2)]


def _fft_dims(t1):
    n1, ncp2, na = t1.shape
    ncp = ncp2 // 2
    return n1, ncp, na, 2 * n1 + FFT_PAD, 2 * ncp + FFT_PAD, n1 + FFT_PAD


def _ld(ref, rows):
    return jnp.concatenate([ref[j, rows, :] for j in range(ref.shape[0])], axis=1)


def _st(ref, rows, val):
    for j in range(ref.shape[0]):
        ref[j, rows, :] = val[:, j * LANES:(j + 1) * LANES]


def _dft_forward(uf_ref, t1_ref, zs_ref):
    n1, ncp, na, sb, _, su = _fft_dims(t1_ref)
    for b in range(n1):
        ub = _ld(uf_ref, pl.ds(b, na, stride=su)).astype(bf16)
        zb = _dot(t1_ref[b], ub)
        _st(zs_ref, pl.ds(b, ncp, stride=sb), zb[:ncp])
        _st(zs_ref, pl.ds(n1 + b, ncp, stride=sb), zb[ncp:])


def _spectrum_kernel(f_ref, t1_ref, a3_ref, o_ref, uf_ref, zs_ref, *, scale):
    n1, ncp, na, sb, _, su = _fft_dims(t1_ref)
    for a in range(na):
        _st(uf_ref, pl.ds(a * su, n1), f_ref[pl.ds(a * n1, n1), :])
    _dft_forward(uf_ref, t1_ref, zs_ref)
    a3 = a3_ref[...]
    for c in range(ncp):
        zc = _ld(zs_ref, pl.ds(c * sb, 2 * n1)).astype(bf16)
        o_ref[c] = (_dot(a3, zc) * scale).astype(o_ref.dtype)


def _spectrum(filt, tabs, dt=256):
    l, n = filt.shape
    t1, a3, _, _ = tabs
    n1, ncp, na, sb, _, su = _fft_dims(t1)
    nj = dt // LANES
    return pl.pallas_call(
        functools.partial(_spectrum_kernel, scale=1.0 / (2 * l)),
        grid=(n // dt,),
        in_specs=[pl.BlockSpec((l, dt), lambda j: (0, j)),
                  pl.BlockSpec(t1.shape, lambda j: (0, 0, 0)),
                  pl.BlockSpec(a3.shape, lambda j: (0, 0))],
        out_specs=pl.BlockSpec((ncp, 2 * n1, dt), lambda j: (0, 0, j)),
        out_shape=jax.ShapeDtypeStruct((ncp, 2 * n1, n), bf16),
        scratch_shapes=[pltpu.VMEM((nj, na * su, LANES), f32),
                        pltpu.VMEM((nj, ncp * sb, LANES), f32)],
        compiler_params=_cp("arbitrary"),
        name="hyena_spectrum",
    )(filt, t1, a3)


def _fftconv_kernel(u_ref, xg_ref, kf_ref, fb_ref, t1_ref, a3_ref, a3i_ref, t2_ref, o_ref,
                    uf_ref, zs_ref, qs_ref, y_ref):
    n1, ncp, na, sb, sq, su = _fft_dims(t1_ref)
    for a in range(na):
        _st(uf_ref, pl.ds(a * su, n1), u_ref[pl.ds(a * n1, n1), :].astype(f32))
    _dft_forward(uf_ref, t1_ref, zs_ref)
    a3 = a3_ref[...]
    a3i = a3i_ref[...]
    for c in range(ncp):
        zc = _ld(zs_ref, pl.ds(c * sb, 2 * n1)).astype(bf16)
        xc = _dot(a3, zc)
        kc = kf_ref[c].astype(f32)
        xr, xi = xc[:n1], xc[n1:]
        kr, ki = kc[:n1], kc[n1:]
        pc = jnp.concatenate([xr * kr - xi * ki, xr * ki + xi * kr], axis=0).astype(bf16)
        qc = _dot(a3i, pc)
        _st(qs_ref, pl.ds(c, n1, stride=sq), qc[:n1])
        _st(qs_ref, pl.ds(ncp + c, n1, stride=sq), qc[n1:])
    for b in range(n1):
        qb = _ld(qs_ref, pl.ds(b * sq, 2 * ncp)).astype(bf16)
        _st(y_ref, pl.ds(b, na, stride=su), _dot(t2_ref[b], qb))
    fb = fb_ref[...]
    for a in range(na):
        rows = pl.ds(a * n1, n1)
        uv = _ld(uf_ref, pl.ds(a * su, n1))
        yv = _ld(y_ref, pl.ds(a * su, n1))
        o_ref[rows, :] = (xg_ref[rows, :].astype(f32) * (yv + uv * fb)).astype(o_ref.dtype)


def _fftconv(u, u_col, xg, xg_col, kf, kf_col, fbias, tabs, d, dt=256):
    bsz, l, _ = u.shape
    t1, a3, a3i, t2 = tabs
    n1, ncp, na, sb, sq, su = _fft_dims(t1)
    nd = d // dt
    nj = dt // LANES
    uo, go, ko = u_col // dt, xg_col // dt, kf_col // dt
    return pl.pallas_call(
        _fftconv_kernel,
        grid=(nd, bsz),
        in_specs=[pl.BlockSpec((None, l, dt), lambda j, b: (b, 0, j + uo)),
                  pl.BlockSpec((None, l, dt), lambda j, b: (b, 0, j + go)),
                  pl.BlockSpec((ncp, 2 * n1, dt), lambda j, b: (0, 0, j + ko)),
                  pl.BlockSpec((1, dt), lambda j, b: (0, j)),
                  pl.BlockSpec(t1.shape, lambda j, b: (0, 0, 0)),
                  pl.BlockSpec(a3.shape, lambda j, b: (0, 0)),
                  pl.BlockSpec(a3i.shape, lambda j, b: (0, 0)),
                  pl.BlockSpec(t2.shape, lambda j, b: (0, 0, 0))],
        out_specs=pl.BlockSpec((None, l, dt), lambda j, b: (b, 0, j)),
        out_shape=jax.ShapeDtypeStruct((bsz, l, d), bf16),
        scratch_shapes=[pltpu.VMEM((nj, na * su, LANES), f32),
                        pltpu.VMEM((nj, ncp * sb, LANES), f32),
                        pltpu.VMEM((nj, n1 * sq, LANES), f32),
                        pltpu.VMEM((nj, na * su, LANES), f32)],
        compiler_params=_cp("parallel", "arbitrary"),
        name="hyena_fftconv",
    )(u, xg, kf, fbias.reshape(1, d), t1, a3, a3i, t2)


def _rope_tables(l):
    rows = l // GRID_W
    row = jnp.repeat(jnp.arange(rows), GRID_W)
    col = jnp.tile(jnp.arange(GRID_W), rows)
    inv = ROPE_BASE ** (-jnp.arange(ROPE_AXIS_PAIRS, dtype=f32) / ROPE_AXIS_PAIRS)
    ang = jnp.stack([row, col], axis=-1).astype(f32)[..., None] * inv
    ang = jnp.broadcast_to(ang[:, :, None, :], (l, 2, 2, ROPE_AXIS_PAIRS)).reshape(l, A_DQK)
    reps = A_QW // A_DQK
    return jnp.tile(jnp.cos(ang), (1, reps)), jnp.tile(jnp.sin(ang), (1, reps))


def _rotate_cols(w):
    j = np.arange(w.shape[1])
    lo = (j % (2 * ROPE_AXIS_PAIRS)) < ROPE_AXIS_PAIRS
    perm = np.where(lo, j + ROPE_AXIS_PAIRS, j - ROPE_AXIS_PAIRS)
    sign = np.where(lo, -1.0, 1.0).astype(np.float32)
    return w[:, perm] * sign


def _gate_cols(w_g, b_g):
    idx_i = np.array([d * 2 * B_HEADS + hd for d in range(2) for hd in range(B_HEADS)])
    idx_f = idx_i + B_HEADS
    pad = LANES - _NCHAIN
    k = w_g.shape[0]
    w = jnp.concatenate([w_g[:, idx_i], jnp.zeros((k, pad), f32),
                         w_g[:, idx_f], jnp.zeros((k, pad), f32)], axis=1)
    b = jnp.concatenate([b_g[idx_i], jnp.zeros((pad,), f32), b_g[idx_f], jnp.zeros((pad,), f32)])
    return w, b


def _hyena_consts(l, d):
    j = jnp.arange(l, dtype=f32)
    bands = (POS_EMB_DIM - 1) // 2
    freqs = jnp.linspace(1e-4, bands - 1, bands, dtype=f32)
    ang = (2.0 * math.pi / l) * j[:, None] * freqs[None, :]
    z = jnp.concatenate([(j / (l - 1))[:, None], jnp.cos(ang), -jnp.sin(ang)], axis=-1)
    dist = jnp.abs(j - l // 2) / (l // 2)
    max_decay = math.log(DECAY_TARGET) / DECAY_FAST_PCT
    min_decay = math.log(DECAY_TARGET) / DECAY_SLOW_PCT
    deltas = jnp.abs(jnp.linspace(min_decay, max_decay, d, dtype=f32))
    window = jnp.exp(-dist[:, None] * deltas[None, :])
    return z, window


def _ab_layer(x, ctx, mod_vecs, mod_ctx, norm_g, w_in, conv_w, conv_b, gate_b, lam_vecs,
              g_a, g_b, w_out, lam_init):
    sh_m, sc_m, g_m = mod_vecs
    bsz, s, d = x.shape
    h = _norm(x, norm_g[0], sh_m, sc_m)
    hc = _norm(ctx, norm_g[0], mod_ctx[0], mod_ctx[1])
    w = B_WIDTH
    o = 0
    cols = {}
    for name, width in (("aq", A_QW), ("bq", w), ("bo", w), ("ak", A_QW), ("av", A_VW),
                        ("bk", w), ("bv", w), ("g", 4 * B_HEADS)):
        cols[name] = w_in[:, o:o + width]
        o += width
    cos, sin = _rope_tables(s)
    cat = lambda *ws: jnp.concatenate(ws, axis=1).astype(bf16)
    q = _mm(h, cat(cols["aq"], _rotate_cols(cols["aq"])), rope=(cos, sin, A_DQK ** -0.5))
    k = _mm(h, cat(cols["ak"], _rotate_cols(cols["ak"])), rope=(cos, sin, 1.0))
    qk = _mm(h, cat(cols["bq"], cols["bk"]), conv=(conv_w, conv_b, True))
    vvo = _mm(h, cat(cols["av"], cols["bv"], cols["bo"]))
    wg, bg = _gate_cols(cols["g"], gate_b)
    gates = _mm(h, wg.astype(bf16), out_dtype=f32, bias=bg, tn=2 * LANES)
    ckv = _mm(hc, cat(cols["ak"], cols["av"], cols["bv"]))
    cbk = _mm(hc, cols["bk"].astype(bf16), conv=(conv_w[:, w:], conv_b[w:], True))
    cg = _mm(hc, wg.astype(bf16), out_dtype=f32, bias=bg, tn=2 * LANES)
    out_a = _attn(lam_vecs, q, k, vvo, ckv, g_a, lam_init)
    out_b = _mlstm(qk, vvo, gates, cbk, ckv, cg, g_b)
    wo = w_out.astype(bf16)
    return _out_proj([out_a, out_b], [wo[:A_VW], wo[A_VW:]], x, norm_g[1], g_m)


def _hyena_layer(x, mod_vecs, norm_g, w_in, conv_w, conv_b, fw1, fb1, fw2, fb2, fw3, fbias, w_out):
    sh_m, sc_m, g_m = mod_vecs
    bsz, l, d = x.shape
    h = _norm(x, norm_g[0], sh_m, sc_m)
    u = _mm(h, w_in.astype(bf16), conv=(conv_w, conv_b, False))
    z, window = _hyena_consts(l, d)
    pz, ph = LANES - z.shape[1], LANES - fw1.shape[1]
    filt = _filters(jnp.pad(z, ((0, 0), (0, pz))), jnp.pad(fw1, ((0, pz), (0, ph))),
                    jnp.pad(fb1, (0, ph)), jnp.pad(fw2, ((0, ph), (0, ph))), jnp.pad(fb2, (0, ph)),
                    jnp.pad(fw3, ((0, ph), (0, 0))), window)
    tabs = _dft_tables(l)
    kf = _spectrum(filt, tabs)
    zz = _fftconv(u, 0, u, d, kf, 0, fbias[0], tabs, d)
    y = _fftconv(zz, 0, u, 2 * d, kf, d, fbias[1], tabs, d)
    return _out_proj([y], [w_out.astype(bf16)], x, norm_g[1], g_m)


def kernel(x, c, ctx, c_ctx, w_mod, b_mod, norm_g, w_in_ab, conv_ab_w, conv_ab_b, gate_b_ab, diff_lambda, head_g_a, head_g_b, w_out_ab, w_in_hy, conv_hy_w, conv_hy_b, filt_w1, filt_b1, filt_w2, filt_b2, filt_w3, filt_bias, w_out_hy, router_w, router_b, exp_gu, exp_down, sh_gu, sh_down):
    bsz, s, d = x.shape
    depth = w_mod.shape[0]
    rows = -(-(bsz + 1) // 8) * 8
    cc = jnp.concatenate([c, c_ctx[None, :], jnp.zeros((rows - bsz - 1, d), f32)], axis=0)
    for l in range(depth):
        mod = _mod(cc, w_mod[l], b_mod[l])
        vec = lambda i: mod[:bsz, i * d:(i + 1) * d].reshape(bsz, 1, d)
        sh_m, sc_m, g_m, sh_f, sc_f, g_f = [vec(i) for i in range(6)]
        if l % 2 == 0:
            e = l // 2
            lam_init = 0.8 - 0.6 * math.exp(-0.3 * l)
            mod_ctx = (mod[bsz:bsz + 1, 0:d].reshape(1, 1, d), mod[bsz:bsz + 1, d:2 * d].reshape(1, 1, d))
            x = _ab_layer(x, ctx, (sh_m, sc_m, g_m), mod_ctx, norm_g[l], w_in_ab[e], conv_ab_w[e],
                          conv_ab_b[e], gate_b_ab[e], diff_lambda[e], head_g_a[e], head_g_b[e],
                          w_out_ab[e], lam_init)
        else:
            o = l // 2
            x = _hyena_layer(x, (sh_m, sc_m, g_m), norm_g[l], w_in_hy[o], conv_hy_w[o], conv_hy_b[o],
                             filt_w1[o], filt_b1[o], filt_w2[o], filt_b2[o], filt_w3[o], filt_bias[o],
                             w_out_hy[o])
        h2, rk, gt, cmax = _router(x, norm_g[l, 2], sh_f, sc_f, router_w[l].T, router_b[l])
        x = _moe(h2, rk, gt, cmax, exp_gu[l].astype(bf16), exp_down[l].astype(bf16), sh_gu[l].astype(bf16),
                 sh_down[l].astype(bf16), x, norm_g[l, 3], g_f)
    return x
```

```python
import functools
import math

import numpy as np
import jax
import jax.numpy as jnp
from jax import lax
from jax.experimental import pallas as pl
from jax.experimental.pallas import tpu as pltpu

f32 = jnp.float32
bf16 = jnp.bfloat16

RMS_EPS = 1e-6
A_HEADS = 4
A_DQK = 64
A_DV = 128
B_HEADS = 4
B_DH = 128
B_WIDTH = B_HEADS * B_DH
A_QW = A_HEADS * 2 * A_DQK
A_VW = A_HEADS * A_DV
GRID_W = 64
ROPE_BASE = 10000.0
ROPE_AXIS_PAIRS = A_DQK // 4
N_EXPERTS = 64
TOP_K = 8
N_GROUPS = 8
TOPK_GROUPS = 4
D_EXPERT = 256
ROUTED_SCALE = 2.5
HY_ORDER = 2
POS_EMB_DIM = 33
FILTER_SIN_W = 1.0
DECAY_FAST_PCT = 0.3
DECAY_SLOW_PCT = 1.5
DECAY_TARGET = 1e-2

LANES = 128
VMEM_LIMIT = 56 * 1024 * 1024
MLSTM_CHUNK = 256
FFT_PAD = 8
MOE_SUB = 256
MOE_WIN = 64
MOE_GROUP = 4


def _cp(*sem):
    return pltpu.CompilerParams(dimension_semantics=sem, vmem_limit_bytes=VMEM_LIMIT)


def _split_bf16(a):
    hi = a.astype(bf16)
    lo = (a - hi.astype(f32)).astype(bf16)
    return hi, lo


def _dot(a, b, dims=(((1,), (0,)), ((), ()))):
    return lax.dot_general(a, b, dims, preferred_element_type=f32)


_NT = (((1,), (1,)), ((), ()))
_TN = (((0,), (0,)), ((), ()))


def _dot3(a, b, dims=(((1,), (0,)), ((), ()))):
    ah, al = _split_bf16(a)
    bh, bl = _split_bf16(b)
    return _dot(ah, bh, dims) + (_dot(ah, bl, dims) + _dot(al, bh, dims))


def _silu(v):
    return v / (1.0 + jnp.exp(-v))


def _sigmoid(v):
    return 1.0 / (1.0 + jnp.exp(-v))


def _log_sigmoid(v):
    return jnp.minimum(v, 0.0) - jnp.log(1.0 + jnp.exp(-jnp.abs(v)))


def _mod_kernel(c_ref, w_ref, b_ref, o_ref):
    o_ref[...] = _dot3(_silu(c_ref[...]), w_ref[...]) + b_ref[...]


def _mod(cc, w, b):
    rows, d = cc.shape
    n = w.shape[1]
    tn = d
    return pl.pallas_call(
        _mod_kernel,
        grid=(n // tn,),
        in_specs=[pl.BlockSpec((rows, d), lambda j: (0, 0)),
                  pl.BlockSpec((d, tn), lambda j: (0, j)),
                  pl.BlockSpec((1, tn), lambda j: (0, j))],
        out_specs=pl.BlockSpec((rows, tn), lambda j: (0, j)),
        out_shape=jax.ShapeDtypeStruct((rows, n), f32),
        compiler_params=_cp("arbitrary"),
        name="mod",
    )(cc, w, b.reshape(1, n))


def _norm_mod(xv, g, shift, scale):
    y = xv * lax.rsqrt(jnp.mean(xv * xv, axis=-1, keepdims=True) + RMS_EPS)
    return (y * g) * (1.0 + scale) + shift


def _norm_kernel(x_ref, g_ref, sh_ref, sc_ref, o_ref):
    o_ref[...] = _norm_mod(x_ref[...], g_ref[...], sh_ref[...], sc_ref[...]).astype(o_ref.dtype)


def _bidx(arr):
    if arr.shape[0] == 1:
        return lambda b, *_: (0, 0, 0)
    return lambda b, *_: (b, 0, 0)


def _norm(x, g, shift, scale, tl=512):
    bsz, l, d = x.shape
    tl = min(tl, l)
    return pl.pallas_call(
        _norm_kernel,
        grid=(bsz, l // tl),
        in_specs=[pl.BlockSpec((None, tl, d), lambda b, i: (b, i, 0)),
                  pl.BlockSpec((1, d), lambda b, i: (0, 0)),
                  pl.BlockSpec((None, 1, d), _bidx(shift)),
                  pl.BlockSpec((None, 1, d), _bidx(scale))],
        out_specs=pl.BlockSpec((None, tl, d), lambda b, i: (b, i, 0)),
        out_shape=jax.ShapeDtypeStruct((bsz, l, d), bf16),
        compiler_params=_cp("parallel", "parallel"),
        name="norm",
    )(x, g.reshape(1, d), shift, scale)


def _mm_plain_kernel(h_ref, w_ref, b_ref, o_ref):
    o_ref[...] = (_dot(h_ref[...], w_ref[...]) + b_ref[...]).astype(o_ref.dtype)


def _mm_rope_kernel(h_ref, w_ref, cos_ref, sin_ref, o_ref, *, scale):
    p = _dot(h_ref[...], w_ref[...])
    n = o_ref.shape[-1]
    o_ref[...] = ((p[:, :n] * cos_ref[...] + p[:, n:] * sin_ref[...]) * scale).astype(o_ref.dtype)


def _mm_conv_kernel(h_ref, w_ref, cw_ref, cb_ref, o_ref, scr_ref, *, act, rc):
    l = h_ref.shape[0]
    w = w_ref[...]
    w0, w1, w2, cb = cw_ref[0:1, :], cw_ref[1:2, :], cw_ref[2:3, :], cb_ref[...]
    halo = 16
    zrow = jnp.zeros((8, o_ref.shape[1]), f32)
    for c in range(l // rc):
        lo, hi = max(c * rc - halo, 0), min((c + 1) * rc + halo, l)
        n = hi - lo
        scr = scr_ref.at[c % 2]
        scr[8:8 + n, :] = _dot(h_ref[lo:hi, :], w)
        if lo == 0:
            scr[0:8, :] = zrow
        if hi == l:
            scr[8 + n:16 + n, :] = zrow
        off = 8 + c * rc - lo
        y = (scr[off - 1:off - 1 + rc, :] * w0 + scr[off:off + rc, :] * w1
             + scr[off + 1:off + 1 + rc, :] * w2 + cb)
        o_ref[c * rc:(c + 1) * rc, :] = (_silu(y) if act else y).astype(o_ref.dtype)


def _mm(h, w, *, out_dtype=bf16, bias=None, rope=None, conv=None, tl=512, tn=512):
    bsz, l, k = h.shape
    n = w.shape[1]
    if rope is not None:
        cos, sin, scale = rope
        n_out = n // 2
        tl = min(tl, l)
        return pl.pallas_call(
            functools.partial(_mm_rope_kernel, scale=scale),
            grid=(bsz, l // tl),
            in_specs=[pl.BlockSpec((None, tl, k), lambda b, i: (b, i, 0)),
                      pl.BlockSpec((k, n), lambda b, i: (0, 0)),
                      pl.BlockSpec((tl, n_out), lambda b, i: (i, 0)),
                      pl.BlockSpec((tl, n_out), lambda b, i: (i, 0))],
            out_specs=pl.BlockSpec((None, tl, n_out), lambda b, i: (b, i, 0)),
            out_shape=jax.ShapeDtypeStruct((bsz, l, n_out), out_dtype),
            compiler_params=_cp("parallel", "parallel"),
            name="mm_rope",
        )(h, w, cos, sin)
    tn = min(tn, n)
    if conv is not None:
        cw, cb, act = conv
        rc = min(512, l)
        return pl.pallas_call(
            functools.partial(_mm_conv_kernel, act=act, rc=rc),
            grid=(bsz, n // tn),
            in_specs=[pl.BlockSpec((None, l, k), lambda b, j: (b, 0, 0)),
                      pl.BlockSpec((k, tn), lambda b, j: (0, j)),
                      pl.BlockSpec((3, tn), lambda b, j: (0, j)),
                      pl.BlockSpec((1, tn), lambda b, j: (0, j))],
            out_specs=pl.BlockSpec((None, l, tn), lambda b, j: (b, 0, j)),
            out_shape=jax.ShapeDtypeStruct((bsz, l, n), out_dtype),
            scratch_shapes=[pltpu.VMEM((2, rc + 48, tn), f32)],
            compiler_params=_cp("parallel", "arbitrary"),
            name="mm_conv",
        )(h, w, cw, cb.reshape(1, n))
    if bias is None:
        bias = jnp.zeros((n,), f32)
    tl = min(tl, l)
    return pl.pallas_call(
        _mm_plain_kernel,
        grid=(bsz, l // tl, n // tn),
        in_specs=[pl.BlockSpec((None, tl, k), lambda b, i, j: (b, i, 0)),
                  pl.BlockSpec((k, tn), lambda b, i, j: (0, j)),
                  pl.BlockSpec((1, tn), lambda b, i, j: (0, j))],
        out_specs=pl.BlockSpec((None, tl, tn), lambda b, i, j: (b, i, j)),
        out_shape=jax.ShapeDtypeStruct((bsz, l, n), out_dtype),
        compiler_params=_cp("parallel", "parallel", "arbitrary"),
        name="mm_plain",
    )(h, w, bias.reshape(1, n))


def _attn_kernel(lv_ref, q_ref, kc_ref, k_ref, vc_ref, v_ref, g_ref, o_ref, *, lam_init):
    tq = q_ref.shape[0]
    lv = lv_ref[...]
    lam = (jnp.exp(jnp.sum(lv[0:1] * lv[1:2], axis=1, keepdims=True))
           - jnp.exp(jnp.sum(lv[2:3] * lv[3:4], axis=1, keepdims=True)) + lam_init)
    first = lax.broadcasted_iota(jnp.int32, (tq, A_DV), 1) < A_DQK
    one0 = jnp.where(lax.broadcasted_iota(jnp.int32, (1, A_DV), 1) == 0, 1.0, 0.0).astype(bf16)
    ones_c = jnp.broadcast_to(one0, (kc_ref.shape[0], A_DV))
    ones_l = jnp.broadcast_to(one0, (k_ref.shape[0], A_DV))
    for hd in range(A_HEADS):
        cs = slice(hd * A_DV, (hd + 1) * A_DV)
        qh = q_ref[:, cs]
        zero = jnp.zeros_like(qh)
        q2 = jnp.concatenate([jnp.where(first, qh, zero), jnp.where(first, zero, qh)], axis=0)
        s_c = _dot(q2, kc_ref[:, cs], _NT)
        s_l = _dot(q2, k_ref[:, cs], _NT)
        m = jnp.maximum(jnp.max(s_c, axis=1, keepdims=True), jnp.max(s_l, axis=1, keepdims=True))
        p_c = jnp.exp((s_c - m).astype(bf16))
        p_l = jnp.exp((s_l - m).astype(bf16))
        oa = (_dot(p_c, jnp.concatenate([vc_ref[:, cs], ones_c], axis=1))
              + _dot(p_l, jnp.concatenate([v_ref[:, cs], ones_l], axis=1)))
        on = oa[:, :A_DV] * (1.0 / oa[:, A_DV:A_DV + 1])
        o = on[:tq] - lam * on[tq:]
        o = o * lax.rsqrt(jnp.mean(o * o, axis=1, keepdims=True) + RMS_EPS)
        o_ref[:, cs] = (o * g_ref[:, cs] * (1.0 - lam_init)).astype(o_ref.dtype)


def _attn(lv, q, k, vvo, ckv, g_a, lam_init, tq=256):
    bsz, s, _ = q.shape
    lc = ckv.shape[1]
    tq = min(tq, s)
    w = A_QW
    return pl.pallas_call(
        functools.partial(_attn_kernel, lam_init=lam_init),
        grid=(bsz, s // tq),
        in_specs=[pl.BlockSpec(lv.shape, lambda b, i: (0, 0)),
                  pl.BlockSpec((None, tq, w), lambda b, i: (b, i, 0)),
                  pl.BlockSpec((None, lc, w), lambda b, i: (b, 0, 0)),
                  pl.BlockSpec((None, s, w), lambda b, i: (b, 0, 0)),
                  pl.BlockSpec((None, lc, w), lambda b, i: (b, 0, 1)),
                  pl.BlockSpec((None, s, w), lambda b, i: (b, 0, 0)),
                  pl.BlockSpec((1, w), lambda b, i: (0, 0))],
        out_specs=pl.BlockSpec((None, tq, w), lambda b, i: (b, i, 0)),
        out_shape=jax.ShapeDtypeStruct((bsz, s, w), bf16),
        compiler_params=_cp("parallel", "arbitrary"),
        name="diff_attn",
    )(lv, q, ckv, k, ckv, vvo, g_a.reshape(1, w))


_LN_QSCALE = math.log(B_DH ** -0.5)
_NCHAIN = 2 * B_HEADS


def _chunk_gate_sums(gi, gf, tri):
    lf = _log_sigmoid(gf)
    hi, lo = _split_bf16(lf)
    cum = _dot(tri, hi) + _dot(tri, lo)
    t = gf.shape[0]
    tot = cum[t - 1:t, :]
    rcum = tot - cum + lf
    fwd = lax.broadcasted_iota(jnp.int32, gf.shape, 1) < B_HEADS
    bd = jnp.where(fwd, cum, rcum)
    return bd, tot, (bd - gi).T


def _lower_tri(t):
    r = lax.broadcasted_iota(jnp.int32, (t, t), 0)
    c = lax.broadcasted_iota(jnp.int32, (t, t), 1)
    return r, c


def _ones_block(t):
    one0 = jnp.where(lax.broadcasted_iota(jnp.int32, (1, B_DH), 1) == 0, 1.0, 0.0).astype(bf16)
    return jnp.broadcast_to(one0, (t, B_DH))


def _absorb(c_ref, m_ref, ch, x_row, tot_c, kb, vaug):
    m_prev = m_ref[ch][:, 0:1]
    g = tot_c - x_row
    m_new = jnp.maximum(tot_c + m_prev, jnp.max(g, axis=1, keepdims=True))
    wgt = jnp.exp(g - m_new)
    decay = jnp.exp(tot_c + m_prev - m_new)
    kw_t = kb.astype(f32).T * wgt
    c_ref[ch] = decay * c_ref[ch] + _dot(kw_t.astype(bf16), vaug)
    m_ref[ch] = jnp.broadcast_to(m_new, m_ref.shape[1:])


def _mlstm_kernel(qk_ref, vvo_ref, g_ref, ck_ref, ckv_ref, cg_ref, gb_ref, o_ref,
                  hf_ref, hb_ref, c_ref, m_ref, *, tc):
    s = o_ref.shape[0]
    lc = ck_ref.shape[0]
    nc = s // tc
    w = B_WIDTH
    dh = B_DH

    c_ref[...] = jnp.zeros_like(c_ref)
    m_ref[...] = jnp.zeros_like(m_ref)

    r, cidx = _lower_tri(lc)
    tri_c = jnp.where(cidx <= r, 1.0, 0.0).astype(bf16)
    cg = cg_ref[...]
    _, tot, xt = _chunk_gate_sums(cg[:, :LANES], cg[:, LANES:], tri_c)
    ones_c = _ones_block(lc)
    for ch in range(_NCHAIN):
        hs = slice((ch % B_HEADS) * dh, (ch % B_HEADS + 1) * dh)
        vs = slice(2 * w + (ch % B_HEADS) * dh, 2 * w + (ch % B_HEADS + 1) * dh)
        _absorb(c_ref, m_ref, ch, xt[ch:ch + 1, :], tot[:, ch:ch + 1], ck_ref[:, hs],
                jnp.concatenate([ckv_ref[:, vs], ones_c], axis=1))
    ones_t = _ones_block(tc)

    r, cidx = _lower_tri(tc)
    tri = jnp.where(cidx <= r, 1.0, 0.0).astype(bf16)
    causal = cidx <= r
    anti = cidx >= r

    def step(i, carry):
        for d in range(2):
            row0 = pl.multiple_of((i if d == 0 else nc - 1 - i) * tc, tc)
            rows = pl.ds(row0, tc)
            gch = g_ref[rows, :]
            gi = gch[:, :LANES]
            bd, tot, xt = _chunk_gate_sums(gi, gch[:, LANES:], tri)
            mask = causal if d == 0 else anti
            dst = hf_ref if d == 0 else hb_ref
            for hd in range(B_HEADS):
                ch = d * B_HEADS + hd
                hs = slice(hd * dh, (hd + 1) * dh)
                qb = qk_ref[rows, hs]
                kb = qk_ref[rows, slice(w + hd * dh, w + (hd + 1) * dh)]
                vaug = jnp.concatenate([vvo_ref[rows, slice(w + hd * dh, w + (hd + 1) * dh)], ones_t],
                                       axis=1)
                bcol = bd[:, ch:ch + 1]
                x_row = xt[ch:ch + 1, :]
                dmat = jnp.where(mask, bcol - x_row, -jnp.inf)
                m_prev = m_ref[ch][:, 0:1]
                inter = bcol + m_prev
                m_t = jnp.maximum(inter, jnp.max(dmat, axis=1, keepdims=True))
                e = jnp.exp(dmat - m_t + _LN_QSCALE)
                smat = _dot(qb, kb, _NT) * e
                sc = jnp.exp(inter - m_t + _LN_QSCALE)
                both = sc * _dot(qb, c_ref[ch].astype(bf16)) + _dot(smat.astype(bf16), vaug)
                den = both[:, dh:dh + 1]
                dst[rows, hs] = both[:, :dh] * (1.0 / jnp.maximum(jnp.abs(den), jnp.exp(-m_t)))
                _absorb(c_ref, m_ref, ch, x_row, tot[:, ch:ch + 1], kb, vaug)
        return carry

    lax.fori_loop(0, nc, step, 0)

    for hd in range(B_HEADS):
        hs = slice(hd * dh, (hd + 1) * dh)
        hsum = hf_ref[:, hs] + hb_ref[:, hs]
        hn = hsum * lax.rsqrt(jnp.mean(hsum * hsum, axis=1, keepdims=True) + RMS_EPS)
        og = _sigmoid(vvo_ref[:, slice(2 * w + hd * dh, 2 * w + (hd + 1) * dh)].astype(f32))
        o_ref[:, hs] = (hn * gb_ref[:, hs] * og).astype(o_ref.dtype)


def _mlstm(qk, vvo, gates, cbk, ckv, cg, g_b):
    bsz, s, _ = qk.shape
    lc = cbk.shape[1]
    w = B_WIDTH
    tc = min(MLSTM_CHUNK, s)
    return pl.pallas_call(
        functools.partial(_mlstm_kernel, tc=tc),
        grid=(bsz,),
        in_specs=[pl.BlockSpec((None, s, 2 * w), lambda b: (b, 0, 0)),
                  pl.BlockSpec((None, s, 3 * w), lambda b: (b, 0, 0)),
                  pl.BlockSpec((None, s, 2 * LANES), lambda b: (b, 0, 0)),
                  pl.BlockSpec((None, lc, w), lambda b: (b, 0, 0)),
                  pl.BlockSpec((None, lc, 3 * w), lambda b: (b, 0, 0)),
                  pl.BlockSpec((None, lc, 2 * LANES), lambda b: (b, 0, 0)),
                  pl.BlockSpec((1, w), lambda b: (0, 0))],
        out_specs=pl.BlockSpec((None, s, w), lambda b: (b, 0, 0)),
        out_shape=jax.ShapeDtypeStruct((bsz, s, w), bf16),
        scratch_shapes=[pltpu.VMEM((s, w), f32), pltpu.VMEM((s, w), f32),
                        pltpu.VMEM((_NCHAIN, B_DH, 2 * B_DH), f32),
                        pltpu.VMEM((_NCHAIN, 1, LANES), f32)],
        compiler_params=_cp("arbitrary"),
        name="mlstm",
    )(qk, vvo, gates, cbk, ckv, cg, g_b.reshape(1, w))


def _out_kernel(*refs, n_act):
    acts = refs[:n_act]
    ws = refs[n_act:2 * n_act]
    x_ref, g_ref, gate_ref, o_ref = refs[2 * n_act:]
    mix = _dot(acts[0][...], ws[0][...])
    for a, wr in zip(acts[1:], ws[1:]):
        mix = mix + _dot(a[...], wr[...])
    y = mix * lax.rsqrt(jnp.mean(mix * mix, axis=-1, keepdims=True) + RMS_EPS) * g_ref[...]
    o_ref[...] = x_ref[...] + gate_ref[...] * y


def _out_proj(acts, ws, x, g, gate, tl=512):
    bsz, l, d = x.shape
    tl = min(tl, l)
    n_act = len(acts)
    in_specs = [pl.BlockSpec((None, tl, a.shape[2]), lambda b, i: (b, i, 0)) for a in acts]
    in_specs += [pl.BlockSpec(wm.shape, lambda b, i: (0, 0)) for wm in ws]
    in_specs += [pl.BlockSpec((None, tl, d), lambda b, i: (b, i, 0)),
                 pl.BlockSpec((1, d), lambda b, i: (0, 0)),
                 pl.BlockSpec((None, 1, d), _bidx(gate))]
    return pl.pallas_call(
        functools.partial(_out_kernel, n_act=n_act),
        grid=(bsz, l // tl),
        in_specs=in_specs,
        out_specs=pl.BlockSpec((None, tl, d), lambda b, i: (b, i, 0)),
        out_shape=jax.ShapeDtypeStruct((bsz, l, d), f32),
        compiler_params=_cp("parallel", "parallel"),
        name="out_proj",
    )(*acts, *ws, x, g.reshape(1, d), gate)


def _router_kernel(x_ref, g_ref, sh_ref, sc_ref, rw_ref, rb_ref, h_ref, rk_ref, gt_ref, cm_ref):
    hf = _norm_mod(x_ref[...], g_ref[...], sh_ref[...], sc_ref[...])
    tl = hf.shape[0]
    h_ref[...] = hf.astype(h_ref.dtype)
    per = N_EXPERTS // N_GROUPS
    logits = _dot3(rw_ref[...], hf, _NT)
    s3 = _sigmoid(logits).reshape(N_GROUPS, per, tl)
    b3 = s3 + rb_ref[...].reshape(N_GROUPS, per, 1)
    neg = -jnp.inf
    jdx = lax.broadcasted_iota(jnp.int32, b3.shape, 1)
    gdx = lax.broadcasted_iota(jnp.int32, b3.shape, 0)
    m1 = jnp.max(b3, axis=1, keepdims=True)
    f1 = jnp.min(jnp.where(b3 == m1, jdx, per), axis=1, keepdims=True)
    m2 = jnp.max(jnp.where(jdx == f1, neg, b3), axis=1, keepdims=True)
    grp = m1 + m2
    g1 = lax.broadcasted_iota(jnp.int32, grp.shape, 0)
    cnt = jnp.zeros(grp.shape, jnp.int32)
    for gp in range(N_GROUPS):
        rv = grp[gp:gp + 1]
        ahead = jnp.where(rv > grp, 1, jnp.where(rv == grp, jnp.where(g1 > gp, 1, 0), 0))
        cnt = cnt + ahead
    v = jnp.where(cnt < TOPK_GROUPS, b3, neg)
    eidx = gdx * per + jdx
    sel = jnp.zeros(b3.shape, f32)
    for _ in range(TOP_K):
        m = jnp.max(jnp.max(v, axis=1, keepdims=True), axis=0, keepdims=True)
        cand = jnp.where(v == m, eidx, N_EXPERTS)
        fi = jnp.min(jnp.min(cand, axis=1, keepdims=True), axis=0, keepdims=True)
        hit = eidx == fi
        sel = jnp.where(hit, 1.0, sel)
        v = jnp.where(hit, neg, v)
    ssel = sel * s3
    den = jnp.sum(jnp.sum(ssel, axis=1, keepdims=True), axis=0, keepdims=True)
    gt_ref[...] = ((ROUTED_SCALE * ssel) / den).reshape(N_EXPERTS, tl)
    sel2 = sel.reshape(N_EXPERTS, tl)
    r = lax.broadcasted_iota(jnp.int32, (MOE_SUB, MOE_SUB), 0)
    c = lax.broadcasted_iota(jnp.int32, (MOE_SUB, MOE_SUB), 1)
    before = jnp.where(r < c, 1.0, 0.0).astype(bf16)
    cmax = jnp.zeros((N_EXPERTS, 1), f32)
    for j in range(tl // MOE_SUB):
        sub = sel2[:, j * MOE_SUB:(j + 1) * MOE_SUB]
        rank = _dot(sub.astype(bf16), before)
        rk_ref[:, j * MOE_SUB:(j + 1) * MOE_SUB] = jnp.where(sub > 0.0, rank, -1.0)
        cmax = jnp.maximum(cmax, jnp.sum(sub, axis=1, keepdims=True))
    cm_ref[...] = jnp.broadcast_to(cmax, cm_ref.shape)


def _router(x, g, shift, scale, rw_t, rb, tl=512):
    bsz, l, d = x.shape
    tl = min(tl, l)
    nl = l // tl
    return pl.pallas_call(
        _router_kernel,
        grid=(bsz, nl),
        in_specs=[pl.BlockSpec((None, tl, d), lambda b, i: (b, i, 0)),
                  pl.BlockSpec((1, d), lambda b, i: (0, 0)),
                  pl.BlockSpec((None, 1, d), _bidx(shift)),
                  pl.BlockSpec((None, 1, d), _bidx(scale)),
                  pl.BlockSpec((N_EXPERTS, d), lambda b, i: (0, 0)),
                  pl.BlockSpec((N_EXPERTS, 1), lambda b, i: (0, 0))],
        out_specs=[pl.BlockSpec((None, tl, d), lambda b, i: (b, i, 0)),
                   pl.BlockSpec((N_EXPERTS, tl), lambda b, i: (0, b * nl + i)),
                   pl.BlockSpec((N_EXPERTS, tl), lambda b, i: (0, b * nl + i)),
                   pl.BlockSpec((None, N_EXPERTS, LANES), lambda b, i: (b * nl + i, 0, 0))],
        out_shape=[jax.ShapeDtypeStruct((bsz, l, d), bf16),
                   jax.ShapeDtypeStruct((N_EXPERTS, bsz * l), f32),
                   jax.ShapeDtypeStruct((N_EXPERTS, bsz * l), f32),
                   jax.ShapeDtypeStruct((bsz * nl, N_EXPERTS, LANES), f32)],
        compiler_params=_cp("parallel", "parallel"),
        name="router",
    )(x, g.reshape(1, d), shift, scale, rw_t, rb.reshape(N_EXPERTS, 1))


def _swiglu_act(hh):
    half = hh.shape[1] // 2
    return _silu(hh[:, :half]) * hh[:, half:]


def _moe_kernel(cnt_ref, ord_ref, h_ref, rk_ref, gt_ref, *refs):
    ng = MOE_GROUP
    gu_refs, dn_refs = refs[:ng], refs[ng:2 * ng]
    (sgu_ref, sdn_ref, x_ref, g_ref, gate_ref, o_ref,
     acc_ref, xg_ref, ys_ref, p_ref, gr_ref) = refs[2 * ng:]
    tile = pl.program_id(0)
    grp = pl.program_id(1)
    tm, d = acc_ref.shape
    ns = tm // MOE_SUB
    win = MOE_WIN
    eids = [ord_ref[tile, grp * ng + el] for el in range(ng)]

    @pl.when(grp == 0)
    def _():
        act = _swiglu_act(_dot(h_ref[...], sgu_ref[...]))
        acc_ref[...] = _dot(act.astype(bf16), sdn_ref[...])

    riota = lax.broadcasted_iota(jnp.int32, (win, MOE_SUB), 0).astype(f32)

    def expert_ffn(el):
        hh = _dot(xg_ref[el], gu_refs[el][...])
        gr = gr_ref[el]
        act = _swiglu_act(hh) * jnp.concatenate([gr] * (hh.shape[1] // (2 * LANES)), axis=1)
        y = _dot(act.astype(bf16), dn_refs[el][...]).astype(bf16)
        for s in range(ns):
            ys_ref[s, el * win:(el + 1) * win, :] = y[s * win:(s + 1) * win]

    def one_pass(p, skip_idle):
        base = p * win
        for s in range(ns):
            cols = slice(s * MOE_SUB, (s + 1) * MOE_SUB)
            onehots = []
            for el in range(ng):
                row = pl.ds(eids[el], 1)
                hit = (rk_ref[row, cols] - base) == riota
                onehots.append(jnp.where(hit, 1.0, 0.0).astype(bf16))
                gsel = jnp.sum(jnp.where(hit, gt_ref[row, cols], 0.0), axis=1, keepdims=True)
                gr_ref[el, s * win:(s + 1) * win, :] = jnp.broadcast_to(gsel, (win, LANES))
            pm = jnp.concatenate(onehots, axis=0)
            p_ref[s] = pm
            gx = _dot(pm, h_ref[cols, :])
            for el in range(ng):
                xg_ref[el, s * win:(s + 1) * win, :] = gx[el * win:(el + 1) * win].astype(bf16)
        for el in range(ng):
            if not skip_idle:
                expert_ffn(el)
                continue
            busy = cnt_ref[tile, eids[el]] > base
            pl.when(busy)(functools.partial(expert_ffn, el))

            @pl.when(jnp.logical_not(busy))
            def _():
                for s in range(ns):
                    ys_ref[s, el * win:(el + 1) * win, :] = jnp.zeros((win, d), bf16)
        for s in range(ns):
            acc_ref[s * MOE_SUB:(s + 1) * MOE_SUB, :] += _dot(p_ref[s], ys_ref[s], _TN)

    one_pass(0, False)
    most = cnt_ref[tile, eids[0]]
    for el in range(1, ng):
        most = jnp.maximum(most, cnt_ref[tile, eids[el]])

    def later_pass(p, carry):
        one_pass(p, True)
        return carry

    lax.fori_loop(1, (most + win - 1) // win, later_pass, 0)

    @pl.when(grp == pl.num_programs(1) - 1)
    def _():
        mo = acc_ref[...]
        y = mo * lax.rsqrt(jnp.mean(mo * mo, axis=-1, keepdims=True) + RMS_EPS) * g_ref[...]
        o_ref[...] = x_ref[...] + gate_ref[...] * y


def _moe(h2, rk, gt, cmax, gu, dn, sgu, sdn, x, g, gate, tm=1024):
    bsz, l, d = x.shape
    tm = min(tm, l)
    per_b = l // tm
    nt = bsz * per_b
    ne = gu.shape[0]
    ng = MOE_GROUP
    ns = tm // MOE_SUB
    counts = jnp.max(cmax[:, :, 0].reshape(nt, -1, ne), axis=1).astype(jnp.int32)
    order = jnp.argsort(-counts, axis=1).astype(jnp.int32)

    def expert_spec(arr, k):
        return pl.BlockSpec((None,) + arr.shape[1:], lambda t, e, cnt, order_ref: (order_ref[t, e * ng + k], 0, 0))

    grid_spec = pltpu.PrefetchScalarGridSpec(
        num_scalar_prefetch=2,
        grid=(nt, ne // ng),
        in_specs=[pl.BlockSpec((tm, d), lambda t, e, *_: (t, 0)),
                  pl.BlockSpec((ne, tm), lambda t, e, *_: (0, t)),
                  pl.BlockSpec((ne, tm), lambda t, e, *_: (0, t))]
                 + [expert_spec(gu, k) for k in range(ng)]
                 + [expert_spec(dn, k) for k in range(ng)]
                 + [pl.BlockSpec(sgu.shape, lambda t, e, *_: (0, 0)),
                    pl.BlockSpec(sdn.shape, lambda t, e, *_: (0, 0)),
                    pl.BlockSpec((tm, d), lambda t, e, *_: (t, 0)),
                    pl.BlockSpec((1, d), lambda t, e, *_: (0, 0)),
                    pl.BlockSpec((None, 1, d), lambda t, e, *_: (t // per_b, 0, 0))],
        out_specs=pl.BlockSpec((tm, d), lambda t, e, *_: (t, 0)),
        scratch_shapes=[pltpu.VMEM((tm, d), f32),
                        pltpu.VMEM((ng, ns * MOE_WIN, d), bf16),
                        pltpu.VMEM((ns, ng * MOE_WIN, d), bf16),
                        pltpu.VMEM((ns, ng * MOE_WIN, MOE_SUB), bf16),
                        pltpu.VMEM((ng, ns * MOE_WIN, LANES), f32)])
    out = pl.pallas_call(
        _moe_kernel,
        grid_spec=grid_spec,
        out_shape=jax.ShapeDtypeStruct((bsz * l, d), f32),
        compiler_params=_cp("parallel", "arbitrary"),
        name="moe",
    )(counts, order, h2.reshape(bsz * l, d), rk, gt, *([gu] * ng), *([dn] * ng), sgu, sdn,
      x.reshape(bsz * l, d), g.reshape(1, d), gate)
    return out.reshape(bsz, l, d)


def _filter_kernel(z_ref, w1_ref, b1_ref, w2_ref, b2_ref, w3_ref, win_ref, o_ref):
    hid = jnp.sin(FILTER_SIN_W * (_dot3(z_ref[...], w1_ref[...]) + b1_ref[...]))
    hid = jnp.sin(FILTER_SIN_W * (_dot3(hid, w2_ref[...]) + b2_ref[...]))
    o_ref[...] = _dot3(hid, w3_ref[...]) * win_ref[...]


def _filters(z, w1, b1, w2, b2, w3, window, tn=512):
    l, p = z.shape
    hdim = w1.shape[1]
    n = w3.shape[1]
    d = window.shape[1]
    nd = d // tn
    return pl.pallas_call(
        _filter_kernel,
        grid=(n // tn,),
        in_specs=[pl.BlockSpec((l, p), lambda j: (0, 0)),
                  pl.BlockSpec((p, hdim), lambda j: (0, 0)),
                  pl.BlockSpec((1, hdim), lambda j: (0, 0)),
                  pl.BlockSpec((hdim, hdim), lambda j: (0, 0)),
                  pl.BlockSpec((1, hdim), lambda j: (0, 0)),
                  pl.BlockSpec((hdim, tn), lambda j: (0, j)),
                  pl.BlockSpec((l, tn), lambda j: (0, j % nd))],
        out_specs=pl.BlockSpec((l, tn), lambda j: (0, j)),
        out_shape=jax.ShapeDtypeStruct((l, n), f32),
        compiler_params=_cp("arbitrary"),
        name="hyena_filter",
    )(z, w1, b1.reshape(1, hdim), w2, b2.reshape(1, hdim), w3, window)


def _dft_tables(l):
    n = 2 * l
    n1 = math.isqrt(n)
    assert n == n1 * n1 and n1 % 16 == 0
    na = l // n1
    ncp = -(-(n1 // 2 + 1) // 8) * 8
    a = np.arange(na)
    b = np.arange(n1)
    c = np.arange(ncp)
    th = 2.0 * np.pi * ((n1 * a[None, None, :] + b[:, None, None]) * c[None, :, None]) / n
    t1 = np.concatenate([np.cos(th), -np.sin(th)], axis=1)
    ph = 2.0 * np.pi * (b[:, None] * b[None, :]) / n1
    cs, sn = np.cos(ph), np.sin(ph)
    a3 = np.block([[cs, sn], [-sn, cs]])
    a3i = np.block([[cs, -sn], [sn, cs]])
    a2 = np.arange(na) + na // 2
    th2 = 2.0 * np.pi * ((n1 * a2[None, :, None] + b[:, None, None]) * c[None, None, :]) / n
    wc = np.where((c == 0) | (c == n1 // 2), 1.0, np.where(c < n1 // 2, 2.0, 0.0))[None, None, :]
    t2 = np.concatenate([wc * np.cos(th2), -wc * np.sin(th2)], axis=2)
    return [jnp.asarray(t, f32).astype(bf16) for t in (t1, a3, a3i, t2)]


def _fft_dims(t1):
    n1, ncp2, na = t1.shape
    ncp = ncp2 // 2
    return n1, ncp, na, 2 * n1 + FFT_PAD, 2 * ncp + FFT_PAD, n1 + FFT_PAD


def _ld(ref, rows):
    return jnp.concatenate([ref[j, rows, :] for j in range(ref.shape[0])], axis=1)


def _st(ref, rows, val):
    for j in range(ref.shape[0]):
        ref[j, rows, :] = val[:, j * LANES:(j + 1) * LANES]


def _dft_forward(uf_ref, t1_ref, zs_ref):
    n1, ncp, na, sb, _, su = _fft_dims(t1_ref)
    for b in range(n1):
        ub = _ld(uf_ref, pl.ds(b, na, stride=su)).astype(bf16)
        zb = _dot(t1_ref[b], ub)
        _st(zs_ref, pl.ds(b, ncp, stride=sb), zb[:ncp])
        _st(zs_ref, pl.ds(n1 + b, ncp, stride=sb), zb[ncp:])


def _spectrum_kernel(f_ref, t1_ref, a3_ref, o_ref, uf_ref, zs_ref, *, scale):
    n1, ncp, na, sb, _, su = _fft_dims(t1_ref)
    for a in range(na):
        _st(uf_ref, pl.ds(a * su, n1), f_ref[pl.ds(a * n1, n1), :])
    _dft_forward(uf_ref, t1_ref, zs_ref)
    a3 = a3_ref[...]
    for c in range(ncp):
        zc = _ld(zs_ref, pl.ds(c * sb, 2 * n1)).astype(bf16)
        o_ref[c] = (_dot(a3, zc) * scale).astype(o_ref.dtype)


def _spectrum(filt, tabs, dt=256):
    l, n = filt.shape
    t1, a3, _, _ = tabs
    n1, ncp, na, sb, _, su = _fft_dims(t1)
    nj = dt // LANES
    return pl.pallas_call(
        functools.partial(_spectrum_kernel, scale=1.0 / (2 * l)),
        grid=(n // dt,),
        in_specs=[pl.BlockSpec((l, dt), lambda j: (0, j)),
                  pl.BlockSpec(t1.shape, lambda j: (0, 0, 0)),
                  pl.BlockSpec(a3.shape, lambda j: (0, 0))],
        out_specs=pl.BlockSpec((ncp, 2 * n1, dt), lambda j: (0, 0, j)),
        out_shape=jax.ShapeDtypeStruct((ncp, 2 * n1, n), bf16),
        scratch_shapes=[pltpu.VMEM((nj, na * su, LANES), f32),
                        pltpu.VMEM((nj, ncp * sb, LANES), f32)],
        compiler_params=_cp("arbitrary"),
        name="hyena_spectrum",
    )(filt, t1, a3)


def _fftconv_kernel(u_ref, xg_ref, kf_ref, fb_ref, t1_ref, a3_ref, a3i_ref, t2_ref, o_ref,
                    uf_ref, zs_ref, qs_ref, y_ref):
    n1, ncp, na, sb, sq, su = _fft_dims(t1_ref)
    for a in range(na):
        _st(uf_ref, pl.ds(a * su, n1), u_ref[pl.ds(a * n1, n1), :].astype(f32))
    _dft_forward(uf_ref, t1_ref, zs_ref)
    a3 = a3_ref[...]
    a3i = a3i_ref[...]
    for c in range(ncp):
        zc = _ld(zs_ref, pl.ds(c * sb, 2 * n1)).astype(bf16)
        xc = _dot(a3, zc)
        kc = kf_ref[c].astype(f32)
        xr, xi = xc[:n1], xc[n1:]
        kr, ki = kc[:n1], kc[n1:]
        pc = jnp.concatenate([xr * kr - xi * ki, xr * ki + xi * kr], axis=0).astype(bf16)
        qc = _dot(a3i, pc)
        _st(qs_ref, pl.ds(c, n1, stride=sq), qc[:n1])
        _st(qs_ref, pl.ds(ncp + c, n1, stride=sq), qc[n1:])
    for b in range(n1):
        qb = _ld(qs_ref, pl.ds(b * sq, 2 * ncp)).astype(bf16)
        _st(y_ref, pl.ds(b, na, stride=su), _dot(t2_ref[b], qb))
    fb = fb_ref[...]
    for a in range(na):
        rows = pl.ds(a * n1, n1)
        uv = _ld(uf_ref, pl.ds(a * su, n1))
        yv = _ld(y_ref, pl.ds(a * su, n1))
        o_ref[rows, :] = (xg_ref[rows, :].astype(f32) * (yv + uv * fb)).astype(o_ref.dtype)


def _fftconv(u, u_col, xg, xg_col, kf, kf_col, fbias, tabs, d, dt=256):
    bsz, l, _ = u.shape
    t1, a3, a3i, t2 = tabs
    n1, ncp, na, sb, sq, su = _fft_dims(t1)
    nd = d // dt
    nj = dt // LANES
    uo, go, ko = u_col // dt, xg_col // dt, kf_col // dt
    return pl.pallas_call(
        _fftconv_kernel,
        grid=(nd, bsz),
        in_specs=[pl.BlockSpec((None, l, dt), lambda j, b: (b, 0, j + uo)),
                  pl.BlockSpec((None, l, dt), lambda j, b: (b, 0, j + go)),
                  pl.BlockSpec((ncp, 2 * n1, dt), lambda j, b: (0, 0, j + ko)),
                  pl.BlockSpec((1, dt), lambda j, b: (0, j)),
                  pl.BlockSpec(t1.shape, lambda j, b: (0, 0, 0)),
                  pl.BlockSpec(a3.shape, lambda j, b: (0, 0)),
                  pl.BlockSpec(a3i.shape, lambda j, b: (0, 0)),
                  pl.BlockSpec(t2.shape, lambda j, b: (0, 0, 0))],
        out_specs=pl.BlockSpec((None, l, dt), lambda j, b: (b, 0, j)),
        out_shape=jax.ShapeDtypeStruct((bsz, l, d), bf16),
        scratch_shapes=[pltpu.VMEM((nj, na * su, LANES), f32),
                        pltpu.VMEM((nj, ncp * sb, LANES), f32),
                        pltpu.VMEM((nj, n1 * sq, LANES), f32),
                        pltpu.VMEM((nj, na * su, LANES), f32)],
        compiler_params=_cp("parallel", "arbitrary"),
        name="hyena_fftconv",
    )(u, xg, kf, fbias.reshape(1, d), t1, a3, a3i, t2)


def _rope_tables(l):
    rows = l // GRID_W
    row = jnp.repeat(jnp.arange(rows), GRID_W)
    col = jnp.tile(jnp.arange(GRID_W), rows)
    inv = ROPE_BASE ** (-jnp.arange(ROPE_AXIS_PAIRS, dtype=f32) / ROPE_AXIS_PAIRS)
    ang = jnp.stack([row, col], axis=-1).astype(f32)[..., None] * inv
    ang = jnp.broadcast_to(ang[:, :, None, :], (l, 2, 2, ROPE_AXIS_PAIRS)).reshape(l, A_DQK)
    reps = A_QW // A_DQK
    return jnp.tile(jnp.cos(ang), (1, reps)), jnp.tile(jnp.sin(ang), (1, reps))


def _rotate_cols(w):
    j = np.arange(w.shape[1])
    lo = (j % (2 * ROPE_AXIS_PAIRS)) < ROPE_AXIS_PAIRS
    perm = np.where(lo, j + ROPE_AXIS_PAIRS, j - ROPE_AXIS_PAIRS)
    sign = np.where(lo, -1.0, 1.0).astype(np.float32)
    return w[:, perm] * sign


def _gate_cols(w_g, b_g):
    idx_i = np.array([d * 2 * B_HEADS + hd for d in range(2) for hd in range(B_HEADS)])
    idx_f = idx_i + B_HEADS
    pad = LANES - _NCHAIN
    k = w_g.shape[0]
    w = jnp.concatenate([w_g[:, idx_i], jnp.zeros((k, pad), f32),
                         w_g[:, idx_f], jnp.zeros((k, pad), f32)], axis=1)
    b = jnp.concatenate([b_g[idx_i], jnp.zeros((pad,), f32), b_g[idx_f], jnp.zeros((pad,), f32)])
    return w, b


def _hyena_consts(l, d):
    j = jnp.arange(l, dtype=f32)
    bands = (POS_EMB_DIM - 1) // 2
    freqs = jnp.linspace(1e-4, bands - 1, bands, dtype=f32)
    ang = (2.0 * math.pi / l) * j[:, None] * freqs[None, :]
    z = jnp.concatenate([(j / (l - 1))[:, None], jnp.cos(ang), -jnp.sin(ang)], axis=-1)
    dist = jnp.abs(j - l // 2) / (l // 2)
    max_decay = math.log(DECAY_TARGET) / DECAY_FAST_PCT
    min_decay = math.log(DECAY_TARGET) / DECAY_SLOW_PCT
    deltas = jnp.abs(jnp.linspace(min_decay, max_decay, d, dtype=f32))
    window = jnp.exp(-dist[:, None] * deltas[None, :])
    return z, window


def _ab_layer(x, ctx, mod_vecs, mod_ctx, norm_g, w_in, conv_w, conv_b, gate_b, lam_vecs,
              g_a, g_b, w_out, lam_init):
    sh_m, sc_m, g_m = mod_vecs
    bsz, s, d = x.shape
    h = _norm(x, norm_g[0], sh_m, sc_m)
    hc = _norm(ctx, norm_g[0], mod_ctx[0], mod_ctx[1])
    w = B_WIDTH
    o = 0
    cols = {}
    for name, width in (("aq", A_QW), ("bq", w), ("bo", w), ("ak", A_QW), ("av", A_VW),
                        ("bk", w), ("bv", w), ("g", 4 * B_HEADS)):
        cols[name] = w_in[:, o:o + width]
        o += width
    cos, sin = _rope_tables(s)
    cat = lambda *ws: jnp.concatenate(ws, axis=1).astype(bf16)
    q = _mm(h, cat(cols["aq"], _rotate_cols(cols["aq"])), rope=(cos, sin, A_DQK ** -0.5))
    k = _mm(h, cat(cols["ak"], _rotate_cols(cols["ak"])), rope=(cos, sin, 1.0))
    qk = _mm(h, cat(cols["bq"], cols["bk"]), conv=(conv_w, conv_b, True))
    vvo = _mm(h, cat(cols["av"], cols["bv"], cols["bo"]))
    wg, bg = _gate_cols(cols["g"], gate_b)
    gates = _mm(h, wg.astype(bf16), out_dtype=f32, bias=bg, tn=2 * LANES)
    ckv = _mm(hc, cat(cols["ak"], cols["av"], cols["bv"]))
    cbk = _mm(hc, cols["bk"].astype(bf16), conv=(conv_w[:, w:], conv_b[w:], True))
    cg = _mm(hc, wg.astype(bf16), out_dtype=f32, bias=bg, tn=2 * LANES)
    out_a = _attn(lam_vecs, q, k, vvo, ckv, g_a, lam_init)
    out_b = _mlstm(qk, vvo, gates, cbk, ckv, cg, g_b)
    wo = w_out.astype(bf16)
    return _out_proj([out_a, out_b], [wo[:A_VW], wo[A_VW:]], x, norm_g[1], g_m)


def _hyena_layer(x, mod_vecs, norm_g, w_in, conv_w, conv_b, fw1, fb1, fw2, fb2, fw3, fbias, w_out):
    sh_m, sc_m, g_m = mod_vecs
    bsz, l, d = x.shape
    h = _norm(x, norm_g[0], sh_m, sc_m)
    u = _mm(h, w_in.astype(bf16), conv=(conv_w, conv_b, False))
    z, window = _hyena_consts(l, d)
    pz, ph = LANES - z.shape[1], LANES - fw1.shape[1]
    filt = _filters(jnp.pad(z, ((0, 0), (0, pz))), jnp.pad(fw1, ((0, pz), (0, ph))),
                    jnp.pad(fb1, (0, ph)), jnp.pad(fw2, ((0, ph), (0, ph))), jnp.pad(fb2, (0, ph)),
                    jnp.pad(fw3, ((0, ph), (0, 0))), window)
    tabs = _dft_tables(l)
    kf = _spectrum(filt, tabs)
    zz = _fftconv(u, 0, u, d, kf, 0, fbias[0], tabs, d)
    y = _fftconv(zz, 0, u, 2 * d, kf, d, fbias[1], tabs, d)
    return _out_proj([y], [w_out.astype(bf16)], x, norm_g[1], g_m)


def kernel(x, c, ctx, c_ctx, w_mod, b_mod, norm_g, w_in_ab, conv_ab_w, conv_ab_b, gate_b_ab, diff_lambda, head_g_a, head_g_b, w_out_ab, w_in_hy, conv_hy_w, conv_hy_b, filt_w1, filt_b1, filt_w2, filt_b2, filt_w3, filt_bias, w_out_hy, router_w, router_b, exp_gu, exp_down, sh_gu, sh_down):
    bsz, s, d = x.shape
    depth = w_mod.shape[0]
    rows = -(-(bsz + 1) // 8) * 8
    cc = jnp.concatenate([c, c_ctx[None, :], jnp.zeros((rows - bsz - 1, d), f32)], axis=0)
    for l in range(depth):
        mod = _mod(cc, w_mod[l], b_mod[l])
        vec = lambda i: mod[:bsz, i * d:(i + 1) * d].reshape(bsz, 1, d)
        sh_m, sc_m, g_m, sh_f, sc_f, g_f = [vec(i) for i in range(6)]
        if l % 2 == 0:
            e = l // 2
            lam_init = 0.8 - 0.6 * math.exp(-0.3 * l)
            mod_ctx = (mod[bsz:bsz + 1, 0:d].reshape(1, 1, d), mod[bsz:bsz + 1, d:2 * d].reshape(1, 1, d))
            x = _ab_layer(x, ctx, (sh_m, sc_m, g_m), mod_ctx, norm_g[l], w_in_ab[e], conv_ab_w[e],
                          conv_ab_b[e], gate_b_ab[e], diff_lambda[e], head_g_a[e], head_g_b[e],
                          w_out_ab[e], lam_init)
        else:
            o = l // 2
            x = _hyena_layer(x, (sh_m, sc_m, g_m), norm_g[l], w_in_hy[o], conv_hy_w[o], conv_hy_b[o],
                             filt_w1[o], filt_b1[o], filt_w2[o], filt_b2[o], filt_w3[o], filt_bias[o],
                             w_out_hy[o])
        h2, rk, gt, cmax = _router(x, norm_g[l, 2], sh_f, sc_f, router_w[l].T, router_b[l])
        x = _moe(h2, rk, gt, cmax, exp_gu[l].astype(bf16), exp_down[l].astype(bf16), sh_gu[l].astype(bf16),
                 sh_down[l].astype(bf16), x, norm_g[l, 3], g_f)
    return x
```

```python
import functools
import math

import numpy as np
import jax
import jax.numpy as jnp
from jax import lax
from jax.experimental import pallas as pl
from jax.experimental.pallas import tpu as pltpu

f32 = jnp.float32
bf16 = jnp.bfloat16

RMS_EPS = 1e-6
A_HEADS = 4
A_DQK = 64
A_DV = 128
B_HEADS = 4
B_DH = 128
B_WIDTH = B_HEADS * B_DH
A_QW = A_HEADS * 2 * A_DQK
A_VW = A_HEADS * A_DV
GRID_W = 64
ROPE_BASE = 10000.0
ROPE_AXIS_PAIRS = A_DQK // 4
N_EXPERTS = 64
TOP_K = 8
N_GROUPS = 8
TOPK_GROUPS = 4
D_EXPERT = 256
ROUTED_SCALE = 2.5
HY_ORDER = 2
POS_EMB_DIM = 33
FILTER_SIN_W = 1.0
DECAY_FAST_PCT = 0.3
DECAY_SLOW_PCT = 1.5
DECAY_TARGET = 1e-2

LANES = 128
VMEM_LIMIT = 56 * 1024 * 1024
MLSTM_CHUNK = 256
FFT_PAD = 8
MOE_SUB = 256
MOE_WIN = 64
MOE_WIN_SMALL = 32
MOE_GROUP = 4


def _cp(*sem):
    return pltpu.CompilerParams(dimension_semantics=sem, vmem_limit_bytes=VMEM_LIMIT)


def _split_bf16(a):
    hi = a.astype(bf16)
    lo = (a - hi.astype(f32)).astype(bf16)
    return hi, lo


def _dot(a, b, dims=(((1,), (0,)), ((), ()))):
    return lax.dot_general(a, b, dims, preferred_element_type=f32)


_NT = (((1,), (1,)), ((), ()))
_TN = (((0,), (0,)), ((), ()))


def _dot3(a, b, dims=(((1,), (0,)), ((), ()))):
    ah, al = _split_bf16(a)
    bh, bl = _split_bf16(b)
    return _dot(ah, bh, dims) + (_dot(ah, bl, dims) + _dot(al, bh, dims))


def _silu(v):
    return v / (1.0 + jnp.exp(-v))


def _sigmoid(v):
    return 1.0 / (1.0 + jnp.exp(-v))


def _log_sigmoid(v):
    return jnp.minimum(v, 0.0) - jnp.log(1.0 + jnp.exp(-jnp.abs(v)))


def _mod_kernel(c_ref, w_ref, b_ref, o_ref):
    o_ref[...] = _dot3(_silu(c_ref[...]), w_ref[...]) + b_ref[...]


def _mod(cc, w, b):
    rows, d = cc.shape
    n = w.shape[1]
    tn = d
    return pl.pallas_call(
        _mod_kernel,
        grid=(n // tn,),
        in_specs=[pl.BlockSpec((rows, d), lambda j: (0, 0)),
                  pl.BlockSpec((d, tn), lambda j: (0, j)),
                  pl.BlockSpec((1, tn), lambda j: (0, j))],
        out_specs=pl.BlockSpec((rows, tn), lambda j: (0, j)),
        out_shape=jax.ShapeDtypeStruct((rows, n), f32),
        compiler_params=_cp("arbitrary"),
        name="mod",
    )(cc, w, b.reshape(1, n))


def _norm_mod(xv, g, shift, scale):
    y = xv * lax.rsqrt(jnp.mean(xv * xv, axis=-1, keepdims=True) + RMS_EPS)
    return (y * g) * (1.0 + scale) + shift


def _norm_kernel(x_ref, g_ref, sh_ref, sc_ref, o_ref):
    o_ref[...] = _norm_mod(x_ref[...], g_ref[...], sh_ref[...], sc_ref[...]).astype(o_ref.dtype)


def _bidx(arr):
    if arr.shape[0] == 1:
        return lambda b, *_: (0, 0, 0)
    return lambda b, *_: (b, 0, 0)


def _norm(x, g, shift, scale, tl=512):
    bsz, l, d = x.shape
    tl = min(tl, l)
    return pl.pallas_call(
        _norm_kernel,
        grid=(bsz, l // tl),
        in_specs=[pl.BlockSpec((None, tl, d), lambda b, i: (b, i, 0)),
                  pl.BlockSpec((1, d), lambda b, i: (0, 0)),
                  pl.BlockSpec((None, 1, d), _bidx(shift)),
                  pl.BlockSpec((None, 1, d), _bidx(scale))],
        out_specs=pl.BlockSpec((None, tl, d), lambda b, i: (b, i, 0)),
        out_shape=jax.ShapeDtypeStruct((bsz, l, d), bf16),
        compiler_params=_cp("parallel", "parallel"),
        name="norm",
    )(x, g.reshape(1, d), shift, scale)


def _mm_plain_kernel(h_ref, w_ref, b_ref, o_ref):
    o_ref[...] = (_dot(h_ref[...], w_ref[...]) + b_ref[...]).astype(o_ref.dtype)


def _mm_rope_kernel(h_ref, w_ref, cos_ref, sin_ref, o_ref, *, scale):
    p = _dot(h_ref[...], w_ref[...])
    n = o_ref.shape[-1]
    o_ref[...] = ((p[:, :n] * cos_ref[...] + p[:, n:] * sin_ref[...]) * scale).astype(o_ref.dtype)


def _mm_conv_kernel(h_ref, w_ref, cw_ref, cb_ref, o_ref, scr_ref, *, act, rc):
    l = h_ref.shape[0]
    w = w_ref[...]
    w0, w1, w2, cb = cw_ref[0:1, :], cw_ref[1:2, :], cw_ref[2:3, :], cb_ref[...]
    halo = 16
    zrow = jnp.zeros((8, o_ref.shape[1]), f32)
    for c in range(l // rc):
        lo, hi = max(c * rc - halo, 0), min((c + 1) * rc + halo, l)
        n = hi - lo
        scr = scr_ref.at[c % 2]
        scr[8:8 + n, :] = _dot(h_ref[lo:hi, :], w)
        if lo == 0:
            scr[0:8, :] = zrow
        if hi == l:
            scr[8 + n:16 + n, :] = zrow
        off = 8 + c * rc - lo
        y = (scr[off - 1:off - 1 + rc, :] * w0 + scr[off:off + rc, :] * w1
             + scr[off + 1:off + 1 + rc, :] * w2 + cb)
        o_ref[c * rc:(c + 1) * rc, :] = (_silu(y) if act else y).astype(o_ref.dtype)


def _mm(h, w, *, out_dtype=bf16, bias=None, rope=None, conv=None, tl=512, tn=512):
    bsz, l, k = h.shape
    n = w.shape[1]
    if rope is not None:
        cos, sin, scale = rope
        n_out = n // 2
        tl = min(tl, l)
        return pl.pallas_call(
            functools.partial(_mm_rope_kernel, scale=scale),
            grid=(bsz, l // tl),
            in_specs=[pl.BlockSpec((None, tl, k), lambda b, i: (b, i, 0)),
                      pl.BlockSpec((k, n), lambda b, i: (0, 0)),
                      pl.BlockSpec((tl, n_out), lambda b, i: (i, 0)),
                      pl.BlockSpec((tl, n_out), lambda b, i: (i, 0))],
            out_specs=pl.BlockSpec((None, tl, n_out), lambda b, i: (b, i, 0)),
            out_shape=jax.ShapeDtypeStruct((bsz, l, n_out), out_dtype),
            compiler_params=_cp("parallel", "parallel"),
            name="mm_rope",
        )(h, w, cos, sin)
    tn = min(tn, n)
    if conv is not None:
        cw, cb, act = conv
        rc = min(512, l)
        return pl.pallas_call(
            functools.partial(_mm_conv_kernel, act=act, rc=rc),
            grid=(bsz, n // tn),
            in_specs=[pl.BlockSpec((None, l, k), lambda b, j: (b, 0, 0)),
                      pl.BlockSpec((k, tn), lambda b, j: (0, j)),
                      pl.BlockSpec((3, tn), lambda b, j: (0, j)),
                      pl.BlockSpec((1, tn), lambda b, j: (0, j))],
            out_specs=pl.BlockSpec((None, l, tn), lambda b, j: (b, 0, j)),
            out_shape=jax.ShapeDtypeStruct((bsz, l, n), out_dtype),
            scratch_shapes=[pltpu.VMEM((2, rc + 48, tn), f32)],
            compiler_params=_cp("parallel", "arbitrary"),
            name="mm_conv",
        )(h, w, cw, cb.reshape(1, n))
    if bias is None:
        bias = jnp.zeros((n,), f32)
    if k * n * w.dtype.itemsize <= 4 * 1024 * 1024:
        tn = n
    tl = min(tl, l)
    return pl.pallas_call(
        _mm_plain_kernel,
        grid=(bsz, l // tl, n // tn),
        in_specs=[pl.BlockSpec((None, tl, k), lambda b, i, j: (b, i, 0)),
                  pl.BlockSpec((k, tn), lambda b, i, j: (0, j)),
                  pl.BlockSpec((1, tn), lambda b, i, j: (0, j))],
        out_specs=pl.BlockSpec((None, tl, tn), lambda b, i, j: (b, i, j)),
        out_shape=jax.ShapeDtypeStruct((bsz, l, n), out_dtype),
        compiler_params=_cp("parallel", "parallel", "arbitrary"),
        name="mm_plain",
    )(h, w, bias.reshape(1, n))


def _attn_kernel(lv_ref, q_ref, kc_ref, k_ref, vc_ref, v_ref, g_ref, o_ref, *, lam_init):
    tq = q_ref.shape[0]
    lv = lv_ref[...]
    lam = (jnp.exp(jnp.sum(lv[0:1] * lv[1:2], axis=1, keepdims=True))
           - jnp.exp(jnp.sum(lv[2:3] * lv[3:4], axis=1, keepdims=True)) + lam_init)
    first = lax.broadcasted_iota(jnp.int32, (tq, A_DV), 1) < A_DQK
    one0 = jnp.where(lax.broadcasted_iota(jnp.int32, (1, A_DV), 1) == 0, 1.0, 0.0).astype(bf16)
    ones_c = jnp.broadcast_to(one0, (kc_ref.shape[0], A_DV))
    ones_l = jnp.broadcast_to(one0, (k_ref.shape[0], A_DV))
    for hd in range(A_HEADS):
        cs = slice(hd * A_DV, (hd + 1) * A_DV)
        qh = q_ref[:, cs]
        zero = jnp.zeros_like(qh)
        q2 = jnp.concatenate([jnp.where(first, qh, zero), jnp.where(first, zero, qh)], axis=0)
        s_c = _dot(q2, kc_ref[:, cs], _NT)
        s_l = _dot(q2, k_ref[:, cs], _NT)
        m = jnp.maximum(jnp.max(s_c, axis=1, keepdims=True), jnp.max(s_l, axis=1, keepdims=True))
        p_c = jnp.exp((s_c - m).astype(bf16))
        p_l = jnp.exp((s_l - m).astype(bf16))
        oa = (_dot(p_c, jnp.concatenate([vc_ref[:, cs], ones_c], axis=1))
              + _dot(p_l, jnp.concatenate([v_ref[:, cs], ones_l], axis=1)))
        on = oa[:, :A_DV] * (1.0 / oa[:, A_DV:A_DV + 1])
        o = on[:tq] - lam * on[tq:]
        o = o * lax.rsqrt(jnp.mean(o * o, axis=1, keepdims=True) + RMS_EPS)
        o_ref[:, cs] = (o * g_ref[:, cs] * (1.0 - lam_init)).astype(o_ref.dtype)


def _attn(lv, q, k, vvo, ckv, g_a, lam_init, tq=256):
    bsz, s, _ = q.shape
    lc = ckv.shape[1]
    tq = min(tq, s)
    w = A_QW
    return pl.pallas_call(
        functools.partial(_attn_kernel, lam_init=lam_init),
        grid=(bsz, s // tq),
        in_specs=[pl.BlockSpec(lv.shape, lambda b, i: (0, 0)),
                  pl.BlockSpec((None, tq, w), lambda b, i: (b, i, 0)),
                  pl.BlockSpec((None, lc, w), lambda b, i: (b, 0, 0)),
                  pl.BlockSpec((None, s, w), lambda b, i: (b, 0, 0)),
                  pl.BlockSpec((None, lc, w), lambda b, i: (b, 0, 1)),
                  pl.BlockSpec((None, s, w), lambda b, i: (b, 0, 0)),
                  pl.BlockSpec((1, w), lambda b, i: (0, 0))],
        out_specs=pl.BlockSpec((None, tq, w), lambda b, i: (b, i, 0)),
        out_shape=jax.ShapeDtypeStruct((bsz, s, w), bf16),
        compiler_params=_cp("parallel", "arbitrary"),
        name="diff_attn",
    )(lv, q, ckv, k, ckv, vvo, g_a.reshape(1, w))


_LN_QSCALE = math.log(B_DH ** -0.5)
_NCHAIN = 2 * B_HEADS


def _chunk_gate_sums(gi, gf, tri):
    lf = _log_sigmoid(gf)
    hi, lo = _split_bf16(lf)
    cum = _dot(tri, hi) + _dot(tri, lo)
    t = gf.shape[0]
    tot = cum[t - 1:t, :]
    rcum = tot - cum + lf
    fwd = lax.broadcasted_iota(jnp.int32, gf.shape, 1) < B_HEADS
    bd = jnp.where(fwd, cum, rcum)
    return bd, tot, (bd - gi).T


def _lower_tri(t):
    r = lax.broadcasted_iota(jnp.int32, (t, t), 0)
    c = lax.broadcasted_iota(jnp.int32, (t, t), 1)
    return r, c


def _ones_block(t):
    one0 = jnp.where(lax.broadcasted_iota(jnp.int32, (1, B_DH), 1) == 0, 1.0, 0.0).astype(bf16)
    return jnp.broadcast_to(one0, (t, B_DH))


def _absorb(c_ref, m_ref, ch, x_row, tot_c, kb, vaug):
    m_prev = m_ref[ch][:, 0:1]
    g = tot_c - x_row
    m_new = jnp.maximum(tot_c + m_prev, jnp.max(g, axis=1, keepdims=True))
    wgt = jnp.exp(g - m_new)
    decay = jnp.exp(tot_c + m_prev - m_new)
    kw_t = kb.astype(f32).T * wgt
    c_ref[ch] = decay * c_ref[ch] + _dot(kw_t.astype(bf16), vaug)
    m_ref[ch] = jnp.broadcast_to(m_new, m_ref.shape[1:])


def _mlstm_kernel(qk_ref, vvo_ref, g_ref, ck_ref, ckv_ref, cg_ref, gb_ref, o_ref,
                  hf_ref, hb_ref, c_ref, m_ref, *, tc):
    s = o_ref.shape[0]
    lc = ck_ref.shape[0]
    nc = s // tc
    w = B_WIDTH
    dh = B_DH

    c_ref[...] = jnp.zeros_like(c_ref)
    m_ref[...] = jnp.zeros_like(m_ref)

    r, cidx = _lower_tri(lc)
    tri_c = jnp.where(cidx <= r, 1.0, 0.0).astype(bf16)
    cg = cg_ref[...]
    _, tot, xt = _chunk_gate_sums(cg[:, :LANES], cg[:, LANES:], tri_c)
    ones_c = _ones_block(lc)
    for ch in range(_NCHAIN):
        hs = slice((ch % B_HEADS) * dh, (ch % B_HEADS + 1) * dh)
        vs = slice(2 * w + (ch % B_HEADS) * dh, 2 * w + (ch % B_HEADS + 1) * dh)
        _absorb(c_ref, m_ref, ch, xt[ch:ch + 1, :], tot[:, ch:ch + 1], ck_ref[:, hs],
                jnp.concatenate([ckv_ref[:, vs], ones_c], axis=1))
    ones_t = _ones_block(tc)

    r, cidx = _lower_tri(tc)
    tri = jnp.where(cidx <= r, 1.0, 0.0).astype(bf16)
    causal = cidx <= r
    anti = cidx >= r

    def step(i, carry):
        for d in range(2):
            row0 = pl.multiple_of((i if d == 0 else nc - 1 - i) * tc, tc)
            rows = pl.ds(row0, tc)
            gch = g_ref[rows, :]
            gi = gch[:, :LANES]
            bd, tot, xt = _chunk_gate_sums(gi, gch[:, LANES:], tri)
            mask = causal if d == 0 else anti
            dst = hf_ref if d == 0 else hb_ref
            for hd in range(B_HEADS):
                ch = d * B_HEADS + hd
                hs = slice(hd * dh, (hd + 1) * dh)
                qb = qk_ref[rows, hs]
                kb = qk_ref[rows, slice(w + hd * dh, w + (hd + 1) * dh)]
                vaug = jnp.concatenate([vvo_ref[rows, slice(w + hd * dh, w + (hd + 1) * dh)], ones_t],
                                       axis=1)
                bcol = bd[:, ch:ch + 1]
                x_row = xt[ch:ch + 1, :]
                dmat = jnp.where(mask, bcol - x_row, -jnp.inf)
                m_prev = m_ref[ch][:, 0:1]
                inter = bcol + m_prev
                m_t = jnp.maximum(inter, jnp.max(dmat, axis=1, keepdims=True))
                e = jnp.exp(dmat - m_t + _LN_QSCALE)
                smat = _dot(qb, kb, _NT) * e
                sc = jnp.exp(inter - m_t + _LN_QSCALE)
                both = sc * _dot(qb, c_ref[ch].astype(bf16)) + _dot(smat.astype(bf16), vaug)
                den = both[:, dh:dh + 1]
                dst[rows, hs] = both[:, :dh] * (1.0 / jnp.maximum(jnp.abs(den), jnp.exp(-m_t)))
                _absorb(c_ref, m_ref, ch, x_row, tot[:, ch:ch + 1], kb, vaug)
        return carry

    lax.fori_loop(0, nc, step, 0)

    for hd in range(B_HEADS):
        hs = slice(hd * dh, (hd + 1) * dh)
        hsum = hf_ref[:, hs] + hb_ref[:, hs]
        hn = hsum * lax.rsqrt(jnp.mean(hsum * hsum, axis=1, keepdims=True) + RMS_EPS)
        og = _sigmoid(vvo_ref[:, slice(2 * w + hd * dh, 2 * w + (hd + 1) * dh)].astype(f32))
        o_ref[:, hs] = (hn * gb_ref[:, hs] * og).astype(o_ref.dtype)


def _mlstm(qk, vvo, gates, cbk, ckv, cg, g_b):
    bsz, s, _ = qk.shape
    lc = cbk.shape[1]
    w = B_WIDTH
    tc = min(MLSTM_CHUNK, s)
    return pl.pallas_call(
        functools.partial(_mlstm_kernel, tc=tc),
        grid=(bsz,),
        in_specs=[pl.BlockSpec((None, s, 2 * w), lambda b: (b, 0, 0)),
                  pl.BlockSpec((None, s, 3 * w), lambda b: (b, 0, 0)),
                  pl.BlockSpec((None, s, 2 * LANES), lambda b: (b, 0, 0)),
                  pl.BlockSpec((None, lc, w), lambda b: (b, 0, 0)),
                  pl.BlockSpec((None, lc, 3 * w), lambda b: (b, 0, 0)),
                  pl.BlockSpec((None, lc, 2 * LANES), lambda b: (b, 0, 0)),
                  pl.BlockSpec((1, w), lambda b: (0, 0))],
        out_specs=pl.BlockSpec((None, s, w), lambda b: (b, 0, 0)),
        out_shape=jax.ShapeDtypeStruct((bsz, s, w), bf16),
        scratch_shapes=[pltpu.VMEM((s, w), f32), pltpu.VMEM((s, w), f32),
                        pltpu.VMEM((_NCHAIN, B_DH, 2 * B_DH), f32),
                        pltpu.VMEM((_NCHAIN, 1, LANES), f32)],
        compiler_params=_cp("arbitrary"),
        name="mlstm",
    )(qk, vvo, gates, cbk, ckv, cg, g_b.reshape(1, w))


def _out_kernel(*refs, n_act):
    acts = refs[:n_act]
    ws = refs[n_act:2 * n_act]
    x_ref, g_ref, gate_ref, o_ref = refs[2 * n_act:]
    mix = _dot(acts[0][...], ws[0][...])
    for a, wr in zip(acts[1:], ws[1:]):
        mix = mix + _dot(a[...], wr[...])
    y = mix * lax.rsqrt(jnp.mean(mix * mix, axis=-1, keepdims=True) + RMS_EPS) * g_ref[...]
    o_ref[...] = x_ref[...] + gate_ref[...] * y


def _out_proj(acts, ws, x, g, gate, tl=512):
    bsz, l, d = x.shape
    tl = min(tl, l)
    n_act = len(acts)
    in_specs = [pl.BlockSpec((None, tl, a.shape[2]), lambda b, i: (b, i, 0)) for a in acts]
    in_specs += [pl.BlockSpec(wm.shape, lambda b, i: (0, 0)) for wm in ws]
    in_specs += [pl.BlockSpec((None, tl, d), lambda b, i: (b, i, 0)),
                 pl.BlockSpec((1, d), lambda b, i: (0, 0)),
                 pl.BlockSpec((None, 1, d), _bidx(gate))]
    return pl.pallas_call(
        functools.partial(_out_kernel, n_act=n_act),
        grid=(bsz, l // tl),
        in_specs=in_specs,
        out_specs=pl.BlockSpec((None, tl, d), lambda b, i: (b, i, 0)),
        out_shape=jax.ShapeDtypeStruct((bsz, l, d), f32),
        compiler_params=_cp("parallel", "parallel"),
        name="out_proj",
    )(*acts, *ws, x, g.reshape(1, d), gate)


def _router_kernel(x_ref, g_ref, sh_ref, sc_ref, rw_ref, rb_ref, h_ref, rk_ref, gt_ref, cm_ref):
    hf = _norm_mod(x_ref[...], g_ref[...], sh_ref[...], sc_ref[...])
    tl = hf.shape[0]
    h_ref[...] = hf.astype(h_ref.dtype)
    per = N_EXPERTS // N_GROUPS
    logits = _dot3(rw_ref[...], hf, _NT)
    s3 = _sigmoid(logits).reshape(N_GROUPS, per, tl)
    b3 = s3 + rb_ref[...].reshape(N_GROUPS, per, 1)
    neg = -jnp.inf
    jdx = lax.broadcasted_iota(jnp.int32, b3.shape, 1)
    gdx = lax.broadcasted_iota(jnp.int32, b3.shape, 0)
    m1 = jnp.max(b3, axis=1, keepdims=True)
    f1 = jnp.min(jnp.where(b3 == m1, jdx, per), axis=1, keepdims=True)
    m2 = jnp.max(jnp.where(jdx == f1, neg, b3), axis=1, keepdims=True)
    grp = m1 + m2
    g1 = lax.broadcasted_iota(jnp.int32, grp.shape, 0)
    cnt = jnp.zeros(grp.shape, jnp.int32)
    for gp in range(N_GROUPS):
        rv = grp[gp:gp + 1]
        ahead = jnp.where(rv > grp, 1, jnp.where(rv == grp, jnp.where(g1 > gp, 1, 0), 0))
        cnt = cnt + ahead
    v = jnp.where(cnt < TOPK_GROUPS, b3, neg)
    eidx = gdx * per + jdx
    sel = jnp.zeros(b3.shape, f32)
    for _ in range(TOP_K):
        m = jnp.max(jnp.max(v, axis=1, keepdims=True), axis=0, keepdims=True)
        cand = jnp.where(v == m, eidx, N_EXPERTS)
        fi = jnp.min(jnp.min(cand, axis=1, keepdims=True), axis=0, keepdims=True)
        hit = eidx == fi
        sel = jnp.where(hit, 1.0, sel)
        v = jnp.where(hit, neg, v)
    ssel = sel * s3
    den = jnp.sum(jnp.sum(ssel, axis=1, keepdims=True), axis=0, keepdims=True)
    gt_ref[...] = ((ROUTED_SCALE * ssel) / den).reshape(N_EXPERTS, tl)
    sel2 = sel.reshape(N_EXPERTS, tl)
    r = lax.broadcasted_iota(jnp.int32, (MOE_SUB, MOE_SUB), 0)
    c = lax.broadcasted_iota(jnp.int32, (MOE_SUB, MOE_SUB), 1)
    before = jnp.where(r < c, 1.0, 0.0).astype(bf16)
    cmax = jnp.zeros((N_EXPERTS, 1), f32)
    for j in range(tl // MOE_SUB):
        sub = sel2[:, j * MOE_SUB:(j + 1) * MOE_SUB]
        rank = _dot(sub.astype(bf16), before)
        rk_ref[:, j * MOE_SUB:(j + 1) * MOE_SUB] = jnp.where(sub > 0.0, rank, -1.0)
        cmax = jnp.maximum(cmax, jnp.sum(sub, axis=1, keepdims=True))
    cm_ref[...] = jnp.broadcast_to(cmax, cm_ref.shape)


def _router(x, g, shift, scale, rw_t, rb, tl=512):
    bsz, l, d = x.shape
    tl = min(tl, l)
    nl = l // tl
    return pl.pallas_call(
        _router_kernel,
        grid=(bsz, nl),
        in_specs=[pl.BlockSpec((None, tl, d), lambda b, i: (b, i, 0)),
                  pl.BlockSpec((1, d), lambda b, i: (0, 0)),
                  pl.BlockSpec((None, 1, d), _bidx(shift)),
                  pl.BlockSpec((None, 1, d), _bidx(scale)),
                  pl.BlockSpec((N_EXPERTS, d), lambda b, i: (0, 0)),
                  pl.BlockSpec((N_EXPERTS, 1), lambda b, i: (0, 0))],
        out_specs=[pl.BlockSpec((None, tl, d), lambda b, i: (b, i, 0)),
                   pl.BlockSpec((N_EXPERTS, tl), lambda b, i: (0, b * nl + i)),
                   pl.BlockSpec((N_EXPERTS, tl), lambda b, i: (0, b * nl + i)),
                   pl.BlockSpec((None, N_EXPERTS, LANES), lambda b, i: (b * nl + i, 0, 0))],
        out_shape=[jax.ShapeDtypeStruct((bsz, l, d), bf16),
                   jax.ShapeDtypeStruct((N_EXPERTS, bsz * l), f32),
                   jax.ShapeDtypeStruct((N_EXPERTS, bsz * l), f32),
                   jax.ShapeDtypeStruct((bsz * nl, N_EXPERTS, LANES), f32)],
        compiler_params=_cp("parallel", "parallel"),
        name="router",
    )(x, g.reshape(1, d), shift, scale, rw_t, rb.reshape(N_EXPERTS, 1))


def _swiglu_act(hh):
    half = hh.shape[1] // 2
    return _silu(hh[:, :half]) * hh[:, half:]


def _moe_kernel(cnt_ref, ord_ref, h_ref, rk_ref, gt_ref, *refs):
    ng = MOE_GROUP
    gu_refs, dn_refs = refs[:ng], refs[ng:2 * ng]
    (sgu_ref, sdn_ref, x_ref, g_ref, gate_ref, o_ref,
     acc_ref, xg_ref, ys_ref, p_ref, gr_ref) = refs[2 * ng:]
    tile = pl.program_id(0)
    grp = pl.program_id(1)
    tm, d = acc_ref.shape
    ns = tm // MOE_SUB
    win = MOE_WIN
    eids = [ord_ref[tile, grp * ng + el] for el in range(ng)]

    @pl.when(grp == 0)
    def _():
        act = _swiglu_act(_dot(h_ref[...], sgu_ref[...]))
        acc_ref[...] = _dot(act.astype(bf16), sdn_ref[...])

    def expert_ffn(el, win):
        hh = _dot(xg_ref[el, 0:ns * win, :], gu_refs[el][...])
        gr = gr_ref[el, 0:ns * win, :]
        act = _swiglu_act(hh) * jnp.concatenate([gr] * (hh.shape[1] // (2 * LANES)), axis=1)
        y = _dot(act.astype(bf16), dn_refs[el][...]).astype(bf16)
        for s in range(ns):
            ys_ref[s, el * win:(el + 1) * win, :] = y[s * win:(s + 1) * win]

    def one_pass(p, skip_idle, win):
        base = p * win
        riota = lax.broadcasted_iota(jnp.int32, (win, MOE_SUB), 0).astype(f32)
        for s in range(ns):
            cols = slice(s * MOE_SUB, (s + 1) * MOE_SUB)
            onehots = []
            for el in range(ng):
                row = pl.ds(eids[el], 1)
                hit = (rk_ref[row, cols] - base) == riota
                onehots.append(jnp.where(hit, 1.0, 0.0).astype(bf16))
                gsel = jnp.sum(jnp.where(hit, gt_ref[row, cols], 0.0), axis=1, keepdims=True)
                gr_ref[el, s * win:(s + 1) * win, :] = jnp.broadcast_to(gsel, (win, LANES))
            pm = jnp.concatenate(onehots, axis=0)
            p_ref[s, 0:ng * win, :] = pm
            gx = _dot(pm, h_ref[cols, :])
            for el in range(ng):
                xg_ref[el, s * win:(s + 1) * win, :] = gx[el * win:(el + 1) * win].astype(bf16)
        for el in range(ng):
            if not skip_idle:
                expert_ffn(el, win)
                continue
            busy = cnt_ref[tile, eids[el]] > base
            pl.when(busy)(functools.partial(expert_ffn, el, win))

            @pl.when(jnp.logical_not(busy))
            def _():
                for s in range(ns):
                    ys_ref[s, el * win:(el + 1) * win, :] = jnp.zeros((win, d), bf16)
        for s in range(ns):
            acc_ref[s * MOE_SUB:(s + 1) * MOE_SUB, :] += _dot(
                p_ref[s, 0:ng * win, :], ys_ref[s, 0:ng * win, :], _TN)

    most = least = cnt_ref[tile, eids[0]]
    for el in range(1, ng):
        most = jnp.maximum(most, cnt_ref[tile, eids[el]])
        least = jnp.minimum(least, cnt_ref[tile, eids[el]])

    @pl.when(most <= MOE_WIN_SMALL)
    def _():
        one_pass(0, False, MOE_WIN_SMALL)

    @pl.when(most > MOE_WIN_SMALL)
    def _():
        one_pass(0, False, win)

        def later_pass(p, carry):
            all_busy = least > p * win
            pl.when(all_busy)(functools.partial(one_pass, p, False, win))
            pl.when(jnp.logical_not(all_busy))(functools.partial(one_pass, p, True, win))
            return carry

        lax.fori_loop(1, (most + win - 1) // win, later_pass, 0)

    @pl.when(grp == pl.num_programs(1) - 1)
    def _():
        mo = acc_ref[...]
        y = mo * lax.rsqrt(jnp.mean(mo * mo, axis=-1, keepdims=True) + RMS_EPS) * g_ref[...]
        o_ref[...] = x_ref[...] + gate_ref[...] * y


def _moe(h2, rk, gt, cmax, gu, dn, sgu, sdn, x, g, gate, tm=1024):
    bsz, l, d = x.shape
    tm = min(tm, l)
    per_b = l // tm
    nt = bsz * per_b
    ne = gu.shape[0]
    ng = MOE_GROUP
    ns = tm // MOE_SUB
    counts = jnp.max(cmax[:, :, 0].reshape(nt, -1, ne), axis=1).astype(jnp.int32)
    order = jnp.argsort(-counts, axis=1).astype(jnp.int32)

    def expert_spec(arr, k):
        return pl.BlockSpec((None,) + arr.shape[1:], lambda t, e, cnt, order_ref: (order_ref[t, e * ng + k], 0, 0))

    grid_spec = pltpu.PrefetchScalarGridSpec(
        num_scalar_prefetch=2,
        grid=(nt, ne // ng),
        in_specs=[pl.BlockSpec((tm, d), lambda t, e, *_: (t, 0)),
                  pl.BlockSpec((ne, tm), lambda t, e, *_: (0, t)),
                  pl.BlockSpec((ne, tm), lambda t, e, *_: (0, t))]
                 + [expert_spec(gu, k) for k in range(ng)]
                 + [expert_spec(dn, k) for k in range(ng)]
                 + [pl.BlockSpec(sgu.shape, lambda t, e, *_: (0, 0)),
                    pl.BlockSpec(sdn.shape, lambda t, e, *_: (0, 0)),
                    pl.BlockSpec((tm, d), lambda t, e, *_: (t, 0)),
                    pl.BlockSpec((1, d), lambda t, e, *_: (0, 0)),
                    pl.BlockSpec((None, 1, d), lambda t, e, *_: (t // per_b, 0, 0))],
        out_specs=pl.BlockSpec((tm, d), lambda t, e, *_: (t, 0)),
        scratch_shapes=[pltpu.VMEM((tm, d), f32),
                        pltpu.VMEM((ng, ns * MOE_WIN, d), bf16),
                        pltpu.VMEM((ns, ng * MOE_WIN, d), bf16),
                        pltpu.VMEM((ns, ng * MOE_WIN, MOE_SUB), bf16),
                        pltpu.VMEM((ng, ns * MOE_WIN, LANES), f32)])
    out = pl.pallas_call(
        _moe_kernel,
        grid_spec=grid_spec,
        out_shape=jax.ShapeDtypeStruct((bsz * l, d), f32),
        compiler_params=_cp("parallel", "arbitrary"),
        name="moe",
    )(counts, order, h2.reshape(bsz * l, d), rk, gt, *([gu] * ng), *([dn] * ng), sgu, sdn,
      x.reshape(bsz * l, d), g.reshape(1, d), gate)
    return out.reshape(bsz, l, d)


def _filter_kernel(z_ref, w1_ref, b1_ref, w2_ref, b2_ref, w3_ref, win_ref, o_ref):
    hid = jnp.sin(FILTER_SIN_W * (_dot3(z_ref[...], w1_ref[...]) + b1_ref[...]))
    hid = jnp.sin(FILTER_SIN_W * (_dot3(hid, w2_ref[...]) + b2_ref[...]))
    o_ref[...] = _dot3(hid, w3_ref[...]) * win_ref[...]


def _filters(z, w1, b1, w2, b2, w3, window, tn=512):
    l, p = z.shape
    hdim = w1.shape[1]
    n = w3.shape[1]
    d = window.shape[1]
    nd = d // tn
    return pl.pallas_call(
        _filter_kernel,
        grid=(n // tn,),
        in_specs=[pl.BlockSpec((l, p), lambda j: (0, 0)),
                  pl.BlockSpec((p, hdim), lambda j: (0, 0)),
                  pl.BlockSpec((1, hdim), lambda j: (0, 0)),
                  pl.BlockSpec((hdim, hdim), lambda j: (0, 0)),
                  pl.BlockSpec((1, hdim), lambda j: (0, 0)),
                  pl.BlockSpec((hdim, tn), lambda j: (0, j)),
                  pl.BlockSpec((l, tn), lambda j: (0, j % nd))],
        out_specs=pl.BlockSpec((l, tn), lambda j: (0, j)),
        out_shape=jax.ShapeDtypeStruct((l, n), f32),
        compiler_params=_cp("arbitrary"),
        name="hyena_filter",
    )(z, w1, b1.reshape(1, hdim), w2, b2.reshape(1, hdim), w3, window)


def _dft_tables(l):
    n = 2 * l
    n1 = math.isqrt(n)
    assert n == n1 * n1 and n1 % 16 == 0
    na = l // n1
    ncp = -(-(n1 // 2 + 1) // 8) * 8
    a = np.arange(na)
    b = np.arange(n1)
    c = np.arange(ncp)
    th = 2.0 * np.pi * ((n1 * a[None, None, :] + b[:, None, None]) * c[None, :, None]) / n
    t1 = np.concatenate([np.cos(th), -np.sin(th)], axis=1)
    ph = 2.0 * np.pi * (b[:, None] * b[None, :]) / n1
    cs, sn = np.cos(ph), np.sin(ph)
    a3 = np.block([[cs, sn], [-sn, cs]])
    a3i = np.block([[cs, -sn], [sn, cs]])
    a2 = np.arange(na) + na // 2
    th2 = 2.0 * np.pi * ((n1 * a2[None, :, None] + b[:, None, None]) * c[None, None, :]) / n
    wc = np.where((c == 0) | (c == n1 // 2), 1.0, np.where(c < n1 // 2, 2.0, 0.0))[None, None, :]
    t2 = np.concatenate([wc * np.cos(th2), -wc * np.sin(th2)], axis=2)
    return [jnp.asarray(t, f32).astype(bf16) for t in (t1, a3, a3i, t2)]


def _fft_dims(t1):
    n1, ncp2, na = t1.shape
    ncp = ncp2 // 2
    return n1, ncp, na, 2 * n1 + FFT_PAD, 2 * ncp + FFT_PAD, n1 + FFT_PAD


def _ld(ref, rows):
    return jnp.concatenate([ref[j, rows, :] for j in range(ref.shape[0])], axis=1)


def _st(ref, rows, val):
    for j in range(ref.shape[0]):
        ref[j, rows, :] = val[:, j * LANES:(j + 1) * LANES]


def _dft_forward(uf_ref, t1_ref, zs_ref):
    n1, ncp, na, sb, _, su = _fft_dims(t1_ref)
    for b in range(n1):
        ub = _ld(uf_ref, pl.ds(b, na, stride=su)).astype(bf16)
        zb = _dot(t1_ref[b], ub)
        _st(zs_ref, pl.ds(b, ncp, stride=sb), zb[:ncp])
        _st(zs_ref, pl.ds(n1 + b, ncp, stride=sb), zb[ncp:])


def _spectrum_kernel(f_ref, t1_ref, a3_ref, o_ref, uf_ref, zs_ref, *, scale):
    n1, ncp, na, sb, _, su = _fft_dims(t1_ref)
    for a in range(na):
        _st(uf_ref, pl.ds(a * su, n1), f_ref[pl.ds(a * n1, n1), :])
    _dft_forward(uf_ref, t1_ref, zs_ref)
    a3 = a3_ref[...]
    for c in range(ncp):
        zc = _ld(zs_ref, pl.ds(c * sb, 2 * n1)).astype(bf16)
        o_ref[c] = (_dot(a3, zc) * scale).astype(o_ref.dtype)


def _spectrum(filt, tabs, dt=256):
    l, n = filt.shape
    t1, a3, _, _ = tabs
    n1, ncp, na, sb, _, su = _fft_dims(t1)
    nj = dt // LANES
    return pl.pallas_call(
        functools.partial(_spectrum_kernel, scale=1.0 / (2 * l)),
        grid=(n // dt,),
        in_specs=[pl.BlockSpec((l, dt), lambda j: (0, j)),
                  pl.BlockSpec(t1.shape, lambda j: (0, 0, 0)),
                  pl.BlockSpec(a3.shape, lambda j: (0, 0))],
        out_specs=pl.BlockSpec((ncp, 2 * n1, dt), lambda j: (0, 0, j)),
        out_shape=jax.ShapeDtypeStruct((ncp, 2 * n1, n), bf16),
        scratch_shapes=[pltpu.VMEM((nj, na * su, LANES), f32),
                        pltpu.VMEM((nj, ncp * sb, LANES), f32)],
        compiler_params=_cp("arbitrary"),
        name="hyena_spectrum",
    )(filt, t1, a3)


def _fftconv_kernel(u_ref, xg_ref, kf_ref, fb_ref, t1_ref, a3_ref, a3i_ref, t2_ref, o_ref,
                    uf_ref, zs_ref, qs_ref, y_ref):
    n1, ncp, na, sb, sq, su = _fft_dims(t1_ref)
    for a in range(na):
        _st(uf_ref, pl.ds(a * su, n1), u_ref[pl.ds(a * n1, n1), :].astype(f32))
    _dft_forward(uf_ref, t1_ref, zs_ref)
    a3 = a3_ref[...]
    a3i = a3i_ref[...]
    for c in range(ncp):
        zc = _ld(zs_ref, pl.ds(c * sb, 2 * n1)).astype(bf16)
        xc = _dot(a3, zc)
        kc = kf_ref[c].astype(f32)
        xr, xi = xc[:n1], xc[n1:]
        kr, ki = kc[:n1], kc[n1:]
        pc = jnp.concatenate([xr * kr - xi * ki, xr * ki + xi * kr], axis=0).astype(bf16)
        qc = _dot(a3i, pc)
        _st(qs_ref, pl.ds(c, n1, stride=sq), qc[:n1])
        _st(qs_ref, pl.ds(ncp + c, n1, stride=sq), qc[n1:])
    for b in range(n1):
        qb = _ld(qs_ref, pl.ds(b * sq, 2 * ncp)).astype(bf16)
        _st(y_ref, pl.ds(b, na, stride=su), _dot(t2_ref[b], qb))
    fb = fb_ref[...]
    for a in range(na):
        rows = pl.ds(a * n1, n1)
        uv = _ld(uf_ref, pl.ds(a * su, n1))
        yv = _ld(y_ref, pl.ds(a * su, n1))
        o_ref[rows, :] = (xg_ref[rows, :].astype(f32) * (yv + uv * fb)).astype(o_ref.dtype)


def _fftconv(u, u_col, xg, xg_col, kf, kf_col, fbias, tabs, d, dt=256):
    bsz, l, _ = u.shape
    t1, a3, a3i, t2 = tabs
    n1, ncp, na, sb, sq, su = _fft_dims(t1)
    nd = d // dt
    nj = dt // LANES
    uo, go, ko = u_col // dt, xg_col // dt, kf_col // dt
    return pl.pallas_call(
        _fftconv_kernel,
        grid=(nd, bsz),
        in_specs=[pl.BlockSpec((None, l, dt), lambda j, b: (b, 0, j + uo)),
                  pl.BlockSpec((None, l, dt), lambda j, b: (b, 0, j + go)),
                  pl.BlockSpec((ncp, 2 * n1, dt), lambda j, b: (0, 0, j + ko)),
                  pl.BlockSpec((1, dt), lambda j, b: (0, j)),
                  pl.BlockSpec(t1.shape, lambda j, b: (0, 0, 0)),
                  pl.BlockSpec(a3.shape, lambda j, b: (0, 0)),
                  pl.BlockSpec(a3i.shape, lambda j, b: (0, 0)),
                  pl.BlockSpec(t2.shape, lambda j, b: (0, 0, 0))],
        out_specs=pl.BlockSpec((None, l, dt), lambda j, b: (b, 0, j)),
        out_shape=jax.ShapeDtypeStruct((bsz, l, d), bf16),
        scratch_shapes=[pltpu.VMEM((nj, na * su, LANES), f32),
                        pltpu.VMEM((nj, ncp * sb, LANES), f32),
                        pltpu.VMEM((nj, n1 * sq, LANES), f32),
                        pltpu.VMEM((nj, na * su, LANES), f32)],
        compiler_params=_cp("parallel", "arbitrary"),
        name="hyena_fftconv",
    )(u, xg, kf, fbias.reshape(1, d), t1, a3, a3i, t2)


def _rope_tables(l):
    rows = l // GRID_W
    row = jnp.repeat(jnp.arange(rows), GRID_W)
    col = jnp.tile(jnp.arange(GRID_W), rows)
    inv = ROPE_BASE ** (-jnp.arange(ROPE_AXIS_PAIRS, dtype=f32) / ROPE_AXIS_PAIRS)
    ang = jnp.stack([row, col], axis=-1).astype(f32)[..., None] * inv
    ang = jnp.broadcast_to(ang[:, :, None, :], (l, 2, 2, ROPE_AXIS_PAIRS)).reshape(l, A_DQK)
    reps = A_QW // A_DQK
    return jnp.tile(jnp.cos(ang), (1, reps)), jnp.tile(jnp.sin(ang), (1, reps))


def _rotate_cols(w):
    j = np.arange(w.shape[1])
    lo = (j % (2 * ROPE_AXIS_PAIRS)) < ROPE_AXIS_PAIRS
    perm = np.where(lo, j + ROPE_AXIS_PAIRS, j - ROPE_AXIS_PAIRS)
    sign = np.where(lo, -1.0, 1.0).astype(np.float32)
    return w[:, perm] * sign


def _gate_cols(w_g, b_g):
    idx_i = np.array([d * 2 * B_HEADS + hd for d in range(2) for hd in range(B_HEADS)])
    idx_f = idx_i + B_HEADS
    pad = LANES - _NCHAIN
    k = w_g.shape[0]
    w = jnp.concatenate([w_g[:, idx_i], jnp.zeros((k, pad), f32),
                         w_g[:, idx_f], jnp.zeros((k, pad), f32)], axis=1)
    b = jnp.concatenate([b_g[idx_i], jnp.zeros((pad,), f32), b_g[idx_f], jnp.zeros((pad,), f32)])
    return w, b


def _hyena_consts(l, d):
    j = jnp.arange(l, dtype=f32)
    bands = (POS_EMB_DIM - 1) // 2
    freqs = jnp.linspace(1e-4, bands - 1, bands, dtype=f32)
    ang = (2.0 * math.pi / l) * j[:, None] * freqs[None, :]
    z = jnp.concatenate([(j / (l - 1))[:, None], jnp.cos(ang), -jnp.sin(ang)], axis=-1)
    dist = jnp.abs(j - l // 2) / (l // 2)
    max_decay = math.log(DECAY_TARGET) / DECAY_FAST_PCT
    min_decay = math.log(DECAY_TARGET) / DECAY_SLOW_PCT
    deltas = jnp.abs(jnp.linspace(min_decay, max_decay, d, dtype=f32))
    window = jnp.exp(-dist[:, None] * deltas[None, :])
    return z, window


def _ab_layer(x, ctx, mod_vecs, mod_ctx, norm_g, w_in, conv_w, conv_b, gate_b, lam_vecs,
              g_a, g_b, w_out, lam_init):
    sh_m, sc_m, g_m = mod_vecs
    bsz, s, d = x.shape
    h = _norm(x, norm_g[0], sh_m, sc_m)
    hc = _norm(ctx, norm_g[0], mod_ctx[0], mod_ctx[1])
    w = B_WIDTH
    o = 0
    cols = {}
    for name, width in (("aq", A_QW), ("bq", w), ("bo", w), ("ak", A_QW), ("av", A_VW),
                        ("bk", w), ("bv", w), ("g", 4 * B_HEADS)):
        cols[name] = w_in[:, o:o + width]
        o += width
    cos, sin = _rope_tables(s)
    cat = lambda *ws: jnp.concatenate(ws, axis=1).astype(bf16)
    q = _mm(h, cat(cols["aq"], _rotate_cols(cols["aq"])), rope=(cos, sin, A_DQK ** -0.5))
    k = _mm(h, cat(cols["ak"], _rotate_cols(cols["ak"])), rope=(cos, sin, 1.0))
    qk = _mm(h, cat(cols["bq"], cols["bk"]), conv=(conv_w, conv_b, True))
    vvo = _mm(h, cat(cols["av"], cols["bv"], cols["bo"]))
    wg, bg = _gate_cols(cols["g"], gate_b)
    gates = _mm(h, wg.astype(bf16), out_dtype=f32, bias=bg, tn=2 * LANES)
    ckv = _mm(hc, cat(cols["ak"], cols["av"], cols["bv"]))
    cbk = _mm(hc, cols["bk"].astype(bf16), conv=(conv_w[:, w:], conv_b[w:], True))
    cg = _mm(hc, wg.astype(bf16), out_dtype=f32, bias=bg, tn=2 * LANES)
    out_a = _attn(lam_vecs, q, k, vvo, ckv, g_a, lam_init)
    out_b = _mlstm(qk, vvo, gates, cbk, ckv, cg, g_b)
    wo = w_out.astype(bf16)
    return _out_proj([out_a, out_b], [wo[:A_VW], wo[A_VW:]], x, norm_g[1], g_m)


def _hyena_layer(x, mod_vecs, norm_g, w_in, conv_w, conv_b, fw1, fb1, fw2, fb2, fw3, fbias, w_out):
    sh_m, sc_m, g_m = mod_vecs
    bsz, l, d = x.shape
    h = _norm(x, norm_g[0], sh_m, sc_m)
    u = _mm(h, w_in.astype(bf16), conv=(conv_w, conv_b, False))
    z, window = _hyena_consts(l, d)
    pz, ph = LANES - z.shape[1], LANES - fw1.shape[1]
    filt = _filters(jnp.pad(z, ((0, 0), (0, pz))), jnp.pad(fw1, ((0, pz), (0, ph))),
                    jnp.pad(fb1, (0, ph)), jnp.pad(fw2, ((0, ph), (0, ph))), jnp.pad(fb2, (0, ph)),
                    jnp.pad(fw3, ((0, ph), (0, 0))), window)
    tabs = _dft_tables(l)
    kf = _spectrum(filt, tabs)
    zz = _fftconv(u, 0, u, d, kf, 0, fbias[0], tabs, d)
    y = _fftconv(zz, 0, u, 2 * d, kf, d, fbias[1], tabs, d)
    return _out_proj([y], [w_out.astype(bf16)], x, norm_g[1], g_m)


def kernel(x, c, ctx, c_ctx, w_mod, b_mod, norm_g, w_in_ab, conv_ab_w, conv_ab_b, gate_b_ab, diff_lambda, head_g_a, head_g_b, w_out_ab, w_in_hy, conv_hy_w, conv_hy_b, filt_w1, filt_b1, filt_w2, filt_b2, filt_w3, filt_bias, w_out_hy, router_w, router_b, exp_gu, exp_down, sh_gu, sh_down):
    bsz, s, d = x.shape
    depth = w_mod.shape[0]
    rows = -(-(bsz + 1) // 8) * 8
    cc = jnp.concatenate([c, c_ctx[None, :], jnp.zeros((rows - bsz - 1, d), f32)], axis=0)
    for l in range(depth):
        mod = _mod(cc, w_mod[l], b_mod[l])
        vec = lambda i: mod[:bsz, i * d:(i + 1) * d].reshape(bsz, 1, d)
        sh_m, sc_m, g_m, sh_f, sc_f, g_f = [vec(i) for i in range(6)]
        if l % 2 == 0:
            e = l // 2
            lam_init = 0.8 - 0.6 * math.exp(-0.3 * l)
            mod_ctx = (mod[bsz:bsz + 1, 0:d].reshape(1, 1, d), mod[bsz:bsz + 1, d:2 * d].reshape(1, 1, d))
            x = _ab_layer(x, ctx, (sh_m, sc_m, g_m), mod_ctx, norm_g[l], w_in_ab[e], conv_ab_w[e],
                          conv_ab_b[e], gate_b_ab[e], diff_lambda[e], head_g_a[e], head_g_b[e],
                          w_out_ab[e], lam_init)
        else:
            o = l // 2
            x = _hyena_layer(x, (sh_m, sc_m, g_m), norm_g[l], w_in_hy[o], conv_hy_w[o], conv_hy_b[o],
                             filt_w1[o], filt_b1[o], filt_w2[o], filt_b2[o], filt_w3[o], filt_bias[o],
                             w_out_hy[o])
        h2, rk, gt, cmax = _router(x, norm_g[l, 2], sh_f, sc_f, router_w[l].T, router_b[l])
        x = _moe(h2, rk, gt, cmax, exp_gu[l].astype(bf16), exp_down[l].astype(bf16), sh_gu[l].astype(bf16),
                 sh_down[l].astype(bf16), x, norm_g[l, 3], g_f)
    return x
```

```python
import functools
import math

import numpy as np
import jax
import jax.numpy as jnp
from jax import lax
from jax.experimental import pallas as pl
from jax.experimental.pallas import tpu as pltpu

f32 = jnp.float32
bf16 = jnp.bfloat16

RMS_EPS = 1e-6
A_HEADS = 4
A_DQK = 64
A_DV = 128
B_HEADS = 4
B_DH = 128
B_WIDTH = B_HEADS * B_DH
A_QW = A_HEADS * 2 * A_DQK
A_VW = A_HEADS * A_DV
GRID_W = 64
ROPE_BASE = 10000.0
ROPE_AXIS_PAIRS = A_DQK // 4
N_EXPERTS = 64
TOP_K = 8
N_GROUPS = 8
TOPK_GROUPS = 4
D_EXPERT = 256
ROUTED_SCALE = 2.5
HY_ORDER = 2
POS_EMB_DIM = 33
FILTER_SIN_W = 1.0
DECAY_FAST_PCT = 0.3
DECAY_SLOW_PCT = 1.5
DECAY_TARGET = 1e-2

LANES = 128
VMEM_LIMIT = 56 * 1024 * 1024
MLSTM_CHUNK = 256
FFT_PAD = 8
MOE_SUB = 256
MOE_WINDOWS = (32, 48, 64, 128)
MOE_GROUP = 4


def _cp(*sem):
    return pltpu.CompilerParams(dimension_semantics=sem, vmem_limit_bytes=VMEM_LIMIT)


def _split_bf16(a):
    hi = a.astype(bf16)
    lo = (a - hi.astype(f32)).astype(bf16)
    return hi, lo


def _dot(a, b, dims=(((1,), (0,)), ((), ()))):
    return lax.dot_general(a, b, dims, preferred_element_type=f32)


_NT = (((1,), (1,)), ((), ()))
_TN = (((0,), (0,)), ((), ()))


def _dot3(a, b, dims=(((1,), (0,)), ((), ()))):
    ah, al = _split_bf16(a)
    bh, bl = _split_bf16(b)
    return _dot(ah, bh, dims) + (_dot(ah, bl, dims) + _dot(al, bh, dims))


def _silu(v):
    return v / (1.0 + jnp.exp(-v))


def _sigmoid(v):
    return 1.0 / (1.0 + jnp.exp(-v))


def _log_sigmoid(v):
    return jnp.minimum(v, 0.0) - jnp.log(1.0 + jnp.exp(-jnp.abs(v)))


def _mod_kernel(c_ref, w_ref, b_ref, o_ref):
    o_ref[...] = _dot3(_silu(c_ref[...]), w_ref[...]) + b_ref[...]


def _mod(cc, w, b):
    rows, d = cc.shape
    n = w.shape[1]
    tn = d
    return pl.pallas_call(
        _mod_kernel,
        grid=(n // tn,),
        in_specs=[pl.BlockSpec((rows, d), lambda j: (0, 0)),
                  pl.BlockSpec((d, tn), lambda j: (0, j)),
                  pl.BlockSpec((1, tn), lambda j: (0, j))],
        out_specs=pl.BlockSpec((rows, tn), lambda j: (0, j)),
        out_shape=jax.ShapeDtypeStruct((rows, n), f32),
        compiler_params=_cp("arbitrary"),
        name="mod",
    )(cc, w, b.reshape(1, n))


def _norm_mod(xv, g, shift, scale):
    y = xv * lax.rsqrt(jnp.mean(xv * xv, axis=-1, keepdims=True) + RMS_EPS)
    return (y * g) * (1.0 + scale) + shift


def _norm_kernel(x_ref, g_ref, sh_ref, sc_ref, o_ref):
    o_ref[...] = _norm_mod(x_ref[...], g_ref[...], sh_ref[...], sc_ref[...]).astype(o_ref.dtype)


def _bidx(arr):
    if arr.shape[0] == 1:
        return lambda b, *_: (0, 0, 0)
    return lambda b, *_: (b, 0, 0)


def _norm(x, g, shift, scale, tl=512):
    bsz, l, d = x.shape
    tl = min(tl, l)
    return pl.pallas_call(
        _norm_kernel,
        grid=(bsz, l // tl),
        in_specs=[pl.BlockSpec((None, tl, d), lambda b, i: (b, i, 0)),
                  pl.BlockSpec((1, d), lambda b, i: (0, 0)),
                  pl.BlockSpec((None, 1, d), _bidx(shift)),
                  pl.BlockSpec((None, 1, d), _bidx(scale))],
        out_specs=pl.BlockSpec((None, tl, d), lambda b, i: (b, i, 0)),
        out_shape=jax.ShapeDtypeStruct((bsz, l, d), bf16),
        compiler_params=_cp("parallel", "parallel"),
        name="norm",
    )(x, g.reshape(1, d), shift, scale)


def _mm_plain_kernel(h_ref, w_ref, b_ref, o_ref):
    o_ref[...] = (_dot(h_ref[...], w_ref[...]) + b_ref[...]).astype(o_ref.dtype)


def _mm_rope_kernel(h_ref, w_ref, cos_ref, sin_ref, o_ref, *, scale):
    p = _dot(h_ref[...], w_ref[...])
    n = o_ref.shape[-1]
    o_ref[...] = ((p[:, :n] * cos_ref[...] + p[:, n:] * sin_ref[...]) * scale).astype(o_ref.dtype)


def _mm_conv_kernel(h_ref, w_ref, cw_ref, cb_ref, o_ref, scr_ref, *, act, rc):
    l = h_ref.shape[0]
    w = w_ref[...]
    w0, w1, w2, cb = cw_ref[0:1, :], cw_ref[1:2, :], cw_ref[2:3, :], cb_ref[...]
    halo = 16
    zrow = jnp.zeros((8, o_ref.shape[1]), f32)
    for c in range(l // rc):
        lo, hi = max(c * rc - halo, 0), min((c + 1) * rc + halo, l)
        n = hi - lo
        scr = scr_ref.at[c % 2]
        scr[8:8 + n, :] = _dot(h_ref[lo:hi, :], w)
        if lo == 0:
            scr[0:8, :] = zrow
        if hi == l:
            scr[8 + n:16 + n, :] = zrow
        off = 8 + c * rc - lo
        y = (scr[off - 1:off - 1 + rc, :] * w0 + scr[off:off + rc, :] * w1
             + scr[off + 1:off + 1 + rc, :] * w2 + cb)
        o_ref[c * rc:(c + 1) * rc, :] = (_silu(y) if act else y).astype(o_ref.dtype)


def _mm(h, w, *, out_dtype=bf16, bias=None, rope=None, conv=None, tl=512, tn=512):
    bsz, l, k = h.shape
    n = w.shape[1]
    if rope is not None:
        cos, sin, scale = rope
        n_out = n // 2
        tl = min(tl, l)
        return pl.pallas_call(
            functools.partial(_mm_rope_kernel, scale=scale),
            grid=(bsz, l // tl),
            in_specs=[pl.BlockSpec((None, tl, k), lambda b, i: (b, i, 0)),
                      pl.BlockSpec((k, n), lambda b, i: (0, 0)),
                      pl.BlockSpec((tl, n_out), lambda b, i: (i, 0)),
                      pl.BlockSpec((tl, n_out), lambda b, i: (i, 0))],
            out_specs=pl.BlockSpec((None, tl, n_out), lambda b, i: (b, i, 0)),
            out_shape=jax.ShapeDtypeStruct((bsz, l, n_out), out_dtype),
            compiler_params=_cp("parallel", "parallel"),
            name="mm_rope",
        )(h, w, cos, sin)
    tn = min(tn, n)
    if conv is not None:
        cw, cb, act = conv
        rc = min(512, l)
        return pl.pallas_call(
            functools.partial(_mm_conv_kernel, act=act, rc=rc),
            grid=(bsz, n // tn),
            in_specs=[pl.BlockSpec((None, l, k), lambda b, j: (b, 0, 0)),
                      pl.BlockSpec((k, tn), lambda b, j: (0, j)),
                      pl.BlockSpec((3, tn), lambda b, j: (0, j)),
                      pl.BlockSpec((1, tn), lambda b, j: (0, j))],
            out_specs=pl.BlockSpec((None, l, tn), lambda b, j: (b, 0, j)),
            out_shape=jax.ShapeDtypeStruct((bsz, l, n), out_dtype),
            scratch_shapes=[pltpu.VMEM((2, rc + 48, tn), f32)],
            compiler_params=_cp("parallel", "arbitrary"),
            name="mm_conv",
        )(h, w, cw, cb.reshape(1, n))
    if bias is None:
        bias = jnp.zeros((n,), f32)
    if k * n * w.dtype.itemsize <= 4 * 1024 * 1024:
        tn = n
    tl = min(tl, l)
    return pl.pallas_call(
        _mm_plain_kernel,
        grid=(bsz, l // tl, n // tn),
        in_specs=[pl.BlockSpec((None, tl, k), lambda b, i, j: (b, i, 0)),
                  pl.BlockSpec((k, tn), lambda b, i, j: (0, j)),
                  pl.BlockSpec((1, tn), lambda b, i, j: (0, j))],
        out_specs=pl.BlockSpec((None, tl, tn), lambda b, i, j: (b, i, j)),
        out_shape=jax.ShapeDtypeStruct((bsz, l, n), out_dtype),
        compiler_params=_cp("parallel", "parallel", "arbitrary"),
        name="mm_plain",
    )(h, w, bias.reshape(1, n))


def _attn_kernel(lv_ref, q_ref, kc_ref, k_ref, vc_ref, v_ref, g_ref, o_ref, *, lam_init):
    tq = q_ref.shape[0]
    lv = lv_ref[...]
    lam = (jnp.exp(jnp.sum(lv[0:1] * lv[1:2], axis=1, keepdims=True))
           - jnp.exp(jnp.sum(lv[2:3] * lv[3:4], axis=1, keepdims=True)) + lam_init)
    first = lax.broadcasted_iota(jnp.int32, (tq, A_DV), 1) < A_DQK
    one0 = jnp.where(lax.broadcasted_iota(jnp.int32, (1, A_DV), 1) == 0, 1.0, 0.0).astype(bf16)
    ones_c = jnp.broadcast_to(one0, (kc_ref.shape[0], A_DV))
    ones_l = jnp.broadcast_to(one0, (k_ref.shape[0], A_DV))
    for hd in range(A_HEADS):
        cs = slice(hd * A_DV, (hd + 1) * A_DV)
        qh = q_ref[:, cs]
        zero = jnp.zeros_like(qh)
        q2 = jnp.concatenate([jnp.where(first, qh, zero), jnp.where(first, zero, qh)], axis=0)
        s_c = _dot(q2, kc_ref[:, cs], _NT)
        s_l = _dot(q2, k_ref[:, cs], _NT)
        m = jnp.maximum(jnp.max(s_c, axis=1, keepdims=True), jnp.max(s_l, axis=1, keepdims=True))
        p_c = jnp.exp((s_c - m).astype(bf16))
        p_l = jnp.exp((s_l - m).astype(bf16))
        oa = (_dot(p_c, jnp.concatenate([vc_ref[:, cs], ones_c], axis=1))
              + _dot(p_l, jnp.concatenate([v_ref[:, cs], ones_l], axis=1)))
        on = oa[:, :A_DV] * (1.0 / oa[:, A_DV:A_DV + 1])
        o = on[:tq] - lam * on[tq:]
        o = o * lax.rsqrt(jnp.mean(o * o, axis=1, keepdims=True) + RMS_EPS)
        o_ref[:, cs] = (o * g_ref[:, cs] * (1.0 - lam_init)).astype(o_ref.dtype)


def _attn(lv, q, k, vvo, ckv, g_a, lam_init, tq=256):
    bsz, s, _ = q.shape
    lc = ckv.shape[1]
    tq = min(tq, s)
    w = A_QW
    return pl.pallas_call(
        functools.partial(_attn_kernel, lam_init=lam_init),
        grid=(bsz, s // tq),
        in_specs=[pl.BlockSpec(lv.shape, lambda b, i: (0, 0)),
                  pl.BlockSpec((None, tq, w), lambda b, i: (b, i, 0)),
                  pl.BlockSpec((None, lc, w), lambda b, i: (b, 0, 0)),
                  pl.BlockSpec((None, s, w), lambda b, i: (b, 0, 0)),
                  pl.BlockSpec((None, lc, w), lambda b, i: (b, 0, 1)),
                  pl.BlockSpec((None, s, w), lambda b, i: (b, 0, 0)),
                  pl.BlockSpec((1, w), lambda b, i: (0, 0))],
        out_specs=pl.BlockSpec((None, tq, w), lambda b, i: (b, i, 0)),
        out_shape=jax.ShapeDtypeStruct((bsz, s, w), bf16),
        compiler_params=_cp("parallel", "arbitrary"),
        name="diff_attn",
    )(lv, q, ckv, k, ckv, vvo, g_a.reshape(1, w))


_LN_QSCALE = math.log(B_DH ** -0.5)
_NCHAIN = 2 * B_HEADS


def _chunk_gate_sums(gi, gf, tri):
    lf = _log_sigmoid(gf)
    hi, lo = _split_bf16(lf)
    cum = _dot(tri, hi) + _dot(tri, lo)
    t = gf.shape[0]
    tot = cum[t - 1:t, :]
    rcum = tot - cum + lf
    fwd = lax.broadcasted_iota(jnp.int32, gf.shape, 1) < B_HEADS
    bd = jnp.where(fwd, cum, rcum)
    return bd, tot, (bd - gi).T


def _lower_tri(t):
    r = lax.broadcasted_iota(jnp.int32, (t, t), 0)
    c = lax.broadcasted_iota(jnp.int32, (t, t), 1)
    return r, c


def _ones_block(t):
    one0 = jnp.where(lax.broadcasted_iota(jnp.int32, (1, B_DH), 1) == 0, 1.0, 0.0).astype(bf16)
    return jnp.broadcast_to(one0, (t, B_DH))


def _absorb(c_ref, m_ref, ch, x_row, tot_c, kb, vaug):
    m_prev = m_ref[ch][:, 0:1]
    g = tot_c - x_row
    m_new = jnp.maximum(tot_c + m_prev, jnp.max(g, axis=1, keepdims=True))
    wgt = jnp.exp(g - m_new)
    decay = jnp.exp(tot_c + m_prev - m_new)
    kw_t = kb.astype(f32).T * wgt
    c_ref[ch] = decay * c_ref[ch] + _dot(kw_t.astype(bf16), vaug)
    m_ref[ch] = jnp.broadcast_to(m_new, m_ref.shape[1:])


def _mlstm_kernel(qk_ref, vvo_ref, g_ref, ck_ref, ckv_ref, cg_ref, gb_ref, o_ref,
                  hf_ref, hb_ref, c_ref, m_ref, *, tc):
    s = o_ref.shape[0]
    lc = ck_ref.shape[0]
    nc = s // tc
    w = B_WIDTH
    dh = B_DH

    c_ref[...] = jnp.zeros_like(c_ref)
    m_ref[...] = jnp.zeros_like(m_ref)

    r, cidx = _lower_tri(lc)
    tri_c = jnp.where(cidx <= r, 1.0, 0.0).astype(bf16)
    cg = cg_ref[...]
    _, tot, xt = _chunk_gate_sums(cg[:, :LANES], cg[:, LANES:], tri_c)
    ones_c = _ones_block(lc)
    for ch in range(_NCHAIN):
        hs = slice((ch % B_HEADS) * dh, (ch % B_HEADS + 1) * dh)
        vs = slice(2 * w + (ch % B_HEADS) * dh, 2 * w + (ch % B_HEADS + 1) * dh)
        _absorb(c_ref, m_ref, ch, xt[ch:ch + 1, :], tot[:, ch:ch + 1], ck_ref[:, hs],
                jnp.concatenate([ckv_ref[:, vs], ones_c], axis=1))
    ones_t = _ones_block(tc)

    r, cidx = _lower_tri(tc)
    tri = jnp.where(cidx <= r, 1.0, 0.0).astype(bf16)
    causal = cidx <= r
    anti = cidx >= r

    def step(i, carry):
        for d in range(2):
            row0 = pl.multiple_of((i if d == 0 else nc - 1 - i) * tc, tc)
            rows = pl.ds(row0, tc)
            gch = g_ref[rows, :]
            gi = gch[:, :LANES]
            bd, tot, xt = _chunk_gate_sums(gi, gch[:, LANES:], tri)
            mask = causal if d == 0 else anti
            dst = hf_ref if d == 0 else hb_ref
            for hd in range(B_HEADS):
                ch = d * B_HEADS + hd
                hs = slice(hd * dh, (hd + 1) * dh)
                qb = qk_ref[rows, hs]
                kb = qk_ref[rows, slice(w + hd * dh, w + (hd + 1) * dh)]
                vaug = jnp.concatenate([vvo_ref[rows, slice(w + hd * dh, w + (hd + 1) * dh)], ones_t],
                                       axis=1)
                bcol = bd[:, ch:ch + 1]
                x_row = xt[ch:ch + 1, :]
                dmat = jnp.where(mask, bcol - x_row, -jnp.inf)
                m_prev = m_ref[ch][:, 0:1]
                inter = bcol + m_prev
                m_t = jnp.maximum(inter, jnp.max(dmat, axis=1, keepdims=True))
                e = jnp.exp(dmat - m_t + _LN_QSCALE)
                smat = _dot(qb, kb, _NT) * e
                sc = jnp.exp(inter - m_t + _LN_QSCALE)
                both = sc * _dot(qb, c_ref[ch].astype(bf16)) + _dot(smat.astype(bf16), vaug)
                den = both[:, dh:dh + 1]
                dst[rows, hs] = both[:, :dh] * (1.0 / jnp.maximum(jnp.abs(den), jnp.exp(-m_t)))
                _absorb(c_ref, m_ref, ch, x_row, tot[:, ch:ch + 1], kb, vaug)
        return carry

    lax.fori_loop(0, nc, step, 0)

    for hd in range(B_HEADS):
        hs = slice(hd * dh, (hd + 1) * dh)
        hsum = hf_ref[:, hs] + hb_ref[:, hs]
        hn = hsum * lax.rsqrt(jnp.mean(hsum * hsum, axis=1, keepdims=True) + RMS_EPS)
        og = _sigmoid(vvo_ref[:, slice(2 * w + hd * dh, 2 * w + (hd + 1) * dh)].astype(f32))
        o_ref[:, hs] = (hn * gb_ref[:, hs] * og).astype(o_ref.dtype)


def _mlstm(qk, vvo, gates, cbk, ckv, cg, g_b):
    bsz, s, _ = qk.shape
    lc = cbk.shape[1]
    w = B_WIDTH
    tc = min(MLSTM_CHUNK, s)
    return pl.pallas_call(
        functools.partial(_mlstm_kernel, tc=tc),
        grid=(bsz,),
        in_specs=[pl.BlockSpec((None, s, 2 * w), lambda b: (b, 0, 0)),
                  pl.BlockSpec((None, s, 3 * w), lambda b: (b, 0, 0)),
                  pl.BlockSpec((None, s, 2 * LANES), lambda b: (b, 0, 0)),
                  pl.BlockSpec((None, lc, w), lambda b: (b, 0, 0)),
                  pl.BlockSpec((None, lc, 3 * w), lambda b: (b, 0, 0)),
                  pl.BlockSpec((None, lc, 2 * LANES), lambda b: (b, 0, 0)),
                  pl.BlockSpec((1, w), lambda b: (0, 0))],
        out_specs=pl.BlockSpec((None, s, w), lambda b: (b, 0, 0)),
        out_shape=jax.ShapeDtypeStruct((bsz, s, w), bf16),
        scratch_shapes=[pltpu.VMEM((s, w), f32), pltpu.VMEM((s, w), f32),
                        pltpu.VMEM((_NCHAIN, B_DH, 2 * B_DH), f32),
                        pltpu.VMEM((_NCHAIN, 1, LANES), f32)],
        compiler_params=_cp("arbitrary"),
        name="mlstm",
    )(qk, vvo, gates, cbk, ckv, cg, g_b.reshape(1, w))


def _out_kernel(*refs, n_act):
    acts = refs[:n_act]
    ws = refs[n_act:2 * n_act]
    x_ref, g_ref, gate_ref, o_ref = refs[2 * n_act:]
    mix = _dot(acts[0][...], ws[0][...])
    for a, wr in zip(acts[1:], ws[1:]):
        mix = mix + _dot(a[...], wr[...])
    y = mix * lax.rsqrt(jnp.mean(mix * mix, axis=-1, keepdims=True) + RMS_EPS) * g_ref[...]
    o_ref[...] = x_ref[...] + gate_ref[...] * y


def _out_proj(acts, ws, x, g, gate, tl=512):
    bsz, l, d = x.shape
    tl = min(tl, l)
    n_act = len(acts)
    in_specs = [pl.BlockSpec((None, tl, a.shape[2]), lambda b, i: (b, i, 0)) for a in acts]
    in_specs += [pl.BlockSpec(wm.shape, lambda b, i: (0, 0)) for wm in ws]
    in_specs += [pl.BlockSpec((None, tl, d), lambda b, i: (b, i, 0)),
                 pl.BlockSpec((1, d), lambda b, i: (0, 0)),
                 pl.BlockSpec((None, 1, d), _bidx(gate))]
    return pl.pallas_call(
        functools.partial(_out_kernel, n_act=n_act),
        grid=(bsz, l // tl),
        in_specs=in_specs,
        out_specs=pl.BlockSpec((None, tl, d), lambda b, i: (b, i, 0)),
        out_shape=jax.ShapeDtypeStruct((bsz, l, d), f32),
        compiler_params=_cp("parallel", "parallel"),
        name="out_proj",
    )(*acts, *ws, x, g.reshape(1, d), gate)


def _router_kernel(x_ref, g_ref, sh_ref, sc_ref, rw_ref, rb_ref, h_ref, rk_ref, gt_ref, cm_ref):
    hf = _norm_mod(x_ref[...], g_ref[...], sh_ref[...], sc_ref[...])
    tl = hf.shape[0]
    h_ref[...] = hf.astype(h_ref.dtype)
    per = N_EXPERTS // N_GROUPS
    logits = _dot3(rw_ref[...], hf, _NT)
    s3 = _sigmoid(logits).reshape(N_GROUPS, per, tl)
    b3 = s3 + rb_ref[...].reshape(N_GROUPS, per, 1)
    neg = -jnp.inf
    jdx = lax.broadcasted_iota(jnp.int32, b3.shape, 1)
    gdx = lax.broadcasted_iota(jnp.int32, b3.shape, 0)
    m1 = jnp.max(b3, axis=1, keepdims=True)
    f1 = jnp.min(jnp.where(b3 == m1, jdx, per), axis=1, keepdims=True)
    m2 = jnp.max(jnp.where(jdx == f1, neg, b3), axis=1, keepdims=True)
    grp = m1 + m2
    g1 = lax.broadcasted_iota(jnp.int32, grp.shape, 0)
    cnt = jnp.zeros(grp.shape, jnp.int32)
    for gp in range(N_GROUPS):
        rv = grp[gp:gp + 1]
        ahead = jnp.where(rv > grp, 1, jnp.where(rv == grp, jnp.where(g1 > gp, 1, 0), 0))
        cnt = cnt + ahead
    v = jnp.where(cnt < TOPK_GROUPS, b3, neg)
    eidx = gdx * per + jdx
    sel = jnp.zeros(b3.shape, f32)
    for _ in range(TOP_K):
        m = jnp.max(jnp.max(v, axis=1, keepdims=True), axis=0, keepdims=True)
        cand = jnp.where(v == m, eidx, N_EXPERTS)
        fi = jnp.min(jnp.min(cand, axis=1, keepdims=True), axis=0, keepdims=True)
        hit = eidx == fi
        sel = jnp.where(hit, 1.0, sel)
        v = jnp.where(hit, neg, v)
    ssel = sel * s3
    den = jnp.sum(jnp.sum(ssel, axis=1, keepdims=True), axis=0, keepdims=True)
    gt_ref[...] = ((ROUTED_SCALE * ssel) / den).reshape(N_EXPERTS, tl)
    sel2 = sel.reshape(N_EXPERTS, tl)
    r = lax.broadcasted_iota(jnp.int32, (MOE_SUB, MOE_SUB), 0)
    c = lax.broadcasted_iota(jnp.int32, (MOE_SUB, MOE_SUB), 1)
    before = jnp.where(r < c, 1.0, 0.0).astype(bf16)
    cmax = jnp.zeros((N_EXPERTS, 1), f32)
    for j in range(tl // MOE_SUB):
        sub = sel2[:, j * MOE_SUB:(j + 1) * MOE_SUB]
        rank = _dot(sub.astype(bf16), before)
        rk_ref[:, j * MOE_SUB:(j + 1) * MOE_SUB] = jnp.where(sub > 0.0, rank, -1.0)
        cmax = jnp.maximum(cmax, jnp.sum(sub, axis=1, keepdims=True))
    cm_ref[...] = jnp.broadcast_to(cmax, cm_ref.shape)


def _router(x, g, shift, scale, rw_t, rb, tl=512):
    bsz, l, d = x.shape
    tl = min(tl, l)
    nl = l // tl
    return pl.pallas_call(
        _router_kernel,
        grid=(bsz, nl),
        in_specs=[pl.BlockSpec((None, tl, d), lambda b, i: (b, i, 0)),
                  pl.BlockSpec((1, d), lambda b, i: (0, 0)),
                  pl.BlockSpec((None, 1, d), _bidx(shift)),
                  pl.BlockSpec((None, 1, d), _bidx(scale)),
                  pl.BlockSpec((N_EXPERTS, d), lambda b, i: (0, 0)),
                  pl.BlockSpec((N_EXPERTS, 1), lambda b, i: (0, 0))],
        out_specs=[pl.BlockSpec((None, tl, d), lambda b, i: (b, i, 0)),
                   pl.BlockSpec((N_EXPERTS, tl), lambda b, i: (0, b * nl + i)),
                   pl.BlockSpec((N_EXPERTS, tl), lambda b, i: (0, b * nl + i)),
                   pl.BlockSpec((None, N_EXPERTS, LANES), lambda b, i: (b * nl + i, 0, 0))],
        out_shape=[jax.ShapeDtypeStruct((bsz, l, d), bf16),
                   jax.ShapeDtypeStruct((N_EXPERTS, bsz * l), f32),
                   jax.ShapeDtypeStruct((N_EXPERTS, bsz * l), f32),
                   jax.ShapeDtypeStruct((bsz * nl, N_EXPERTS, LANES), f32)],
        compiler_params=_cp("parallel", "parallel"),
        name="router",
    )(x, g.reshape(1, d), shift, scale, rw_t, rb.reshape(N_EXPERTS, 1))


def _swiglu_act(hh):
    half = hh.shape[1] // 2
    return _silu(hh[:, :half]) * hh[:, half:]


def _moe_kernel(cnt_ref, ord_ref, h_ref, rk_ref, gt_ref, *refs):
    ng = MOE_GROUP
    gu_refs, dn_refs = refs[:ng], refs[ng:2 * ng]
    (sgu_ref, sdn_ref, x_ref, g_ref, gate_ref, o_ref,
     acc_ref, xg_ref, ys_ref, p_ref, gr_ref) = refs[2 * ng:]
    tile = pl.program_id(0)
    grp = pl.program_id(1)
    tm, d = acc_ref.shape
    ns = tm // MOE_SUB
    eids =[ord_ref[tile, grp * ng + el] for el in range(ng)]

    @pl.when(grp == 0)
    def _():
        act = _swiglu_act(_dot(h_ref[...], sgu_ref[...]))
        acc_ref[...] = _dot(act.astype(bf16), sdn_ref[...])

    def expert_ffn(el, win):
        hh = _dot(xg_ref[el, 0:ns * win, :], gu_refs[el][...])
        gr = gr_ref[el, 0:ns * win, :]
        act = _swiglu_act(hh) * jnp.concatenate([gr] * (hh.shape[1] // (2 * LANES)), axis=1)
        y = _dot(act.astype(bf16), dn_refs[el][...]).astype(bf16)
        for s in range(ns):
            ys_ref[s, el * win:(el + 1) * win, :] = y[s * win:(s + 1) * win]

    def one_pass(p, win):
        base = p * win
        riota = lax.broadcasted_iota(jnp.int32, (win, MOE_SUB), 0).astype(f32)
        for s in range(ns):
            cols = slice(s * MOE_SUB, (s + 1) * MOE_SUB)
            onehots = []
            for el in range(ng):
                row = pl.ds(eids[el], 1)
                hit = (rk_ref[row, cols] - base) == riota
                onehots.append(jnp.where(hit, 1.0, 0.0).astype(bf16))
                gsel = jnp.sum(jnp.where(hit, gt_ref[row, cols], 0.0), axis=1, keepdims=True)
                gr_ref[el, s * win:(s + 1) * win, :] = jnp.broadcast_to(gsel, (win, LANES))
            pm = jnp.concatenate(onehots, axis=0)
            p_ref[s, 0:ng * win, :] = pm
            gx = _dot(pm, h_ref[cols, :])
            for el in range(ng):
                xg_ref[el, s * win:(s + 1) * win, :] = gx[el * win:(el + 1) * win].astype(bf16)
        for el in range(ng):
            expert_ffn(el, win)
        for s in range(ns):
            acc_ref[s * MOE_SUB:(s + 1) * MOE_SUB, :] += _dot(
                p_ref[s, 0:ng * win, :], ys_ref[s, 0:ng * win, :], _TN)

    most = cnt_ref[tile, eids[0]]
    for el in range(1, ng):
        most = jnp.maximum(most, cnt_ref[tile, eids[el]])

    lo = 0
    for win in MOE_WINDOWS[:-1]:
        pl.when(jnp.logical_and(most > lo, most <= win))(functools.partial(one_pass, 0, win))
        lo = win
    big = MOE_WINDOWS[-1]

    def big_pass(p, carry):
        one_pass(p, big)
        return carry

    lax.fori_loop(0, jnp.where(most > lo, (most + big - 1) // big, 0), big_pass, 0)

    @pl.when(grp == pl.num_programs(1) - 1)
    def _():
        mo = acc_ref[...]
        y = mo * lax.rsqrt(jnp.mean(mo * mo, axis=-1, keepdims=True) + RMS_EPS) * g_ref[...]
        o_ref[...] = x_ref[...] + gate_ref[...] * y


def _moe(h2, rk, gt, cmax, gu, dn, sgu, sdn, x, g, gate, tm=1024):
    bsz, l, d = x.shape
    tm = min(tm, l)
    per_b = l // tm
    nt = bsz * per_b
    ne = gu.shape[0]
    ng = MOE_GROUP
    ns = tm // MOE_SUB
    wmax = MOE_WINDOWS[-1]
    counts = jnp.max(cmax[:, :, 0].reshape(nt, -1, ne), axis=1).astype(jnp.int32)
    order = jnp.argsort(-counts, axis=1).astype(jnp.int32)

    def expert_spec(arr, k):
        return pl.BlockSpec((None,) + arr.shape[1:], lambda t, e, cnt, order_ref: (order_ref[t, e * ng + k], 0, 0))

    grid_spec = pltpu.PrefetchScalarGridSpec(
        num_scalar_prefetch=2,
        grid=(nt, ne // ng),
        in_specs=[pl.BlockSpec((tm, d), lambda t, e, *_: (t, 0)),
                  pl.BlockSpec((ne, tm), lambda t, e, *_: (0, t)),
                  pl.BlockSpec((ne, tm), lambda t, e, *_: (0, t))]
                 + [expert_spec(gu, k) for k in range(ng)]
                 + [expert_spec(dn, k) for k in range(ng)]
                 + [pl.BlockSpec(sgu.shape, lambda t, e, *_: (0, 0)),
                    pl.BlockSpec(sdn.shape, lambda t, e, *_: (0, 0)),
                    pl.BlockSpec((tm, d), lambda t, e, *_: (t, 0)),
                    pl.BlockSpec((1, d), lambda t, e, *_: (0, 0)),
                    pl.BlockSpec((None, 1, d), lambda t, e, *_: (t // per_b, 0, 0))],
        out_specs=pl.BlockSpec((tm, d), lambda t, e, *_: (t, 0)),
        scratch_shapes=[pltpu.VMEM((tm, d), f32),
                        pltpu.VMEM((ng, ns * wmax, d), bf16),
                        pltpu.VMEM((ns, ng * wmax, d), bf16),
                        pltpu.VMEM((ns, ng * wmax, MOE_SUB), bf16),
                        pltpu.VMEM((ng, ns * wmax, LANES), f32)])
    out = pl.pallas_call(
        _moe_kernel,
        grid_spec=grid_spec,
        out_shape=jax.ShapeDtypeStruct((bsz * l, d), f32),
        compiler_params=_cp("parallel", "arbitrary"),
        name="moe",
    )(counts, order, h2.reshape(bsz * l, d), rk, gt, *([gu] * ng), *([dn] * ng), sgu, sdn,
      x.reshape(bsz * l, d), g.reshape(1, d), gate)
    return out.reshape(bsz, l, d)


def _filter_kernel(z_ref, w1_ref, b1_ref, w2_ref, b2_ref, w3_ref, win_ref, o_ref):
    hid = jnp.sin(FILTER_SIN_W * (_dot3(z_ref[...], w1_ref[...]) + b1_ref[...]))
    hid = jnp.sin(FILTER_SIN_W * (_dot3(hid, w2_ref[...]) + b2_ref[...]))
    o_ref[...] = _dot3(hid, w3_ref[...]) * win_ref[...]


def _filters(z, w1, b1, w2, b2, w3, window, tn=512):
    l, p = z.shape
    hdim = w1.shape[1]
    n = w3.shape[1]
    d = window.shape[1]
    nd = d // tn
    return pl.pallas_call(
        _filter_kernel,
        grid=(n // tn,),
        in_specs=[pl.BlockSpec((l, p), lambda j: (0, 0)),
                  pl.BlockSpec((p, hdim), lambda j: (0, 0)),
                  pl.BlockSpec((1, hdim), lambda j: (0, 0)),
                  pl.BlockSpec((hdim, hdim), lambda j: (0, 0)),
                  pl.BlockSpec((1, hdim), lambda j: (0, 0)),
                  pl.BlockSpec((hdim, tn), lambda j: (0, j)),
                  pl.BlockSpec((l, tn), lambda j: (0, j % nd))],
        out_specs=pl.BlockSpec((l, tn), lambda j: (0, j)),
        out_shape=jax.ShapeDtypeStruct((l, n), f32),
        compiler_params=_cp("arbitrary"),
        name="hyena_filter",
    )(z, w1, b1.reshape(1, hdim), w2, b2.reshape(1, hdim), w3, window)


def _dft_tables(l):
    n = 2 * l
    n1 = math.isqrt(n)
    assert n == n1 * n1 and n1 % 16 == 0
    na = l // n1
    ncp = -(-(n1 // 2 + 1) // 8) * 8
    a = np.arange(na)
    b = np.arange(n1)
    c = np.arange(ncp)
    th = 2.0 * np.pi * ((n1 * a[None, None, :] + b[:, None, None]) * c[None, :, None]) / n
    t1 = np.concatenate([np.cos(th), -np.sin(th)], axis=1)
    ph = 2.0 * np.pi * (b[:, None] * b[None, :]) / n1
    cs, sn = np.cos(ph), np.sin(ph)
    a3 = np.block([[cs, sn], [-sn, cs]])
    a3i = np.block([[cs, -sn], [sn, cs]])
    a2 = np.arange(na) + na // 2
    th2 = 2.0 * np.pi * ((n1 * a2[None, :, None] + b[:, None, None]) * c[None, None, :]) / n
    wc = np.where((c == 0) | (c == n1 // 2), 1.0, np.where(c < n1 // 2, 2.0, 0.0))[None, None, :]
    t2 = np.concatenate([wc * np.cos(th2), -wc * np.sin(th2)], axis=2)
    return [jnp.asarray(t, f32).astype(bf16) for t in (t1, a3, a3i, t2)]


def _fft_dims(t1):
    n1, ncp2, na = t1.shape
    ncp = ncp2 // 2
    return n1, ncp, na, 2 * n1 + FFT_PAD, 2 * ncp + FFT_PAD, n1 + FFT_PAD


def _ld(ref, rows):
    return jnp.concatenate([ref[j, rows, :] for j in range(ref.shape[0])], axis=1)


def _st(ref, rows, val):
    for j in range(ref.shape[0]):
        ref[j, rows, :] = val[:, j * LANES:(j + 1) * LANES]


def _dft_forward(uf_ref, t1_ref, zs_ref):
    n1, ncp, na, sb, _, su = _fft_dims(t1_ref)
    for b in range(n1):
        ub = _ld(uf_ref, pl.ds(b, na, stride=su)).astype(bf16)
        zb = _dot(t1_ref[b], ub)
        _st(zs_ref, pl.ds(b, ncp, stride=sb), zb[:ncp])
        _st(zs_ref, pl.ds(n1 + b, ncp, stride=sb), zb[ncp:])


def _spectrum_kernel(f_ref, t1_ref, a3_ref, o_ref, uf_ref, zs_ref, *, scale):
    n1, ncp, na, sb, _, su = _fft_dims(t1_ref)
    for a in range(na):
        _st(uf_ref, pl.ds(a * su, n1), f_ref[pl.ds(a * n1, n1), :])
    _dft_forward(uf_ref, t1_ref, zs_ref)
    a3 = a3_ref[...]
    for c in range(ncp):
        zc = _ld(zs_ref, pl.ds(c * sb, 2 * n1)).astype(bf16)
        o_ref[c] = (_dot(a3, zc) * scale).astype(o_ref.dtype)


def _spectrum(filt, tabs, dt=256):
    l, n = filt.shape
    t1, a3, _, _ = tabs
    n1, ncp, na, sb, _, su = _fft_dims(t1)
    nj = dt // LANES
    return pl.pallas_call(
        functools.partial(_spectrum_kernel, scale=1.0 / (2 * l)),
        grid=(n // dt,),
        in_specs=[pl.BlockSpec((l, dt), lambda j: (0, j)),
                  pl.BlockSpec(t1.shape, lambda j: (0, 0, 0)),
                  pl.BlockSpec(a3.shape, lambda j: (0, 0))],
        out_specs=pl.BlockSpec((ncp, 2 * n1, dt), lambda j: (0, 0, j)),
        out_shape=jax.ShapeDtypeStruct((ncp, 2 * n1, n), bf16),
        scratch_shapes=[pltpu.VMEM((nj, na * su, LANES), f32),
                        pltpu.VMEM((nj, ncp * sb, LANES), f32)],
        compiler_params=_cp("arbitrary"),
        name="hyena_spectrum",
    )(filt, t1, a3)


def _fftconv_kernel(u_ref, xg_ref, kf_ref, fb_ref, t1_ref, a3_ref, a3i_ref, t2_ref, o_ref,
                    uf_ref, zs_ref, qs_ref, y_ref):
    n1, ncp, na, sb, sq, su = _fft_dims(t1_ref)
    for a in range(na):
        _st(uf_ref, pl.ds(a * su, n1), u_ref[pl.ds(a * n1, n1), :].astype(f32))
    _dft_forward(uf_ref, t1_ref, zs_ref)
    a3 = a3_ref[...]
    a3i = a3i_ref[...]
    for c in range(ncp):
        zc = _ld(zs_ref, pl.ds(c * sb, 2 * n1)).astype(bf16)
        xc = _dot(a3, zc)
        kc = kf_ref[c].astype(f32)
        xr, xi = xc[:n1], xc[n1:]
        kr, ki = kc[:n1], kc[n1:]
        pc = jnp.concatenate([xr * kr - xi * ki, xr * ki + xi * kr], axis=0).astype(bf16)
        qc = _dot(a3i, pc)
        _st(qs_ref, pl.ds(c, n1, stride=sq), qc[:n1])
        _st(qs_ref, pl.ds(ncp + c, n1, stride=sq), qc[n1:])
    for b in range(n1):
        qb = _ld(qs_ref, pl.ds(b * sq, 2 * ncp)).astype(bf16)
        _st(y_ref, pl.ds(b, na, stride=su), _dot(t2_ref[b], qb))
    fb = fb_ref[...]
    for a in range(na):
        rows = pl.ds(a * n1, n1)
        uv = _ld(uf_ref, pl.ds(a * su, n1))
        yv = _ld(y_ref, pl.ds(a * su, n1))
        o_ref[rows, :] = (xg_ref[rows, :].astype(f32) * (yv + uv * fb)).astype(o_ref.dtype)


def _fftconv(u, u_col, xg, xg_col, kf, kf_col, fbias, tabs, d, dt=256):
    bsz, l, _ = u.shape
    t1, a3, a3i, t2 = tabs
    n1, ncp, na, sb, sq, su = _fft_dims(t1)
    nd = d // dt
    nj = dt // LANES
    uo, go, ko = u_col // dt, xg_col // dt, kf_col // dt
    return pl.pallas_call(
        _fftconv_kernel,
        grid=(nd, bsz),
        in_specs=[pl.BlockSpec((None, l, dt), lambda j, b: (b, 0, j + uo)),
                  pl.BlockSpec((None, l, dt), lambda j, b: (b, 0, j + go)),
                  pl.BlockSpec((ncp, 2 * n1, dt), lambda j, b: (0, 0, j + ko)),
                  pl.BlockSpec((1, dt), lambda j, b: (0, j)),
                  pl.BlockSpec(t1.shape, lambda j, b: (0, 0, 0)),
                  pl.BlockSpec(a3.shape, lambda j, b: (0, 0)),
                  pl.BlockSpec(a3i.shape, lambda j, b: (0, 0)),
                  pl.BlockSpec(t2.shape, lambda j, b: (0, 0, 0))],
        out_specs=pl.BlockSpec((None, l, dt), lambda j, b: (b, 0, j)),
        out_shape=jax.ShapeDtypeStruct((bsz, l, d), bf16),
        scratch_shapes=[pltpu.VMEM((nj, na * su, LANES), f32),
                        pltpu.VMEM((nj, ncp * sb, LANES), f32),
                        pltpu.VMEM((nj, n1 * sq, LANES), f32),
                        pltpu.VMEM((nj, na * su, LANES), f32)],
        compiler_params=_cp("parallel", "arbitrary"),
        name="hyena_fftconv",
    )(u, xg, kf, fbias.reshape(1, d), t1, a3, a3i, t2)


def _rope_tables(l):
    rows = l // GRID_W
    row = jnp.repeat(jnp.arange(rows), GRID_W)
    col = jnp.tile(jnp.arange(GRID_W), rows)
    inv = ROPE_BASE ** (-jnp.arange(ROPE_AXIS_PAIRS, dtype=f32) / ROPE_AXIS_PAIRS)
    ang = jnp.stack([row, col], axis=-1).astype(f32)[..., None] * inv
    ang = jnp.broadcast_to(ang[:, :, None, :], (l, 2, 2, ROPE_AXIS_PAIRS)).reshape(l, A_DQK)
    reps = A_QW // A_DQK
    return jnp.tile(jnp.cos(ang), (1, reps)), jnp.tile(jnp.sin(ang), (1, reps))


def _rotate_cols(w):
    j = np.arange(w.shape[1])
    lo = (j % (2 * ROPE_AXIS_PAIRS)) < ROPE_AXIS_PAIRS
    perm = np.where(lo, j + ROPE_AXIS_PAIRS, j - ROPE_AXIS_PAIRS)
    sign = np.where(lo, -1.0, 1.0).astype(np.float32)
    return w[:, perm] * sign


def _gate_cols(w_g, b_g):
    idx_i = np.array([d * 2 * B_HEADS + hd for d in range(2) for hd in range(B_HEADS)])
    idx_f = idx_i + B_HEADS
    pad = LANES - _NCHAIN
    k = w_g.shape[0]
    w = jnp.concatenate([w_g[:, idx_i], jnp.zeros((k, pad), f32),
                         w_g[:, idx_f], jnp.zeros((k, pad), f32)], axis=1)
    b = jnp.concatenate([b_g[idx_i], jnp.zeros((pad,), f32), b_g[idx_f], jnp.zeros((pad,), f32)])
    return w, b


def _hyena_consts(l, d):
    j = jnp.arange(l, dtype=f32)
    bands = (POS_EMB_DIM - 1) // 2
    freqs = jnp.linspace(1e-4, bands - 1, bands, dtype=f32)
    ang = (2.0 * math.pi / l) * j[:, None] * freqs[None, :]
    z = jnp.concatenate([(j / (l - 1))[:, None], jnp.cos(ang), -jnp.sin(ang)], axis=-1)
    dist = jnp.abs(j - l // 2) / (l // 2)
    max_decay = math.log(DECAY_TARGET) / DECAY_FAST_PCT
    min_decay = math.log(DECAY_TARGET) / DECAY_SLOW_PCT
    deltas = jnp.abs(jnp.linspace(min_decay, max_decay, d, dtype=f32))
    window = jnp.exp(-dist[:, None] * deltas[None, :])
    return z, window


def _ab_layer(x, ctx, mod_vecs, mod_ctx, norm_g, w_in, conv_w, conv_b, gate_b, lam_vecs,
              g_a, g_b, w_out, lam_init):
    sh_m, sc_m, g_m = mod_vecs
    bsz, s, d = x.shape
    h = _norm(x, norm_g[0], sh_m, sc_m)
    hc = _norm(ctx, norm_g[0], mod_ctx[0], mod_ctx[1])
    w = B_WIDTH
    o = 0
    cols = {}
    for name, width in (("aq", A_QW), ("bq", w), ("bo", w), ("ak", A_QW), ("av", A_VW),
                        ("bk", w), ("bv", w), ("g", 4 * B_HEADS)):
        cols[name] = w_in[:, o:o + width]
        o += width
    cos, sin = _rope_tables(s)
    cat = lambda *ws: jnp.concatenate(ws, axis=1).astype(bf16)
    q = _mm(h, cat(cols["aq"], _rotate_cols(cols["aq"])), rope=(cos, sin, A_DQK ** -0.5))
    k = _mm(h, cat(cols["ak"], _rotate_cols(cols["ak"])), rope=(cos, sin, 1.0))
    qk = _mm(h, cat(cols["bq"], cols["bk"]), conv=(conv_w, conv_b, True))
    vvo = _mm(h, cat(cols["av"], cols["bv"], cols["bo"]))
    wg, bg = _gate_cols(cols["g"], gate_b)
    gates = _mm(h, wg.astype(bf16), out_dtype=f32, bias=bg, tn=2 * LANES)
    ckv = _mm(hc, cat(cols["ak"], cols["av"], cols["bv"]))
    cbk = _mm(hc, cols["bk"].astype(bf16), conv=(conv_w[:, w:], conv_b[w:], True))
    cg = _mm(hc, wg.astype(bf16), out_dtype=f32, bias=bg, tn=2 * LANES)
    out_a = _attn(lam_vecs, q, k, vvo, ckv, g_a, lam_init)
    out_b = _mlstm(qk, vvo, gates, cbk, ckv, cg, g_b)
    wo = w_out.astype(bf16)
    return _out_proj([out_a, out_b], [wo[:A_VW], wo[A_VW:]], x, norm_g[1], g_m)


def _hyena_layer(x, mod_vecs, norm_g, w_in, conv_w, conv_b, fw1, fb1, fw2, fb2, fw3, fbias, w_out):
    sh_m, sc_m, g_m = mod_vecs
    bsz, l, d = x.shape
    h = _norm(x, norm_g[0], sh_m, sc_m)
    u = _mm(h, w_in.astype(bf16), conv=(conv_w, conv_b, False))
    z, window = _hyena_consts(l, d)
    pz, ph = LANES - z.shape[1], LANES - fw1.shape[1]
    filt = _filters(jnp.pad(z, ((0, 0), (0, pz))), jnp.pad(fw1, ((0, pz), (0, ph))),
                    jnp.pad(fb1, (0, ph)), jnp.pad(fw2, ((0, ph), (0, ph))), jnp.pad(fb2, (0, ph)),
                    jnp.pad(fw3, ((0, ph), (0, 0))), window)
    tabs = _dft_tables(l)
    kf = _spectrum(filt, tabs)
    zz = _fftconv(u, 0, u, d, kf, 0, fbias[0], tabs, d)
    y = _fftconv(zz, 0, u, 2 * d, kf, d, fbias[1], tabs, d)
    return _out_proj([y], [w_out.astype(bf16)], x, norm_g[1], g_m)


def kernel(x, c, ctx, c_ctx, w_mod, b_mod, norm_g, w_in_ab, conv_ab_w, conv_ab_b, gate_b_ab, diff_lambda, head_g_a, head_g_b, w_out_ab, w_in_hy, conv_hy_w, conv_hy_b, filt_w1, filt_b1, filt_w2, filt_b2, filt_w3, filt_bias, w_out_hy, router_w, router_b, exp_gu, exp_down, sh_gu, sh_down):
    bsz, s, d = x.shape
    depth = w_mod.shape[0]
    rows = -(-(bsz + 1) // 8) * 8
    cc = jnp.concatenate([c, c_ctx[None, :], jnp.zeros((rows - bsz - 1, d), f32)], axis=0)
    for l in range(depth):
        mod = _mod(cc, w_mod[l], b_mod[l])
        vec = lambda i: mod[:bsz, i * d:(i + 1) * d].reshape(bsz, 1, d)
        sh_m, sc_m, g_m, sh_f, sc_f, g_f = [vec(i) for i in range(6)]
        if l % 2 == 0:
            e = l // 2
            lam_init = 0.8 - 0.6 * math.exp(-0.3 * l)
            mod_ctx = (mod[bsz:bsz + 1, 0:d].reshape(1, 1, d), mod[bsz:bsz + 1, d:2 * d].reshape(1, 1, d))
            x = _ab_layer(x, ctx, (sh_m, sc_m, g_m), mod_ctx, norm_g[l], w_in_ab[e], conv_ab_w[e],
                          conv_ab_b[e], gate_b_ab[e], diff_lambda[e], head_g_a[e], head_g_b[e],
                          w_out_ab[e], lam_init)
        else:
            o = l // 2
            x = _hyena_layer(x, (sh_m, sc_m, g_m), norm_g[l], w_in_hy[o], conv_hy_w[o], conv_hy_b[o],
                             filt_w1[o], filt_b1[o], filt_w2[o], filt_b2[o], filt_w3[o], filt_bias[o],
                             w_out_hy[o])
        h2, rk, gt, cmax = _router(x, norm_g[l, 2], sh_f, sc_f, router_w[l].T, router_b[l])
        x = _moe(h2, rk, gt, cmax, exp_gu[l].astype(bf16), exp_down[l].astype(bf16), sh_gu[l].astype(bf16),
                 sh_down[l].astype(bf16), x, norm_g[l, 3], g_f)
    return x
```

```python
import functools
import math

import numpy as np
import jax
import jax.numpy as jnp
from jax import lax
from jax.experimental import pallas as pl
from jax.experimental.pallas import tpu as pltpu

f32 = jnp.float32
bf16 = jnp.bfloat16

RMS_EPS = 1e-6
A_HEADS = 4
A_DQK = 64
A_DV = 128
B_HEADS = 4
B_DH = 128
B_WIDTH = B_HEADS * B_DH
A_QW = A_HEADS * 2 * A_DQK
A_VW = A_HEADS * A_DV
GRID_W = 64
ROPE_BASE = 10000.0
ROPE_AXIS_PAIRS = A_DQK // 4
N_EXPERTS = 64
TOP_K = 8
N_GROUPS = 8
TOPK_GROUPS = 4
D_EXPERT = 256
ROUTED_SCALE = 2.5
HY_ORDER = 2
POS_EMB_DIM = 33
FILTER_SIN_W = 1.0
DECAY_FAST_PCT = 0.3
DECAY_SLOW_PCT = 1.5
DECAY_TARGET = 1e-2

LANES = 128
VMEM_LIMIT = 56 * 1024 * 1024
MLSTM_CHUNK = 256
FFT_PAD = 8
MOE_SUB = 256
MOE_WINDOWS = (16, 32, 48, 64, 96, 128)
MOE_GROUP = 4


def _cp(*sem):
    return pltpu.CompilerParams(dimension_semantics=sem, vmem_limit_bytes=VMEM_LIMIT)


def _split_bf16(a):
    hi = a.astype(bf16)
    lo = (a - hi.astype(f32)).astype(bf16)
    return hi, lo


def _dot(a, b, dims=(((1,), (0,)), ((), ()))):
    return lax.dot_general(a, b, dims, preferred_element_type=f32)


_NT = (((1,), (1,)), ((), ()))
_TN = (((0,), (0,)), ((), ()))


def _dot3(a, b, dims=(((1,), (0,)), ((), ()))):
    ah, al = _split_bf16(a)
    bh, bl = _split_bf16(b)
    return _dot(ah, bh, dims) + (_dot(ah, bl, dims) + _dot(al, bh, dims))


def _silu(v):
    return v / (1.0 + jnp.exp(-v))


def _sigmoid(v):
    return 1.0 / (1.0 + jnp.exp(-v))


def _log_sigmoid(v):
    return jnp.minimum(v, 0.0) - jnp.log(1.0 + jnp.exp(-jnp.abs(v)))


def _mod_kernel(c_ref, w_ref, b_ref, o_ref):
    o_ref[...] = _dot3(_silu(c_ref[...]), w_ref[...]) + b_ref[...]


def _mod(cc, w, b):
    rows, d = cc.shape
    n = w.shape[1]
    tn = d
    return pl.pallas_call(
        _mod_kernel,
        grid=(n // tn,),
        in_specs=[pl.BlockSpec((rows, d), lambda j: (0, 0)),
                  pl.BlockSpec((d, tn), lambda j: (0, j)),
                  pl.BlockSpec((1, tn), lambda j: (0, j))],
        out_specs=pl.BlockSpec((rows, tn), lambda j: (0, j)),
        out_shape=jax.ShapeDtypeStruct((rows, n), f32),
        compiler_params=_cp("arbitrary"),
        name="mod",
    )(cc, w, b.reshape(1, n))


def _norm_mod(xv, g, shift, scale):
    y = xv * lax.rsqrt(jnp.mean(xv * xv, axis=-1, keepdims=True) + RMS_EPS)
    return (y * g) * (1.0 + scale) + shift


def _norm_kernel(x_ref, g_ref, sh_ref, sc_ref, o_ref):
    o_ref[...] = _norm_mod(x_ref[...], g_ref[...], sh_ref[...], sc_ref[...]).astype(o_ref.dtype)


def _bidx(arr):
    if arr.shape[0] == 1:
        return lambda b, *_: (0, 0, 0)
    return lambda b, *_: (b, 0, 0)


def _norm(x, g, shift, scale, tl=512):
    bsz, l, d = x.shape
    tl = min(tl, l)
    return pl.pallas_call(
        _norm_kernel,
        grid=(bsz, l // tl),
        in_specs=[pl.BlockSpec((None, tl, d), lambda b, i: (b, i, 0)),
                  pl.BlockSpec((1, d), lambda b, i: (0, 0)),
                  pl.BlockSpec((None, 1, d), _bidx(shift)),
                  pl.BlockSpec((None, 1, d), _bidx(scale))],
        out_specs=pl.BlockSpec((None, tl, d), lambda b, i: (b, i, 0)),
        out_shape=jax.ShapeDtypeStruct((bsz, l, d), bf16),
        compiler_params=_cp("parallel", "parallel"),
        name="norm",
    )(x, g.reshape(1, d), shift, scale)


def _mm_plain_kernel(h_ref, w_ref, b_ref, o_ref):
    o_ref[...] = (_dot(h_ref[...], w_ref[...]) + b_ref[...]).astype(o_ref.dtype)


def _mm_rope_kernel(h_ref, w_ref, cos_ref, sin_ref, o_ref, *, scale):
    p = _dot(h_ref[...], w_ref[...])
    n = o_ref.shape[-1]
    o_ref[...] = ((p[:, :n] * cos_ref[...] + p[:, n:] * sin_ref[...]) * scale).astype(o_ref.dtype)


def _mm_conv_kernel(h_ref, w_ref, cw_ref, cb_ref, o_ref, scr_ref, *, act, rc):
    l = h_ref.shape[0]
    w = w_ref[...]
    w0, w1, w2, cb = cw_ref[0:1, :], cw_ref[1:2, :], cw_ref[2:3, :], cb_ref[...]
    halo = 16
    zrow = jnp.zeros((8, o_ref.shape[1]), f32)
    for c in range(l // rc):
        lo, hi = max(c * rc - halo, 0), min((c + 1) * rc + halo, l)
        n = hi - lo
        scr = scr_ref.at[c % 2]
        scr[8:8 + n, :] = _dot(h_ref[lo:hi, :], w)
        if lo == 0:
            scr[0:8, :] = zrow
        if hi == l:
            scr[8 + n:16 + n, :] = zrow
        off = 8 + c * rc - lo
        y = (scr[off - 1:off - 1 + rc, :] * w0 + scr[off:off + rc, :] * w1
             + scr[off + 1:off + 1 + rc, :] * w2 + cb)
        o_ref[c * rc:(c + 1) * rc, :] = (_silu(y) if act else y).astype(o_ref.dtype)


def _mm(h, w, *, out_dtype=bf16, bias=None, rope=None, conv=None, tl=512, tn=512):
    bsz, l, k = h.shape
    n = w.shape[1]
    if rope is not None:
        cos, sin, scale = rope
        n_out = n // 2
        tl = min(tl, l)
        return pl.pallas_call(
            functools.partial(_mm_rope_kernel, scale=scale),
            grid=(bsz, l // tl),
            in_specs=[pl.BlockSpec((None, tl, k), lambda b, i: (b, i, 0)),
                      pl.BlockSpec((k, n), lambda b, i: (0, 0)),
                      pl.BlockSpec((tl, n_out), lambda b, i: (i, 0)),
                      pl.BlockSpec((tl, n_out), lambda b, i: (i, 0))],
            out_specs=pl.BlockSpec((None, tl, n_out), lambda b, i: (b, i, 0)),
            out_shape=jax.ShapeDtypeStruct((bsz, l, n_out), out_dtype),
            compiler_params=_cp("parallel", "parallel"),
            name="mm_rope",
        )(h, w, cos, sin)
    tn = min(tn, n)
    if conv is not None:
        cw, cb, act = conv
        rc = min(512, l)
        return pl.pallas_call(
            functools.partial(_mm_conv_kernel, act=act, rc=rc),
            grid=(bsz, n // tn),
            in_specs=[pl.BlockSpec((None, l, k), lambda b, j: (b, 0, 0)),
                      pl.BlockSpec((k, tn), lambda b, j: (0, j)),
                      pl.BlockSpec((3, tn), lambda b, j: (0, j)),
                      pl.BlockSpec((1, tn), lambda b, j: (0, j))],
            out_specs=pl.BlockSpec((None, l, tn), lambda b, j: (b, 0, j)),
            out_shape=jax.ShapeDtypeStruct((bsz, l, n), out_dtype),
            scratch_shapes=[pltpu.VMEM((2, rc + 48, tn), f32)],
            compiler_params=_cp("parallel", "arbitrary"),
            name="mm_conv",
        )(h, w, cw, cb.reshape(1, n))
    if bias is None:
        bias = jnp.zeros((n,), f32)
    if k * n * w.dtype.itemsize <= 4 * 1024 * 1024:
        tn = n
    tl = min(tl, l)
    return pl.pallas_call(
        _mm_plain_kernel,
        grid=(bsz, l // tl, n // tn),
        in_specs=[pl.BlockSpec((None, tl, k), lambda b, i, j: (b, i, 0)),
                  pl.BlockSpec((k, tn), lambda b, i, j: (0, j)),
                  pl.BlockSpec((1, tn), lambda b, i, j: (0, j))],
        out_specs=pl.BlockSpec((None, tl, tn), lambda b, i, j: (b, i, j)),
        out_shape=jax.ShapeDtypeStruct((bsz, l, n), out_dtype),
        compiler_params=_cp("parallel", "parallel", "arbitrary"),
        name="mm_plain",
    )(h, w, bias.reshape(1, n))


def _attn_kernel(lv_ref, q_ref, kc_ref, k_ref, vc_ref, v_ref, g_ref, o_ref, *, lam_init):
    tq = q_ref.shape[0]
    lv = lv_ref[...]
    lam = (jnp.exp(jnp.sum(lv[0:1] * lv[1:2], axis=1, keepdims=True))
           - jnp.exp(jnp.sum(lv[2:3] * lv[3:4], axis=1, keepdims=True)) + lam_init)
    first = lax.broadcasted_iota(jnp.int32, (tq, A_DV), 1) < A_DQK
    one0 = jnp.where(lax.broadcasted_iota(jnp.int32, (1, A_DV), 1) == 0, 1.0, 0.0).astype(bf16)
    ones_c = jnp.broadcast_to(one0, (kc_ref.shape[0], A_DV))
    ones_l = jnp.broadcast_to(one0, (k_ref.shape[0], A_DV))
    for hd in range(A_HEADS):
        cs = slice(hd * A_DV, (hd + 1) * A_DV)
        qh = q_ref[:, cs]
        zero = jnp.zeros_like(qh)
        q2 = jnp.concatenate([jnp.where(first, qh, zero), jnp.where(first, zero, qh)], axis=0)
        s_c = _dot(q2, kc_ref[:, cs], _NT)
        s_l = _dot(q2, k_ref[:, cs], _NT)
        m = jnp.maximum(jnp.max(s_c, axis=1, keepdims=True), jnp.max(s_l, axis=1, keepdims=True))
        p_c = jnp.exp((s_c - m).astype(bf16))
        p_l = jnp.exp((s_l - m).astype(bf16))
        oa = (_dot(p_c, jnp.concatenate([vc_ref[:, cs], ones_c], axis=1))
              + _dot(p_l, jnp.concatenate([v_ref[:, cs], ones_l], axis=1)))
        on = oa[:, :A_DV] * (1.0 / oa[:, A_DV:A_DV + 1])
        o = on[:tq] - lam * on[tq:]
        o = o * lax.rsqrt(jnp.mean(o * o, axis=1, keepdims=True) + RMS_EPS)
        o_ref[:, cs] = (o * g_ref[:, cs] * (1.0 - lam_init)).astype(o_ref.dtype)


def _attn(lv, q, k, vvo, ckv, g_a, lam_init, tq=256):
    bsz, s, _ = q.shape
    lc = ckv.shape[1]
    tq = min(tq, s)
    w = A_QW
    return pl.pallas_call(
        functools.partial(_attn_kernel, lam_init=lam_init),
        grid=(bsz, s // tq),
        in_specs=[pl.BlockSpec(lv.shape, lambda b, i: (0, 0)),
                  pl.BlockSpec((None, tq, w), lambda b, i: (b, i, 0)),
                  pl.BlockSpec((None, lc, w), lambda b, i: (b, 0, 0)),
                  pl.BlockSpec((None, s, w), lambda b, i: (b, 0, 0)),
                  pl.BlockSpec((None, lc, w), lambda b, i: (b, 0, 1)),
                  pl.BlockSpec((None, s, w), lambda b, i: (b, 0, 0)),
                  pl.BlockSpec((1, w), lambda b, i: (0, 0))],
        out_specs=pl.BlockSpec((None, tq, w), lambda b, i: (b, i, 0)),
        out_shape=jax.ShapeDtypeStruct((bsz, s, w), bf16),
        compiler_params=_cp("parallel", "arbitrary"),
        name="diff_attn",
    )(lv, q, ckv, k, ckv, vvo, g_a.reshape(1, w))


_LN_QSCALE = math.log(B_DH ** -0.5)
_NCHAIN = 2 * B_HEADS


def _chunk_gate_sums(gi, gf, tri):
    lf = _log_sigmoid(gf)
    hi, lo = _split_bf16(lf)
    cum = _dot(tri, hi) + _dot(tri, lo)
    t = gf.shape[0]
    tot = cum[t - 1:t, :]
    rcum = tot - cum + lf
    fwd = lax.broadcasted_iota(jnp.int32, gf.shape, 1) < B_HEADS
    bd = jnp.where(fwd, cum, rcum)
    return bd, tot, (bd - gi).T


def _lower_tri(t):
    r = lax.broadcasted_iota(jnp.int32, (t, t), 0)
    c = lax.broadcasted_iota(jnp.int32, (t, t), 1)
    return r, c


def _ones_block(t):
    one0 = jnp.where(lax.broadcasted_iota(jnp.int32, (1, B_DH), 1) == 0, 1.0, 0.0).astype(bf16)
    return jnp.broadcast_to(one0, (t, B_DH))


def _absorb(c_ref, m_ref, ch, x_row, tot_c, kb, vaug):
    m_prev = m_ref[ch][:, 0:1]
    g = tot_c - x_row
    m_new = jnp.maximum(tot_c + m_prev, jnp.max(g, axis=1, keepdims=True))
    wgt = jnp.exp(g - m_new)
    decay = jnp.exp(tot_c + m_prev - m_new)
    kw_t = kb.astype(f32).T * wgt
    c_ref[ch] = decay * c_ref[ch] + _dot(kw_t.astype(bf16), vaug)
    m_ref[ch] = jnp.broadcast_to(m_new, m_ref.shape[1:])


def _mlstm_kernel(qk_ref, vvo_ref, g_ref, ck_ref, ckv_ref, cg_ref, gb_ref, o_ref,
                  hf_ref, hb_ref, c_ref, m_ref, *, tc):
    s = o_ref.shape[0]
    lc = ck_ref.shape[0]
    nc = s // tc
    w = B_WIDTH
    dh = B_DH

    c_ref[...] = jnp.zeros_like(c_ref)
    m_ref[...] = jnp.zeros_like(m_ref)

    r, cidx = _lower_tri(lc)
    tri_c = jnp.where(cidx <= r, 1.0, 0.0).astype(bf16)
    cg = cg_ref[...]
    _, tot, xt = _chunk_gate_sums(cg[:, :LANES], cg[:, LANES:], tri_c)
    ones_c = _ones_block(lc)
    for ch in range(_NCHAIN):
        hs = slice((ch % B_HEADS) * dh, (ch % B_HEADS + 1) * dh)
        vs = slice(2 * w + (ch % B_HEADS) * dh, 2 * w + (ch % B_HEADS + 1) * dh)
        _absorb(c_ref, m_ref, ch, xt[ch:ch + 1, :], tot[:, ch:ch + 1], ck_ref[:, hs],
                jnp.concatenate([ckv_ref[:, vs], ones_c], axis=1))
    ones_t = _ones_block(tc)

    r, cidx = _lower_tri(tc)
    tri = jnp.where(cidx <= r, 1.0, 0.0).astype(bf16)
    causal = cidx <= r
    anti = cidx >= r

    def step(i, carry):
        for d in range(2):
            row0 = pl.multiple_of((i if d == 0 else nc - 1 - i) * tc, tc)
            rows = pl.ds(row0, tc)
            gch = g_ref[rows, :]
            gi = gch[:, :LANES]
            bd, tot, xt = _chunk_gate_sums(gi, gch[:, LANES:], tri)
            mask = causal if d == 0 else anti
            dst = hf_ref if d == 0 else hb_ref
            for hd in range(B_HEADS):
                ch = d * B_HEADS + hd
                hs = slice(hd * dh, (hd + 1) * dh)
                qb = qk_ref[rows, hs]
                kb = qk_ref[rows, slice(w + hd * dh, w + (hd + 1) * dh)]
                vaug = jnp.concatenate([vvo_ref[rows, slice(w + hd * dh, w + (hd + 1) * dh)], ones_t],
                                       axis=1)
                bcol = bd[:, ch:ch + 1]
                x_row = xt[ch:ch + 1, :]
                dmat = jnp.where(mask, bcol - x_row, -jnp.inf)
                m_prev = m_ref[ch][:, 0:1]
                inter = bcol + m_prev
                m_t = jnp.maximum(inter, jnp.max(dmat, axis=1, keepdims=True))
                e = jnp.exp(dmat - m_t + _LN_QSCALE)
                smat = _dot(qb, kb, _NT) * e
                sc = jnp.exp(inter - m_t + _LN_QSCALE)
                both = sc * _dot(qb, c_ref[ch].astype(bf16)) + _dot(smat.astype(bf16), vaug)
                den = both[:, dh:dh + 1]
                dst[rows, hs] = both[:, :dh] * (1.0 / jnp.maximum(jnp.abs(den), jnp.exp(-m_t)))
                _absorb(c_ref, m_ref, ch, x_row, tot[:, ch:ch + 1], kb, vaug)
        return carry

    lax.fori_loop(0, nc, step, 0)

    for hd in range(B_HEADS):
        hs = slice(hd * dh, (hd + 1) * dh)
        hsum = hf_ref[:, hs] + hb_ref[:, hs]
        hn = hsum * lax.rsqrt(jnp.mean(hsum * hsum, axis=1, keepdims=True) + RMS_EPS)
        og = _sigmoid(vvo_ref[:, slice(2 * w + hd * dh, 2 * w + (hd + 1) * dh)].astype(f32))
        o_ref[:, hs] = (hn * gb_ref[:, hs] * og).astype(o_ref.dtype)


def _mlstm(qk, vvo, gates, cbk, ckv, cg, g_b):
    bsz, s, _ = qk.shape
    lc = cbk.shape[1]
    w = B_WIDTH
    tc = min(MLSTM_CHUNK, s)
    return pl.pallas_call(
        functools.partial(_mlstm_kernel, tc=tc),
        grid=(bsz,),
        in_specs=[pl.BlockSpec((None, s, 2 * w), lambda b: (b, 0, 0)),
                  pl.BlockSpec((None, s, 3 * w), lambda b: (b, 0, 0)),
                  pl.BlockSpec((None, s, 2 * LANES), lambda b: (b, 0, 0)),
                  pl.BlockSpec((None, lc, w), lambda b: (b, 0, 0)),
                  pl.BlockSpec((None, lc, 3 * w), lambda b: (b, 0, 0)),
                  pl.BlockSpec((None, lc, 2 * LANES), lambda b: (b, 0, 0)),
                  pl.BlockSpec((1, w), lambda b: (0, 0))],
        out_specs=pl.BlockSpec((None, s, w), lambda b: (b, 0, 0)),
        out_shape=jax.ShapeDtypeStruct((bsz, s, w), bf16),
        scratch_shapes=[pltpu.VMEM((s, w), f32), pltpu.VMEM((s, w), f32),
                        pltpu.VMEM((_NCHAIN, B_DH, 2 * B_DH), f32),
                        pltpu.VMEM((_NCHAIN, 1, LANES), f32)],
        compiler_params=_cp("arbitrary"),
        name="mlstm",
    )(qk, vvo, gates, cbk, ckv, cg, g_b.reshape(1, w))


def _out_kernel(*refs, n_act):
    acts = refs[:n_act]
    ws = refs[n_act:2 * n_act]
    (x_ref, g_ref, gate_ref, g2_ref, sh_ref, sc_ref, rw_ref, rb_ref,
     o_ref, h_ref, rk_ref, gt_ref, cm_ref) = refs[2 * n_act:]
    mix = _dot(acts[0][...], ws[0][...])
    for a, wr in zip(acts[1:], ws[1:]):
        mix = mix + _dot(a[...], wr[...])
    y = mix * lax.rsqrt(jnp.mean(mix * mix, axis=-1, keepdims=True) + RMS_EPS) * g_ref[...]
    xn = x_ref[...] + gate_ref[...] * y
    o_ref[...] = xn
    _route(xn, g2_ref[...], sh_ref[...], sc_ref[...], rw_ref, rb_ref, h_ref, rk_ref, gt_ref, cm_ref)


def _out_proj_route(acts, ws, x, g, gate, g2, shift, scale, rw_t, rb, tl=512):
    bsz, l, d = x.shape
    tl = min(tl, l)
    nl = l // tl
    n_act = len(acts)
    row = lambda b, i: (b, i, 0)
    fix2 = lambda b, i: (0, 0)
    in_specs = [pl.BlockSpec((None, tl, a.shape[2]), row) for a in acts]
    in_specs += [pl.BlockSpec(wm.shape, fix2) for wm in ws]
    in_specs += [pl.BlockSpec((None, tl, d), row),
                 pl.BlockSpec((1, d), fix2),
                 pl.BlockSpec((None, 1, d), _bidx(gate)),
                 pl.BlockSpec((1, d), fix2),
                 pl.BlockSpec((None, 1, d), _bidx(shift)),
                 pl.BlockSpec((None, 1, d), _bidx(scale)),
                 pl.BlockSpec((N_EXPERTS, d), fix2),
                 pl.BlockSpec((N_EXPERTS, 1), fix2)]
    return pl.pallas_call(
        functools.partial(_out_kernel, n_act=n_act),
        grid=(bsz, nl),
        in_specs=in_specs,
        out_specs=[pl.BlockSpec((None, tl, d), row),
                   pl.BlockSpec((None, tl, d), row),
                   pl.BlockSpec((N_EXPERTS, tl), lambda b, i: (0, b * nl + i)),
                   pl.BlockSpec((N_EXPERTS, tl), lambda b, i: (0, b * nl + i)),
                   pl.BlockSpec((None, N_EXPERTS, LANES), lambda b, i: (b * nl + i, 0, 0))],
        out_shape=[jax.ShapeDtypeStruct((bsz, l, d), f32),
                   jax.ShapeDtypeStruct((bsz, l, d), bf16),
                   jax.ShapeDtypeStruct((N_EXPERTS, bsz * l), f32),
                   jax.ShapeDtypeStruct((N_EXPERTS, bsz * l), f32),
                   jax.ShapeDtypeStruct((bsz * nl, N_EXPERTS, LANES), f32)],
        compiler_params=_cp("parallel", "parallel"),
        name="out_proj_route",
    )(*acts, *ws, x, g.reshape(1, d), gate, g2.reshape(1, d), shift, scale, rw_t,
      rb.reshape(N_EXPERTS, 1))


def _route(xv, g, shift, scale, rw_ref, rb_ref, h_ref, rk_ref, gt_ref, cm_ref):
    hf = _norm_mod(xv, g, shift, scale)
    tl = hf.shape[0]
    h_ref[...] = hf.astype(h_ref.dtype)
    per = N_EXPERTS // N_GROUPS
    logits = _dot3(rw_ref[...], hf, _NT)
    s3 = _sigmoid(logits).reshape(N_GROUPS, per, tl)
    b3 = s3 + rb_ref[...].reshape(N_GROUPS, per, 1)
    neg = -jnp.inf
    jdx = lax.broadcasted_iota(jnp.int32, b3.shape, 1)
    gdx = lax.broadcasted_iota(jnp.int32, b3.shape, 0)
    m1 = jnp.max(b3, axis=1, keepdims=True)
    f1 = jnp.min(jnp.where(b3 == m1, jdx, per), axis=1, keepdims=True)
    m2 = jnp.max(jnp.where(jdx == f1, neg, b3), axis=1, keepdims=True)
    grp = m1 + m2
    g1 = lax.broadcasted_iota(jnp.int32, grp.shape, 0)
    cnt = jnp.zeros(grp.shape, jnp.int32)
    for gp in range(N_GROUPS):
        rv = grp[gp:gp + 1]
        ahead = jnp.where(rv > grp, 1, jnp.where(rv == grp, jnp.where(g1 > gp, 1, 0), 0))
        cnt = cnt + ahead
    v = jnp.where(cnt < TOPK_GROUPS, b3, neg)
    eidx = gdx * per + jdx
    sel = jnp.zeros(b3.shape, f32)
    for _ in range(TOP_K):
        m = jnp.max(jnp.max(v, axis=1, keepdims=True), axis=0, keepdims=True)
        cand = jnp.where(v == m, eidx, N_EXPERTS)
        fi = jnp.min(jnp.min(cand, axis=1, keepdims=True), axis=0, keepdims=True)
        hit = eidx == fi
        sel = jnp.where(hit, 1.0, sel)
        v = jnp.where(hit, neg, v)
    ssel = sel * s3
    den = jnp.sum(jnp.sum(ssel, axis=1, keepdims=True), axis=0, keepdims=True)
    gt_ref[...] = ((ROUTED_SCALE * ssel) / den).reshape(N_EXPERTS, tl)
    sel2 = sel.reshape(N_EXPERTS, tl)
    r = lax.broadcasted_iota(jnp.int32, (MOE_SUB, MOE_SUB), 0)
    c = lax.broadcasted_iota(jnp.int32, (MOE_SUB, MOE_SUB), 1)
    before = jnp.where(r < c, 1.0, 0.0).astype(bf16)
    cmax = jnp.zeros((N_EXPERTS, 1), f32)
    for j in range(tl // MOE_SUB):
        sub = sel2[:, j * MOE_SUB:(j + 1) * MOE_SUB]
        rank = _dot(sub.astype(bf16), before)
        rk_ref[:, j * MOE_SUB:(j + 1) * MOE_SUB] = jnp.where(sub > 0.0, rank, -1.0)
        cmax = jnp.maximum(cmax, jnp.sum(sub, axis=1, keepdims=True))
    cm_ref[...] = jnp.broadcast_to(cmax, cm_ref.shape)


def _swiglu_act(hh):
    half = hh.shape[1] // 2
    return _silu(hh[:, :half]) * hh[:, half:]


def _moe_kernel(cnt_ref, ord_ref, h_ref, rk_ref, gt_ref, *refs, has_next):
    ng = MOE_GROUP
    gu_refs, dn_refs = refs[:ng], refs[ng:2 * ng]
    sgu_ref, sdn_ref, x_ref, g_ref, gate_ref = refs[2 * ng:2 * ng + 5]
    rest = refs[2 * ng + 5:]
    if has_next:
        gn_ref, shn_ref, scn_ref, o_ref, hn_ref = rest[:5]
        rest = rest[5:]
    else:
        o_ref = rest[0]
        rest = rest[1:]
    acc_ref, xg_ref, ys_ref, p_ref, gr_ref = rest
    tile = pl.program_id(0)
    grp = pl.program_id(1)
    tm, d = acc_ref.shape
    ns = tm // MOE_SUB
    eids =[ord_ref[tile, grp * ng + el] for el in range(ng)]

    @pl.when(grp == 0)
    def _():
        act = _swiglu_act(_dot(h_ref[...], sgu_ref[...]))
        acc_ref[...] = _dot(act.astype(bf16), sdn_ref[...])

    def expert_ffn(el, win):
        hh = _dot(xg_ref[el, 0:ns * win, :], gu_refs[el][...])
        gr = gr_ref[el, 0:ns * win, :]
        act = _swiglu_act(hh) * jnp.concatenate([gr] * (hh.shape[1] // (2 * LANES)), axis=1)
        y = _dot(act.astype(bf16), dn_refs[el][...]).astype(bf16)
        for s in range(ns):
            ys_ref[s, el * win:(el + 1) * win, :] = y[s * win:(s + 1) * win]

    def one_pass(p, win):
        base = p * win
        riota = lax.broadcasted_iota(jnp.int32, (win, MOE_SUB), 0).astype(f32)
        for s in range(ns):
            cols = slice(s * MOE_SUB, (s + 1) * MOE_SUB)
            onehots = []
            for el in range(ng):
                row = pl.ds(eids[el], 1)
                hit = (rk_ref[row, cols] - base) == riota
                onehots.append(jnp.where(hit, 1.0, 0.0).astype(bf16))
                gsel = jnp.sum(jnp.where(hit, gt_ref[row, cols], 0.0), axis=1, keepdims=True)
                gr_ref[el, s * win:(s + 1) * win, :] = jnp.broadcast_to(gsel, (win, LANES))
            pm = jnp.concatenate(onehots, axis=0)
            p_ref[s, 0:ng * win, :] = pm
            gx = _dot(pm, h_ref[cols, :])
            for el in range(ng):
                xg_ref[el, s * win:(s + 1) * win, :] = gx[el * win:(el + 1) * win].astype(bf16)
        for el in range(ng):
            expert_ffn(el, win)
        for s in range(ns):
            acc_ref[s * MOE_SUB:(s + 1) * MOE_SUB, :] += _dot(
                p_ref[s, 0:ng * win, :], ys_ref[s, 0:ng * win, :], _TN)

    most = cnt_ref[tile, eids[0]]
    for el in range(1, ng):
        most = jnp.maximum(most, cnt_ref[tile, eids[el]])

    lo = 0
    for win in MOE_WINDOWS[:-1]:
        pl.when(jnp.logical_and(most > lo, most <= win))(functools.partial(one_pass, 0, win))
        lo = win
    big = MOE_WINDOWS[-1]

    def big_pass(p, carry):
        one_pass(p, big)
        return carry

    lax.fori_loop(0, jnp.where(most > lo, (most + big - 1) // big, 0), big_pass, 0)

    @pl.when(grp == pl.num_programs(1) - 1)
    def _():
        mo = acc_ref[...]
        y = mo * lax.rsqrt(jnp.mean(mo * mo, axis=-1, keepdims=True) + RMS_EPS) * g_ref[...]
        xn = x_ref[...] + gate_ref[...] * y
        o_ref[...] = xn
        if has_next:
            hn_ref[...] = _norm_mod(xn, gn_ref[...], shn_ref[...], scn_ref[...]).astype(hn_ref.dtype)


def _moe(h2, rk, gt, cmax, gu, dn, sgu, sdn, x, g, gate, nxt=None, tm=1024):
    bsz, l, d = x.shape
    tm = min(tm, l)
    per_b = l // tm
    nt = bsz * per_b
    ne = gu.shape[0]
    ng = MOE_GROUP
    ns = tm // MOE_SUB
    wmax = MOE_WINDOWS[-1]
    counts = jnp.max(cmax[:, :, 0].reshape(nt, -1, ne), axis=1).astype(jnp.int32)
    order = jnp.argsort(-counts, axis=1).astype(jnp.int32)

    def expert_spec(arr, k):
        return pl.BlockSpec((None,) + arr.shape[1:], lambda t, e, cnt, order_ref: (order_ref[t, e * ng + k], 0, 0))

    tile_spec = pl.BlockSpec((tm, d), lambda t, e, *_: (t, 0))
    vec_spec = pl.BlockSpec((1, d), lambda t, e, *_: (0, 0))
    batch_spec = pl.BlockSpec((None, 1, d), lambda t, e, *_: (t // per_b, 0, 0))
    in_specs = ([tile_spec,
                 pl.BlockSpec((ne, tm), lambda t, e, *_: (0, t)),
                 pl.BlockSpec((ne, tm), lambda t, e, *_: (0, t))]
                + [expert_spec(gu, k) for k in range(ng)]
                + [expert_spec(dn, k) for k in range(ng)]
                + [pl.BlockSpec(sgu.shape, lambda t, e, *_: (0, 0)),
                   pl.BlockSpec(sdn.shape, lambda t, e, *_: (0, 0)),
                   tile_spec, vec_spec, batch_spec])
    args = [counts, order, h2.reshape(bsz * l, d), rk, gt, *([gu] * ng), *([dn] * ng), sgu, sdn,
            x.reshape(bsz * l, d), g.reshape(1, d), gate]
    out_specs, out_shape = [tile_spec], [jax.ShapeDtypeStruct((bsz * l, d), f32)]
    if nxt is not None:
        in_specs += [vec_spec, batch_spec, batch_spec]
        args += [nxt[0].reshape(1, d), nxt[1], nxt[2]]
        out_specs.append(tile_spec)
        out_shape.append(jax.ShapeDtypeStruct((bsz * l, d), bf16))
    grid_spec = pltpu.PrefetchScalarGridSpec(
        num_scalar_prefetch=2,
        grid=(nt, ne // ng),
        in_specs=in_specs,
        out_specs=out_specs,
        scratch_shapes=[pltpu.VMEM((tm, d), f32),
                        pltpu.VMEM((ng, ns * wmax, d), bf16),
                        pltpu.VMEM((ns, ng * wmax, d), bf16),
                        pltpu.VMEM((ns, ng * wmax, MOE_SUB), bf16),
                        pltpu.VMEM((ng, ns * wmax, LANES), f32)])
    outs = pl.pallas_call(
        functools.partial(_moe_kernel, has_next=nxt is not None),
        grid_spec=grid_spec,
        out_shape=out_shape,
        compiler_params=_cp("parallel", "arbitrary"),
        name="moe",
    )(*args)
    return [o.reshape(bsz, l, d) for o in outs]


def _filter_kernel(z_ref, w1_ref, b1_ref, w2_ref, b2_ref, w3_ref, win_ref, o_ref):
    hid = jnp.sin(FILTER_SIN_W * (_dot3(z_ref[...], w1_ref[...]) + b1_ref[...]))
    hid = jnp.sin(FILTER_SIN_W * (_dot3(hid, w2_ref[...]) + b2_ref[...]))
    o_ref[...] = _dot3(hid, w3_ref[...]) * win_ref[...]


def _filters(z, w1, b1, w2, b2, w3, window, tn=512):
    l, p = z.shape
    hdim = w1.shape[1]
    n = w3.shape[1]
    d = window.shape[1]
    nd = d // tn
    return pl.pallas_call(
        _filter_kernel,
        grid=(n // tn,),
        in_specs=[pl.BlockSpec((l, p), lambda j: (0, 0)),
                  pl.BlockSpec((p, hdim), lambda j: (0, 0)),
                  pl.BlockSpec((1, hdim), lambda j: (0, 0)),
                  pl.BlockSpec((hdim, hdim), lambda j: (0, 0)),
                  pl.BlockSpec((1, hdim), lambda j: (0, 0)),
                  pl.BlockSpec((hdim, tn), lambda j: (0, j)),
                  pl.BlockSpec((l, tn), lambda j: (0, j % nd))],
        out_specs=pl.BlockSpec((l, tn), lambda j: (0, j)),
        out_shape=jax.ShapeDtypeStruct((l, n), f32),
        compiler_params=_cp("arbitrary"),
        name="hyena_filter",
    )(z, w1, b1.reshape(1, hdim), w2, b2.reshape(1, hdim), w3, window)


def _dft_tables(l):
    n = 2 * l
    n1 = math.isqrt(n)
    assert n == n1 * n1 and n1 % 16 == 0
    na = l // n1
    ncp = -(-(n1 // 2 + 1) // 8) * 8
    a = np.arange(na)
    b = np.arange(n1)
    c = np.arange(ncp)
    th = 2.0 * np.pi * ((n1 * a[None, None, :] + b[:, None, None]) * c[None, :, None]) / n
    t1 = np.concatenate([np.cos(th), -np.sin(th)], axis=1)
    ph = 2.0 * np.pi * (b[:, None] * b[None, :]) / n1
    cs, sn = np.cos(ph), np.sin(ph)
    a3 = np.block([[cs, sn], [-sn, cs]])
    a3i = np.block([[cs, -sn], [sn, cs]])
    a2 = np.arange(na) + na // 2
    th2 = 2.0 * np.pi * ((n1 * a2[None, :, None] + b[:, None, None]) * c[None, None, :]) / n
    wc = np.where((c == 0) | (c == n1 // 2), 1.0, np.where(c < n1 // 2, 2.0, 0.0))[None, None, :]
    t2 = np.concatenate([wc * np.cos(th2), -wc * np.sin(th2)], axis=2)
    return [jnp.asarray(t, f32).astype(bf16) for t in (t1, a3, a3i, t2)]


def _fft_dims(t1):
    n1, ncp2, na = t1.shape
    ncp = ncp2 // 2
    return n1, ncp, na, 2 * n1 + FFT_PAD, 2 * ncp + FFT_PAD, n1 + FFT_PAD


def _ld(ref, rows):
    return jnp.concatenate([ref[j, rows, :] for j in range(ref.shape[0])], axis=1)


def _st(ref, rows, val):
    for j in range(ref.shape[0]):
        ref[j, rows, :] = val[:, j * LANES:(j + 1) * LANES]


def _dft_forward(uf_ref, t1_ref, zs_ref):
    n1, ncp, na, sb, _, su = _fft_dims(t1_ref)
    for b in range(n1):
        ub = _ld(uf_ref, pl.ds(b, na, stride=su)).astype(bf16)
        zb = _dot(t1_ref[b], ub)
        _st(zs_ref, pl.ds(b, ncp, stride=sb), zb[:ncp])
        _st(zs_ref, pl.ds(n1 + b, ncp, stride=sb), zb[ncp:])


def _spectrum_kernel(f_ref, t1_ref, a3_ref, o_ref, uf_ref, zs_ref, *, scale):
    n1, ncp, na, sb, _, su = _fft_dims(t1_ref)
    for a in range(na):
        _st(uf_ref, pl.ds(a * su, n1), f_ref[pl.ds(a * n1, n1), :])
    _dft_forward(uf_ref, t1_ref, zs_ref)
    a3 = a3_ref[...]
    for c in range(ncp):
        zc = _ld(zs_ref, pl.ds(c * sb, 2 * n1)).astype(bf16)
        o_ref[c] = (_dot(a3, zc) * scale).astype(o_ref.dtype)


def _spectrum(filt, tabs, dt=256):
    l, n = filt.shape
    t1, a3, _, _ = tabs
    n1, ncp, na, sb, _, su = _fft_dims(t1)
    nj = dt // LANES
    return pl.pallas_call(
        functools.partial(_spectrum_kernel, scale=1.0 / (2 * l)),
        grid=(n // dt,),
        in_specs=[pl.BlockSpec((l, dt), lambda j: (0, j)),
                  pl.BlockSpec(t1.shape, lambda j: (0, 0, 0)),
                  pl.BlockSpec(a3.shape, lambda j: (0, 0))],
        out_specs=pl.BlockSpec((ncp, 2 * n1, dt), lambda j: (0, 0, j)),
        out_shape=jax.ShapeDtypeStruct((ncp, 2 * n1, n), bf16),
        scratch_shapes=[pltpu.VMEM((nj, na * su, LANES), f32),
                        pltpu.VMEM((nj, ncp * sb, LANES), f32)],
        compiler_params=_cp("arbitrary"),
        name="hyena_spectrum",
    )(filt, t1, a3)


def _fftconv_kernel(u_ref, xg_ref, kf_ref, fb_ref, t1_ref, a3_ref, a3i_ref, t2_ref, o_ref,
                    uf_ref, zs_ref, qs_ref, y_ref):
    n1, ncp, na, sb, sq, su = _fft_dims(t1_ref)
    for a in range(na):
        _st(uf_ref, pl.ds(a * su, n1), u_ref[pl.ds(a * n1, n1), :].astype(f32))
    _dft_forward(uf_ref, t1_ref, zs_ref)
    a3 = a3_ref[...]
    a3i = a3i_ref[...]
    for c in range(ncp):
        zc = _ld(zs_ref, pl.ds(c * sb, 2 * n1)).astype(bf16)
        xc = _dot(a3, zc)
        kc = kf_ref[c].astype(f32)
        xr, xi = xc[:n1], xc[n1:]
        kr, ki = kc[:n1], kc[n1:]
        pc = jnp.concatenate([xr * kr - xi * ki, xr * ki + xi * kr], axis=0).astype(bf16)
        qc = _dot(a3i, pc)
        _st(qs_ref, pl.ds(c, n1, stride=sq), qc[:n1])
        _st(qs_ref, pl.ds(ncp + c, n1, stride=sq), qc[n1:])
    for b in range(n1):
        qb = _ld(qs_ref, pl.ds(b * sq, 2 * ncp)).astype(bf16)
        _st(y_ref, pl.ds(b, na, stride=su), _dot(t2_ref[b], qb))
    fb = fb_ref[...]
    for a in range(na):
        rows = pl.ds(a * n1, n1)
        uv = _ld(uf_ref, pl.ds(a * su, n1))
        yv = _ld(y_ref, pl.ds(a * su, n1))
        o_ref[rows, :] = (xg_ref[rows, :].astype(f32) * (yv + uv * fb)).astype(o_ref.dtype)


def _fftconv(u, u_col, xg, xg_col, kf, kf_col, fbias, tabs, d, dt=256):
    bsz, l, _ = u.shape
    t1, a3, a3i, t2 = tabs
    n1, ncp, na, sb, sq, su = _fft_dims(t1)
    nd = d // dt
    nj = dt // LANES
    uo, go, ko = u_col // dt, xg_col // dt, kf_col // dt
    return pl.pallas_call(
        _fftconv_kernel,
        grid=(nd, bsz),
        in_specs=[pl.BlockSpec((None, l, dt), lambda j, b: (b, 0, j + uo)),
                  pl.BlockSpec((None, l, dt), lambda j, b: (b, 0, j + go)),
                  pl.BlockSpec((ncp, 2 * n1, dt), lambda j, b: (0, 0, j + ko)),
                  pl.BlockSpec((1, dt), lambda j, b: (0, j)),
                  pl.BlockSpec(t1.shape, lambda j, b: (0, 0, 0)),
                  pl.BlockSpec(a3.shape, lambda j, b: (0, 0)),
                  pl.BlockSpec(a3i.shape, lambda j, b: (0, 0)),
                  pl.BlockSpec(t2.shape, lambda j, b: (0, 0, 0))],
        out_specs=pl.BlockSpec((None, l, dt), lambda j, b: (b, 0, j)),
        out_shape=jax.ShapeDtypeStruct((bsz, l, d), bf16),
        scratch_shapes=[pltpu.VMEM((nj, na * su, LANES), f32),
                        pltpu.VMEM((nj, ncp * sb, LANES), f32),
                        pltpu.VMEM((nj, n1 * sq, LANES), f32),
                        pltpu.VMEM((nj, na * su, LANES), f32)],
        compiler_params=_cp("parallel", "arbitrary"),
        name="hyena_fftconv",
    )(u, xg, kf, fbias.reshape(1, d), t1, a3, a3i, t2)


def _rope_tables(l):
    rows = l // GRID_W
    row = jnp.repeat(jnp.arange(rows), GRID_W)
    col = jnp.tile(jnp.arange(GRID_W), rows)
    inv = ROPE_BASE ** (-jnp.arange(ROPE_AXIS_PAIRS, dtype=f32) / ROPE_AXIS_PAIRS)
    ang = jnp.stack([row, col], axis=-1).astype(f32)[..., None] * inv
    ang = jnp.broadcast_to(ang[:, :, None, :], (l, 2, 2, ROPE_AXIS_PAIRS)).reshape(l, A_DQK)
    reps = A_QW // A_DQK
    return jnp.tile(jnp.cos(ang), (1, reps)), jnp.tile(jnp.sin(ang), (1, reps))


def _rotate_cols(w):
    j = np.arange(w.shape[1])
    lo = (j % (2 * ROPE_AXIS_PAIRS)) < ROPE_AXIS_PAIRS
    perm = np.where(lo, j + ROPE_AXIS_PAIRS, j - ROPE_AXIS_PAIRS)
    sign = np.where(lo, -1.0, 1.0).astype(np.float32)
    return w[:, perm] * sign


def _gate_cols(w_g, b_g):
    idx_i = np.array([d * 2 * B_HEADS + hd for d in range(2) for hd in range(B_HEADS)])
    idx_f = idx_i + B_HEADS
    pad = LANES - _NCHAIN
    k = w_g.shape[0]
    w = jnp.concatenate([w_g[:, idx_i], jnp.zeros((k, pad), f32),
                         w_g[:, idx_f], jnp.zeros((k, pad), f32)], axis=1)
    b = jnp.concatenate([b_g[idx_i], jnp.zeros((pad,), f32), b_g[idx_f], jnp.zeros((pad,), f32)])
    return w, b


def _hyena_consts(l, d):
    j = jnp.arange(l, dtype=f32)
    bands = (POS_EMB_DIM - 1) // 2
    freqs = jnp.linspace(1e-4, bands - 1, bands, dtype=f32)
    ang = (2.0 * math.pi / l) * j[:, None] * freqs[None, :]
    z = jnp.concatenate([(j / (l - 1))[:, None], jnp.cos(ang), -jnp.sin(ang)], axis=-1)
    dist = jnp.abs(j - l // 2) / (l // 2)
    max_decay = math.log(DECAY_TARGET) / DECAY_FAST_PCT
    min_decay = math.log(DECAY_TARGET) / DECAY_SLOW_PCT
    deltas = jnp.abs(jnp.linspace(min_decay, max_decay, d, dtype=f32))
    window = jnp.exp(-dist[:, None] * deltas[None, :])
    return z, window


def _ab_mixer(h, hc, w_in, conv_w, conv_b, gate_b, lam_vecs, g_a, g_b, w_out, lam_init):
    s = h.shape[1]
    w = B_WIDTH
    o = 0
    cols = {}
    for name, width in (("aq", A_QW), ("bq", w), ("bo", w), ("ak", A_QW), ("av", A_VW),
                        ("bk", w), ("bv", w), ("g", 4 * B_HEADS)):
        cols[name] = w_in[:, o:o + width]
        o += width
    cos, sin = _rope_tables(s)
    cat = lambda *ws: jnp.concatenate(ws, axis=1).astype(bf16)
    q = _mm(h, cat(cols["aq"], _rotate_cols(cols["aq"])), rope=(cos, sin, A_DQK ** -0.5))
    k = _mm(h, cat(cols["ak"], _rotate_cols(cols["ak"])), rope=(cos, sin, 1.0))
    qk = _mm(h, cat(cols["bq"], cols["bk"]), conv=(conv_w, conv_b, True))
    vvo = _mm(h, cat(cols["av"], cols["bv"], cols["bo"]))
    wg, bg = _gate_cols(cols["g"], gate_b)
    gates = _mm(h, wg.astype(bf16), out_dtype=f32, bias=bg, tn=2 * LANES)
    ckv = _mm(hc, cat(cols["ak"], cols["av"], cols["bv"]))
    cbk = _mm(hc, cols["bk"].astype(bf16), conv=(conv_w[:, w:], conv_b[w:], True))
    cg = _mm(hc, wg.astype(bf16), out_dtype=f32, bias=bg, tn=2 * LANES)
    out_a = _attn(lam_vecs, q, k, vvo, ckv, g_a, lam_init)
    out_b = _mlstm(qk, vvo, gates, cbk, ckv, cg, g_b)
    wo = w_out.astype(bf16)
    return [out_a, out_b], [wo[:A_VW], wo[A_VW:]]


def _hyena_mixer(h, w_in, conv_w, conv_b, fw1, fb1, fw2, fb2, fw3, fbias, w_out):
    _, l, d = h.shape
    u = _mm(h, w_in.astype(bf16), conv=(conv_w, conv_b, False))
    z, window = _hyena_consts(l, d)
    pz, ph = LANES - z.shape[1], LANES - fw1.shape[1]
    filt = _filters(jnp.pad(z, ((0, 0), (0, pz))), jnp.pad(fw1, ((0, pz), (0, ph))),
                    jnp.pad(fb1, (0, ph)), jnp.pad(fw2, ((0, ph), (0, ph))), jnp.pad(fb2, (0, ph)),
                    jnp.pad(fw3, ((0, ph), (0, 0))), window)
    tabs = _dft_tables(l)
    kf = _spectrum(filt, tabs)
    zz = _fftconv(u, 0, u, d, kf, 0, fbias[0], tabs, d)
    y = _fftconv(zz, 0, u, 2 * d, kf, d, fbias[1], tabs, d)
    return [y], [w_out.astype(bf16)]


def kernel(x, c, ctx, c_ctx, w_mod, b_mod, norm_g, w_in_ab, conv_ab_w, conv_ab_b, gate_b_ab, diff_lambda, head_g_a, head_g_b, w_out_ab, w_in_hy, conv_hy_w, conv_hy_b, filt_w1, filt_b1, filt_w2, filt_b2, filt_w3, filt_bias, w_out_hy, router_w, router_b, exp_gu, exp_down, sh_gu, sh_down):
    bsz, s, d = x.shape
    depth = w_mod.shape[0]
    rows = -(-(bsz + 1) // 8) * 8
    cc = jnp.concatenate([c, c_ctx[None, :], jnp.zeros((rows - bsz - 1, d), f32)], axis=0)
    mods = [_mod(cc, w_mod[l], b_mod[l]) for l in range(depth)]
    vec = lambda l, i: mods[l][:bsz, i * d:(i + 1) * d].reshape(bsz, 1, d)
    h = _norm(x, norm_g[0, 0], vec(0, 0), vec(0, 1))
    for l in range(depth):
        g_m, sh_f, sc_f, g_f = [vec(l, i) for i in range(2, 6)]
        if l % 2 == 0:
            e = l // 2
            lam_init = 0.8 - 0.6 * math.exp(-0.3 * l)
            row_c = lambda i: mods[l][bsz:bsz + 1, i * d:(i + 1) * d].reshape(1, 1, d)
            hc = _norm(ctx, norm_g[l, 0], row_c(0), row_c(1))
            acts, ws = _ab_mixer(h, hc, w_in_ab[e], conv_ab_w[e], conv_ab_b[e], gate_b_ab[e],
                                 diff_lambda[e], head_g_a[e], head_g_b[e], w_out_ab[e], lam_init)
        else:
            o = l // 2
            acts, ws = _hyena_mixer(h, w_in_hy[o], conv_hy_w[o], conv_hy_b[o], filt_w1[o], filt_b1[o],
                                    filt_w2[o], filt_b2[o], filt_w3[o], filt_bias[o], w_out_hy[o])
        x, h2, rk, gt, cmax = _out_proj_route(acts, ws, x, norm_g[l, 1], g_m, norm_g[l, 2], sh_f, sc_f,
                                              router_w[l].T, router_b[l])
        nxt = (norm_g[l + 1, 0], vec(l + 1, 0), vec(l + 1, 1)) if l + 1 < depth else None
        outs = _moe(h2, rk, gt, cmax, exp_gu[l].astype(bf16), exp_down[l].astype(bf16),
                    sh_gu[l].astype(bf16), sh_down[l].astype(bf16), x, norm_g[l, 3], g_f, nxt)
        x = outs[0]
        if nxt is not None:
            h = outs[1]
    return x
```

```python
import functools
import math

import numpy as np
import jax
import jax.numpy as jnp
from jax import lax
from jax.experimental import pallas as pl
from jax.experimental.pallas import tpu as pltpu

f32 = jnp.float32
bf16 = jnp.bfloat16

RMS_EPS = 1e-6
A_HEADS = 4
A_DQK = 64
A_DV = 128
B_HEADS = 4
B_DH = 128
B_WIDTH = B_HEADS * B_DH
A_QW = A_HEADS * 2 * A_DQK
A_VW = A_HEADS * A_DV
GRID_W = 64
ROPE_BASE = 10000.0
ROPE_AXIS_PAIRS = A_DQK // 4
N_EXPERTS = 64
TOP_K = 8
N_GROUPS = 8
TOPK_GROUPS = 4
D_EXPERT = 256
ROUTED_SCALE = 2.5
HY_ORDER = 2
POS_EMB_DIM = 33
FILTER_SIN_W = 1.0
DECAY_FAST_PCT = 0.3
DECAY_SLOW_PCT = 1.5
DECAY_TARGET = 1e-2

LANES = 128
VMEM_LIMIT = 56 * 1024 * 1024
MLSTM_CHUNK = 256
FFT_PAD = 8
MOE_SUB = 256
MOE_WINDOWS = (16, 32, 48, 64)
MOE_GROUP = 4


def _cp(*sem):
    return pltpu.CompilerParams(dimension_semantics=sem, vmem_limit_bytes=VMEM_LIMIT)


def _split_bf16(a):
    hi = a.astype(bf16)
    lo = (a - hi.astype(f32)).astype(bf16)
    return hi, lo


def _dot(a, b, dims=(((1,), (0,)), ((), ()))):
    return lax.dot_general(a, b, dims, preferred_element_type=f32)


_NT = (((1,), (1,)), ((), ()))
_TN = (((0,), (0,)), ((), ()))


def _dot3(a, b, dims=(((1,), (0,)), ((), ()))):
    ah, al = _split_bf16(a)
    bh, bl = _split_bf16(b)
    return _dot(ah, bh, dims) + (_dot(ah, bl, dims) + _dot(al, bh, dims))


def _silu(v):
    return v / (1.0 + jnp.exp(-v))


def _sigmoid(v):
    return 1.0 / (1.0 + jnp.exp(-v))


def _log_sigmoid(v):
    return jnp.minimum(v, 0.0) - jnp.log(1.0 + jnp.exp(-jnp.abs(v)))


def _mod_kernel(c_ref, w_ref, b_ref, o_ref):
    o_ref[...] = _dot3(_silu(c_ref[...]), w_ref[...]) + b_ref[...]


def _mod(cc, w, b):
    rows, d = cc.shape
    n = w.shape[1]
    tn = d
    return pl.pallas_call(
        _mod_kernel,
        grid=(n // tn,),
        in_specs=[pl.BlockSpec((rows, d), lambda j: (0, 0)),
                  pl.BlockSpec((d, tn), lambda j: (0, j)),
                  pl.BlockSpec((1, tn), lambda j: (0, j))],
        out_specs=pl.BlockSpec((rows, tn), lambda j: (0, j)),
        out_shape=jax.ShapeDtypeStruct((rows, n), f32),
        compiler_params=_cp("arbitrary"),
        name="mod",
    )(cc, w, b.reshape(1, n))


def _norm_mod(xv, g, shift, scale):
    y = xv * lax.rsqrt(jnp.mean(xv * xv, axis=-1, keepdims=True) + RMS_EPS)
    return (y * g) * (1.0 + scale) + shift


def _norm_kernel(x_ref, g_ref, sh_ref, sc_ref, o_ref):
    o_ref[...] = _norm_mod(x_ref[...], g_ref[...], sh_ref[...], sc_ref[...]).astype(o_ref.dtype)


def _bidx(arr):
    if arr.shape[0] == 1:
        return lambda b, *_: (0, 0, 0)
    return lambda b, *_: (b, 0, 0)


def _norm(x, g, shift, scale, tl=512):
    bsz, l, d = x.shape
    tl = min(tl, l)
    return pl.pallas_call(
        _norm_kernel,
        grid=(bsz, l // tl),
        in_specs=[pl.BlockSpec((None, tl, d), lambda b, i: (b, i, 0)),
                  pl.BlockSpec((1, d), lambda b, i: (0, 0)),
                  pl.BlockSpec((None, 1, d), _bidx(shift)),
                  pl.BlockSpec((None, 1, d), _bidx(scale))],
        out_specs=pl.BlockSpec((None, tl, d), lambda b, i: (b, i, 0)),
        out_shape=jax.ShapeDtypeStruct((bsz, l, d), bf16),
        compiler_params=_cp("parallel", "parallel"),
        name="norm",
    )(x, g.reshape(1, d), shift, scale)


def _mm_plain_kernel(h_ref, w_ref, b_ref, o_ref):
    o_ref[...] = (_dot(h_ref[...], w_ref[...]) + b_ref[...]).astype(o_ref.dtype)


def _mm_rope_kernel(h_ref, w_ref, cos_ref, sin_ref, o_ref, *, scale):
    p = _dot(h_ref[...], w_ref[...])
    n = o_ref.shape[-1]
    o_ref[...] = ((p[:, :n] * cos_ref[...] + p[:, n:] * sin_ref[...]) * scale).astype(o_ref.dtype)


def _mm_conv_kernel(h_ref, w_ref, cw_ref, cb_ref, o_ref, scr_ref, *, act, rc):
    l = h_ref.shape[0]
    w = w_ref[...]
    w0, w1, w2, cb = cw_ref[0:1, :], cw_ref[1:2, :], cw_ref[2:3, :], cb_ref[...]
    halo = 16
    zrow = jnp.zeros((8, o_ref.shape[1]), f32)
    for c in range(l // rc):
        lo, hi = max(c * rc - halo, 0), min((c + 1) * rc + halo, l)
        n = hi - lo
        scr = scr_ref.at[c % 2]
        scr[8:8 + n, :] = _dot(h_ref[lo:hi, :], w)
        if lo == 0:
            scr[0:8, :] = zrow
        if hi == l:
            scr[8 + n:16 + n, :] = zrow
        off = 8 + c * rc - lo
        y = (scr[off - 1:off - 1 + rc, :] * w0 + scr[off:off + rc, :] * w1
             + scr[off + 1:off + 1 + rc, :] * w2 + cb)
        o_ref[c * rc:(c + 1) * rc, :] = (_silu(y) if act else y).astype(o_ref.dtype)


def _mm(h, w, *, out_dtype=bf16, bias=None, rope=None, conv=None, tl=512, tn=512):
    bsz, l, k = h.shape
    n = w.shape[1]
    if rope is not None:
        cos, sin, scale = rope
        n_out = n // 2
        tl = min(tl, l)
        return pl.pallas_call(
            functools.partial(_mm_rope_kernel, scale=scale),
            grid=(bsz, l // tl),
            in_specs=[pl.BlockSpec((None, tl, k), lambda b, i: (b, i, 0)),
                      pl.BlockSpec((k, n), lambda b, i: (0, 0)),
                      pl.BlockSpec((tl, n_out), lambda b, i: (i, 0)),
                      pl.BlockSpec((tl, n_out), lambda b, i: (i, 0))],
            out_specs=pl.BlockSpec((None, tl, n_out), lambda b, i: (b, i, 0)),
            out_shape=jax.ShapeDtypeStruct((bsz, l, n_out), out_dtype),
            compiler_params=_cp("parallel", "parallel"),
            name="mm_rope",
        )(h, w, cos, sin)
    tn = min(tn, n)
    if conv is not None:
        cw, cb, act = conv
        rc = min(512, l)
        return pl.pallas_call(
            functools.partial(_mm_conv_kernel, act=act, rc=rc),
            grid=(bsz, n // tn),
            in_specs=[pl.BlockSpec((None, l, k), lambda b, j: (b, 0, 0)),
                      pl.BlockSpec((k, tn), lambda b, j: (0, j)),
                      pl.BlockSpec((3, tn), lambda b, j: (0, j)),
                      pl.BlockSpec((1, tn), lambda b, j: (0, j))],
            out_specs=pl.BlockSpec((None, l, tn), lambda b, j: (b, 0, j)),
            out_shape=jax.ShapeDtypeStruct((bsz, l, n), out_dtype),
            scratch_shapes=[pltpu.VMEM((2, rc + 48, tn), f32)],
            compiler_params=_cp("parallel", "arbitrary"),
            name="mm_conv",
        )(h, w, cw, cb.reshape(1, n))
    if bias is None:
        bias = jnp.zeros((n,), f32)
    if k * n * w.dtype.itemsize <= 4 * 1024 * 1024:
        tn = n
    tl = min(tl, l)
    return pl.pallas_call(
        _mm_plain_kernel,
        grid=(bsz, l // tl, n // tn),
        in_specs=[pl.BlockSpec((None, tl, k), lambda b, i, j: (b, i, 0)),
                  pl.BlockSpec((k, tn), lambda b, i, j: (0, j)),
                  pl.BlockSpec((1, tn), lambda b, i, j: (0, j))],
        out_specs=pl.BlockSpec((None, tl, tn), lambda b, i, j: (b, i, j)),
        out_shape=jax.ShapeDtypeStruct((bsz, l, n), out_dtype),
        compiler_params=_cp("parallel", "parallel", "arbitrary"),
        name="mm_plain",
    )(h, w, bias.reshape(1, n))


def _attn_kernel(lv_ref, q_ref, kc_ref, k_ref, vc_ref, v_ref, g_ref, o_ref, *, lam_init):
    tq = q_ref.shape[0]
    lv = lv_ref[...]
    lam = (jnp.exp(jnp.sum(lv[0:1] * lv[1:2], axis=1, keepdims=True))
           - jnp.exp(jnp.sum(lv[2:3] * lv[3:4], axis=1, keepdims=True)) + lam_init)
    first = lax.broadcasted_iota(jnp.int32, (tq, A_DV), 1) < A_DQK
    one0 = jnp.where(lax.broadcasted_iota(jnp.int32, (1, A_DV), 1) == 0, 1.0, 0.0).astype(bf16)
    ones_c = jnp.broadcast_to(one0, (kc_ref.shape[0], A_DV))
    ones_l = jnp.broadcast_to(one0, (k_ref.shape[0], A_DV))
    for hd in range(A_HEADS):
        cs = slice(hd * A_DV, (hd + 1) * A_DV)
        qh = q_ref[:, cs]
        zero = jnp.zeros_like(qh)
        q2 = jnp.concatenate([jnp.where(first, qh, zero), jnp.where(first, zero, qh)], axis=0)
        s_c = _dot(q2, kc_ref[:, cs], _NT)
        s_l = _dot(q2, k_ref[:, cs], _NT)
        m = jnp.maximum(jnp.max(s_c, axis=1, keepdims=True), jnp.max(s_l, axis=1, keepdims=True))
        p_c = jnp.exp((s_c - m).astype(bf16))
        p_l = jnp.exp((s_l - m).astype(bf16))
        oa = (_dot(p_c, jnp.concatenate([vc_ref[:, cs], ones_c], axis=1))
              + _dot(p_l, jnp.concatenate([v_ref[:, cs], ones_l], axis=1)))
        on = oa[:, :A_DV] * (1.0 / oa[:, A_DV:A_DV + 1])
        o = on[:tq] - lam * on[tq:]
        o = o * lax.rsqrt(jnp.mean(o * o, axis=1, keepdims=True) + RMS_EPS)
        o_ref[:, cs] = (o * g_ref[:, cs] * (1.0 - lam_init)).astype(o_ref.dtype)


def _attn(lv, q, k, vvo, ckv, g_a, lam_init, tq=256):
    bsz, s, _ = q.shape
    lc = ckv.shape[1]
    tq = min(tq, s)
    w = A_QW
    return pl.pallas_call(
        functools.partial(_attn_kernel, lam_init=lam_init),
        grid=(bsz, s // tq),
        in_specs=[pl.BlockSpec(lv.shape, lambda b, i: (0, 0)),
                  pl.BlockSpec((None, tq, w), lambda b, i: (b, i, 0)),
                  pl.BlockSpec((None, lc, w), lambda b, i: (b, 0, 0)),
                  pl.BlockSpec((None, s, w), lambda b, i: (b, 0, 0)),
                  pl.BlockSpec((None, lc, w), lambda b, i: (b, 0, 1)),
                  pl.BlockSpec((None, s, w), lambda b, i: (b, 0, 0)),
                  pl.BlockSpec((1, w), lambda b, i: (0, 0))],
        out_specs=pl.BlockSpec((None, tq, w), lambda b, i: (b, i, 0)),
        out_shape=jax.ShapeDtypeStruct((bsz, s, w), bf16),
        compiler_params=_cp("parallel", "arbitrary"),
        name="diff_attn",
    )(lv, q, ckv, k, ckv, vvo, g_a.reshape(1, w))


_LN_QSCALE = math.log(B_DH ** -0.5)
_NCHAIN = 2 * B_HEADS


def _chunk_gate_sums(gi, gf, tri):
    lf = _log_sigmoid(gf)
    hi, lo = _split_bf16(lf)
    cum = _dot(tri, hi) + _dot(tri, lo)
    t = gf.shape[0]
    tot = cum[t - 1:t, :]
    rcum = tot - cum + lf
    fwd = lax.broadcasted_iota(jnp.int32, gf.shape, 1) < B_HEADS
    bd = jnp.where(fwd, cum, rcum)
    return bd, tot, (bd - gi).T


def _lower_tri(t):
    r = lax.broadcasted_iota(jnp.int32, (t, t), 0)
    c = lax.broadcasted_iota(jnp.int32, (t, t), 1)
    return r, c


def _ones_block(t):
    one0 = jnp.where(lax.broadcasted_iota(jnp.int32, (1, B_DH), 1) == 0, 1.0, 0.0).astype(bf16)
    return jnp.broadcast_to(one0, (t, B_DH))


def _absorb(c_ref, m_ref, ch, x_row, tot_c, kb, vaug):
    m_prev = m_ref[ch][:, 0:1]
    g = tot_c - x_row
    m_new = jnp.maximum(tot_c + m_prev, jnp.max(g, axis=1, keepdims=True))
    wgt = jnp.exp(g - m_new)
    decay = jnp.exp(tot_c + m_prev - m_new)
    kw_t = kb.astype(f32).T * wgt
    c_ref[ch] = decay * c_ref[ch] + _dot(kw_t.astype(bf16), vaug)
    m_ref[ch] = jnp.broadcast_to(m_new, m_ref.shape[1:])


def _mlstm_kernel(qk_ref, vvo_ref, g_ref, ck_ref, ckv_ref, cg_ref, gb_ref, o_ref,
                  hf_ref, hb_ref, c_ref, m_ref, *, tc):
    s = o_ref.shape[0]
    lc = ck_ref.shape[0]
    nc = s // tc
    w = B_WIDTH
    dh = B_DH

    c_ref[...] = jnp.zeros_like(c_ref)
    m_ref[...] = jnp.zeros_like(m_ref)

    r, cidx = _lower_tri(lc)
    tri_c = jnp.where(cidx <= r, 1.0, 0.0).astype(bf16)
    cg = cg_ref[...]
    _, tot, xt = _chunk_gate_sums(cg[:, :LANES], cg[:, LANES:], tri_c)
    ones_c = _ones_block(lc)
    for ch in range(_NCHAIN):
        hs = slice((ch % B_HEADS) * dh, (ch % B_HEADS + 1) * dh)
        vs = slice(2 * w + (ch % B_HEADS) * dh, 2 * w + (ch % B_HEADS + 1) * dh)
        _absorb(c_ref, m_ref, ch, xt[ch:ch + 1, :], tot[:, ch:ch + 1], ck_ref[:, hs],
                jnp.concatenate([ckv_ref[:, vs], ones_c], axis=1))
    ones_t = _ones_block(tc)

    r, cidx = _lower_tri(tc)
    tri = jnp.where(cidx <= r, 1.0, 0.0).astype(bf16)
    causal = cidx <= r
    anti = cidx >= r

    def step(i, carry):
        for d in range(2):
            row0 = pl.multiple_of((i if d == 0 else nc - 1 - i) * tc, tc)
            rows = pl.ds(row0, tc)
            gch = g_ref[rows, :]
            gi = gch[:, :LANES]
            bd, tot, xt = _chunk_gate_sums(gi, gch[:, LANES:], tri)
            mask = causal if d == 0 else anti
            dst = hf_ref if d == 0 else hb_ref
            for hd in range(B_HEADS):
                ch = d * B_HEADS + hd
                hs = slice(hd * dh, (hd + 1) * dh)
                qb = qk_ref[rows, hs]
                kb = qk_ref[rows, slice(w + hd * dh, w + (hd + 1) * dh)]
                vaug = jnp.concatenate([vvo_ref[rows, slice(w + hd * dh, w + (hd + 1) * dh)], ones_t],
                                       axis=1)
                bcol = bd[:, ch:ch + 1]
                x_row = xt[ch:ch + 1, :]
                dmat = jnp.where(mask, bcol - x_row, -jnp.inf)
                m_prev = m_ref[ch][:, 0:1]
                inter = bcol + m_prev
                m_t = jnp.maximum(inter, jnp.max(dmat, axis=1, keepdims=True))
                e = jnp.exp(dmat - m_t + _LN_QSCALE)
                smat = _dot(qb, kb, _NT) * e
                sc = jnp.exp(inter - m_t + _LN_QSCALE)
                both = sc * _dot(qb, c_ref[ch].astype(bf16)) + _dot(smat.astype(bf16), vaug)
                den = both[:, dh:dh + 1]
                dst[rows, hs] = both[:, :dh] * (1.0 / jnp.maximum(jnp.abs(den), jnp.exp(-m_t)))
                _absorb(c_ref, m_ref, ch, x_row, tot[:, ch:ch + 1], kb, vaug)
        return carry

    lax.fori_loop(0, nc, step, 0)

    for hd in range(B_HEADS):
        hs = slice(hd * dh, (hd + 1) * dh)
        hsum = hf_ref[:, hs] + hb_ref[:, hs]
        hn = hsum * lax.rsqrt(jnp.mean(hsum * hsum, axis=1, keepdims=True) + RMS_EPS)
        og = _sigmoid(vvo_ref[:, slice(2 * w + hd * dh, 2 * w + (hd + 1) * dh)].astype(f32))
        o_ref[:, hs] = (hn * gb_ref[:, hs] * og).astype(o_ref.dtype)


def _mlstm(qk, vvo, gates, cbk, ckv, cg, g_b):
    bsz, s, _ = qk.shape
    lc = cbk.shape[1]
    w = B_WIDTH
    tc = min(MLSTM_CHUNK, s)
    return pl.pallas_call(
        functools.partial(_mlstm_kernel, tc=tc),
        grid=(bsz,),
        in_specs=[pl.BlockSpec((None, s, 2 * w), lambda b: (b, 0, 0)),
                  pl.BlockSpec((None, s, 3 * w), lambda b: (b, 0, 0)),
                  pl.BlockSpec((None, s, 2 * LANES), lambda b: (b, 0, 0)),
                  pl.BlockSpec((None, lc, w), lambda b: (b, 0, 0)),
                  pl.BlockSpec((None, lc, 3 * w), lambda b: (b, 0, 0)),
                  pl.BlockSpec((None, lc, 2 * LANES), lambda b: (b, 0, 0)),
                  pl.BlockSpec((1, w), lambda b: (0, 0))],
        out_specs=pl.BlockSpec((None, s, w), lambda b: (b, 0, 0)),
        out_shape=jax.ShapeDtypeStruct((bsz, s, w), bf16),
        scratch_shapes=[pltpu.VMEM((s, w), f32), pltpu.VMEM((s, w), f32),
                        pltpu.VMEM((_NCHAIN, B_DH, 2 * B_DH), f32),
                        pltpu.VMEM((_NCHAIN, 1, LANES), f32)],
        compiler_params=_cp("arbitrary"),
        name="mlstm",
    )(qk, vvo, gates, cbk, ckv, cg, g_b.reshape(1, w))


def _out_kernel(*refs, n_act):
    acts = refs[:n_act]
    ws = refs[n_act:2 * n_act]
    (x_ref, g_ref, gate_ref, g2_ref, sh_ref, sc_ref, rw_ref, rb_ref,
     o_ref, h_ref, rk_ref, gt_ref, cm_ref) = refs[2 * n_act:]
    mix = _dot(acts[0][...], ws[0][...])
    for a, wr in zip(acts[1:], ws[1:]):
        mix = mix + _dot(a[...], wr[...])
    y = mix * lax.rsqrt(jnp.mean(mix * mix, axis=-1, keepdims=True) + RMS_EPS) * g_ref[...]
    xn = x_ref[...] + gate_ref[...] * y
    o_ref[...] = xn
    _route(xn, g2_ref[...], sh_ref[...], sc_ref[...], rw_ref, rb_ref, h_ref, rk_ref, gt_ref, cm_ref)


def _out_proj_route(acts, ws, x, g, gate, g2, shift, scale, rw_t, rb, tl=512):
    bsz, l, d = x.shape
    tl = min(tl, l)
    nl = l // tl
    n_act = len(acts)
    row = lambda b, i: (b, i, 0)
    fix2 = lambda b, i: (0, 0)
    in_specs = [pl.BlockSpec((None, tl, a.shape[2]), row) for a in acts]
    in_specs += [pl.BlockSpec(wm.shape, fix2) for wm in ws]
    in_specs += [pl.BlockSpec((None, tl, d), row),
                 pl.BlockSpec((1, d), fix2),
                 pl.BlockSpec((None, 1, d), _bidx(gate)),
                 pl.BlockSpec((1, d), fix2),
                 pl.BlockSpec((None, 1, d), _bidx(shift)),
                 pl.BlockSpec((None, 1, d), _bidx(scale)),
                 pl.BlockSpec((N_EXPERTS, d), fix2),
                 pl.BlockSpec((N_EXPERTS, 1), fix2)]
    return pl.pallas_call(
        functools.partial(_out_kernel, n_act=n_act),
        grid=(bsz, nl),
        in_specs=in_specs,
        out_specs=[pl.BlockSpec((None, tl, d), row),
                   pl.BlockSpec((None, tl, d), row),
                   pl.BlockSpec((N_EXPERTS, tl), lambda b, i: (0, b * nl + i)),
                   pl.BlockSpec((N_EXPERTS, tl), lambda b, i: (0, b * nl + i)),
                   pl.BlockSpec((None, N_EXPERTS, LANES), lambda b, i: (b * nl + i, 0, 0))],
        out_shape=[jax.ShapeDtypeStruct((bsz, l, d), f32),
                   jax.ShapeDtypeStruct((bsz, l, d), bf16),
                   jax.ShapeDtypeStruct((N_EXPERTS, bsz * l), f32),
                   jax.ShapeDtypeStruct((N_EXPERTS, bsz * l), f32),
                   jax.ShapeDtypeStruct((bsz * nl, N_EXPERTS, LANES), f32)],
        compiler_params=_cp("parallel", "parallel"),
        name="out_proj_route",
    )(*acts, *ws, x, g.reshape(1, d), gate, g2.reshape(1, d), shift, scale, rw_t,
      rb.reshape(N_EXPERTS, 1))


def _route(xv, g, shift, scale, rw_ref, rb_ref, h_ref, rk_ref, gt_ref, cm_ref):
    hf = _norm_mod(xv, g, shift, scale)
    tl = hf.shape[0]
    h_ref[...] = hf.astype(h_ref.dtype)
    per = N_EXPERTS // N_GROUPS
    logits = _dot3(rw_ref[...], hf, _NT)
    s3 = _sigmoid(logits).reshape(N_GROUPS, per, tl)
    b3 = s3 + rb_ref[...].reshape(N_GROUPS, per, 1)
    neg = -jnp.inf
    jdx = lax.broadcasted_iota(jnp.int32, b3.shape, 1)
    gdx = lax.broadcasted_iota(jnp.int32, b3.shape, 0)
    m1 = jnp.max(b3, axis=1, keepdims=True)
    f1 = jnp.min(jnp.where(b3 == m1, jdx, per), axis=1, keepdims=True)
    m2 = jnp.max(jnp.where(jdx == f1, neg, b3), axis=1, keepdims=True)
    grp = m1 + m2
    g1 = lax.broadcasted_iota(jnp.int32, grp.shape, 0)
    cnt = jnp.zeros(grp.shape, jnp.int32)
    for gp in range(N_GROUPS):
        rv = grp[gp:gp + 1]
        ahead = jnp.where(rv > grp, 1, jnp.where(rv == grp, jnp.where(g1 > gp, 1, 0), 0))
        cnt = cnt + ahead
    v = jnp.where(cnt < TOPK_GROUPS, b3, neg)
    eidx = gdx * per + jdx
    sel = jnp.zeros(b3.shape, f32)
    for _ in range(TOP_K):
        m = jnp.max(jnp.max(v, axis=1, keepdims=True), axis=0, keepdims=True)
        cand = jnp.where(v == m, eidx, N_EXPERTS)
        fi = jnp.min(jnp.min(cand, axis=1, keepdims=True), axis=0, keepdims=True)
        hit = eidx == fi
        sel = jnp.where(hit, 1.0, sel)
        v = jnp.where(hit, neg, v)
    ssel = sel * s3
    den = jnp.sum(jnp.sum(ssel, axis=1, keepdims=True), axis=0, keepdims=True)
    gt_ref[...] = ((ROUTED_SCALE * ssel) / den).reshape(N_EXPERTS, tl)
    sel2 = sel.reshape(N_EXPERTS, tl)
    r = lax.broadcasted_iota(jnp.int32, (MOE_SUB, MOE_SUB), 0)
    c = lax.broadcasted_iota(jnp.int32, (MOE_SUB, MOE_SUB), 1)
    before = jnp.where(r < c, 1.0, 0.0).astype(bf16)
    cmax = jnp.zeros((N_EXPERTS, 1), f32)
    for j in range(tl // MOE_SUB):
        sub = sel2[:, j * MOE_SUB:(j + 1) * MOE_SUB]
        rank = _dot(sub.astype(bf16), before)
        rk_ref[:, j * MOE_SUB:(j + 1) * MOE_SUB] = jnp.where(sub > 0.0, rank, -1.0)
        cmax = jnp.maximum(cmax, jnp.sum(sub, axis=1, keepdims=True))
    cm_ref[...] = jnp.broadcast_to(cmax, cm_ref.shape)


def _swiglu_act(hh):
    half = hh.shape[1] // 2
    return _silu(hh[:, :half]) * hh[:, half:]


def _moe_kernel(cnt_ref, ord_ref, h_ref, rk_ref, gt_ref, *refs):
    ng = MOE_GROUP
    gu_refs, dn_refs = refs[:ng], refs[ng:2 * ng]
    acc_ref, xg_ref, ys_ref, p_ref, gr_ref = refs[2 * ng:]
    tile = pl.program_id(0)
    grp = pl.program_id(1)
    tm, d = acc_ref.shape
    ns = tm // MOE_SUB
    eids = [ord_ref[tile, grp * ng + el] for el in range(ng)]

    @pl.when(grp == 0)
    def _():
        acc_ref[...] = jnp.zeros_like(acc_ref)

    def expert_ffn(el, win):
        hh = _dot(xg_ref[el, 0:ns * win, :], gu_refs[el][...])
        gr = gr_ref[el, 0:ns * win, :]
        act = _swiglu_act(hh) * jnp.concatenate([gr] * (hh.shape[1] // (2 * LANES)), axis=1)
        y = _dot(act.astype(bf16), dn_refs[el][...]).astype(bf16)
        for s in range(ns):
            ys_ref[s, el * win:(el + 1) * win, :] = y[s * win:(s + 1) * win]

    def one_pass(p, win):
        base = p * win
        riota = lax.broadcasted_iota(jnp.int32, (win, MOE_SUB), 0).astype(f32)
        for s in range(ns):
            cols = slice(s * MOE_SUB, (s + 1) * MOE_SUB)
            onehots = []
            for el in range(ng):
                row = pl.ds(eids[el], 1)
                hit = (rk_ref[row, cols] - base) == riota
                onehots.append(jnp.where(hit, 1.0, 0.0).astype(bf16))
                gsel = jnp.sum(jnp.where(hit, gt_ref[row, cols], 0.0), axis=1, keepdims=True)
                gr_ref[el, s * win:(s + 1) * win, :] = jnp.broadcast_to(gsel, (win, LANES))
            pm = jnp.concatenate(onehots, axis=0)
            p_ref[s, 0:ng * win, :] = pm
            gx = _dot(pm, h_ref[cols, :])
            for el in range(ng):
                xg_ref[el, s * win:(s + 1) * win, :] = gx[el * win:(el + 1) * win].astype(bf16)
        for el in range(ng):
            expert_ffn(el, win)
        for s in range(ns):
            acc_ref[s * MOE_SUB:(s + 1) * MOE_SUB, :] += _dot(
                p_ref[s, 0:ng * win, :], ys_ref[s, 0:ng * win, :], _TN)

    most = cnt_ref[tile, eids[0]]
    for el in range(1, ng):
        most = jnp.maximum(most, cnt_ref[tile, eids[el]])

    lo = 0
    for win in MOE_WINDOWS[:-1]:
        pl.when(jnp.logical_and(most > lo, most <= win))(functools.partial(one_pass, 0, win))
        lo = win
    big = MOE_WINDOWS[-1]

    def big_pass(p, carry):
        one_pass(p, big)
        return carry

    lax.fori_loop(0, jnp.where(most > lo, (most + big - 1) // big, 0), big_pass, 0)


def _moe_routed(h2, rk, gt, cmax, gu, dn, tm=2048):
    bsz, l, d = h2.shape
    tm = min(tm, l)
    nt = bsz * (l // tm)
    ne = gu.shape[0]
    ng = MOE_GROUP
    ns = tm // MOE_SUB
    wmax = MOE_WINDOWS[-1]
    counts = jnp.max(cmax[:, :, 0].reshape(nt, -1, ne), axis=1).astype(jnp.int32)
    order = jnp.argsort(-counts, axis=1).astype(jnp.int32)

    def expert_spec(arr, k):
        return pl.BlockSpec((None,) + arr.shape[1:], lambda t, e, cnt, order_ref: (order_ref[t, e * ng + k], 0, 0))

    tile_spec = pl.BlockSpec((tm, d), lambda t, e, *_: (t, 0))
    grid_spec = pltpu.PrefetchScalarGridSpec(
        num_scalar_prefetch=2,
        grid=(nt, ne // ng),
        in_specs=([tile_spec,
                   pl.BlockSpec((ne, tm), lambda t, e, *_: (0, t)),
                   pl.BlockSpec((ne, tm), lambda t, e, *_: (0, t))]
                  + [expert_spec(gu, k) for k in range(ng)]
                  + [expert_spec(dn, k) for k in range(ng)]),
        out_specs=tile_spec,
        scratch_shapes=[pltpu.VMEM((ng, ns * wmax, d), bf16),
                        pltpu.VMEM((ns, ng * wmax, d), bf16),
                        pltpu.VMEM((ns, ng * wmax, MOE_SUB), bf16),
                        pltpu.VMEM((ng, ns * wmax, LANES), f32)])
    return pl.pallas_call(
        _moe_kernel,
        grid_spec=grid_spec,
        out_shape=jax.ShapeDtypeStruct((bsz * l, d), f32),
        compiler_params=_cp("parallel", "arbitrary"),
        name="moe",
    )(counts, order, h2.reshape(bsz * l, d), rk, gt, *([gu] * ng), *([dn] * ng))


def _moe_finish_kernel(h_ref, r_ref, sgu_ref, sdn_ref, x_ref, g_ref, gate_ref, *refs, has_next):
    act = _swiglu_act(_dot(h_ref[...], sgu_ref[...]))
    mo = _dot(act.astype(bf16), sdn_ref[...]) + r_ref[...]
    y = mo * lax.rsqrt(jnp.mean(mo * mo, axis=-1, keepdims=True) + RMS_EPS) * g_ref[...]
    xn = x_ref[...] + gate_ref[...] * y
    if has_next:
        gn_ref, shn_ref, scn_ref, o_ref, hn_ref = refs
        hn_ref[...] = _norm_mod(xn, gn_ref[...], shn_ref[...], scn_ref[...]).astype(hn_ref.dtype)
    else:
        (o_ref,) = refs
    o_ref[...] = xn


def _moe_finish(h2, routed, sgu, sdn, x, g, gate, nxt=None, tl=512):
    bsz, l, d = x.shape
    tl = min(tl, l)
    row = lambda b, i: (b, i, 0)
    fix2 = lambda b, i: (0, 0)
    in_specs = [pl.BlockSpec((None, tl, d), row), pl.BlockSpec((None, tl, d), row),
                pl.BlockSpec(sgu.shape, fix2), pl.BlockSpec(sdn.shape, fix2),
                pl.BlockSpec((None, tl, d), row), pl.BlockSpec((1, d), fix2),
                pl.BlockSpec((None, 1, d), _bidx(gate))]
    args = [h2, routed.reshape(bsz, l, d), sgu, sdn, x, g.reshape(1, d), gate]
    out_specs, out_shape = [pl.BlockSpec((None, tl, d), row)], [jax.ShapeDtypeStruct((bsz, l, d), f32)]
    if nxt is not None:
        in_specs += [pl.BlockSpec((1, d), fix2), pl.BlockSpec((None, 1, d), _bidx(nxt[1])),
                     pl.BlockSpec((None, 1, d), _bidx(nxt[2]))]
        args += [nxt[0].reshape(1, d), nxt[1], nxt[2]]
        out_specs.append(pl.BlockSpec((None, tl, d), row))
        out_shape.append(jax.ShapeDtypeStruct((bsz, l, d), bf16))
    return pl.pallas_call(
        functools.partial(_moe_finish_kernel, has_next=nxt is not None),
        grid=(bsz, l // tl),
        in_specs=in_specs,
        out_specs=out_specs,
        out_shape=out_shape,
        compiler_params=_cp("parallel", "parallel"),
        name="moe_finish",
    )(*args)


def _filter_kernel(z_ref, w1_ref, b1_ref, w2_ref, b2_ref, w3_ref, win_ref, o_ref):
    hid = jnp.sin(FILTER_SIN_W * (_dot3(z_ref[...], w1_ref[...]) + b1_ref[...]))
    hid = jnp.sin(FILTER_SIN_W * (_dot3(hid, w2_ref[...]) + b2_ref[...]))
    o_ref[...] = _dot3(hid, w3_ref[...]) * win_ref[...]


def _filters(z, w1, b1, w2, b2, w3, window, tn=512):
    l, p = z.shape
    hdim = w1.shape[1]
    n = w3.shape[1]
    d = window.shape[1]
    nd = d // tn
    return pl.pallas_call(
        _filter_kernel,
        grid=(n // tn,),
        in_specs=[pl.BlockSpec((l, p), lambda j: (0, 0)),
                  pl.BlockSpec((p, hdim), lambda j: (0, 0)),
                  pl.BlockSpec((1, hdim), lambda j: (0, 0)),
                  pl.BlockSpec((hdim, hdim), lambda j: (0, 0)),
                  pl.BlockSpec((1, hdim), lambda j: (0, 0)),
                  pl.BlockSpec((hdim, tn), lambda j: (0, j)),
                  pl.BlockSpec((l, tn), lambda j: (0, j % nd))],
        out_specs=pl.BlockSpec((l, tn), lambda j: (0, j)),
        out_shape=jax.ShapeDtypeStruct((l, n), f32),
        compiler_params=_cp("arbitrary"),
        name="hyena_filter",
    )(z, w1, b1.reshape(1, hdim), w2, b2.reshape(1, hdim), w3, window)


def _dft_tables(l):
    n = 2 * l
    n1 = math.isqrt(n)
    assert n == n1 * n1 and n1 % 16 == 0
    na = l // n1
    ncp = -(-(n1 // 2 + 1) // 8) * 8
    a = np.arange(na)
    b = np.arange(n1)
    c = np.arange(ncp)
    th = 2.0 * np.pi * ((n1 * a[None, None, :] + b[:, None, None]) * c[None, :, None]) / n
    t1 = np.concatenate([np.cos(th), -np.sin(th)], axis=1)
    ph = 2.0 * np.pi * (b[:, None] * b[None, :]) / n1
    cs, sn = np.cos(ph), np.sin(ph)
    a3 = np.block([[cs, sn], [-sn, cs]])
    a3i = np.block([[cs, -sn], [sn, cs]])
    a2 = np.arange(na) + na // 2
    th2 = 2.0 * np.pi * ((n1 * a2[None, :, None] + b[:, None, None]) * c[None, None, :]) / n
    wc = np.where((c == 0) | (c == n1 // 2), 1.0, np.where(c < n1 // 2, 2.0, 0.0))[None, None, :]
    t2 = np.concatenate([wc * np.cos(th2), -wc * np.sin(th2)], axis=2)
    return [jnp.asarray(t, f32).astype(bf16) for t in (t1, a3, a3i, t2)]


def _fft_dims(t1):
    n1, ncp2, na = t1.shape
    ncp = ncp2 // 2
    return n1, ncp, na, 2 * n1 + FFT_PAD, 2 * ncp + FFT_PAD, n1 + FFT_PAD


def _ld(ref, rows):
    return jnp.concatenate([ref[j, rows, :] for j in range(ref.shape[0])], axis=1)


def _st(ref, rows, val):
    for j in range(ref.shape[0]):
        ref[j, rows, :] = val[:, j * LANES:(j + 1) * LANES]


def _dft_forward(uf_ref, t1_ref, zs_ref):
    n1, ncp, na, sb, _, su = _fft_dims(t1_ref)
    for b in range(n1):
        ub = _ld(uf_ref, pl.ds(b, na, stride=su)).astype(bf16)
        zb = _dot(t1_ref[b], ub)
        _st(zs_ref, pl.ds(b, ncp, stride=sb), zb[:ncp])
        _st(zs_ref, pl.ds(n1 + b, ncp, stride=sb), zb[ncp:])


def _spectrum_kernel(f_ref, t1_ref, a3_ref, o_ref, uf_ref, zs_ref, *, scale):
    n1, ncp, na, sb, _, su = _fft_dims(t1_ref)
    for a in range(na):
        _st(uf_ref, pl.ds(a * su, n1), f_ref[pl.ds(a * n1, n1), :])
    _dft_forward(uf_ref, t1_ref, zs_ref)
    a3 = a3_ref[...]
    for c in range(ncp):
        zc = _ld(zs_ref, pl.ds(c * sb, 2 * n1)).astype(bf16)
        o_ref[c] = (_dot(a3, zc) * scale).astype(o_ref.dtype)


def _spectrum(filt, tabs, dt=256):
    l, n = filt.shape
    t1, a3, _, _ = tabs
    n1, ncp, na, sb, _, su = _fft_dims(t1)
    nj = dt // LANES
    return pl.pallas_call(
        functools.partial(_spectrum_kernel, scale=1.0 / (2 * l)),
        grid=(n // dt,),
        in_specs=[pl.BlockSpec((l, dt), lambda j: (0, j)),
                  pl.BlockSpec(t1.shape, lambda j: (0, 0, 0)),
                  pl.BlockSpec(a3.shape, lambda j: (0, 0))],
        out_specs=pl.BlockSpec((ncp, 2 * n1, dt), lambda j: (0, 0, j)),
        out_shape=jax.ShapeDtypeStruct((ncp, 2 * n1, n), bf16),
        scratch_shapes=[pltpu.VMEM((nj, na * su, LANES), f32),
                        pltpu.VMEM((nj, ncp * sb, LANES), f32)],
        compiler_params=_cp("arbitrary"),
        name="hyena_spectrum",
    )(filt, t1, a3)


def _fftconv_kernel(u_ref, xg_ref, kf_ref, fb_ref, t1_ref, a3_ref, a3i_ref, t2_ref, o_ref,
                    uf_ref, zs_ref, qs_ref, y_ref):
    n1, ncp, na, sb, sq, su = _fft_dims(t1_ref)
    for a in range(na):
        _st(uf_ref, pl.ds(a * su, n1), u_ref[pl.ds(a * n1, n1), :].astype(f32))
    _dft_forward(uf_ref, t1_ref, zs_ref)
    a3 = a3_ref[...]
    a3i = a3i_ref[...]
    for c in range(ncp):
        zc = _ld(zs_ref, pl.ds(c * sb, 2 * n1)).astype(bf16)
        xc = _dot(a3, zc)
        kc = kf_ref[c].astype(f32)
        xr, xi = xc[:n1], xc[n1:]
        kr, ki = kc[:n1], kc[n1:]
        pc = jnp.concatenate([xr * kr - xi * ki, xr * ki + xi * kr], axis=0).astype(bf16)
        qc = _dot(a3i, pc)
        _st(qs_ref, pl.ds(c, n1, stride=sq), qc[:n1])
        _st(qs_ref, pl.ds(ncp + c, n1, stride=sq), qc[n1:])
    for b in range(n1):
        qb = _ld(qs_ref, pl.ds(b * sq, 2 * ncp)).astype(bf16)
        _st(y_ref, pl.ds(b, na, stride=su), _dot(t2_ref[b], qb))
    fb = fb_ref[...]
    for a in range(na):
        rows = pl.ds(a * n1, n1)
        uv = _ld(uf_ref, pl.ds(a * su, n1))
        yv = _ld(y_ref, pl.ds(a * su, n1))
        o_ref[rows, :] = (xg_ref[rows, :].astype(f32) * (yv + uv * fb)).astype(o_ref.dtype)


def _fftconv(u, u_col, xg, xg_col, kf, kf_col, fbias, tabs, d, dt=256):
    bsz, l, _ = u.shape
    t1, a3, a3i, t2 = tabs
    n1, ncp, na, sb, sq, su = _fft_dims(t1)
    nd = d // dt
    nj = dt // LANES
    uo, go, ko = u_col // dt, xg_col // dt, kf_col // dt
    return pl.pallas_call(
        _fftconv_kernel,
        grid=(nd, bsz),
        in_specs=[pl.BlockSpec((None, l, dt), lambda j, b: (b, 0, j + uo)),
                  pl.BlockSpec((None, l, dt), lambda j, b: (b, 0, j + go)),
                  pl.BlockSpec((ncp, 2 * n1, dt), lambda j, b: (0, 0, j + ko)),
                  pl.BlockSpec((1, dt), lambda j, b: (0, j)),
                  pl.BlockSpec(t1.shape, lambda j, b: (0, 0, 0)),
                  pl.BlockSpec(a3.shape, lambda j, b: (0, 0)),
                  pl.BlockSpec(a3i.shape, lambda j, b: (0, 0)),
                  pl.BlockSpec(t2.shape, lambda j, b: (0, 0, 0))],
        out_specs=pl.BlockSpec((None, l, dt), lambda j, b: (b, 0, j)),
        out_shape=jax.ShapeDtypeStruct((bsz, l, d), bf16),
        scratch_shapes=[pltpu.VMEM((nj, na * su, LANES), f32),
                        pltpu.VMEM((nj, ncp * sb, LANES), f32),
                        pltpu.VMEM((nj, n1 * sq, LANES), f32),
                        pltpu.VMEM((nj, na * su, LANES), f32)],
        compiler_params=_cp("parallel", "arbitrary"),
        name="hyena_fftconv",
    )(u, xg, kf, fbias.reshape(1, d), t1, a3, a3i, t2)


def _rope_tables(l):
    rows = l // GRID_W
    row = jnp.repeat(jnp.arange(rows), GRID_W)
    col = jnp.tile(jnp.arange(GRID_W), rows)
    inv = ROPE_BASE ** (-jnp.arange(ROPE_AXIS_PAIRS, dtype=f32) / ROPE_AXIS_PAIRS)
    ang = jnp.stack([row, col], axis=-1).astype(f32)[..., None] * inv
    ang = jnp.broadcast_to(ang[:, :, None, :], (l, 2, 2, ROPE_AXIS_PAIRS)).reshape(l, A_DQK)
    reps = A_QW // A_DQK
    return jnp.tile(jnp.cos(ang), (1, reps)), jnp.tile(jnp.sin(ang), (1, reps))


def _rotate_cols(w):
    j = np.arange(w.shape[1])
    lo = (j % (2 * ROPE_AXIS_PAIRS)) < ROPE_AXIS_PAIRS
    perm = np.where(lo, j + ROPE_AXIS_PAIRS, j - ROPE_AXIS_PAIRS)
    sign = np.where(lo, -1.0, 1.0).astype(np.float32)
    return w[:, perm] * sign


def _gate_cols(w_g, b_g):
    idx_i = np.array([d * 2 * B_HEADS + hd for d in range(2) for hd in range(B_HEADS)])
    idx_f = idx_i + B_HEADS
    pad = LANES - _NCHAIN
    k = w_g.shape[0]
    w = jnp.concatenate([w_g[:, idx_i], jnp.zeros((k, pad), f32),
                         w_g[:, idx_f], jnp.zeros((k, pad), f32)], axis=1)
    b = jnp.concatenate([b_g[idx_i], jnp.zeros((pad,), f32), b_g[idx_f], jnp.zeros((pad,), f32)])
    return w, b


def _hyena_consts(l, d):
    j = jnp.arange(l, dtype=f32)
    bands = (POS_EMB_DIM - 1) // 2
    freqs = jnp.linspace(1e-4, bands - 1, bands, dtype=f32)
    ang = (2.0 * math.pi / l) * j[:, None] * freqs[None, :]
    z = jnp.concatenate([(j / (l - 1))[:, None], jnp.cos(ang), -jnp.sin(ang)], axis=-1)
    dist = jnp.abs(j - l // 2) / (l // 2)
    max_decay = math.log(DECAY_TARGET) / DECAY_FAST_PCT
    min_decay = math.log(DECAY_TARGET) / DECAY_SLOW_PCT
    deltas = jnp.abs(jnp.linspace(min_decay, max_decay, d, dtype=f32))
    window = jnp.exp(-dist[:, None] * deltas[None, :])
    return z, window


def _ab_mixer(h, hc, w_in, conv_w, conv_b, gate_b, lam_vecs, g_a, g_b, w_out, lam_init):
    s = h.shape[1]
    w = B_WIDTH
    o = 0
    cols = {}
    for name, width in (("aq", A_QW), ("bq", w), ("bo", w), ("ak", A_QW), ("av", A_VW),
                        ("bk", w), ("bv", w), ("g", 4 * B_HEADS)):
        cols[name] = w_in[:, o:o + width]
        o += width
    cos, sin = _rope_tables(s)
    cat = lambda *ws: jnp.concatenate(ws, axis=1).astype(bf16)
    q = _mm(h, cat(cols["aq"], _rotate_cols(cols["aq"])), rope=(cos, sin, A_DQK ** -0.5))
    k = _mm(h, cat(cols["ak"], _rotate_cols(cols["ak"])), rope=(cos, sin, 1.0))
    qk = _mm(h, cat(cols["bq"], cols["bk"]), conv=(conv_w, conv_b, True))
    vvo = _mm(h, cat(cols["av"], cols["bv"], cols["bo"]))
    wg, bg = _gate_cols(cols["g"], gate_b)
    gates = _mm(h, wg.astype(bf16), out_dtype=f32, bias=bg, tn=2 * LANES)
    ckv = _mm(hc, cat(cols["ak"], cols["av"], cols["bv"]))
    cbk = _mm(hc, cols["bk"].astype(bf16), conv=(conv_w[:, w:], conv_b[w:], True))
    cg = _mm(hc, wg.astype(bf16), out_dtype=f32, bias=bg, tn=2 * LANES)
    out_a = _attn(lam_vecs, q, k, vvo, ckv, g_a, lam_init)
    out_b = _mlstm(qk, vvo, gates, cbk, ckv, cg, g_b)
    wo = w_out.astype(bf16)
    return [out_a, out_b], [wo[:A_VW], wo[A_VW:]]


def _hyena_mixer(h, w_in, conv_w, conv_b, fw1, fb1, fw2, fb2, fw3, fbias, w_out):
    _, l, d = h.shape
    u = _mm(h, w_in.astype(bf16), conv=(conv_w, conv_b, False))
    z, window = _hyena_consts(l, d)
    pz, ph = LANES - z.shape[1], LANES - fw1.shape[1]
    filt = _filters(jnp.pad(z, ((0, 0), (0, pz))), jnp.pad(fw1, ((0, pz), (0, ph))),
                    jnp.pad(fb1, (0, ph)), jnp.pad(fw2, ((0, ph), (0, ph))), jnp.pad(fb2, (0, ph)),
                    jnp.pad(fw3, ((0, ph), (0, 0))), window)
    tabs = _dft_tables(l)
    kf = _spectrum(filt, tabs)
    zz = _fftconv(u, 0, u, d, kf, 0, fbias[0], tabs, d)
    y = _fftconv(zz, 0, u, 2 * d, kf, d, fbias[1], tabs, d)
    return [y], [w_out.astype(bf16)]


def kernel(x, c, ctx, c_ctx, w_mod, b_mod, norm_g, w_in_ab, conv_ab_w, conv_ab_b, gate_b_ab, diff_lambda, head_g_a, head_g_b, w_out_ab, w_in_hy, conv_hy_w, conv_hy_b, filt_w1, filt_b1, filt_w2, filt_b2, filt_w3, filt_bias, w_out_hy, router_w, router_b, exp_gu, exp_down, sh_gu, sh_down):
    bsz, s, d = x.shape
    depth = w_mod.shape[0]
    rows = -(-(bsz + 1) // 8) * 8
    cc = jnp.concatenate([c, c_ctx[None, :], jnp.zeros((rows - bsz - 1, d), f32)], axis=0)
    mods = [_mod(cc, w_mod[l], b_mod[l]) for l in range(depth)]
    vec = lambda l, i: mods[l][:bsz, i * d:(i + 1) * d].reshape(bsz, 1, d)
    h = _norm(x, norm_g[0, 0], vec(0, 0), vec(0, 1))
    for l in range(depth):
        g_m, sh_f, sc_f, g_f = [vec(l, i) for i in range(2, 6)]
        if l % 2 == 0:
            e = l // 2
            lam_init = 0.8 - 0.6 * math.exp(-0.3 * l)
            row_c = lambda i: mods[l][bsz:bsz + 1, i * d:(i + 1) * d].reshape(1, 1, d)
            hc = _norm(ctx, norm_g[l, 0], row_c(0), row_c(1))
            acts, ws = _ab_mixer(h, hc, w_in_ab[e], conv_ab_w[e], conv_ab_b[e], gate_b_ab[e],
                                 diff_lambda[e], head_g_a[e], head_g_b[e], w_out_ab[e], lam_init)
        else:
            o = l // 2
            acts, ws = _hyena_mixer(h, w_in_hy[o], conv_hy_w[o], conv_hy_b[o], filt_w1[o], filt_b1[o],
                                    filt_w2[o], filt_b2[o], filt_w3[o], filt_bias[o], w_out_hy[o])
        x, h2, rk, gt, cmax = _out_proj_route(acts, ws, x, norm_g[l, 1], g_m, norm_g[l, 2], sh_f, sc_f,
                                              router_w[l].T, router_b[l])
        nxt = (norm_g[l + 1, 0], vec(l + 1, 0), vec(l + 1, 1)) if l + 1 < depth else None
        routed = _moe_routed(h2, rk, gt, cmax, exp_gu[l].astype(bf16), exp_down[l].astype(bf16))
        outs = _moe_finish(h2, routed, sh_gu[l].astype(bf16), sh_down[l].astype(bf16), x, norm_g[l, 3],
                           g_f, nxt)
        x = outs[0]
        if nxt is not None:
            h = outs[1]
    return x
```

```python
import functools
import math

import numpy as np
import jax
import jax.numpy as jnp
from jax import lax
from jax.experimental import pallas as pl
from jax.experimental.pallas import tpu as pltpu

f32 = jnp.float32
bf16 = jnp.bfloat16

RMS_EPS = 1e-6
A_HEADS = 4
A_DQK = 64
A_DV = 128
B_HEADS = 4
B_DH = 128
B_WIDTH = B_HEADS * B_DH
A_QW = A_HEADS * 2 * A_DQK
A_VW = A_HEADS * A_DV
GRID_W = 64
ROPE_BASE = 10000.0
ROPE_AXIS_PAIRS = A_DQK // 4
N_EXPERTS = 64
TOP_K = 8
N_GROUPS = 8
TOPK_GROUPS = 4
D_EXPERT = 256
ROUTED_SCALE = 2.5
HY_ORDER = 2
POS_EMB_DIM = 33
FILTER_SIN_W = 1.0
DECAY_FAST_PCT = 0.3
DECAY_SLOW_PCT = 1.5
DECAY_TARGET = 1e-2

LANES = 128
VMEM_LIMIT = 56 * 1024 * 1024
MLSTM_CHUNK = 256
FFT_PAD = 8
MOE_SUB = 256
MOE_WINDOWS = (16, 32, 48, 64)
MOE_GROUP = 4


def _cp(*sem):
    return pltpu.CompilerParams(dimension_semantics=sem, vmem_limit_bytes=VMEM_LIMIT)


def _split_bf16(a):
    hi = a.astype(bf16)
    lo = (a - hi.astype(f32)).astype(bf16)
    return hi, lo


def _dot(a, b, dims=(((1,), (0,)), ((), ()))):
    return lax.dot_general(a, b, dims, preferred_element_type=f32)


_NT = (((1,), (1,)), ((), ()))
_TN = (((0,), (0,)), ((), ()))


def _dot3(a, b, dims=(((1,), (0,)), ((), ()))):
    ah, al = _split_bf16(a)
    bh, bl = _split_bf16(b)
    return _dot(ah, bh, dims) + (_dot(ah, bl, dims) + _dot(al, bh, dims))


def _silu(v):
    return v / (1.0 + jnp.exp(-v))


def _sigmoid(v):
    return 1.0 / (1.0 + jnp.exp(-v))


def _log_sigmoid(v):
    return jnp.minimum(v, 0.0) - jnp.log(1.0 + jnp.exp(-jnp.abs(v)))


def _mod_kernel(c_ref, w_ref, b_ref, o_ref):
    o_ref[...] = _dot3(_silu(c_ref[...]), w_ref[...]) + b_ref[...]


def _mod(cc, w, b):
    rows, d = cc.shape
    n = w.shape[1]
    tn = d
    return pl.pallas_call(
        _mod_kernel,
        grid=(n // tn,),
        in_specs=[pl.BlockSpec((rows, d), lambda j: (0, 0)),
                  pl.BlockSpec((d, tn), lambda j: (0, j)),
                  pl.BlockSpec((1, tn), lambda j: (0, j))],
        out_specs=pl.BlockSpec((rows, tn), lambda j: (0, j)),
        out_shape=jax.ShapeDtypeStruct((rows, n), f32),
        compiler_params=_cp("arbitrary"),
        name="mod",
    )(cc, w, b.reshape(1, n))


def _norm_mod(xv, g, shift, scale):
    y = xv * lax.rsqrt(jnp.mean(xv * xv, axis=-1, keepdims=True) + RMS_EPS)
    return (y * g) * (1.0 + scale) + shift


def _norm_kernel(x_ref, g_ref, sh_ref, sc_ref, o_ref):
    o_ref[...] = _norm_mod(x_ref[...], g_ref[...], sh_ref[...], sc_ref[...]).astype(o_ref.dtype)


def _bidx(arr):
    if arr.shape[0] == 1:
        return lambda b, *_: (0, 0, 0)
    return lambda b, *_: (b, 0, 0)


def _norm(x, g, shift, scale, tl=512):
    bsz, l, d = x.shape
    tl = min(tl, l)
    return pl.pallas_call(
        _norm_kernel,
        grid=(bsz, l // tl),
        in_specs=[pl.BlockSpec((None, tl, d), lambda b, i: (b, i, 0)),
                  pl.BlockSpec((1, d), lambda b, i: (0, 0)),
                  pl.BlockSpec((None, 1, d), _bidx(shift)),
                  pl.BlockSpec((None, 1, d), _bidx(scale))],
        out_specs=pl.BlockSpec((None, tl, d), lambda b, i: (b, i, 0)),
        out_shape=jax.ShapeDtypeStruct((bsz, l, d), bf16),
        compiler_params=_cp("parallel", "parallel"),
        name="norm",
    )(x, g.reshape(1, d), shift, scale)


def _mm_plain_kernel(h_ref, w_ref, b_ref, o_ref):
    o_ref[...] = (_dot(h_ref[...], w_ref[...]) + b_ref[...]).astype(o_ref.dtype)


def _mm_rope_kernel(h_ref, w_ref, cos_ref, sin_ref, o_ref, *, scale):
    p = _dot(h_ref[...], w_ref[...])
    n = o_ref.shape[-1]
    o_ref[...] = ((p[:, :n] * cos_ref[...] + p[:, n:] * sin_ref[...]) * scale).astype(o_ref.dtype)


def _mm_conv_kernel(h_ref, w_ref, cw_ref, cb_ref, o_ref, scr_ref, *, act, rc):
    l = h_ref.shape[0]
    w = w_ref[...]
    w0, w1, w2, cb = cw_ref[0:1, :], cw_ref[1:2, :], cw_ref[2:3, :], cb_ref[...]
    halo = 16
    zrow = jnp.zeros((8, o_ref.shape[1]), f32)
    for c in range(l // rc):
        lo, hi = max(c * rc - halo, 0), min((c + 1) * rc + halo, l)
        n = hi - lo
        scr = scr_ref.at[c % 2]
        scr[8:8 + n, :] = _dot(h_ref[lo:hi, :], w)
        if lo == 0:
            scr[0:8, :] = zrow
        if hi == l:
            scr[8 + n:16 + n, :] = zrow
        off = 8 + c * rc - lo
        y = (scr[off - 1:off - 1 + rc, :] * w0 + scr[off:off + rc, :] * w1
             + scr[off + 1:off + 1 + rc, :] * w2 + cb)
        o_ref[c * rc:(c + 1) * rc, :] = (_silu(y) if act else y).astype(o_ref.dtype)


def _mm(h, w, *, out_dtype=bf16, bias=None, rope=None, conv=None, tl=512, tn=512):
    bsz, l, k = h.shape
    n = w.shape[1]
    if rope is not None:
        cos, sin, scale = rope
        n_out = n // 2
        tl = min(tl, l)
        return pl.pallas_call(
            functools.partial(_mm_rope_kernel, scale=scale),
            grid=(l // tl, bsz),
            in_specs=[pl.BlockSpec((None, tl, k), lambda i, b: (b, i, 0)),
                      pl.BlockSpec((k, n), lambda i, b: (0, 0)),
                      pl.BlockSpec((tl, n_out), lambda i, b: (i, 0)),
                      pl.BlockSpec((tl, n_out), lambda i, b: (i, 0))],
            out_specs=pl.BlockSpec((None, tl, n_out), lambda i, b: (b, i, 0)),
            out_shape=jax.ShapeDtypeStruct((bsz, l, n_out), out_dtype),
            compiler_params=_cp("parallel", "parallel"),
            name="mm_rope",
        )(h, w, cos, sin)
    tn = min(tn, n)
    if conv is not None:
        cw, cb, act = conv
        rc = min(512, l)
        return pl.pallas_call(
            functools.partial(_mm_conv_kernel, act=act, rc=rc),
            grid=(bsz, n // tn),
            in_specs=[pl.BlockSpec((None, l, k), lambda b, j: (b, 0, 0)),
                      pl.BlockSpec((k, tn), lambda b, j: (0, j)),
                      pl.BlockSpec((3, tn), lambda b, j: (0, j)),
                      pl.BlockSpec((1, tn), lambda b, j: (0, j))],
            out_specs=pl.BlockSpec((None, l, tn), lambda b, j: (b, 0, j)),
            out_shape=jax.ShapeDtypeStruct((bsz, l, n), out_dtype),
            scratch_shapes=[pltpu.VMEM((2, rc + 48, tn), f32)],
            compiler_params=_cp("parallel", "arbitrary"),
            name="mm_conv",
        )(h, w, cw, cb.reshape(1, n))
    if bias is None:
        bias = jnp.zeros((n,), f32)
    if k * n * w.dtype.itemsize <= 4 * 1024 * 1024:
        tn = n
    tl = min(tl, l)
    return pl.pallas_call(
        _mm_plain_kernel,
        grid=(bsz, l // tl, n // tn),
        in_specs=[pl.BlockSpec((None, tl, k), lambda b, i, j: (b, i, 0)),
                  pl.BlockSpec((k, tn), lambda b, i, j: (0, j)),
                  pl.BlockSpec((1, tn), lambda b, i, j: (0, j))],
        out_specs=pl.BlockSpec((None, tl, tn), lambda b, i, j: (b, i, j)),
        out_shape=jax.ShapeDtypeStruct((bsz, l, n), out_dtype),
        compiler_params=_cp("parallel", "parallel", "arbitrary"),
        name="mm_plain",
    )(h, w, bias.reshape(1, n))


def _attn_kernel(lv_ref, q_ref, kc_ref, k_ref, vc_ref, v_ref, g_ref, o_ref, *, lam_init):
    tq = q_ref.shape[0]
    lv = lv_ref[...]
    lam = (jnp.exp(jnp.sum(lv[0:1] * lv[1:2], axis=1, keepdims=True))
           - jnp.exp(jnp.sum(lv[2:3] * lv[3:4], axis=1, keepdims=True)) + lam_init)
    first = lax.broadcasted_iota(jnp.int32, (tq, A_DV), 1) < A_DQK
    one0 = jnp.where(lax.broadcasted_iota(jnp.int32, (1, A_DV), 1) == 0, 1.0, 0.0).astype(bf16)
    ones_c = jnp.broadcast_to(one0, (kc_ref.shape[0], A_DV))
    ones_l = jnp.broadcast_to(one0, (k_ref.shape[0], A_DV))
    for hd in range(A_HEADS):
        cs = slice(hd * A_DV, (hd + 1) * A_DV)
        qh = q_ref[:, cs]
        zero = jnp.zeros_like(qh)
        q2 = jnp.concatenate([jnp.where(first, qh, zero), jnp.where(first, zero, qh)], axis=0)
        s_c = _dot(q2, kc_ref[:, cs], _NT)
        s_l = _dot(q2, k_ref[:, cs], _NT)
        m = jnp.maximum(jnp.max(s_c, axis=1, keepdims=True), jnp.max(s_l, axis=1, keepdims=True))
        p_c = jnp.exp((s_c - m).astype(bf16))
        p_l = jnp.exp((s_l - m).astype(bf16))
        oa = (_dot(p_c, jnp.concatenate([vc_ref[:, cs], ones_c], axis=1))
              + _dot(p_l, jnp.concatenate([v_ref[:, cs], ones_l], axis=1)))
        on = oa[:, :A_DV] * (1.0 / oa[:, A_DV:A_DV + 1])
        o = on[:tq] - lam * on[tq:]
        o = o * lax.rsqrt(jnp.mean(o * o, axis=1, keepdims=True) + RMS_EPS)
        o_ref[:, cs] = (o * g_ref[:, cs] * (1.0 - lam_init)).astype(o_ref.dtype)


def _attn(lv, q, k, vvo, ckv, g_a, lam_init, tq=256):
    bsz, s, _ = q.shape
    lc = ckv.shape[1]
    tq = min(tq, s)
    w = A_QW
    return pl.pallas_call(
        functools.partial(_attn_kernel, lam_init=lam_init),
        grid=(bsz, s // tq),
        in_specs=[pl.BlockSpec(lv.shape, lambda b, i: (0, 0)),
                  pl.BlockSpec((None, tq, w), lambda b, i: (b, i, 0)),
                  pl.BlockSpec((None, lc, w), lambda b, i: (b, 0, 0)),
                  pl.BlockSpec((None, s, w), lambda b, i: (b, 0, 0)),
                  pl.BlockSpec((None, lc, w), lambda b, i: (b, 0, 1)),
                  pl.BlockSpec((None, s, w), lambda b, i: (b, 0, 0)),
                  pl.BlockSpec((1, w), lambda b, i: (0, 0))],
        out_specs=pl.BlockSpec((None, tq, w), lambda b, i: (b, i, 0)),
        out_shape=jax.ShapeDtypeStruct((bsz, s, w), bf16),
        compiler_params=_cp("parallel", "arbitrary"),
        name="diff_attn",
    )(lv, q, ckv, k, ckv, vvo, g_a.reshape(1, w))


_LN_QSCALE = math.log(B_DH ** -0.5)
_NCHAIN = 2 * B_HEADS


def _chunk_gate_sums(gi, gf, tri):
    lf = _log_sigmoid(gf)
    hi, lo = _split_bf16(lf)
    cum = _dot(tri, hi) + _dot(tri, lo)
    t = gf.shape[0]
    tot = cum[t - 1:t, :]
    rcum = tot - cum + lf
    fwd = lax.broadcasted_iota(jnp.int32, gf.shape, 1) < B_HEADS
    bd = jnp.where(fwd, cum, rcum)
    return bd, tot, (bd - gi).T


def _lower_tri(t):
    r = lax.broadcasted_iota(jnp.int32, (t, t), 0)
    c = lax.broadcasted_iota(jnp.int32, (t, t), 1)
    return r, c


def _ones_block(t):
    one0 = jnp.where(lax.broadcasted_iota(jnp.int32, (1, B_DH), 1) == 0, 1.0, 0.0).astype(bf16)
    return jnp.broadcast_to(one0, (t, B_DH))


def _absorb(c_ref, m_ref, ch, x_row, tot_c, kb, vaug):
    m_prev = m_ref[ch][:, 0:1]
    g = tot_c - x_row
    m_new = jnp.maximum(tot_c + m_prev, jnp.max(g, axis=1, keepdims=True))
    wgt = jnp.exp(g - m_new)
    decay = jnp.exp(tot_c + m_prev - m_new)
    kw_t = kb.astype(f32).T * wgt
    c_ref[ch] = decay * c_ref[ch] + _dot(kw_t.astype(bf16), vaug)
    m_ref[ch] = jnp.broadcast_to(m_new, m_ref.shape[1:])


def _mlstm_kernel(qk_ref, vvo_ref, g_ref, ck_ref, ckv_ref, cg_ref, gb_ref, o_ref,
                  hf_ref, hb_ref, c_ref, m_ref, *, tc):
    s = o_ref.shape[0]
    lc = ck_ref.shape[0]
    nc = s // tc
    w = B_WIDTH
    dh = B_DH

    c_ref[...] = jnp.zeros_like(c_ref)
    m_ref[...] = jnp.zeros_like(m_ref)

    r, cidx = _lower_tri(lc)
    tri_c = jnp.where(cidx <= r, 1.0, 0.0).astype(bf16)
    cg = cg_ref[...]
    _, tot, xt = _chunk_gate_sums(cg[:, :LANES], cg[:, LANES:], tri_c)
    ones_c = _ones_block(lc)
    for ch in range(_NCHAIN):
        hs = slice((ch % B_HEADS) * dh, (ch % B_HEADS + 1) * dh)
        vs = slice(2 * w + (ch % B_HEADS) * dh, 2 * w + (ch % B_HEADS + 1) * dh)
        _absorb(c_ref, m_ref, ch, xt[ch:ch + 1, :], tot[:, ch:ch + 1], ck_ref[:, hs],
                jnp.concatenate([ckv_ref[:, vs], ones_c], axis=1))
    ones_t = _ones_block(tc)

    r, cidx = _lower_tri(tc)
    tri = jnp.where(cidx <= r, 1.0, 0.0).astype(bf16)
    causal = cidx <= r
    anti = cidx >= r

    def step(i, carry):
        for d in range(2):
            row0 = pl.multiple_of((i if d == 0 else nc - 1 - i) * tc, tc)
            rows = pl.ds(row0, tc)
            gch = g_ref[rows, :]
            gi = gch[:, :LANES]
            bd, tot, xt = _chunk_gate_sums(gi, gch[:, LANES:], tri)
            mask = causal if d == 0 else anti
            dst = hf_ref if d == 0 else hb_ref
            for hd in range(B_HEADS):
                ch = d * B_HEADS + hd
                hs = slice(hd * dh, (hd + 1) * dh)
                qb = qk_ref[rows, hs]
                kb = qk_ref[rows, slice(w + hd * dh, w + (hd + 1) * dh)]
                vaug = jnp.concatenate([vvo_ref[rows, slice(w + hd * dh, w + (hd + 1) * dh)], ones_t],
                                       axis=1)
                bcol = bd[:, ch:ch + 1]
                x_row = xt[ch:ch + 1, :]
                dmat = jnp.where(mask, bcol - x_row, -jnp.inf)
                m_prev = m_ref[ch][:, 0:1]
                inter = bcol + m_prev
                m_t = jnp.maximum(inter, jnp.max(dmat, axis=1, keepdims=True))
                e = jnp.exp(dmat - m_t + _LN_QSCALE)
                smat = _dot(qb, kb, _NT) * e
                sc = jnp.exp(inter - m_t + _LN_QSCALE)
                both = sc * _dot(qb, c_ref[ch].astype(bf16)) + _dot(smat.astype(bf16), vaug)
                den = both[:, dh:dh + 1]
                dst[rows, hs] = both[:, :dh] * (1.0 / jnp.maximum(jnp.abs(den), jnp.exp(-m_t)))
                _absorb(c_ref, m_ref, ch, x_row, tot[:, ch:ch + 1], kb, vaug)
        return carry

    lax.fori_loop(0, nc, step, 0)

    for hd in range(B_HEADS):
        hs = slice(hd * dh, (hd + 1) * dh)
        hsum = hf_ref[:, hs] + hb_ref[:, hs]
        hn = hsum * lax.rsqrt(jnp.mean(hsum * hsum, axis=1, keepdims=True) + RMS_EPS)
        og = _sigmoid(vvo_ref[:, slice(2 * w + hd * dh, 2 * w + (hd + 1) * dh)].astype(f32))
        o_ref[:, hs] = (hn * gb_ref[:, hs] * og).astype(o_ref.dtype)


def _mlstm(qk, vvo, gates, cbk, ckv, cg, g_b):
    bsz, s, _ = qk.shape
    lc = cbk.shape[1]
    w = B_WIDTH
    tc = min(MLSTM_CHUNK, s)
    return pl.pallas_call(
        functools.partial(_mlstm_kernel, tc=tc),
        grid=(bsz,),
        in_specs=[pl.BlockSpec((None, s, 2 * w), lambda b: (b, 0, 0)),
                  pl.BlockSpec((None, s, 3 * w), lambda b: (b, 0, 0)),
                  pl.BlockSpec((None, s, 2 * LANES), lambda b: (b, 0, 0)),
                  pl.BlockSpec((None, lc, w), lambda b: (b, 0, 0)),
                  pl.BlockSpec((None, lc, 3 * w), lambda b: (b, 0, 0)),
                  pl.BlockSpec((None, lc, 2 * LANES), lambda b: (b, 0, 0)),
                  pl.BlockSpec((1, w), lambda b: (0, 0))],
        out_specs=pl.BlockSpec((None, s, w), lambda b: (b, 0, 0)),
        out_shape=jax.ShapeDtypeStruct((bsz, s, w), bf16),
        scratch_shapes=[pltpu.VMEM((s, w), f32), pltpu.VMEM((s, w), f32),
                        pltpu.VMEM((_NCHAIN, B_DH, 2 * B_DH), f32),
                        pltpu.VMEM((_NCHAIN, 1, LANES), f32)],
        compiler_params=_cp("arbitrary"),
        name="mlstm",
    )(qk, vvo, gates, cbk, ckv, cg, g_b.reshape(1, w))


def _out_kernel(*refs, n_act):
    acts = refs[:n_act]
    ws = refs[n_act:2 * n_act]
    (x_ref, g_ref, gate_ref, g2_ref, sh_ref, sc_ref, rw_ref, rb_ref,
     o_ref, h_ref, rk_ref, gt_ref, cm_ref) = refs[2 * n_act:]
    mix = _dot(acts[0][...], ws[0][...])
    for a, wr in zip(acts[1:], ws[1:]):
        mix = mix + _dot(a[...], wr[...])
    y = mix * lax.rsqrt(jnp.mean(mix * mix, axis=-1, keepdims=True) + RMS_EPS) * g_ref[...]
    xn = x_ref[...] + gate_ref[...] * y
    o_ref[...] = xn
    _route(xn, g2_ref[...], sh_ref[...], sc_ref[...], rw_ref, rb_ref, h_ref, rk_ref, gt_ref, cm_ref)


def _out_proj_route(acts, ws, x, g, gate, g2, shift, scale, rw_t, rb, tl=512):
    bsz, l, d = x.shape
    tl = min(tl, l)
    nl = l // tl
    n_act = len(acts)
    row = lambda b, i: (b, i, 0)
    fix2 = lambda b, i: (0, 0)
    in_specs = [pl.BlockSpec((None, tl, a.shape[2]), row) for a in acts]
    in_specs += [pl.BlockSpec(wm.shape, fix2) for wm in ws]
    in_specs += [pl.BlockSpec((None, tl, d), row),
                 pl.BlockSpec((1, d), fix2),
                 pl.BlockSpec((None, 1, d), _bidx(gate)),
                 pl.BlockSpec((1, d), fix2),
                 pl.BlockSpec((None, 1, d), _bidx(shift)),
                 pl.BlockSpec((None, 1, d), _bidx(scale)),
                 pl.BlockSpec((N_EXPERTS, d), fix2),
                 pl.BlockSpec((N_EXPERTS, 1), fix2)]
    return pl.pallas_call(
        functools.partial(_out_kernel, n_act=n_act),
        grid=(bsz, nl),
        in_specs=in_specs,
        out_specs=[pl.BlockSpec((None, tl, d), row),
                   pl.BlockSpec((None, tl, d), row),
                   pl.BlockSpec((N_EXPERTS, tl), lambda b, i: (0, b * nl + i)),
                   pl.BlockSpec((N_EXPERTS, tl), lambda b, i: (0, b * nl + i)),
                   pl.BlockSpec((None, N_EXPERTS, LANES), lambda b, i: (b * nl + i, 0, 0))],
        out_shape=[jax.ShapeDtypeStruct((bsz, l, d), f32),
                   jax.ShapeDtypeStruct((bsz, l, d), bf16),
                   jax.ShapeDtypeStruct((N_EXPERTS, bsz * l), f32),
                   jax.ShapeDtypeStruct((N_EXPERTS, bsz * l), f32),
                   jax.ShapeDtypeStruct((bsz * nl, N_EXPERTS, LANES), f32)],
        compiler_params=_cp("parallel", "parallel"),
        name="out_proj_route",
    )(*acts, *ws, x, g.reshape(1, d), gate, g2.reshape(1, d), shift, scale, rw_t,
      rb.reshape(N_EXPERTS, 1))


def _route(xv, g, shift, scale, rw_ref, rb_ref, h_ref, rk_ref, gt_ref, cm_ref):
    hf = _norm_mod(xv, g, shift, scale)
    tl = hf.shape[0]
    h_ref[...] = hf.astype(h_ref.dtype)
    per = N_EXPERTS // N_GROUPS
    logits = _dot3(rw_ref[...], hf, _NT)
    s3 = _sigmoid(logits).reshape(N_GROUPS, per, tl)
    b3 = s3 + rb_ref[...].reshape(N_GROUPS, per, 1)
    neg = -jnp.inf
    jdx = lax.broadcasted_iota(jnp.int32, b3.shape, 1)
    gdx = lax.broadcasted_iota(jnp.int32, b3.shape, 0)
    m1 = jnp.max(b3, axis=1, keepdims=True)
    f1 = jnp.min(jnp.where(b3 == m1, jdx, per), axis=1, keepdims=True)
    m2 = jnp.max(jnp.where(jdx == f1, neg, b3), axis=1, keepdims=True)
    grp = m1 + m2
    g1 = lax.broadcasted_iota(jnp.int32, grp.shape, 0)
    cnt = jnp.zeros(grp.shape, jnp.int32)
    for gp in range(N_GROUPS):
        rv = grp[gp:gp + 1]
        ahead = jnp.where(rv > grp, 1, jnp.where(rv == grp, jnp.where(g1 > gp, 1, 0), 0))
        cnt = cnt + ahead
    v = jnp.where(cnt < TOPK_GROUPS, b3, neg)
    eidx = gdx * per + jdx
    sel = jnp.zeros(b3.shape, f32)
    for _ in range(TOP_K):
        m = jnp.max(jnp.max(v, axis=1, keepdims=True), axis=0, keepdims=True)
        cand = jnp.where(v == m, eidx, N_EXPERTS)
        fi = jnp.min(jnp.min(cand, axis=1, keepdims=True), axis=0, keepdims=True)
        hit = eidx == fi
        sel = jnp.where(hit, 1.0, sel)
        v = jnp.where(hit, neg, v)
    ssel = sel * s3
    den = jnp.sum(jnp.sum(ssel, axis=1, keepdims=True), axis=0, keepdims=True)
    gt_ref[...] = ((ROUTED_SCALE * ssel) / den).reshape(N_EXPERTS, tl)
    sel2 = sel.reshape(N_EXPERTS, tl)
    r = lax.broadcasted_iota(jnp.int32, (MOE_SUB, MOE_SUB), 0)
    c = lax.broadcasted_iota(jnp.int32, (MOE_SUB, MOE_SUB), 1)
    before = jnp.where(r < c, 1.0, 0.0).astype(bf16)
    cmax = jnp.zeros((N_EXPERTS, 1), f32)
    for j in range(tl // MOE_SUB):
        sub = sel2[:, j * MOE_SUB:(j + 1) * MOE_SUB]
        rank = _dot(sub.astype(bf16), before)
        rk_ref[:, j * MOE_SUB:(j + 1) * MOE_SUB] = jnp.where(sub > 0.0, rank, -1.0)
        cmax = jnp.maximum(cmax, jnp.sum(sub, axis=1, keepdims=True))
    cm_ref[...] = jnp.broadcast_to(cmax, cm_ref.shape)


def _swiglu_act(hh):
    half = hh.shape[1] // 2
    return _silu(hh[:, :half]) * hh[:, half:]


def _moe_kernel(cnt_ref, ord_ref, h_ref, rk_ref, gt_ref, *refs):
    ng = MOE_GROUP
    gu_refs, dn_refs = refs[:ng], refs[ng:2 * ng]
    o_ref, acc_ref, xg_ref, ys_ref, p_ref, gr_ref = refs[2 * ng:]
    tile = pl.program_id(0)
    grp = pl.program_id(1)
    tm, d = acc_ref.shape
    ns = tm // MOE_SUB
    eids = [ord_ref[tile, grp * ng + el] for el in range(ng)]

    @pl.when(grp == 0)
    def _():
        acc_ref[...] = jnp.zeros_like(acc_ref)

    def expert_ffn(el, win):
        hh = _dot(xg_ref[el, 0:ns * win, :], gu_refs[el][...])
        gr = gr_ref[el, 0:ns * win, :]
        act = _swiglu_act(hh) * jnp.concatenate([gr] * (hh.shape[1] // (2 * LANES)), axis=1)
        y = _dot(act.astype(bf16), dn_refs[el][...]).astype(bf16)
        for s in range(ns):
            ys_ref[s, el * win:(el + 1) * win, :] = y[s * win:(s + 1) * win]

    def one_pass(p, win):
        base = p * win
        riota = lax.broadcasted_iota(jnp.int32, (win, MOE_SUB), 0).astype(f32)
        for s in range(ns):
            cols = slice(s * MOE_SUB, (s + 1) * MOE_SUB)
            onehots = []
            for el in range(ng):
                row = pl.ds(eids[el], 1)
                hit = (rk_ref[row, cols] - base) == riota
                onehots.append(jnp.where(hit, 1.0, 0.0).astype(bf16))
                gsel = jnp.sum(jnp.where(hit, gt_ref[row, cols], 0.0), axis=1, keepdims=True)
                gr_ref[el, s * win:(s + 1) * win, :] = jnp.broadcast_to(gsel, (win, LANES))
            pm = jnp.concatenate(onehots, axis=0)
            p_ref[s, 0:ng * win, :] = pm
            gx = _dot(pm, h_ref[cols, :])
            for el in range(ng):
                xg_ref[el, s * win:(s + 1) * win, :] = gx[el * win:(el + 1) * win].astype(bf16)
        for el in range(ng):
            expert_ffn(el, win)
        for s in range(ns):
            acc_ref[s * MOE_SUB:(s + 1) * MOE_SUB, :] += _dot(
                p_ref[s, 0:ng * win, :], ys_ref[s, 0:ng * win, :], _TN)

    most = cnt_ref[tile, eids[0]]
    for el in range(1, ng):
        most = jnp.maximum(most, cnt_ref[tile, eids[el]])

    lo = 0
    for win in MOE_WINDOWS[:-1]:
        pl.when(jnp.logical_and(most > lo, most <= win))(functools.partial(one_pass, 0, win))
        lo = win
    big = MOE_WINDOWS[-1]

    def big_pass(p, carry):
        one_pass(p, big)
        return carry

    lax.fori_loop(0, jnp.where(most > lo, (most + big - 1) // big, 0), big_pass, 0)

    @pl.when(grp == pl.num_programs(1) - 1)
    def _():
        o_ref[...] = acc_ref[...].astype(o_ref.dtype)


def _moe_routed(h2, rk, gt, cmax, gu, dn, tm=2048):
    bsz, l, d = h2.shape
    tm = min(tm, l)
    nt = bsz * (l // tm)
    ne = gu.shape[0]
    ng = MOE_GROUP
    ns = tm // MOE_SUB
    wmax = MOE_WINDOWS[-1]
    counts = jnp.max(cmax[:, :, 0].reshape(nt, -1, ne), axis=1).astype(jnp.int32)
    order = jnp.argsort(-counts, axis=1).astype(jnp.int32)

    def expert_spec(arr, k):
        return pl.BlockSpec((None,) + arr.shape[1:], lambda t, e, cnt, order_ref: (order_ref[t, e * ng + k], 0, 0))

    tile_spec = pl.BlockSpec((tm, d), lambda t, e, *_: (t, 0))
    grid_spec = pltpu.PrefetchScalarGridSpec(
        num_scalar_prefetch=2,
        grid=(nt, ne // ng),
        in_specs=([tile_spec,
                   pl.BlockSpec((ne, tm), lambda t, e, *_: (0, t)),
                   pl.BlockSpec((ne, tm), lambda t, e, *_: (0, t))]
                  + [expert_spec(gu, k) for k in range(ng)]
                  + [expert_spec(dn, k) for k in range(ng)]),
        out_specs=tile_spec,
        scratch_shapes=[pltpu.VMEM((tm, d), f32),
                        pltpu.VMEM((ng, ns * wmax, d), bf16),
                        pltpu.VMEM((ns, ng * wmax, d), bf16),
                        pltpu.VMEM((ns, ng * wmax, MOE_SUB), bf16),
                        pltpu.VMEM((ng, ns * wmax, LANES), f32)])
    return pl.pallas_call(
        _moe_kernel,
        grid_spec=grid_spec,
        out_shape=jax.ShapeDtypeStruct((bsz * l, d), bf16),
        compiler_params=_cp("parallel", "arbitrary"),
        name="moe",
    )(counts, order, h2.reshape(bsz * l, d), rk, gt, *([gu] * ng), *([dn] * ng))


def _moe_finish_kernel(h_ref, r_ref, sgu_ref, sdn_ref, x_ref, g_ref, gate_ref, *refs, has_next):
    act = _swiglu_act(_dot(h_ref[...], sgu_ref[...]))
    mo = _dot(act.astype(bf16), sdn_ref[...]) + r_ref[...]
    y = mo * lax.rsqrt(jnp.mean(mo * mo, axis=-1, keepdims=True) + RMS_EPS) * g_ref[...]
    xn = x_ref[...] + gate_ref[...] * y
    if has_next:
        gn_ref, shn_ref, scn_ref, o_ref, hn_ref = refs
        hn_ref[...] = _norm_mod(xn, gn_ref[...], shn_ref[...], scn_ref[...]).astype(hn_ref.dtype)
    else:
        (o_ref,) = refs
    o_ref[...] = xn


def _moe_finish(h2, routed, sgu, sdn, x, g, gate, nxt=None, tl=512):
    bsz, l, d = x.shape
    tl = min(tl, l)
    row = lambda b, i: (b, i, 0)
    fix2 = lambda b, i: (0, 0)
    in_specs = [pl.BlockSpec((None, tl, d), row), pl.BlockSpec((None, tl, d), row),
                pl.BlockSpec(sgu.shape, fix2), pl.BlockSpec(sdn.shape, fix2),
                pl.BlockSpec((None, tl, d), row), pl.BlockSpec((1, d), fix2),
                pl.BlockSpec((None, 1, d), _bidx(gate))]
    args = [h2, routed.reshape(bsz, l, d), sgu, sdn, x, g.reshape(1, d), gate]
    out_specs, out_shape = [pl.BlockSpec((None, tl, d), row)], [jax.ShapeDtypeStruct((bsz, l, d), f32)]
    if nxt is not None:
        in_specs += [pl.BlockSpec((1, d), fix2), pl.BlockSpec((None, 1, d), _bidx(nxt[1])),
                     pl.BlockSpec((None, 1, d), _bidx(nxt[2]))]
        args += [nxt[0].reshape(1, d), nxt[1], nxt[2]]
        out_specs.append(pl.BlockSpec((None, tl, d), row))
        out_shape.append(jax.ShapeDtypeStruct((bsz, l, d), bf16))
    return pl.pallas_call(
        functools.partial(_moe_finish_kernel, has_next=nxt is not None),
        grid=(bsz, l // tl),
        in_specs=in_specs,
        out_specs=out_specs,
        out_shape=out_shape,
        compiler_params=_cp("parallel", "parallel"),
        name="moe_finish",
    )(*args)


def _filter_kernel(z_ref, w1_ref, b1_ref, w2_ref, b2_ref, w3_ref, win_ref, o_ref):
    hid = jnp.sin(FILTER_SIN_W * (_dot3(z_ref[...], w1_ref[...]) + b1_ref[...]))
    hid = jnp.sin(FILTER_SIN_W * (_dot3(hid, w2_ref[...]) + b2_ref[...]))
    o_ref[...] = _dot3(hid, w3_ref[...]) * win_ref[...]


def _filters(z, w1, b1, w2, b2, w3, window, tn=512):
    l, p = z.shape
    hdim = w1.shape[1]
    n = w3.shape[1]
    d = window.shape[1]
    nd = d // tn
    return pl.pallas_call(
        _filter_kernel,
        grid=(n // tn,),
        in_specs=[pl.BlockSpec((l, p), lambda j: (0, 0)),
                  pl.BlockSpec((p, hdim), lambda j: (0, 0)),
                  pl.BlockSpec((1, hdim), lambda j: (0, 0)),
                  pl.BlockSpec((hdim, hdim), lambda j: (0, 0)),
                  pl.BlockSpec((1, hdim), lambda j: (0, 0)),
                  pl.BlockSpec((hdim, tn), lambda j: (0, j)),
                  pl.BlockSpec((l, tn), lambda j: (0, j % nd))],
        out_specs=pl.BlockSpec((l, tn), lambda j: (0, j)),
        out_shape=jax.ShapeDtypeStruct((l, n), f32),
        compiler_params=_cp("arbitrary"),
        name="hyena_filter",
    )(z, w1, b1.reshape(1, hdim), w2, b2.reshape(1, hdim), w3, window)


def _dft_tables(l):
    n = 2 * l
    n1 = math.isqrt(n)
    assert n == n1 * n1 and n1 % 16 == 0
    na = l // n1
    ncp = -(-(n1 // 2 + 1) // 8) * 8
    a = np.arange(na)
    b = np.arange(n1)
    c = np.arange(ncp)
    th = 2.0 * np.pi * ((n1 * a[None, None, :] + b[:, None, None]) * c[None, :, None]) / n
    t1 = np.concatenate([np.cos(th), -np.sin(th)], axis=1)
    ph = 2.0 * np.pi * (b[:, None] * b[None, :]) / n1
    cs, sn = np.cos(ph), np.sin(ph)
    a3 = np.block([[cs, sn], [-sn, cs]])
    a3i = np.block([[cs, -sn], [sn, cs]])
    a2 = np.arange(na) + na // 2
    th2 = 2.0 * np.pi * ((n1 * a2[None, :, None] + b[:, None, None]) * c[None, None, :]) / n
    wc = np.where((c == 0) | (c == n1 // 2), 1.0, np.where(c < n1 // 2, 2.0, 0.0))[None, None, :]
    t2 = np.concatenate([wc * np.cos(th2), -wc * np.sin(th2)], axis=2)
    return [jnp.asarray(t, f32).astype(bf16) for t in (t1, a3, a3i, t2)]


def _fft_dims(t1):
    n1, ncp2, na = t1.shape
    ncp = ncp2 // 2
    return n1, ncp, na, 2 * n1 + FFT_PAD, 2 * ncp + FFT_PAD, n1 + FFT_PAD


def _ld(ref, rows):
    return jnp.concatenate([ref[j, rows, :] for j in range(ref.shape[0])], axis=1)


def _st(ref, rows, val):
    for j in range(ref.shape[0]):
        ref[j, rows, :] = val[:, j * LANES:(j + 1) * LANES]


def _dft_forward(uf_ref, t1_ref, zs_ref):
    n1, ncp, na, sb, _, su = _fft_dims(t1_ref)
    for b in range(n1):
        ub = _ld(uf_ref, pl.ds(b, na, stride=su)).astype(bf16)
        zb = _dot(t1_ref[b], ub)
        _st(zs_ref, pl.ds(b, ncp, stride=sb), zb[:ncp])
        _st(zs_ref, pl.ds(n1 + b, ncp, stride=sb), zb[ncp:])


def _spectrum_kernel(f_ref, t1_ref, a3_ref, o_ref, uf_ref, zs_ref, *, scale):
    n1, ncp, na, sb, _, su = _fft_dims(t1_ref)
    for a in range(na):
        _st(uf_ref, pl.ds(a * su, n1), f_ref[pl.ds(a * n1, n1), :])
    _dft_forward(uf_ref, t1_ref, zs_ref)
    a3 = a3_ref[...]
    for c in range(ncp):
        zc = _ld(zs_ref, pl.ds(c * sb, 2 * n1)).astype(bf16)
        o_ref[c] = (_dot(a3, zc) * scale).astype(o_ref.dtype)


def _spectrum(filt, tabs, dt=256):
    l, n = filt.shape
    t1, a3, _, _ = tabs
    n1, ncp, na, sb, _, su = _fft_dims(t1)
    nj = dt // LANES
    return pl.pallas_call(
        functools.partial(_spectrum_kernel, scale=1.0 / (2 * l)),
        grid=(n // dt,),
        in_specs=[pl.BlockSpec((l, dt), lambda j: (0, j)),
                  pl.BlockSpec(t1.shape, lambda j: (0, 0, 0)),
                  pl.BlockSpec(a3.shape, lambda j: (0, 0))],
        out_specs=pl.BlockSpec((ncp, 2 * n1, dt), lambda j: (0, 0, j)),
        out_shape=jax.ShapeDtypeStruct((ncp, 2 * n1, n), bf16),
        scratch_shapes=[pltpu.VMEM((nj, na * su, LANES), f32),
                        pltpu.VMEM((nj, ncp * sb, LANES), f32)],
        compiler_params=_cp("arbitrary"),
        name="hyena_spectrum",
    )(filt, t1, a3)


def _fftconv_kernel(u_ref, xg_ref, kf_ref, fb_ref, t1_ref, a3_ref, a3i_ref, t2_ref, o_ref,
                    uf_ref, zs_ref, qs_ref, y_ref):
    n1, ncp, na, sb, sq, su = _fft_dims(t1_ref)
    for a in range(na):
        _st(uf_ref, pl.ds(a * su, n1), u_ref[pl.ds(a * n1, n1), :].astype(f32))
    _dft_forward(uf_ref, t1_ref, zs_ref)
    a3 = a3_ref[...]
    a3i = a3i_ref[...]
    for c in range(ncp):
        zc = _ld(zs_ref, pl.ds(c * sb, 2 * n1)).astype(bf16)
        xc = _dot(a3, zc)
        kc = kf_ref[c].astype(f32)
        xr, xi = xc[:n1], xc[n1:]
        kr, ki = kc[:n1], kc[n1:]
        pc = jnp.concatenate([xr * kr - xi * ki, xr * ki + xi * kr], axis=0).astype(bf16)
        qc = _dot(a3i, pc)
        _st(qs_ref, pl.ds(c, n1, stride=sq), qc[:n1])
        _st(qs_ref, pl.ds(ncp + c, n1, stride=sq), qc[n1:])
    for b in range(n1):
        qb = _ld(qs_ref, pl.ds(b * sq, 2 * ncp)).astype(bf16)
        _st(y_ref, pl.ds(b, na, stride=su), _dot(t2_ref[b], qb))
    fb = fb_ref[...]
    for a in range(na):
        rows = pl.ds(a * n1, n1)
        uv = _ld(uf_ref, pl.ds(a * su, n1))
        yv = _ld(y_ref, pl.ds(a * su, n1))
        o_ref[rows, :] = (xg_ref[rows, :].astype(f32) * (yv + uv * fb)).astype(o_ref.dtype)


def _fftconv(u, u_col, xg, xg_col, kf, kf_col, fbias, tabs, d, dt=256):
    bsz, l, _ = u.shape
    t1, a3, a3i, t2 = tabs
    n1, ncp, na, sb, sq, su = _fft_dims(t1)
    nd = d // dt
    nj = dt // LANES
    uo, go, ko = u_col // dt, xg_col // dt, kf_col // dt
    return pl.pallas_call(
        _fftconv_kernel,
        grid=(nd, bsz),
        in_specs=[pl.BlockSpec((None, l, dt), lambda j, b: (b, 0, j + uo)),
                  pl.BlockSpec((None, l, dt), lambda j, b: (b, 0, j + go)),
                  pl.BlockSpec((ncp, 2 * n1, dt), lambda j, b: (0, 0, j + ko)),
                  pl.BlockSpec((1, dt), lambda j, b: (0, j)),
                  pl.BlockSpec(t1.shape, lambda j, b: (0, 0, 0)),
                  pl.BlockSpec(a3.shape, lambda j, b: (0, 0)),
                  pl.BlockSpec(a3i.shape, lambda j, b: (0, 0)),
                  pl.BlockSpec(t2.shape, lambda j, b: (0, 0, 0))],
        out_specs=pl.BlockSpec((None, l, dt), lambda j, b: (b, 0, j)),
        out_shape=jax.ShapeDtypeStruct((bsz, l, d), bf16),
        scratch_shapes=[pltpu.VMEM((nj, na * su, LANES), f32),
                        pltpu.VMEM((nj, ncp * sb, LANES), f32),
                        pltpu.VMEM((nj, n1 * sq, LANES), f32),
                        pltpu.VMEM((nj, na * su, LANES), f32)],
        compiler_params=_cp("parallel", "arbitrary"),
        name="hyena_fftconv",
    )(u, xg, kf, fbias.reshape(1, d), t1, a3, a3i, t2)


def _rope_tables(l):
    rows = l // GRID_W
    row = jnp.repeat(jnp.arange(rows), GRID_W)
    col = jnp.tile(jnp.arange(GRID_W), rows)
    inv = ROPE_BASE ** (-jnp.arange(ROPE_AXIS_PAIRS, dtype=f32) / ROPE_AXIS_PAIRS)
    ang = jnp.stack([row, col], axis=-1).astype(f32)[..., None] * inv
    ang = jnp.broadcast_to(ang[:, :, None, :], (l, 2, 2, ROPE_AXIS_PAIRS)).reshape(l, A_DQK)
    reps = A_QW // A_DQK
    return jnp.tile(jnp.cos(ang), (1, reps)), jnp.tile(jnp.sin(ang), (1, reps))


def _rotate_cols(w):
    j = np.arange(w.shape[1])
    lo = (j % (2 * ROPE_AXIS_PAIRS)) < ROPE_AXIS_PAIRS
    perm = np.where(lo, j + ROPE_AXIS_PAIRS, j - ROPE_AXIS_PAIRS)
    sign = np.where(lo, -1.0, 1.0).astype(np.float32)
    return w[:, perm] * sign


def _gate_cols(w_g, b_g):
    idx_i = np.array([d * 2 * B_HEADS + hd for d in range(2) for hd in range(B_HEADS)])
    idx_f = idx_i + B_HEADS
    pad = LANES - _NCHAIN
    k = w_g.shape[0]
    w = jnp.concatenate([w_g[:, idx_i], jnp.zeros((k, pad), f32),
                         w_g[:, idx_f], jnp.zeros((k, pad), f32)], axis=1)
    b = jnp.concatenate([b_g[idx_i], jnp.zeros((pad,), f32), b_g[idx_f], jnp.zeros((pad,), f32)])
    return w, b


def _hyena_consts(l, d):
    j = jnp.arange(l, dtype=f32)
    bands = (POS_EMB_DIM - 1) // 2
    freqs = jnp.linspace(1e-4, bands - 1, bands, dtype=f32)
    ang = (2.0 * math.pi / l) * j[:, None] * freqs[None, :]
    z = jnp.concatenate([(j / (l - 1))[:, None], jnp.cos(ang), -jnp.sin(ang)], axis=-1)
    dist = jnp.abs(j - l // 2) / (l // 2)
    max_decay = math.log(DECAY_TARGET) / DECAY_FAST_PCT
    min_decay = math.log(DECAY_TARGET) / DECAY_SLOW_PCT
    deltas = jnp.abs(jnp.linspace(min_decay, max_decay, d, dtype=f32))
    window = jnp.exp(-dist[:, None] * deltas[None, :])
    return z, window


def _ab_mixer(h, hc, w_in, conv_w, conv_b, gate_b, lam_vecs, g_a, g_b, w_out, lam_init):
    s = h.shape[1]
    w = B_WIDTH
    o = 0
    cols = {}
    for name, width in (("aq", A_QW), ("bq", w), ("bo", w), ("ak", A_QW), ("av", A_VW),
                        ("bk", w), ("bv", w), ("g", 4 * B_HEADS)):
        cols[name] = w_in[:, o:o + width]
        o += width
    cos, sin = _rope_tables(s)
    cat = lambda *ws: jnp.concatenate(ws, axis=1).astype(bf16)
    q = _mm(h, cat(cols["aq"], _rotate_cols(cols["aq"])), rope=(cos, sin, A_DQK ** -0.5))
    k = _mm(h, cat(cols["ak"], _rotate_cols(cols["ak"])), rope=(cos, sin, 1.0))
    qk = _mm(h, cat(cols["bq"], cols["bk"]), conv=(conv_w, conv_b, True))
    vvo = _mm(h, cat(cols["av"], cols["bv"], cols["bo"]))
    wg, bg = _gate_cols(cols["g"], gate_b)
    gates = _mm(h, wg.astype(bf16), out_dtype=f32, bias=bg, tn=2 * LANES)
    ckv = _mm(hc, cat(cols["ak"], cols["av"], cols["bv"]))
    cbk = _mm(hc, cols["bk"].astype(bf16), conv=(conv_w[:, w:], conv_b[w:], True))
    cg = _mm(hc, wg.astype(bf16), out_dtype=f32, bias=bg, tn=2 * LANES)
    out_a = _attn(lam_vecs, q, k, vvo, ckv, g_a, lam_init)
    out_b = _mlstm(qk, vvo, gates, cbk, ckv, cg, g_b)
    wo = w_out.astype(bf16)
    return [out_a, out_b], [wo[:A_VW], wo[A_VW:]]


def _hyena_mixer(h, w_in, conv_w, conv_b, fw1, fb1, fw2, fb2, fw3, fbias, w_out):
    _, l, d = h.shape
    u = _mm(h, w_in.astype(bf16), conv=(conv_w, conv_b, False))
    z, window = _hyena_consts(l, d)
    pz, ph = LANES - z.shape[1], LANES - fw1.shape[1]
    filt = _filters(jnp.pad(z, ((0, 0), (0, pz))), jnp.pad(fw1, ((0, pz), (0, ph))),
                    jnp.pad(fb1, (0, ph)), jnp.pad(fw2, ((0, ph), (0, ph))), jnp.pad(fb2, (0, ph)),
                    jnp.pad(fw3, ((0, ph), (0, 0))), window)
    tabs = _dft_tables(l)
    kf = _spectrum(filt, tabs)
    zz = _fftconv(u, 0, u, d, kf, 0, fbias[0], tabs, d)
    y = _fftconv(zz, 0, u, 2 * d, kf, d, fbias[1], tabs, d)
    return [y], [w_out.astype(bf16)]


def kernel(x, c, ctx, c_ctx, w_mod, b_mod, norm_g, w_in_ab, conv_ab_w, conv_ab_b, gate_b_ab, diff_lambda, head_g_a, head_g_b, w_out_ab, w_in_hy, conv_hy_w, conv_hy_b, filt_w1, filt_b1, filt_w2, filt_b2, filt_w3, filt_bias, w_out_hy, router_w, router_b, exp_gu, exp_down, sh_gu, sh_down):
    bsz, s, d = x.shape
    depth = w_mod.shape[0]
    rows = -(-(bsz + 1) // 8) * 8
    cc = jnp.concatenate([c, c_ctx[None, :], jnp.zeros((rows - bsz - 1, d), f32)], axis=0)
    mods = [_mod(cc, w_mod[l], b_mod[l]) for l in range(depth)]
    vec = lambda l, i: mods[l][:bsz, i * d:(i + 1) * d].reshape(bsz, 1, d)
    h = _norm(x, norm_g[0, 0], vec(0, 0), vec(0, 1))
    for l in range(depth):
        g_m, sh_f, sc_f, g_f = [vec(l, i) for i in range(2, 6)]
        if l % 2 == 0:
            e = l // 2
            lam_init = 0.8 - 0.6 * math.exp(-0.3 * l)
            row_c = lambda i: mods[l][bsz:bsz + 1, i * d:(i + 1) * d].reshape(1, 1, d)
            hc = _norm(ctx, norm_g[l, 0], row_c(0), row_c(1))
            acts, ws = _ab_mixer(h, hc, w_in_ab[e], conv_ab_w[e], conv_ab_b[e], gate_b_ab[e],
                                 diff_lambda[e], head_g_a[e], head_g_b[e], w_out_ab[e], lam_init)
        else:
            o = l // 2
            acts, ws = _hyena_mixer(h, w_in_hy[o], conv_hy_w[o], conv_hy_b[o], filt_w1[o], filt_b1[o],
                                    filt_w2[o], filt_b2[o], filt_w3[o], filt_bias[o], w_out_hy[o])
        x, h2, rk, gt, cmax = _out_proj_route(acts, ws, x, norm_g[l, 1], g_m, norm_g[l, 2], sh_f, sc_f,
                                              router_w[l].T, router_b[l])
        nxt = (norm_g[l + 1, 0], vec(l + 1, 0), vec(l + 1, 1)) if l + 1 < depth else None
        routed = _moe_routed(h2, rk, gt, cmax, exp_gu[l].astype(bf16), exp_down[l].astype(bf16))
        outs = _moe_finish(h2, routed, sh_gu[l].astype(bf16), sh_down[l].astype(bf16), x, norm_g[l, 3],
                           g_f, nxt)
        x = outs[0]
        if nxt is not None:
            h = outs[1]
    return x
```

```python
import functools
import math

import numpy as np
import jax
import jax.numpy as jnp
from jax import lax
from jax.experimental import pallas as pl
from jax.experimental.pallas import tpu as pltpu

f32 = jnp.float32
bf16 = jnp.bfloat16

RMS_EPS = 1e-6
A_HEADS = 4
A_DQK = 64
A_DV = 128
B_HEADS = 4
B_DH = 128
B_WIDTH = B_HEADS * B_DH
A_QW = A_HEADS * 2 * A_DQK
A_VW = A_HEADS * A_DV
GRID_W = 64
ROPE_BASE = 10000.0
ROPE_AXIS_PAIRS = A_DQK // 4
N_EXPERTS = 64
TOP_K = 8
N_GROUPS = 8
TOPK_GROUPS = 4
ROUTED_SCALE = 2.5
POS_EMB_DIM = 33
FILTER_SIN_W = 1.0
DECAY_FAST_PCT = 0.3
DECAY_SLOW_PCT = 1.5
DECAY_TARGET = 1e-2

LANES = 128
VMEM_LIMIT = 56 * 1024 * 1024
MLSTM_CHUNK = 256
FFT_PAD = 8
MOE_SUB = 256
MOE_WINDOWS = (16, 32, 48, 64)
MOE_GROUP = 4


def _cp(*sem):
    return pltpu.CompilerParams(dimension_semantics=sem, vmem_limit_bytes=VMEM_LIMIT)


def _split_bf16(a):
    hi = a.astype(bf16)
    lo = (a - hi.astype(f32)).astype(bf16)
    return hi, lo


def _dot(a, b, dims=(((1,), (0,)), ((), ()))):
    return lax.dot_general(a, b, dims, preferred_element_type=f32)


_NT = (((1,), (1,)), ((), ()))
_TN = (((0,), (0,)), ((), ()))


def _dot3(a, b, dims=(((1,), (0,)), ((), ()))):
    ah, al = _split_bf16(a)
    bh, bl = _split_bf16(b)
    return _dot(ah, bh, dims) + (_dot(ah, bl, dims) + _dot(al, bh, dims))


def _silu(v):
    return v / (1.0 + jnp.exp(-v))


def _sigmoid(v):
    return 1.0 / (1.0 + jnp.exp(-v))


def _log_sigmoid(v):
    return jnp.minimum(v, 0.0) - jnp.log(1.0 + jnp.exp(-jnp.abs(v)))


def _mod_kernel(c_ref, w_ref, b_ref, o_ref):
    o_ref[...] = _dot3(_silu(c_ref[...]), w_ref[...]) + b_ref[...]


def _mod(cc, w, b):
    rows, d = cc.shape
    n = w.shape[1]
    tn = d
    return pl.pallas_call(
        _mod_kernel,
        grid=(n // tn,),
        in_specs=[pl.BlockSpec((rows, d), lambda j: (0, 0)),
                  pl.BlockSpec((d, tn), lambda j: (0, j)),
                  pl.BlockSpec((1, tn), lambda j: (0, j))],
        out_specs=pl.BlockSpec((rows, tn), lambda j: (0, j)),
        out_shape=jax.ShapeDtypeStruct((rows, n), f32),
        compiler_params=_cp("arbitrary"),
        name="mod",
    )(cc, w, b.reshape(1, n))


def _norm_mod(xv, g, shift, scale):
    y = xv * lax.rsqrt(jnp.mean(xv * xv, axis=-1, keepdims=True) + RMS_EPS)
    return (y * g) * (1.0 + scale) + shift


def _norm_kernel(x_ref, g_ref, sh_ref, sc_ref, o_ref):
    o_ref[...] = _norm_mod(x_ref[...], g_ref[...], sh_ref[...], sc_ref[...]).astype(o_ref.dtype)


def _bidx(arr):
    if arr.shape[0] == 1:
        return lambda b, *_: (0, 0, 0)
    return lambda b, *_: (b, 0, 0)


def _norm(x, g, shift, scale, tl=512):
    bsz, l, d = x.shape
    tl = min(tl, l)
    return pl.pallas_call(
        _norm_kernel,
        grid=(bsz, l // tl),
        in_specs=[pl.BlockSpec((None, tl, d), lambda b, i: (b, i, 0)),
                  pl.BlockSpec((1, d), lambda b, i: (0, 0)),
                  pl.BlockSpec((None, 1, d), _bidx(shift)),
                  pl.BlockSpec((None, 1, d), _bidx(scale))],
        out_specs=pl.BlockSpec((None, tl, d), lambda b, i: (b, i, 0)),
        out_shape=jax.ShapeDtypeStruct((bsz, l, d), bf16),
        compiler_params=_cp("parallel", "parallel"),
        name="norm",
    )(x, g.reshape(1, d), shift, scale)


def _mm_pair_kernel(h_ref, wa_ref, wb_ref, bb_ref, oa_ref, ob_ref):
    hb = h_ref[...]
    oa_ref[...] = _dot(hb, wa_ref[...]).astype(oa_ref.dtype)
    ob_ref[...] = (_dot(hb, wb_ref[...]) + bb_ref[...]).astype(ob_ref.dtype)


def _mm_pair(h, wa, wb, bias_b, tl=512):
    bsz, l, k = h.shape
    na, nb = wa.shape[1], wb.shape[1]
    tl = min(tl, l)
    row = lambda b, i: (b, i, 0)
    fix2 = lambda b, i: (0, 0)
    return pl.pallas_call(
        _mm_pair_kernel,
        grid=(bsz, l // tl),
        in_specs=[pl.BlockSpec((None, tl, k), row), pl.BlockSpec((k, na), fix2),
                  pl.BlockSpec((k, nb), fix2), pl.BlockSpec((1, nb), fix2)],
        out_specs=[pl.BlockSpec((None, tl, na), row), pl.BlockSpec((None, tl, nb), row)],
        out_shape=[jax.ShapeDtypeStruct((bsz, l, na), bf16), jax.ShapeDtypeStruct((bsz, l, nb), f32)],
        compiler_params=_cp("parallel", "parallel"),
        name="mm_pair",
    )(h, wa, wb, bias_b.reshape(1, nb))


def _qk_rope_kernel(*refs, q_scale, normed):
    if normed:
        x_ref, g_ref, sh_ref, sc_ref, w_ref, cos_ref, sin_ref, h_ref, q_ref, k_ref = refs
        hb = _norm_mod(x_ref[...], g_ref[...], sh_ref[...], sc_ref[...]).astype(bf16)
        h_ref[...] = hb
    else:
        x_ref, w_ref, cos_ref, sin_ref, q_ref, k_ref = refs
        hb = x_ref[...]
    p = _dot(hb, w_ref[...])
    n = q_ref.shape[-1]
    cos, sin = cos_ref[...], sin_ref[...]
    q_ref[...] = ((p[:, :n] * cos + p[:, n:2 * n] * sin) * q_scale).astype(q_ref.dtype)
    k_ref[...] = (p[:, 2 * n:3 * n] * cos + p[:, 3 * n:] * sin).astype(k_ref.dtype)


def _qk_rope(src, norm, w, cos, sin, q_scale, tl=512):
    bsz, l, d = src.shape
    n = w.shape[1] // 4
    tl = min(tl, l)
    row = lambda i, b: (b, i, 0)
    fix2 = lambda i, b: (0, 0)
    tab = lambda i, b: (i, 0)
    in_specs = [pl.BlockSpec((None, tl, d), row)]
    args = [src]
    out_specs = [pl.BlockSpec((None, tl, n), row)] * 2
    out_shape = [jax.ShapeDtypeStruct((bsz, l, n), bf16)] * 2
    if norm is not None:
        g, shift, scale = norm
        bvec = lambda a: (lambda i, b: (b if a.shape[0] > 1 else 0, 0, 0))
        in_specs += [pl.BlockSpec((1, d), fix2), pl.BlockSpec((None, 1, d), bvec(shift)),
                     pl.BlockSpec((None, 1, d), bvec(scale))]
        args += [g.reshape(1, d), shift, scale]
        out_specs = [pl.BlockSpec((None, tl, d), row)] + out_specs
        out_shape = [jax.ShapeDtypeStruct((bsz, l, d), bf16)] + out_shape
    in_specs += [pl.BlockSpec(w.shape, fix2), pl.BlockSpec((tl, n), tab), pl.BlockSpec((tl, n), tab)]
    args += [w, cos, sin]
    outs = pl.pallas_call(
        functools.partial(_qk_rope_kernel, q_scale=q_scale, normed=norm is not None),
        grid=(l // tl, bsz),
        in_specs=in_specs,
        out_specs=out_specs,
        out_shape=out_shape,
        compiler_params=_cp("parallel", "parallel"),
        name="qk_rope",
    )(*args)
    return outs if norm is not None else [None] + list(outs)


def _mm_conv_kernel(h_ref, w_ref, cw_ref, cb_ref, o_ref, scr_ref, *, act, rc):
    l = h_ref.shape[0]
    w = w_ref[...]
    w0, w1, w2, cb = cw_ref[0:1, :], cw_ref[1:2, :], cw_ref[2:3, :], cb_ref[...]
    halo = 16
    zrow = jnp.zeros((8, o_ref.shape[1]), f32)
    for c in range(l // rc):
        lo, hi = max(c * rc - halo, 0), min((c + 1) * rc + halo, l)
        n = hi - lo
        scr = scr_ref.at[c % 2]
        scr[8:8 + n, :] = _dot(h_ref[lo:hi, :], w)
        if lo == 0:
            scr[0:8, :] = zrow
        if hi == l:
            scr[8 + n:16 + n, :] = zrow
        off = 8 + c * rc - lo
        y = (scr[off - 1:off - 1 + rc, :] * w0 + scr[off:off + rc, :] * w1
             + scr[off + 1:off + 1 + rc, :] * w2 + cb)
        o_ref[c * rc:(c + 1) * rc, :] = (_silu(y) if act else y).astype(o_ref.dtype)


def _mm_conv(h, w, cw, cb, act, tn=512):
    bsz, l, k = h.shape
    n = w.shape[1]
    tn = min(tn, n)
    rc = min(512, l)
    return pl.pallas_call(
        functools.partial(_mm_conv_kernel, act=act, rc=rc),
        grid=(bsz, n // tn),
        in_specs=[pl.BlockSpec((None, l, k), lambda b, j: (b, 0, 0)),
                  pl.BlockSpec((k, tn), lambda b, j: (0, j)),
                  pl.BlockSpec((3, tn), lambda b, j: (0, j)),
                  pl.BlockSpec((1, tn), lambda b, j: (0, j))],
        out_specs=pl.BlockSpec((None, l, tn), lambda b, j: (b, 0, j)),
        out_shape=jax.ShapeDtypeStruct((bsz, l, n), bf16),
        scratch_shapes=[pltpu.VMEM((2, rc + 48, tn), f32)],
        compiler_params=_cp("parallel", "arbitrary"),
        name="mm_conv",
    )(h, w, cw, cb.reshape(1, n))


def _attn_kernel(lv_ref, q_ref, kc_ref, k_ref, vc_ref, v_ref, g_ref, o_ref, *, lam_init):
    tq = q_ref.shape[0]
    lv = lv_ref[...]
    lam = (jnp.exp(jnp.sum(lv[0:1] * lv[1:2], axis=1, keepdims=True))
           - jnp.exp(jnp.sum(lv[2:3] * lv[3:4], axis=1, keepdims=True)) + lam_init)
    first = lax.broadcasted_iota(jnp.int32, (tq, A_DV), 1) < A_DQK
    ones_c = _ones_block(kc_ref.shape[0])
    ones_l = _ones_block(k_ref.shape[0])
    for hd in range(A_HEADS):
        cs = slice(hd * A_DV, (hd + 1) * A_DV)
        qh = q_ref[:, cs]
        zero = jnp.zeros_like(qh)
        q2 = jnp.concatenate([jnp.where(first, qh, zero), jnp.where(first, zero, qh)], axis=0)
        s_c = _dot(q2, kc_ref[:, cs], _NT)
        s_l = _dot(q2, k_ref[:, cs], _NT)
        m = jnp.maximum(jnp.max(s_c, axis=1, keepdims=True), jnp.max(s_l, axis=1, keepdims=True))
        p_c = jnp.exp((s_c - m).astype(bf16))
        p_l = jnp.exp((s_l - m).astype(bf16))
        oa = (_dot(p_c, jnp.concatenate([vc_ref[:, cs], ones_c], axis=1))
              + _dot(p_l, jnp.concatenate([v_ref[:, cs], ones_l], axis=1)))
        on = oa[:, :A_DV] * (1.0 / oa[:, A_DV:A_DV + 1])
        o = on[:tq] - lam * on[tq:]
        o = o * lax.rsqrt(jnp.mean(o * o, axis=1, keepdims=True) + RMS_EPS)
        o_ref[:, cs] = (o * g_ref[:, cs] * (1.0 - lam_init)).astype(o_ref.dtype)


def _attn(lv, q, k, vvo, ckv, g_a, lam_init, tq=256):
    bsz, s, _ = q.shape
    lc = ckv.shape[1]
    tq = min(tq, s)
    w = A_QW
    return pl.pallas_call(
        functools.partial(_attn_kernel, lam_init=lam_init),
        grid=(bsz, s // tq),
        in_specs=[pl.BlockSpec(lv.shape, lambda b, i: (0, 0)),
                  pl.BlockSpec((None, tq, w), lambda b, i: (b, i, 0)),
                  pl.BlockSpec((None, lc, w), lambda b, i: (b, 0, 0)),
                  pl.BlockSpec((None, s, w), lambda b, i: (b, 0, 0)),
                  pl.BlockSpec((None, lc, w), lambda b, i: (b, 0, 1)),
                  pl.BlockSpec((None, s, w), lambda b, i: (b, 0, 0)),
                  pl.BlockSpec((1, w), lambda b, i: (0, 0))],
        out_specs=pl.BlockSpec((None, tq, w), lambda b, i: (b, i, 0)),
        out_shape=jax.ShapeDtypeStruct((bsz, s, w), bf16),
        compiler_params=_cp("parallel", "arbitrary"),
        name="diff_attn",
    )(lv, q, ckv, k, ckv, vvo, g_a.reshape(1, w))


_LN_QSCALE = math.log(B_DH ** -0.5)
_NCHAIN = 2 * B_HEADS


def _chunk_gate_sums(gi, gf, tri):
    lf = _log_sigmoid(gf)
    hi, lo = _split_bf16(lf)
    cum = _dot(tri, hi) + _dot(tri, lo)
    t = gf.shape[0]
    tot = cum[t - 1:t, :]
    rcum = tot - cum + lf
    fwd = lax.broadcasted_iota(jnp.int32, gf.shape, 1) < B_HEADS
    bd = jnp.where(fwd, cum, rcum)
    return bd, tot, (bd - gi).T


def _lower_tri(t):
    r = lax.broadcasted_iota(jnp.int32, (t, t), 0)
    c = lax.broadcasted_iota(jnp.int32, (t, t), 1)
    return r, c


def _ones_block(t):
    one0 = jnp.where(lax.broadcasted_iota(jnp.int32, (1, LANES), 1) == 0, 1.0, 0.0).astype(bf16)
    return jnp.broadcast_to(one0, (t, LANES))


def _absorb(c_ref, m_ref, ch, x_row, tot_c, kb, vaug):
    m_prev = m_ref[ch][:, 0:1]
    g = tot_c - x_row
    m_new = jnp.maximum(tot_c + m_prev, jnp.max(g, axis=1, keepdims=True))
    wgt = jnp.exp(g - m_new)
    decay = jnp.exp(tot_c + m_prev - m_new)
    kw_t = kb.astype(f32).T * wgt
    c_ref[ch] = decay * c_ref[ch] + _dot(kw_t.astype(bf16), vaug)
    m_ref[ch] = jnp.broadcast_to(m_new, m_ref.shape[1:])


def _mlstm_kernel(qk_ref, vvo_ref, g_ref, ck_ref, ckv_ref, cg_ref, gb_ref, o_ref,
                  hf_ref, hb_ref, c_ref, m_ref, *, tc):
    s = o_ref.shape[0]
    lc = ck_ref.shape[0]
    nc = s // tc
    w = B_WIDTH
    dh = B_DH

    c_ref[...] = jnp.zeros_like(c_ref)
    m_ref[...] = jnp.zeros_like(m_ref)

    r, cidx = _lower_tri(lc)
    tri_c = jnp.where(cidx <= r, 1.0, 0.0).astype(bf16)
    cg = cg_ref[...]
    _, tot, xt = _chunk_gate_sums(cg[:, :LANES], cg[:, LANES:], tri_c)
    ones_c = _ones_block(lc)
    for ch in range(_NCHAIN):
        hs = slice((ch % B_HEADS) * dh, (ch % B_HEADS + 1) * dh)
        vs = slice(2 * w + (ch % B_HEADS) * dh, 2 * w + (ch % B_HEADS + 1) * dh)
        _absorb(c_ref, m_ref, ch, xt[ch:ch + 1, :], tot[:, ch:ch + 1], ck_ref[:, hs],
                jnp.concatenate([ckv_ref[:, vs], ones_c], axis=1))
    ones_t = _ones_block(tc)

    r, cidx = _lower_tri(tc)
    tri = jnp.where(cidx <= r, 1.0, 0.0).astype(bf16)
    causal = cidx <= r
    anti = cidx >= r

    def step(i, carry):
        for d in range(2):
            row0 = pl.multiple_of((i if d == 0 else nc - 1 - i) * tc, tc)
            rows = pl.ds(row0, tc)
            gch = g_ref[rows, :]
            gi = gch[:, :LANES]
            bd, tot, xt = _chunk_gate_sums(gi, gch[:, LANES:], tri)
            mask = causal if d == 0 else anti
            dst = hf_ref if d == 0 else hb_ref
            for hd in range(B_HEADS):
                ch = d * B_HEADS + hd
                hs = slice(hd * dh, (hd + 1) * dh)
                qb = qk_ref[rows, hs]
                kb = qk_ref[rows, slice(w + hd * dh, w + (hd + 1) * dh)]
                vaug = jnp.concatenate([vvo_ref[rows, slice(w + hd * dh, w + (hd + 1) * dh)], ones_t],
                                       axis=1)
                bcol = bd[:, ch:ch + 1]
                x_row = xt[ch:ch + 1, :]
                dmat = jnp.where(mask, bcol - x_row, -jnp.inf)
                m_prev = m_ref[ch][:, 0:1]
                inter = bcol + m_prev
                m_t = jnp.maximum(inter, jnp.max(dmat, axis=1, keepdims=True))
                e = jnp.exp(dmat - m_t + _LN_QSCALE)
                smat = _dot(qb, kb, _NT) * e
                sc = jnp.exp(inter - m_t + _LN_QSCALE)
                both = sc * _dot(qb, c_ref[ch].astype(bf16)) + _dot(smat.astype(bf16), vaug)
                den = both[:, dh:dh + 1]
                dst[rows, hs] = both[:, :dh] * (1.0 / jnp.maximum(jnp.abs(den), jnp.exp(-m_t)))
                _absorb(c_ref, m_ref, ch, x_row, tot[:, ch:ch + 1], kb, vaug)
        return carry

    lax.fori_loop(0, nc, step, 0)

    for hd in range(B_HEADS):
        hs = slice(hd * dh, (hd + 1) * dh)
        hsum = hf_ref[:, hs] + hb_ref[:, hs]
        hn = hsum * lax.rsqrt(jnp.mean(hsum * hsum, axis=1, keepdims=True) + RMS_EPS)
        og = _sigmoid(vvo_ref[:, slice(2 * w + hd * dh, 2 * w + (hd + 1) * dh)].astype(f32))
        o_ref[:, hs] = (hn * gb_ref[:, hs] * og).astype(o_ref.dtype)


def _mlstm(qk, vvo, gates, cbk, ckv, cg, g_b):
    bsz, s, _ = qk.shape
    lc = cbk.shape[1]
    w = B_WIDTH
    tc = min(MLSTM_CHUNK, s)
    return pl.pallas_call(
        functools.partial(_mlstm_kernel, tc=tc),
        grid=(bsz,),
        in_specs=[pl.BlockSpec((None, s, 2 * w), lambda b: (b, 0, 0)),
                  pl.BlockSpec((None, s, 3 * w), lambda b: (b, 0, 0)),
                  pl.BlockSpec((None, s, 2 * LANES), lambda b: (b, 0, 0)),
                  pl.BlockSpec((None, lc, w), lambda b: (b, 0, 0)),
                  pl.BlockSpec((None, lc, 3 * w), lambda b: (b, 0, 0)),
                  pl.BlockSpec((None, lc, 2 * LANES), lambda b: (b, 0, 0)),
                  pl.BlockSpec((1, w), lambda b: (0, 0))],
        out_specs=pl.BlockSpec((None, s, w), lambda b: (b, 0, 0)),
        out_shape=jax.ShapeDtypeStruct((bsz, s, w), bf16),
        scratch_shapes=[pltpu.VMEM((s, w), f32), pltpu.VMEM((s, w), f32),
                        pltpu.VMEM((_NCHAIN, B_DH, 2 * B_DH), f32),
                        pltpu.VMEM((_NCHAIN, 1, LANES), f32)],
        compiler_params=_cp("arbitrary"),
        name="mlstm",
    )(qk, vvo, gates, cbk, ckv, cg, g_b.reshape(1, w))


def _out_kernel(*refs, n_act):
    acts = refs[:n_act]
    ws = refs[n_act:2 * n_act]
    (x_ref, g_ref, gate_ref, g2_ref, sh_ref, sc_ref, rw_ref, rb_ref,
     o_ref, h_ref, rk_ref, gt_ref, cm_ref) = refs[2 * n_act:]
    mix = _dot(acts[0][...], ws[0][...])
    for a, wr in zip(acts[1:], ws[1:]):
        mix = mix + _dot(a[...], wr[...])
    y = mix * lax.rsqrt(jnp.mean(mix * mix, axis=-1, keepdims=True) + RMS_EPS) * g_ref[...]
    xn = x_ref[...] + gate_ref[...] * y
    o_ref[...] = xn
    _route(xn, g2_ref[...], sh_ref[...], sc_ref[...], rw_ref, rb_ref, h_ref, rk_ref, gt_ref, cm_ref)


def _out_proj_route(acts, ws, x, g, gate, g2, shift, scale, rw_t, rb, tl=512):
    bsz, l, d = x.shape
    tl = min(tl, l)
    nl = l // tl
    n_act = len(acts)
    row = lambda b, i: (b, i, 0)
    fix2 = lambda b, i: (0, 0)
    in_specs = [pl.BlockSpec((None, tl, a.shape[2]), row) for a in acts]
    in_specs += [pl.BlockSpec(wm.shape, fix2) for wm in ws]
    in_specs += [pl.BlockSpec((None, tl, d), row),
                 pl.BlockSpec((1, d), fix2),
                 pl.BlockSpec((None, 1, d), _bidx(gate)),
                 pl.BlockSpec((1, d), fix2),
                 pl.BlockSpec((None, 1, d), _bidx(shift)),
                 pl.BlockSpec((None, 1, d), _bidx(scale)),
                 pl.BlockSpec((N_EXPERTS, d), fix2),
                 pl.BlockSpec((N_EXPERTS, 1), fix2)]
    return pl.pallas_call(
        functools.partial(_out_kernel, n_act=n_act),
        grid=(bsz, nl),
        in_specs=in_specs,
        out_specs=[pl.BlockSpec((None, tl, d), row),
                   pl.BlockSpec((None, tl, d), row),
                   pl.BlockSpec((N_EXPERTS, tl), lambda b, i: (0, b * nl + i)),
                   pl.BlockSpec((N_EXPERTS, tl), lambda b, i: (0, b * nl + i)),
                   pl.BlockSpec((None, N_EXPERTS, LANES), lambda b, i: (b * nl + i, 0, 0))],
        out_shape=[jax.ShapeDtypeStruct((bsz, l, d), f32),
                   jax.ShapeDtypeStruct((bsz, l, d), bf16),
                   jax.ShapeDtypeStruct((N_EXPERTS, bsz * l), f32),
                   jax.ShapeDtypeStruct((N_EXPERTS, bsz * l), f32),
                   jax.ShapeDtypeStruct((bsz * nl, N_EXPERTS, LANES), f32)],
        compiler_params=_cp("parallel", "parallel"),
        name="out_proj_route",
    )(*acts, *ws, x, g.reshape(1, d), gate, g2.reshape(1, d), shift, scale, rw_t,
      rb.reshape(N_EXPERTS, 1))


def _route(xv, g, shift, scale, rw_ref, rb_ref, h_ref, rk_ref, gt_ref, cm_ref):
    hf = _norm_mod(xv, g, shift, scale)
    tl = hf.shape[0]
    h_ref[...] = hf.astype(h_ref.dtype)
    per = N_EXPERTS // N_GROUPS
    logits = _dot3(rw_ref[...], hf, _NT)
    s3 = _sigmoid(logits).reshape(N_GROUPS, per, tl)
    b3 = s3 + rb_ref[...].reshape(N_GROUPS, per, 1)
    neg = -jnp.inf
    jdx = lax.broadcasted_iota(jnp.int32, b3.shape, 1)
    gdx = lax.broadcasted_iota(jnp.int32, b3.shape, 0)
    m1 = jnp.max(b3, axis=1, keepdims=True)
    f1 = jnp.min(jnp.where(b3 == m1, jdx, per), axis=1, keepdims=True)
    m2 = jnp.max(jnp.where(jdx == f1, neg, b3), axis=1, keepdims=True)
    grp = m1 + m2
    g1 = lax.broadcasted_iota(jnp.int32, grp.shape, 0)
    cnt = jnp.zeros(grp.shape, jnp.int32)
    for gp in range(N_GROUPS):
        rv = grp[gp:gp + 1]
        ahead = jnp.where(rv > grp, 1, jnp.where(rv == grp, jnp.where(g1 > gp, 1, 0), 0))
        cnt = cnt + ahead
    v = jnp.where(cnt < TOPK_GROUPS, b3, neg)
    eidx = gdx * per + jdx
    sel = jnp.zeros(b3.shape, f32)
    for _ in range(TOP_K):
        m = jnp.max(jnp.max(v, axis=1, keepdims=True), axis=0, keepdims=True)
        cand = jnp.where(v == m, eidx, N_EXPERTS)
        fi = jnp.min(jnp.min(cand, axis=1, keepdims=True), axis=0, keepdims=True)
        hit = eidx == fi
        sel = jnp.where(hit, 1.0, sel)
        v = jnp.where(hit, neg, v)
    ssel = sel * s3
    den = jnp.sum(jnp.sum(ssel, axis=1, keepdims=True), axis=0, keepdims=True)
    gt_ref[...] = ((ROUTED_SCALE * ssel) / den).reshape(N_EXPERTS, tl)
    sel2 = sel.reshape(N_EXPERTS, tl)
    r = lax.broadcasted_iota(jnp.int32, (MOE_SUB, MOE_SUB), 0)
    c = lax.broadcasted_iota(jnp.int32, (MOE_SUB, MOE_SUB), 1)
    before = jnp.where(r < c, 1.0, 0.0).astype(bf16)
    cmax = jnp.zeros((N_EXPERTS, 1), f32)
    for j in range(tl // MOE_SUB):
        sub = sel2[:, j * MOE_SUB:(j + 1) * MOE_SUB]
        rank = _dot(sub.astype(bf16), before)
        rk_ref[:, j * MOE_SUB:(j + 1) * MOE_SUB] = jnp.where(sub > 0.0, rank, -1.0)
        cmax = jnp.maximum(cmax, jnp.sum(sub, axis=1, keepdims=True))
    cm_ref[...] = jnp.broadcast_to(cmax, cm_ref.shape)


def _swiglu_act(hh):
    half = hh.shape[1] // 2
    return _silu(hh[:, :half]) * hh[:, half:]


def _moe_kernel(cnt_ref, ord_ref, h_ref, rk_ref, gt_ref, *refs):
    ng = MOE_GROUP
    gu_refs, dn_refs = refs[:ng], refs[ng:2 * ng]
    o_ref, acc_ref, xg_ref, ys_ref, p_ref, gr_ref = refs[2 * ng:]
    tile = pl.program_id(0)
    grp = pl.program_id(1)
    tm, d = acc_ref.shape
    ns = tm // MOE_SUB
    eids = [ord_ref[tile, grp * ng + el] for el in range(ng)]

    @pl.when(grp == 0)
    def _():
        acc_ref[...] = jnp.zeros_like(acc_ref)

    def expert_ffn(el, win):
        hh = _dot(xg_ref[el, 0:ns * win, :], gu_refs[el][...])
        gr = gr_ref[el, 0:ns * win, :]
        act = _swiglu_act(hh) * jnp.concatenate([gr] * (hh.shape[1] // (2 * LANES)), axis=1)
        y = _dot(act.astype(bf16), dn_refs[el][...]).astype(bf16)
        for s in range(ns):
            ys_ref[s, el * win:(el + 1) * win, :] = y[s * win:(s + 1) * win]

    def one_pass(p, win):
        base = p * win
        riota = lax.broadcasted_iota(jnp.int32, (win, MOE_SUB), 0).astype(f32)
        for s in range(ns):
            cols = slice(s * MOE_SUB, (s + 1) * MOE_SUB)
            onehots = []
            for el in range(ng):
                row = pl.ds(eids[el], 1)
                hit = (rk_ref[row, cols] - base) == riota
                onehots.append(jnp.where(hit, 1.0, 0.0).astype(bf16))
                gsel = jnp.sum(jnp.where(hit, gt_ref[row, cols], 0.0), axis=1, keepdims=True)
                gr_ref[el, s * win:(s + 1) * win, :] = jnp.broadcast_to(gsel, (win, LANES))
            pm = jnp.concatenate(onehots, axis=0)
            p_ref[s, 0:ng * win, :] = pm
            gx = _dot(pm, h_ref[cols, :])
            for el in range(ng):
                xg_ref[el, s * win:(s + 1) * win, :] = gx[el * win:(el + 1) * win].astype(bf16)
        for el in range(ng):
            expert_ffn(el, win)
        for s in range(ns):
            acc_ref[s * MOE_SUB:(s + 1) * MOE_SUB, :] += _dot(
                p_ref[s, 0:ng * win, :], ys_ref[s, 0:ng * win, :], _TN)

    most = cnt_ref[tile, eids[0]]
    for el in range(1, ng):
        most = jnp.maximum(most, cnt_ref[tile, eids[el]])

    lo = 0
    for win in MOE_WINDOWS[:-1]:
        pl.when(jnp.logical_and(most > lo, most <= win))(functools.partial(one_pass, 0, win))
        lo = win
    big = MOE_WINDOWS[-1]

    def big_pass(p, carry):
        one_pass(p, big)
        return carry

    lax.fori_loop(0, jnp.where(most > lo, (most + big - 1) // big, 0), big_pass, 0)

    @pl.when(grp == pl.num_programs(1) - 1)
    def _():
        o_ref[...] = acc_ref[...].astype(o_ref.dtype)


def _moe_routed(h2, rk, gt, cmax, gu, dn, tm=2048):
    bsz, l, d = h2.shape
    tm = min(tm, l)
    nt = bsz * (l // tm)
    ne = gu.shape[0]
    ng = MOE_GROUP
    ns = tm // MOE_SUB
    wmax = MOE_WINDOWS[-1]
    counts = jnp.max(cmax[:, :, 0].reshape(nt, -1, ne), axis=1).astype(jnp.int32)
    order = jnp.argsort(-counts, axis=1).astype(jnp.int32)

    def expert_spec(arr, k):
        return pl.BlockSpec((None,) + arr.shape[1:], lambda t, e, cnt, order_ref: (order_ref[t, e * ng + k], 0, 0))

    tile_spec = pl.BlockSpec((tm, d), lambda t, e, *_: (t, 0))
    grid_spec = pltpu.PrefetchScalarGridSpec(
        num_scalar_prefetch=2,
        grid=(nt, ne // ng),
        in_specs=([tile_spec,
                   pl.BlockSpec((ne, tm), lambda t, e, *_: (0, t)),
                   pl.BlockSpec((ne, tm), lambda t, e, *_: (0, t))]
                  + [expert_spec(gu, k) for k in range(ng)]
                  + [expert_spec(dn, k) for k in range(ng)]),
        out_specs=tile_spec,
        scratch_shapes=[pltpu.VMEM((tm, d), f32),
                        pltpu.VMEM((ng, ns * wmax, d), bf16),
                        pltpu.VMEM((ns, ng * wmax, d), bf16),
                        pltpu.VMEM((ns, ng * wmax, MOE_SUB), bf16),
                        pltpu.VMEM((ng, ns * wmax, LANES), f32)])
    return pl.pallas_call(
        _moe_kernel,
        grid_spec=grid_spec,
        out_shape=jax.ShapeDtypeStruct((bsz * l, d), bf16),
        compiler_params=_cp("parallel", "arbitrary"),
        name="moe",
    )(counts, order, h2.reshape(bsz * l, d), rk, gt, *([gu] * ng), *([dn] * ng))


def _moe_finish_kernel(h_ref, r_ref, sgu_ref, sdn_ref, x_ref, g_ref, gate_ref, *refs, has_next):
    act = _swiglu_act(_dot(h_ref[...], sgu_ref[...]))
    mo = _dot(act.astype(bf16), sdn_ref[...]) + r_ref[...]
    y = mo * lax.rsqrt(jnp.mean(mo * mo, axis=-1, keepdims=True) + RMS_EPS) * g_ref[...]
    xn = x_ref[...] + gate_ref[...] * y
    if has_next:
        gn_ref, shn_ref, scn_ref, o_ref, hn_ref = refs
        hn_ref[...] = _norm_mod(xn, gn_ref[...], shn_ref[...], scn_ref[...]).astype(hn_ref.dtype)
    else:
        (o_ref,) = refs
    o_ref[...] = xn


def _moe_finish(h2, routed, sgu, sdn, x, g, gate, nxt=None, tl=512):
    bsz, l, d = x.shape
    tl = min(tl, l)
    row = lambda b, i: (b, i, 0)
    fix2 = lambda b, i: (0, 0)
    in_specs = [pl.BlockSpec((None, tl, d), row), pl.BlockSpec((None, tl, d), row),
                pl.BlockSpec(sgu.shape, fix2), pl.BlockSpec(sdn.shape, fix2),
                pl.BlockSpec((None, tl, d), row), pl.BlockSpec((1, d), fix2),
                pl.BlockSpec((None, 1, d), _bidx(gate))]
    args = [h2, routed.reshape(bsz, l, d), sgu, sdn, x, g.reshape(1, d), gate]
    out_specs, out_shape = [pl.BlockSpec((None, tl, d), row)], [jax.ShapeDtypeStruct((bsz, l, d), f32)]
    if nxt is not None:
        in_specs += [pl.BlockSpec((1, d), fix2), pl.BlockSpec((None, 1, d), _bidx(nxt[1])),
                     pl.BlockSpec((None, 1, d), _bidx(nxt[2]))]
        args += [nxt[0].reshape(1, d), nxt[1], nxt[2]]
        out_specs.append(pl.BlockSpec((None, tl, d), row))
        out_shape.append(jax.ShapeDtypeStruct((bsz, l, d), bf16))
    return pl.pallas_call(
        functools.partial(_moe_finish_kernel, has_next=nxt is not None),
        grid=(bsz, l // tl),
        in_specs=in_specs,
        out_specs=out_specs,
        out_shape=out_shape,
        compiler_params=_cp("parallel", "parallel"),
        name="moe_finish",
    )(*args)


def _filter_kernel(z_ref, w1_ref, b1_ref, w2_ref, b2_ref, w3_ref, win_ref, o_ref):
    hid = jnp.sin(FILTER_SIN_W * (_dot3(z_ref[...], w1_ref[...]) + b1_ref[...]))
    hid = jnp.sin(FILTER_SIN_W * (_dot3(hid, w2_ref[...]) + b2_ref[...]))
    o_ref[...] = _dot3(hid, w3_ref[...]) * win_ref[...]


def _filters(z, w1, b1, w2, b2, w3, window, tn=512):
    l, p = z.shape
    hdim = w1.shape[1]
    n = w3.shape[1]
    d = window.shape[1]
    nd = d // tn
    return pl.pallas_call(
        _filter_kernel,
        grid=(n // tn,),
        in_specs=[pl.BlockSpec((l, p), lambda j: (0, 0)),
                  pl.BlockSpec((p, hdim), lambda j: (0, 0)),
                  pl.BlockSpec((1, hdim), lambda j: (0, 0)),
                  pl.BlockSpec((hdim, hdim), lambda j: (0, 0)),
                  pl.BlockSpec((1, hdim), lambda j: (0, 0)),
                  pl.BlockSpec((hdim, tn), lambda j: (0, j)),
                  pl.BlockSpec((l, tn), lambda j: (0, j % nd))],
        out_specs=pl.BlockSpec((l, tn), lambda j: (0, j)),
        out_shape=jax.ShapeDtypeStruct((l, n), f32),
        compiler_params=_cp("arbitrary"),
        name="hyena_filter",
    )(z, w1, b1.reshape(1, hdim), w2, b2.reshape(1, hdim), w3, window)


def _dft_tables(l):
    n = 2 * l
    n1 = math.isqrt(n)
    assert n == n1 * n1 and n1 % 16 == 0
    na = l // n1
    ncp = -(-(n1 // 2 + 1) // 8) * 8
    a = np.arange(na)
    b = np.arange(n1)
    c = np.arange(ncp)
    th = 2.0 * np.pi * ((n1 * a[None, None, :] + b[:, None, None]) * c[None, :, None]) / n
    t1 = np.concatenate([np.cos(th), -np.sin(th)], axis=1)
    ph = 2.0 * np.pi * (b[:, None] * b[None, :]) / n1
    cs, sn = np.cos(ph), np.sin(ph)
    a3 = np.block([[cs, sn], [-sn, cs]])
    a3i = np.block([[cs, -sn], [sn, cs]])
    a2 = np.arange(na) + na // 2
    th2 = 2.0 * np.pi * ((n1 * a2[None, :, None] + b[:, None, None]) * c[None, None, :]) / n
    wc = np.where((c == 0) | (c == n1 // 2), 1.0, np.where(c < n1 // 2, 2.0, 0.0))[None, None, :]
    t2 = np.concatenate([wc * np.cos(th2), -wc * np.sin(th2)], axis=2)
    return [jnp.asarray(t, f32).astype(bf16) for t in (t1, a3, a3i, t2)]


def _fft_dims(t1):
    n1, ncp2, na = t1.shape
    ncp = ncp2 // 2
    return n1, ncp, na, 2 * n1 + FFT_PAD, 2 * ncp + FFT_PAD, n1 + FFT_PAD


def _ld(ref, rows):
    return jnp.concatenate([ref[j, rows, :] for j in range(ref.shape[0])], axis=1)


def _st(ref, rows, val):
    for j in range(ref.shape[0]):
        ref[j, rows, :] = val[:, j * LANES:(j + 1) * LANES]


def _dft_forward(uf_ref, t1_ref, zs_ref):
    n1, ncp, na, sb, _, su = _fft_dims(t1_ref)
    for b in range(n1):
        ub = _ld(uf_ref, pl.ds(b, na, stride=su)).astype(bf16)
        zb = _dot(t1_ref[b], ub)
        _st(zs_ref, pl.ds(b, ncp, stride=sb), zb[:ncp])
        _st(zs_ref, pl.ds(n1 + b, ncp, stride=sb), zb[ncp:])


def _spectrum_kernel(f_ref, t1_ref, a3_ref, o_ref, uf_ref, zs_ref, *, scale):
    n1, ncp, na, sb, _, su = _fft_dims(t1_ref)
    for a in range(na):
        _st(uf_ref, pl.ds(a * su, n1), f_ref[pl.ds(a * n1, n1), :])
    _dft_forward(uf_ref, t1_ref, zs_ref)
    a3 = a3_ref[...]
    for c in range(ncp):
        zc = _ld(zs_ref, pl.ds(c * sb, 2 * n1)).astype(bf16)
        o_ref[c] = (_dot(a3, zc) * scale).astype(o_ref.dtype)


def _spectrum(filt, tabs, dt=256):
    l, n = filt.shape
    t1, a3, _, _ = tabs
    n1, ncp, na, sb, _, su = _fft_dims(t1)
    nj = dt // LANES
    return pl.pallas_call(
        functools.partial(_spectrum_kernel, scale=1.0 / (2 * l)),
        grid=(n // dt,),
        in_specs=[pl.BlockSpec((l, dt), lambda j: (0, j)),
                  pl.BlockSpec(t1.shape, lambda j: (0, 0, 0)),
                  pl.BlockSpec(a3.shape, lambda j: (0, 0))],
        out_specs=pl.BlockSpec((ncp, 2 * n1, dt), lambda j: (0, 0, j)),
        out_shape=jax.ShapeDtypeStruct((ncp, 2 * n1, n), bf16),
        scratch_shapes=[pltpu.VMEM((nj, na * su, LANES), f32),
                        pltpu.VMEM((nj, ncp * sb, LANES), f32)],
        compiler_params=_cp("arbitrary"),
        name="hyena_spectrum",
    )(filt, t1, a3)


def _fftconv_kernel(u_ref, xg_ref, kf_ref, fb_ref, t1_ref, a3_ref, a3i_ref, t2_ref, o_ref,
                    uf_ref, zs_ref, qs_ref, y_ref):
    n1, ncp, na, sb, sq, su = _fft_dims(t1_ref)
    for a in range(na):
        _st(uf_ref, pl.ds(a * su, n1), u_ref[pl.ds(a * n1, n1), :].astype(f32))
    _dft_forward(uf_ref, t1_ref, zs_ref)
    a3 = a3_ref[...]
    a3i = a3i_ref[...]
    for c in range(ncp):
        zc = _ld(zs_ref, pl.ds(c * sb, 2 * n1)).astype(bf16)
        xc = _dot(a3, zc)
        kc = kf_ref[c].astype(f32)
        xr, xi = xc[:n1], xc[n1:]
        kr, ki = kc[:n1], kc[n1:]
        pc = jnp.concatenate([xr * kr - xi * ki, xr * ki + xi * kr], axis=0).astype(bf16)
        qc = _dot(a3i, pc)
        _st(qs_ref, pl.ds(c, n1, stride=sq), qc[:n1])
        _st(qs_ref, pl.ds(ncp + c, n1, stride=sq), qc[n1:])
    for b in range(n1):
        qb = _ld(qs_ref, pl.ds(b * sq, 2 * ncp)).astype(bf16)
        _st(y_ref, pl.ds(b, na, stride=su), _dot(t2_ref[b], qb))
    fb = fb_ref[...]
    for a in range(na):
        rows = pl.ds(a * n1, n1)
        uv = _ld(uf_ref, pl.ds(a * su, n1))
        yv = _ld(y_ref, pl.ds(a * su, n1))
        o_ref[rows, :] = (xg_ref[rows, :].astype(f32) * (yv + uv * fb)).astype(o_ref.dtype)


def _fftconv(u, u_col, xg, xg_col, kf, kf_col, fbias, tabs, d, dt=256):
    bsz, l, _ = u.shape
    t1, a3, a3i, t2 = tabs
    n1, ncp, na, sb, sq, su = _fft_dims(t1)
    nd = d // dt
    nj = dt // LANES
    uo, go, ko = u_col // dt, xg_col // dt, kf_col // dt
    return pl.pallas_call(
        _fftconv_kernel,
        grid=(nd, bsz),
        in_specs=[pl.BlockSpec((None, l, dt), lambda j, b: (b, 0, j + uo)),
                  pl.BlockSpec((None, l, dt), lambda j, b: (b, 0, j + go)),
                  pl.BlockSpec((ncp, 2 * n1, dt), lambda j, b: (0, 0, j + ko)),
                  pl.BlockSpec((1, dt), lambda j, b: (0, j)),
                  pl.BlockSpec(t1.shape, lambda j, b: (0, 0, 0)),
                  pl.BlockSpec(a3.shape, lambda j, b: (0, 0)),
                  pl.BlockSpec(a3i.shape, lambda j, b: (0, 0)),
                  pl.BlockSpec(t2.shape, lambda j, b: (0, 0, 0))],
        out_specs=pl.BlockSpec((None, l, dt), lambda j, b: (b, 0, j)),
        out_shape=jax.ShapeDtypeStruct((bsz, l, d), bf16),
        scratch_shapes=[pltpu.VMEM((nj, na * su, LANES), f32),
                        pltpu.VMEM((nj, ncp * sb, LANES), f32),
                        pltpu.VMEM((nj, n1 * sq, LANES), f32),
                        pltpu.VMEM((nj, na * su, LANES), f32)],
        compiler_params=_cp("parallel", "arbitrary"),
        name="hyena_fftconv",
    )(u, xg, kf, fbias.reshape(1, d), t1, a3, a3i, t2)


def _rope_tables(l):
    rows = l // GRID_W
    row = jnp.repeat(jnp.arange(rows), GRID_W)
    col = jnp.tile(jnp.arange(GRID_W), rows)
    inv = ROPE_BASE ** (-jnp.arange(ROPE_AXIS_PAIRS, dtype=f32) / ROPE_AXIS_PAIRS)
    ang = jnp.stack([row, col], axis=-1).astype(f32)[..., None] * inv
    ang = jnp.broadcast_to(ang[:, :, None, :], (l, 2, 2, ROPE_AXIS_PAIRS)).reshape(l, A_DQK)
    reps = A_QW // A_DQK
    return jnp.tile(jnp.cos(ang), (1, reps)), jnp.tile(jnp.sin(ang), (1, reps))


def _rotate_cols(w):
    j = np.arange(w.shape[1])
    lo = (j % (2 * ROPE_AXIS_PAIRS)) < ROPE_AXIS_PAIRS
    perm = np.where(lo, j + ROPE_AXIS_PAIRS, j - ROPE_AXIS_PAIRS)
    sign = np.where(lo, -1.0, 1.0).astype(np.float32)
    return w[:, perm] * sign


def _gate_cols(w_g, b_g):
    idx_i = np.array([d * 2 * B_HEADS + hd for d in range(2) for hd in range(B_HEADS)])
    idx_f = idx_i + B_HEADS
    pad = LANES - _NCHAIN
    k = w_g.shape[0]
    w = jnp.concatenate([w_g[:, idx_i], jnp.zeros((k, pad), f32),
                         w_g[:, idx_f], jnp.zeros((k, pad), f32)], axis=1)
    b = jnp.concatenate([b_g[idx_i], jnp.zeros((pad,), f32), b_g[idx_f], jnp.zeros((pad,), f32)])
    return w, b


def _hyena_consts(l, d):
    j = jnp.arange(l, dtype=f32)
    bands = (POS_EMB_DIM - 1) // 2
    freqs = jnp.linspace(1e-4, bands - 1, bands, dtype=f32)
    ang = (2.0 * math.pi / l) * j[:, None] * freqs[None, :]
    z = jnp.concatenate([(j / (l - 1))[:, None], jnp.cos(ang), -jnp.sin(ang)], axis=-1)
    dist = jnp.abs(j - l // 2) / (l // 2)
    max_decay = math.log(DECAY_TARGET) / DECAY_FAST_PCT
    min_decay = math.log(DECAY_TARGET) / DECAY_SLOW_PCT
    deltas = jnp.abs(jnp.linspace(min_decay, max_decay, d, dtype=f32))
    window = jnp.exp(-dist[:, None] * deltas[None, :])
    return z, window


def _ab_mixer(src, norm, hc, w_in, conv_w, conv_b, gate_b, lam_vecs, g_a, g_b, w_out, lam_init):
    s = src.shape[1]
    w = B_WIDTH
    o = 0
    cols = {}
    for name, width in (("aq", A_QW), ("bq", w), ("bo", w), ("ak", A_QW), ("av", A_VW),
                        ("bk", w), ("bv", w), ("g", 4 * B_HEADS)):
        cols[name] = w_in[:, o:o + width]
        o += width
    cos, sin = _rope_tables(s)
    cat = lambda *ws: jnp.concatenate(ws, axis=1).astype(bf16)
    hq, q, k = _qk_rope(src, norm, cat(cols["aq"], _rotate_cols(cols["aq"]), cols["ak"],
                                       _rotate_cols(cols["ak"])), cos, sin, A_DQK ** -0.5)
    h = src if norm is None else hq
    qk = _mm_conv(h, cat(cols["bq"], cols["bk"]), conv_w, conv_b, True)
    wg, bg = _gate_cols(cols["g"], gate_b)
    vvo, gates = _mm_pair(h, cat(cols["av"], cols["bv"], cols["bo"]), wg.astype(bf16), bg)
    ckv, cg = _mm_pair(hc, cat(cols["ak"], cols["av"], cols["bv"]), wg.astype(bf16), bg)
    cbk = _mm_conv(hc, cols["bk"].astype(bf16), conv_w[:, w:], conv_b[w:], True)
    out_a = _attn(lam_vecs, q, k, vvo, ckv, g_a, lam_init)
    out_b = _mlstm(qk, vvo, gates, cbk, ckv, cg, g_b)
    wo = w_out.astype(bf16)
    return [out_a, out_b], [wo[:A_VW], wo[A_VW:]]


def _hyena_mixer(h, w_in, conv_w, conv_b, fw1, fb1, fw2, fb2, fw3, fbias, w_out):
    _, l, d = h.shape
    u = _mm_conv(h, w_in.astype(bf16), conv_w, conv_b, False)
    z, window = _hyena_consts(l, d)
    pz, ph = LANES - z.shape[1], LANES - fw1.shape[1]
    filt = _filters(jnp.pad(z, ((0, 0), (0, pz))), jnp.pad(fw1, ((0, pz), (0, ph))),
                    jnp.pad(fb1, (0, ph)), jnp.pad(fw2, ((0, ph), (0, ph))), jnp.pad(fb2, (0, ph)),
                    jnp.pad(fw3, ((0, ph), (0, 0))), window)
    tabs = _dft_tables(l)
    kf = _spectrum(filt, tabs)
    zz = _fftconv(u, 0, u, d, kf, 0, fbias[0], tabs, d)
    y = _fftconv(zz, 0, u, 2 * d, kf, d, fbias[1], tabs, d)
    return [y], [w_out.astype(bf16)]


def kernel(x, c, ctx, c_ctx, w_mod, b_mod, norm_g, w_in_ab, conv_ab_w, conv_ab_b, gate_b_ab, diff_lambda, head_g_a, head_g_b, w_out_ab, w_in_hy, conv_hy_w, conv_hy_b, filt_w1, filt_b1, filt_w2, filt_b2, filt_w3, filt_bias, w_out_hy, router_w, router_b, exp_gu, exp_down, sh_gu, sh_down):
    bsz, s, d = x.shape
    depth = w_mod.shape[0]
    rows = -(-(bsz + 1) // 8) * 8
    cc = jnp.concatenate([c, c_ctx[None, :], jnp.zeros((rows - bsz - 1, d), f32)], axis=0)
    mods = [_mod(cc, w_mod[l], b_mod[l]) for l in range(depth)]
    vec = lambda l, i: mods[l][:bsz, i * d:(i + 1) * d].reshape(bsz, 1, d)
    h = None
    for l in range(depth):
        g_m, sh_f, sc_f, g_f = [vec(l, i) for i in range(2, 6)]
        pre = (norm_g[l, 0], vec(l, 0), vec(l, 1))
        if l % 2 == 0:
            e = l // 2
            lam_init = 0.8 - 0.6 * math.exp(-0.3 * l)
            row_c = lambda i: mods[l][bsz:bsz + 1, i * d:(i + 1) * d].reshape(1, 1, d)
            hc = _norm(ctx, norm_g[l, 0], row_c(0), row_c(1))
            src, norm = (x, pre) if h is None else (h, None)
            acts, ws = _ab_mixer(src, norm, hc, w_in_ab[e], conv_ab_w[e], conv_ab_b[e], gate_b_ab[e],
                                 diff_lambda[e], head_g_a[e], head_g_b[e], w_out_ab[e], lam_init)
        else:
            o = l // 2
            if h is None:
                h = _norm(x, *pre)
            acts, ws = _hyena_mixer(h, w_in_hy[o], conv_hy_w[o], conv_hy_b[o], filt_w1[o], filt_b1[o],
                                    filt_w2[o], filt_b2[o], filt_w3[o], filt_bias[o], w_out_hy[o])
        x, h2, rk, gt, cmax = _out_proj_route(acts, ws, x, norm_g[l, 1], g_m, norm_g[l, 2], sh_f, sc_f,
                                              router_w[l].T, router_b[l])
        nxt = (norm_g[l + 1, 0], vec(l + 1, 0), vec(l + 1, 1)) if l + 1 < depth else None
        routed = _moe_routed(h2, rk, gt, cmax, exp_gu[l].astype(bf16), exp_down[l].astype(bf16))
        outs = _moe_finish(h2, routed, sh_gu[l].astype(bf16), sh_down[l].astype(bf16), x, norm_g[l, 3],
                           g_f, nxt)
        x = outs[0]
        h = outs[1] if nxt is not None else None
    return x
```

```python
import functools
import math

import numpy as np
import jax
import jax.numpy as jnp
from jax import lax
from jax.experimental import pallas as pl
from jax.experimental.pallas import tpu as pltpu

f32 = jnp.float32
bf16 = jnp.bfloat16

RMS_EPS = 1e-6
A_HEADS = 4
A_DQK = 64
A_DV = 128
B_HEADS = 4
B_DH = 128
B_WIDTH = B_HEADS * B_DH
A_QW = A_HEADS * 2 * A_DQK
A_VW = A_HEADS * A_DV
GRID_W = 64
ROPE_BASE = 10000.0
ROPE_AXIS_PAIRS = A_DQK // 4
N_EXPERTS = 64
TOP_K = 8
N_GROUPS = 8
TOPK_GROUPS = 4
ROUTED_SCALE = 2.5
POS_EMB_DIM = 33
FILTER_SIN_W = 1.0
DECAY_FAST_PCT = 0.3
DECAY_SLOW_PCT = 1.5
DECAY_TARGET = 1e-2

LANES = 128
VMEM_LIMIT = 56 * 1024 * 1024
MLSTM_CHUNK = 256
FFT_PAD = 8
MOE_SUB = 256
MOE_WINDOWS = (16, 32, 48, 64)
MOE_GROUP = 4


def _cp(*sem):
    return pltpu.CompilerParams(dimension_semantics=sem, vmem_limit_bytes=VMEM_LIMIT)


def _split_bf16(a):
    hi = a.astype(bf16)
    lo = (a - hi.astype(f32)).astype(bf16)
    return hi, lo


def _dot(a, b, dims=(((1,), (0,)), ((), ()))):
    return lax.dot_general(a, b, dims, preferred_element_type=f32)


_NT = (((1,), (1,)), ((), ()))
_TN = (((0,), (0,)), ((), ()))


def _dot3(a, b, dims=(((1,), (0,)), ((), ()))):
    ah, al = _split_bf16(a)
    bh, bl = _split_bf16(b)
    return _dot(ah, bh, dims) + (_dot(ah, bl, dims) + _dot(al, bh, dims))


def _silu(v):
    return v / (1.0 + jnp.exp(-v))


def _sigmoid(v):
    return 1.0 / (1.0 + jnp.exp(-v))


def _log_sigmoid(v):
    return jnp.minimum(v, 0.0) - jnp.log(1.0 + jnp.exp(-jnp.abs(v)))


def _mod_kernel(c_ref, w_ref, b_ref, o_ref):
    o_ref[...] = _dot3(_silu(c_ref[...]), w_ref[...]) + b_ref[...]


def _mod(cc, w, b):
    rows, d = cc.shape
    n = w.shape[1]
    tn = d
    return pl.pallas_call(
        _mod_kernel,
        grid=(n // tn,),
        in_specs=[pl.BlockSpec((rows, d), lambda j: (0, 0)),
                  pl.BlockSpec((d, tn), lambda j: (0, j)),
                  pl.BlockSpec((1, tn), lambda j: (0, j))],
        out_specs=pl.BlockSpec((rows, tn), lambda j: (0, j)),
        out_shape=jax.ShapeDtypeStruct((rows, n), f32),
        compiler_params=_cp("arbitrary"),
        name="mod",
    )(cc, w, b.reshape(1, n))


def _norm_mod(xv, g, shift, scale):
    y = xv * lax.rsqrt(jnp.mean(xv * xv, axis=-1, keepdims=True) + RMS_EPS)
    return (y * g) * (1.0 + scale) + shift


def _norm_kernel(x_ref, g_ref, sh_ref, sc_ref, o_ref):
    o_ref[...] = _norm_mod(x_ref[...], g_ref[...], sh_ref[...], sc_ref[...]).astype(o_ref.dtype)


def _bidx(arr):
    if arr.shape[0] == 1:
        return lambda b, *_: (0, 0, 0)
    return lambda b, *_: (b, 0, 0)


def _norm(x, g, shift, scale, tl=512):
    bsz, l, d = x.shape
    tl = min(tl, l)
    return pl.pallas_call(
        _norm_kernel,
        grid=(bsz, l // tl),
        in_specs=[pl.BlockSpec((None, tl, d), lambda b, i: (b, i, 0)),
                  pl.BlockSpec((1, d), lambda b, i: (0, 0)),
                  pl.BlockSpec((None, 1, d), _bidx(shift)),
                  pl.BlockSpec((None, 1, d), _bidx(scale))],
        out_specs=pl.BlockSpec((None, tl, d), lambda b, i: (b, i, 0)),
        out_shape=jax.ShapeDtypeStruct((bsz, l, d), bf16),
        compiler_params=_cp("parallel", "parallel"),
        name="norm",
    )(x, g.reshape(1, d), shift, scale)


def _mm_pair_kernel(h_ref, wa_ref, wb_ref, bb_ref, oa_ref, ob_ref):
    hb = h_ref[...]
    oa_ref[...] = _dot(hb, wa_ref[...]).astype(oa_ref.dtype)
    ob_ref[...] = (_dot(hb, wb_ref[...]) + bb_ref[...]).astype(ob_ref.dtype)


def _mm_pair(h, wa, wb, bias_b, tl=512):
    bsz, l, k = h.shape
    na, nb = wa.shape[1], wb.shape[1]
    tl = min(tl, l)
    row = lambda b, i: (b, i, 0)
    fix2 = lambda b, i: (0, 0)
    return pl.pallas_call(
        _mm_pair_kernel,
        grid=(bsz, l // tl),
        in_specs=[pl.BlockSpec((None, tl, k), row), pl.BlockSpec((k, na), fix2),
                  pl.BlockSpec((k, nb), fix2), pl.BlockSpec((1, nb), fix2)],
        out_specs=[pl.BlockSpec((None, tl, na), row), pl.BlockSpec((None, tl, nb), row)],
        out_shape=[jax.ShapeDtypeStruct((bsz, l, na), bf16), jax.ShapeDtypeStruct((bsz, l, nb), f32)],
        compiler_params=_cp("parallel", "parallel"),
        name="mm_pair",
    )(h, wa, wb, bias_b.reshape(1, nb))


def _qk_rope_kernel(*refs, q_scale, normed):
    if normed:
        x_ref, g_ref, sh_ref, sc_ref, w_ref, cos_ref, sin_ref, h_ref, q_ref, k_ref = refs
        hb = _norm_mod(x_ref[...], g_ref[...], sh_ref[...], sc_ref[...]).astype(bf16)
        h_ref[...] = hb
    else:
        x_ref, w_ref, cos_ref, sin_ref, q_ref, k_ref = refs
        hb = x_ref[...]
    p = _dot(hb, w_ref[...])
    n = q_ref.shape[-1]
    cos, sin = cos_ref[...], sin_ref[...]
    q_ref[...] = ((p[:, :n] * cos + p[:, n:2 * n] * sin) * q_scale).astype(q_ref.dtype)
    k_ref[...] = (p[:, 2 * n:3 * n] * cos + p[:, 3 * n:] * sin).astype(k_ref.dtype)


def _qk_rope(src, norm, w, cos, sin, q_scale, tl=512):
    bsz, l, d = src.shape
    n = w.shape[1] // 4
    tl = min(tl, l)
    row = lambda i, b: (b, i, 0)
    fix2 = lambda i, b: (0, 0)
    tab = lambda i, b: (i, 0)
    in_specs = [pl.BlockSpec((None, tl, d), row)]
    args = [src]
    out_specs = [pl.BlockSpec((None, tl, n), row)] * 2
    out_shape = [jax.ShapeDtypeStruct((bsz, l, n), bf16)] * 2
    if norm is not None:
        g, shift, scale = norm
        bvec = lambda a: (lambda i, b: (b if a.shape[0] > 1 else 0, 0, 0))
        in_specs += [pl.BlockSpec((1, d), fix2), pl.BlockSpec((None, 1, d), bvec(shift)),
                     pl.BlockSpec((None, 1, d), bvec(scale))]
        args += [g.reshape(1, d), shift, scale]
        out_specs = [pl.BlockSpec((None, tl, d), row)] + out_specs
        out_shape = [jax.ShapeDtypeStruct((bsz, l, d), bf16)] + out_shape
    in_specs += [pl.BlockSpec(w.shape, fix2), pl.BlockSpec((tl, n), tab), pl.BlockSpec((tl, n), tab)]
    args += [w, cos, sin]
    outs = pl.pallas_call(
        functools.partial(_qk_rope_kernel, q_scale=q_scale, normed=norm is not None),
        grid=(l // tl, bsz),
        in_specs=in_specs,
        out_specs=out_specs,
        out_shape=out_shape,
        compiler_params=_cp("parallel", "parallel"),
        name="qk_rope",
    )(*args)
    return outs if norm is not None else [None] + list(outs)


def _mm_conv_kernel(h_ref, w_ref, cw_ref, cb_ref, o_ref, scr_ref, *, act, rc):
    l = h_ref.shape[0]
    w = w_ref[...]
    w0, w1, w2, cb = cw_ref[0:1, :], cw_ref[1:2, :], cw_ref[2:3, :], cb_ref[...]
    halo = 16
    zrow = jnp.zeros((8, o_ref.shape[1]), f32)
    for c in range(l // rc):
        lo, hi = max(c * rc - halo, 0), min((c + 1) * rc + halo, l)
        n = hi - lo
        scr = scr_ref.at[c % 2]
        scr[8:8 + n, :] = _dot(h_ref[lo:hi, :], w)
        if lo == 0:
            scr[0:8, :] = zrow
        if hi == l:
            scr[8 + n:16 + n, :] = zrow
        off = 8 + c * rc - lo
        y = (scr[off - 1:off - 1 + rc, :] * w0 + scr[off:off + rc, :] * w1
             + scr[off + 1:off + 1 + rc, :] * w2 + cb)
        o_ref[c * rc:(c + 1) * rc, :] = (_silu(y) if act else y).astype(o_ref.dtype)


def _mm_conv(h, w, cw, cb, act, tn=512):
    bsz, l, k = h.shape
    n = w.shape[1]
    tn = min(tn, n)
    rc = min(512, l)
    return pl.pallas_call(
        functools.partial(_mm_conv_kernel, act=act, rc=rc),
        grid=(bsz, n // tn),
        in_specs=[pl.BlockSpec((None, l, k), lambda b, j: (b, 0, 0)),
                  pl.BlockSpec((k, tn), lambda b, j: (0, j)),
                  pl.BlockSpec((3, tn), lambda b, j: (0, j)),
                  pl.BlockSpec((1, tn), lambda b, j: (0, j))],
        out_specs=pl.BlockSpec((None, l, tn), lambda b, j: (b, 0, j)),
        out_shape=jax.ShapeDtypeStruct((bsz, l, n), bf16),
        scratch_shapes=[pltpu.VMEM((2, rc + 48, tn), f32)],
        compiler_params=_cp("parallel", "arbitrary"),
        name="mm_conv",
    )(h, w, cw, cb.reshape(1, n))


def _attn_kernel(lv_ref, q_ref, kc_ref, k_ref, vc_ref, v_ref, g_ref, o_ref, *, lam_init):
    tq = q_ref.shape[0]
    lv = lv_ref[...]
    lam = (jnp.exp(jnp.sum(lv[0:1] * lv[1:2], axis=1, keepdims=True))
           - jnp.exp(jnp.sum(lv[2:3] * lv[3:4], axis=1, keepdims=True)) + lam_init)
    first = lax.broadcasted_iota(jnp.int32, (tq, A_DV), 1) < A_DQK
    ones_c = _ones_block(kc_ref.shape[0])
    ones_l = _ones_block(k_ref.shape[0])
    for hd in range(A_HEADS):
        cs = slice(hd * A_DV, (hd + 1) * A_DV)
        qh = q_ref[:, cs]
        zero = jnp.zeros_like(qh)
        q2 = jnp.concatenate([jnp.where(first, qh, zero), jnp.where(first, zero, qh)], axis=0)
        s_c = _dot(q2, kc_ref[:, cs], _NT)
        s_l = _dot(q2, k_ref[:, cs], _NT)
        m = jnp.maximum(jnp.max(s_c, axis=1, keepdims=True), jnp.max(s_l, axis=1, keepdims=True))
        p_c = jnp.exp((s_c - m).astype(bf16))
        p_l = jnp.exp((s_l - m).astype(bf16))
        oa = (_dot(p_c, jnp.concatenate([vc_ref[:, cs], ones_c], axis=1))
              + _dot(p_l, jnp.concatenate([v_ref[:, cs], ones_l], axis=1)))
        on = oa[:, :A_DV] * (1.0 / oa[:, A_DV:A_DV + 1])
        o = on[:tq] - lam * on[tq:]
        o = o * lax.rsqrt(jnp.mean(o * o, axis=1, keepdims=True) + RMS_EPS)
        o_ref[:, cs] = (o * g_ref[:, cs] * (1.0 - lam_init)).astype(o_ref.dtype)


def _attn(lv, q, k, vvo, ckv, g_a, lam_init, tq=256):
    bsz, s, _ = q.shape
    lc = ckv.shape[1]
    tq = min(tq, s)
    w = A_QW
    return pl.pallas_call(
        functools.partial(_attn_kernel, lam_init=lam_init),
        grid=(bsz, s // tq),
        in_specs=[pl.BlockSpec(lv.shape, lambda b, i: (0, 0)),
                  pl.BlockSpec((None, tq, w), lambda b, i: (b, i, 0)),
                  pl.BlockSpec((None, lc, w), lambda b, i: (b, 0, 0)),
                  pl.BlockSpec((None, s, w), lambda b, i: (b, 0, 0)),
                  pl.BlockSpec((None, lc, w), lambda b, i: (b, 0, 1)),
                  pl.BlockSpec((None, s, w), lambda b, i: (b, 0, 0)),
                  pl.BlockSpec((1, w), lambda b, i: (0, 0))],
        out_specs=pl.BlockSpec((None, tq, w), lambda b, i: (b, i, 0)),
        out_shape=jax.ShapeDtypeStruct((bsz, s, w), bf16),
        compiler_params=_cp("parallel", "arbitrary"),
        name="diff_attn",
    )(lv, q, ckv, k, ckv, vvo, g_a.reshape(1, w))


_LN_QSCALE = math.log(B_DH ** -0.5)
_NCHAIN = 2 * B_HEADS


def _chunk_gate_sums(gi, gf, tri):
    lf = _log_sigmoid(gf)
    hi, lo = _split_bf16(lf)
    cum = _dot(tri, hi) + _dot(tri, lo)
    t = gf.shape[0]
    tot = cum[t - 1:t, :]
    rcum = tot - cum + lf
    fwd = lax.broadcasted_iota(jnp.int32, gf.shape, 1) < B_HEADS
    bd = jnp.where(fwd, cum, rcum)
    return bd, tot, (bd - gi).T


def _lower_tri(t):
    r = lax.broadcasted_iota(jnp.int32, (t, t), 0)
    c = lax.broadcasted_iota(jnp.int32, (t, t), 1)
    return r, c


def _ones_block(t):
    one0 = jnp.where(lax.broadcasted_iota(jnp.int32, (1, LANES), 1) == 0, 1.0, 0.0).astype(bf16)
    return jnp.broadcast_to(one0, (t, LANES))


def _absorb(c_ref, m_ref, ch, x_row, tot_c, kb, vaug):
    m_prev = m_ref[ch][:, 0:1]
    g = tot_c - x_row
    m_new = jnp.maximum(tot_c + m_prev, jnp.max(g, axis=1, keepdims=True))
    wgt = jnp.exp(g - m_new)
    decay = jnp.exp(tot_c + m_prev - m_new)
    kw_t = kb.astype(f32).T * wgt
    c_ref[ch] = decay * c_ref[ch] + _dot(kw_t.astype(bf16), vaug)
    m_ref[ch] = jnp.broadcast_to(m_new, m_ref.shape[1:])


def _mlstm_kernel(qk_ref, vvo_ref, g_ref, ck_ref, ckv_ref, cg_ref, gb_ref, o_ref,
                  hf_ref, hb_ref, c_ref, m_ref, *, tc):
    s = o_ref.shape[0]
    lc = ck_ref.shape[0]
    nc = s // tc
    w = B_WIDTH
    dh = B_DH

    c_ref[...] = jnp.zeros_like(c_ref)
    m_ref[...] = jnp.zeros_like(m_ref)

    r, cidx = _lower_tri(lc)
    tri_c = jnp.where(cidx <= r, 1.0, 0.0).astype(bf16)
    cg = cg_ref[...]
    _, tot, xt = _chunk_gate_sums(cg[:, :LANES], cg[:, LANES:], tri_c)
    ones_c = _ones_block(lc)
    for ch in range(_NCHAIN):
        hs = slice((ch % B_HEADS) * dh, (ch % B_HEADS + 1) * dh)
        vs = slice(2 * w + (ch % B_HEADS) * dh, 2 * w + (ch % B_HEADS + 1) * dh)
        _absorb(c_ref, m_ref, ch, xt[ch:ch + 1, :], tot[:, ch:ch + 1], ck_ref[:, hs],
                jnp.concatenate([ckv_ref[:, vs], ones_c], axis=1))
    ones_t = _ones_block(tc)

    r, cidx = _lower_tri(tc)
    tri = jnp.where(cidx <= r, 1.0, 0.0).astype(bf16)
    causal = cidx <= r
    anti = cidx >= r

    def step(i, carry):
        for d in range(2):
            row0 = pl.multiple_of((i if d == 0 else nc - 1 - i) * tc, tc)
            rows = pl.ds(row0, tc)
            gch = g_ref[rows, :]
            gi = gch[:, :LANES]
            bd, tot, xt = _chunk_gate_sums(gi, gch[:, LANES:], tri)
            mask = causal if d == 0 else anti
            dst = hf_ref if d == 0 else hb_ref
            for hd in range(B_HEADS):
                ch = d * B_HEADS + hd
                hs = slice(hd * dh, (hd + 1) * dh)
                qb = qk_ref[rows, hs]
                kb = qk_ref[rows, slice(w + hd * dh, w + (hd + 1) * dh)]
                vaug = jnp.concatenate([vvo_ref[rows, slice(w + hd * dh, w + (hd + 1) * dh)], ones_t],
                                       axis=1)
                bcol = bd[:, ch:ch + 1]
                x_row = xt[ch:ch + 1, :]
                dmat = jnp.where(mask, bcol - x_row, -jnp.inf)
                m_prev = m_ref[ch][:, 0:1]
                inter = bcol + m_prev
                m_t = jnp.maximum(inter, jnp.max(dmat, axis=1, keepdims=True))
                e = jnp.exp(dmat - m_t + _LN_QSCALE)
                smat = _dot(qb, kb, _NT) * e
                sc = jnp.exp(inter - m_t + _LN_QSCALE)
                both = sc * _dot(qb, c_ref[ch].astype(bf16)) + _dot(smat.astype(bf16), vaug)
                den = both[:, dh:dh + 1]
                dst[rows, hs] = both[:, :dh] * (1.0 / jnp.maximum(jnp.abs(den), jnp.exp(-m_t)))
                _absorb(c_ref, m_ref, ch, x_row, tot[:, ch:ch + 1], kb, vaug)
        return carry

    lax.fori_loop(0, nc, step, 0)

    for hd in range(B_HEADS):
        hs = slice(hd * dh, (hd + 1) * dh)
        hsum = hf_ref[:, hs] + hb_ref[:, hs]
        hn = hsum * lax.rsqrt(jnp.mean(hsum * hsum, axis=1, keepdims=True) + RMS_EPS)
        og = _sigmoid(vvo_ref[:, slice(2 * w + hd * dh, 2 * w + (hd + 1) * dh)].astype(f32))
        o_ref[:, hs] = (hn * gb_ref[:, hs] * og).astype(o_ref.dtype)


def _mlstm(qk, vvo, gates, cbk, ckv, cg, g_b):
    bsz, s, _ = qk.shape
    lc = cbk.shape[1]
    w = B_WIDTH
    tc = min(MLSTM_CHUNK, s)
    return pl.pallas_call(
        functools.partial(_mlstm_kernel, tc=tc),
        grid=(bsz,),
        in_specs=[pl.BlockSpec((None, s, 2 * w), lambda b: (b, 0, 0)),
                  pl.BlockSpec((None, s, 3 * w), lambda b: (b, 0, 0)),
                  pl.BlockSpec((None, s, 2 * LANES), lambda b: (b, 0, 0)),
                  pl.BlockSpec((None, lc, w), lambda b: (b, 0, 0)),
                  pl.BlockSpec((None, lc, 3 * w), lambda b: (b, 0, 0)),
                  pl.BlockSpec((None, lc, 2 * LANES), lambda b: (b, 0, 0)),
                  pl.BlockSpec((1, w), lambda b: (0, 0))],
        out_specs=pl.BlockSpec((None, s, w), lambda b: (b, 0, 0)),
        out_shape=jax.ShapeDtypeStruct((bsz, s, w), bf16),
        scratch_shapes=[pltpu.VMEM((s, w), f32), pltpu.VMEM((s, w), f32),
                        pltpu.VMEM((_NCHAIN, B_DH, 2 * B_DH), f32),
                        pltpu.VMEM((_NCHAIN, 1, LANES), f32)],
        compiler_params=_cp("arbitrary"),
        name="mlstm",
    )(qk, vvo, gates, cbk, ckv, cg, g_b.reshape(1, w))


def _out_kernel(*refs, n_act):
    acts = refs[:n_act]
    ws = refs[n_act:2 * n_act]
    (x_ref, g_ref, gate_ref, g2_ref, sh_ref, sc_ref, rw_ref, rb_ref,
     o_ref, h_ref, rk_ref, gt_ref, cm_ref) = refs[2 * n_act:]
    mix = _dot(acts[0][...], ws[0][...])
    for a, wr in zip(acts[1:], ws[1:]):
        mix = mix + _dot(a[...], wr[...])
    y = mix * lax.rsqrt(jnp.mean(mix * mix, axis=-1, keepdims=True) + RMS_EPS) * g_ref[...]
    xn = x_ref[...] + gate_ref[...] * y
    o_ref[...] = xn
    _route(xn, g2_ref[...], sh_ref[...], sc_ref[...], rw_ref, rb_ref, h_ref, rk_ref, gt_ref, cm_ref)


def _out_proj_route(acts, ws, x, g, gate, g2, shift, scale, rw_t, rb, tl=512):
    bsz, l, d = x.shape
    tl = min(tl, l)
    nl = l // tl
    n_act = len(acts)
    row = lambda b, i: (b, i, 0)
    fix2 = lambda b, i: (0, 0)
    in_specs = [pl.BlockSpec((None, tl, a.shape[2]), row) for a in acts]
    in_specs += [pl.BlockSpec(wm.shape, fix2) for wm in ws]
    in_specs += [pl.BlockSpec((None, tl, d), row),
                 pl.BlockSpec((1, d), fix2),
                 pl.BlockSpec((None, 1, d), _bidx(gate)),
                 pl.BlockSpec((1, d), fix2),
                 pl.BlockSpec((None, 1, d), _bidx(shift)),
                 pl.BlockSpec((None, 1, d), _bidx(scale)),
                 pl.BlockSpec((N_EXPERTS, d), fix2),
                 pl.BlockSpec((N_EXPERTS, 1), fix2)]
    return pl.pallas_call(
        functools.partial(_out_kernel, n_act=n_act),
        grid=(bsz, nl),
        in_specs=in_specs,
        out_specs=[pl.BlockSpec((None, tl, d), row),
                   pl.BlockSpec((None, tl, d), row),
                   pl.BlockSpec((N_EXPERTS, tl), lambda b, i: (0, b * nl + i)),
                   pl.BlockSpec((N_EXPERTS, tl), lambda b, i: (0, b * nl + i)),
                   pl.BlockSpec((None, N_EXPERTS, LANES), lambda b, i: (b * nl + i, 0, 0))],
        out_shape=[jax.ShapeDtypeStruct((bsz, l, d), f32),
                   jax.ShapeDtypeStruct((bsz, l, d), bf16),
                   jax.ShapeDtypeStruct((N_EXPERTS, bsz * l), f32),
                   jax.ShapeDtypeStruct((N_EXPERTS, bsz * l), f32),
                   jax.ShapeDtypeStruct((bsz * nl, N_EXPERTS, LANES), f32)],
        compiler_params=_cp("parallel", "parallel"),
        name="out_proj_route",
    )(*acts, *ws, x, g.reshape(1, d), gate, g2.reshape(1, d), shift, scale, rw_t,
      rb.reshape(N_EXPERTS, 1))


def _route(xv, g, shift, scale, rw_ref, rb_ref, h_ref, rk_ref, gt_ref, cm_ref):
    hf = _norm_mod(xv, g, shift, scale)
    tl = hf.shape[0]
    h_ref[...] = hf.astype(h_ref.dtype)
    per = N_EXPERTS // N_GROUPS
    logits = _dot3(rw_ref[...], hf, _NT)
    s3 = _sigmoid(logits).reshape(N_GROUPS, per, tl)
    b3 = s3 + rb_ref[...].reshape(N_GROUPS, per, 1)
    neg = -jnp.inf
    jdx = lax.broadcasted_iota(jnp.int32, b3.shape, 1)
    gdx = lax.broadcasted_iota(jnp.int32, b3.shape, 0)
    m1 = jnp.max(b3, axis=1, keepdims=True)
    f1 = jnp.min(jnp.where(b3 == m1, jdx, per), axis=1, keepdims=True)
    m2 = jnp.max(jnp.where(jdx == f1, neg, b3), axis=1, keepdims=True)
    grp = m1 + m2
    g1 = lax.broadcasted_iota(jnp.int32, grp.shape, 0)
    cnt = jnp.zeros(grp.shape, jnp.int32)
    for gp in range(N_GROUPS):
        rv = grp[gp:gp + 1]
        ahead = jnp.where(rv > grp, 1, jnp.where(rv == grp, jnp.where(g1 > gp, 1, 0), 0))
        cnt = cnt + ahead
    v = jnp.where(cnt < TOPK_GROUPS, b3, neg)
    eidx = gdx * per + jdx
    sel = jnp.zeros(b3.shape, f32)
    for _ in range(TOP_K):
        m = jnp.max(jnp.max(v, axis=1, keepdims=True), axis=0, keepdims=True)
        cand = jnp.where(v == m, eidx, N_EXPERTS)
        fi = jnp.min(jnp.min(cand, axis=1, keepdims=True), axis=0, keepdims=True)
        hit = eidx == fi
        sel = jnp.where(hit, 1.0, sel)
        v = jnp.where(hit, neg, v)
    ssel = sel * s3
    den = jnp.sum(jnp.sum(ssel, axis=1, keepdims=True), axis=0, keepdims=True)
    gt_ref[...] = ((ROUTED_SCALE * ssel) / den).reshape(N_EXPERTS, tl)
    sel2 = sel.reshape(N_EXPERTS, tl)
    r = lax.broadcasted_iota(jnp.int32, (MOE_SUB, MOE_SUB), 0)
    c = lax.broadcasted_iota(jnp.int32, (MOE_SUB, MOE_SUB), 1)
    before = jnp.where(r < c, 1.0, 0.0).astype(bf16)
    cmax = jnp.zeros((N_EXPERTS, 1), f32)
    for j in range(tl // MOE_SUB):
        sub = sel2[:, j * MOE_SUB:(j + 1) * MOE_SUB]
        rank = _dot(sub.astype(bf16), before)
        rk_ref[:, j * MOE_SUB:(j + 1) * MOE_SUB] = jnp.where(sub > 0.0, rank, -1.0)
        cmax = jnp.maximum(cmax, jnp.sum(sub, axis=1, keepdims=True))
    cm_ref[...] = jnp.broadcast_to(cmax, cm_ref.shape)


def _swiglu_act(hh):
    half = hh.shape[1] // 2
    return _silu(hh[:, :half]) * hh[:, half:]


def _moe_kernel(cnt_ref, ord_ref, h_ref, rk_ref, gt_ref, *refs):
    ng = MOE_GROUP
    gu_refs, dn_refs = refs[:ng], refs[ng:2 * ng]
    o_ref, acc_ref, xg_ref, ys_ref, p_ref, gr_ref, fill_ref = refs[2 * ng:]
    tile = pl.program_id(0)
    grp = pl.program_id(1)
    tm, d = acc_ref.shape
    ns = tm // MOE_SUB
    ktot = p_ref.shape[1]
    eids = [ord_ref[tile, grp * ng + el] for el in range(ng)]

    @pl.when(grp == 0)
    def _():
        acc_ref[...] = jnp.zeros_like(acc_ref)
        fill_ref[0] = 0

    @pl.when(jnp.logical_and(tile == 0, grp == 0))
    def _():
        ys_ref[...] = jnp.zeros_like(ys_ref)

    def combine(rows):
        for s in range(ns):
            acc_ref[s * MOE_SUB:(s + 1) * MOE_SUB, :] += _dot(
                p_ref[s, 0:rows, :], ys_ref[s, 0:rows, :], _TN)

    def flush():
        combine(ktot)
        fill_ref[0] = 0

    def expert_ffn(el, win, off):
        hh = _dot(xg_ref[el, 0:ns * win, :], gu_refs[el][...])
        gr = gr_ref[el, 0:ns * win, :]
        act = _swiglu_act(hh) * jnp.concatenate([gr] * (hh.shape[1] // (2 * LANES)), axis=1)
        y = _dot(act.astype(bf16), dn_refs[el][...]).astype(bf16)
        start = off + el * win
        if not isinstance(start, int):
            start = pl.multiple_of(start, 16)
        for s in range(ns):
            ys_ref[s, pl.ds(start, win), :] = y[s * win:(s + 1) * win]

    def one_pass(p, win, off=0, defer=False):
        base = p * win
        riota = lax.broadcasted_iota(jnp.int32, (win, MOE_SUB), 0).astype(f32)
        for s in range(ns):
            cols = slice(s * MOE_SUB, (s + 1) * MOE_SUB)
            onehots = []
            for el in range(ng):
                row = pl.ds(eids[el], 1)
                hit = (rk_ref[row, cols] - base) == riota
                onehots.append(jnp.where(hit, 1.0, 0.0).astype(bf16))
                gsel = jnp.sum(jnp.where(hit, gt_ref[row, cols], 0.0), axis=1, keepdims=True)
                gr_ref[el, s * win:(s + 1) * win, :] = jnp.broadcast_to(gsel, (win, LANES))
            pm = jnp.concatenate(onehots, axis=0)
            p_ref[s, pl.ds(off, ng * win), :] = pm
            gx = _dot(pm, h_ref[cols, :])
            for el in range(ng):
                xg_ref[el, s * win:(s + 1) * win, :] = gx[el * win:(el + 1) * win].astype(bf16)
        for el in range(ng):
            expert_ffn(el, win, off)
        if not defer:
            combine(ng * win)

    most = cnt_ref[tile, eids[0]]
    for el in range(1, ng):
        most = jnp.maximum(most, cnt_ref[tile, eids[el]])

    thin = [w for w in MOE_WINDOWS if 2 * ng * w <= ktot]

    def batched_pass(win):
        rows = ng * win
        pl.when(fill_ref[0] + rows > ktot)(flush)
        off = pl.multiple_of(fill_ref[0], 16)

        @pl.when(off == 0)
        def _():
            p_ref[...] = jnp.zeros_like(p_ref)

        one_pass(0, win, off, defer=True)
        fill_ref[0] = off + rows

    pl.when(jnp.logical_and(most > (thin[-1] if thin else 0), fill_ref[0] > 0))(flush)
    lo = 0
    for win in MOE_WINDOWS[:-1]:
        body = functools.partial(batched_pass, win) if win in thin else functools.partial(one_pass, 0, win)
        pl.when(jnp.logical_and(most > lo, most <= win))(body)
        lo = win
    big = MOE_WINDOWS[-1]

    def big_pass(p, carry):
        one_pass(p, big)
        return carry

    lax.fori_loop(0, jnp.where(most > lo, (most + big - 1) // big, 0), big_pass, 0)

    @pl.when(grp == pl.num_programs(1) - 1)
    def _():
        pl.when(fill_ref[0] > 0)(flush)
        o_ref[...] = acc_ref[...].astype(o_ref.dtype)


def _moe_routed(h2, rk, gt, cmax, gu, dn, tm=2048):
    bsz, l, d = h2.shape
    tm = min(tm, l)
    nt = bsz * (l // tm)
    ne = gu.shape[0]
    ng = MOE_GROUP
    ns = tm // MOE_SUB
    wmax = MOE_WINDOWS[-1]
    counts = jnp.max(cmax[:, :, 0].reshape(nt, -1, ne), axis=1).astype(jnp.int32)
    order = jnp.argsort(-counts, axis=1).astype(jnp.int32)

    def expert_spec(arr, k):
        return pl.BlockSpec((None,) + arr.shape[1:], lambda t, e, cnt, order_ref: (order_ref[t, e * ng + k], 0, 0))

    tile_spec = pl.BlockSpec((tm, d), lambda t, e, *_: (t, 0))
    grid_spec = pltpu.PrefetchScalarGridSpec(
        num_scalar_prefetch=2,
        grid=(nt, ne // ng),
        in_specs=([tile_spec,
                   pl.BlockSpec((ne, tm), lambda t, e, *_: (0, t)),
                   pl.BlockSpec((ne, tm), lambda t, e, *_: (0, t))]
                  + [expert_spec(gu, k) for k in range(ng)]
                  + [expert_spec(dn, k) for k in range(ng)]),
        out_specs=tile_spec,
        scratch_shapes=[pltpu.VMEM((tm, d), f32),
                        pltpu.VMEM((ng, ns * wmax, d), bf16),
                        pltpu.VMEM((ns, ng * wmax, d), bf16),
                        pltpu.VMEM((ns, ng * wmax, MOE_SUB), bf16),
                        pltpu.VMEM((ng, ns * wmax, LANES), f32),
                        pltpu.SMEM((1,), jnp.int32)])
    return pl.pallas_call(
        _moe_kernel,
        grid_spec=grid_spec,
        out_shape=jax.ShapeDtypeStruct((bsz * l, d), bf16),
        compiler_params=_cp("arbitrary", "arbitrary"),
        name="moe",
    )(counts, order, h2.reshape(bsz * l, d), rk, gt, *([gu] * ng), *([dn] * ng))


def _moe_finish_kernel(h_ref, r_ref, sgu_ref, sdn_ref, x_ref, g_ref, gate_ref, *refs, has_next):
    act = _swiglu_act(_dot(h_ref[...], sgu_ref[...]))
    mo = _dot(act.astype(bf16), sdn_ref[...]) + r_ref[...]
    y = mo * lax.rsqrt(jnp.mean(mo * mo, axis=-1, keepdims=True) + RMS_EPS) * g_ref[...]
    xn = x_ref[...] + gate_ref[...] * y
    if has_next:
        gn_ref, shn_ref, scn_ref, o_ref, hn_ref = refs
        hn_ref[...] = _norm_mod(xn, gn_ref[...], shn_ref[...], scn_ref[...]).astype(hn_ref.dtype)
    else:
        (o_ref,) = refs
    o_ref[...] = xn


def _moe_finish(h2, routed, sgu, sdn, x, g, gate, nxt=None, tl=512):
    bsz, l, d = x.shape
    tl = min(tl, l)
    row = lambda b, i: (b, i, 0)
    fix2 = lambda b, i: (0, 0)
    in_specs = [pl.BlockSpec((None, tl, d), row), pl.BlockSpec((None, tl, d), row),
                pl.BlockSpec(sgu.shape, fix2), pl.BlockSpec(sdn.shape, fix2),
                pl.BlockSpec((None, tl, d), row), pl.BlockSpec((1, d), fix2),
                pl.BlockSpec((None, 1, d), _bidx(gate))]
    args = [h2, routed.reshape(bsz, l, d), sgu, sdn, x, g.reshape(1, d), gate]
    out_specs, out_shape = [pl.BlockSpec((None, tl, d), row)], [jax.ShapeDtypeStruct((bsz, l, d), f32)]
    if nxt is not None:
        in_specs += [pl.BlockSpec((1, d), fix2), pl.BlockSpec((None, 1, d), _bidx(nxt[1])),
                     pl.BlockSpec((None, 1, d), _bidx(nxt[2]))]
        args += [nxt[0].reshape(1, d), nxt[1], nxt[2]]
        out_specs.append(pl.BlockSpec((None, tl, d), row))
        out_shape.append(jax.ShapeDtypeStruct((bsz, l, d), bf16))
    return pl.pallas_call(
        functools.partial(_moe_finish_kernel, has_next=nxt is not None),
        grid=(bsz, l // tl),
        in_specs=in_specs,
        out_specs=out_specs,
        out_shape=out_shape,
        compiler_params=_cp("parallel", "parallel"),
        name="moe_finish",
    )(*args)


def _filter_kernel(z_ref, w1_ref, b1_ref, w2_ref, b2_ref, w3_ref, win_ref, o_ref):
    hid = jnp.sin(FILTER_SIN_W * (_dot3(z_ref[...], w1_ref[...]) + b1_ref[...]))
    hid = jnp.sin(FILTER_SIN_W * (_dot3(hid, w2_ref[...]) + b2_ref[...]))
    o_ref[...] = _dot3(hid, w3_ref[...]) * win_ref[...]


def _filters(z, w1, b1, w2, b2, w3, window, tn=512):
    l, p = z.shape
    hdim = w1.shape[1]
    n = w3.shape[1]
    d = window.shape[1]
    nd = d // tn
    return pl.pallas_call(
        _filter_kernel,
        grid=(n // tn,),
        in_specs=[pl.BlockSpec((l, p), lambda j: (0, 0)),
                  pl.BlockSpec((p, hdim), lambda j: (0, 0)),
                  pl.BlockSpec((1, hdim), lambda j: (0, 0)),
                  pl.BlockSpec((hdim, hdim), lambda j: (0, 0)),
                  pl.BlockSpec((1, hdim), lambda j: (0, 0)),
                  pl.BlockSpec((hdim, tn), lambda j: (0, j)),
                  pl.BlockSpec((l, tn), lambda j: (0, j % nd))],
        out_specs=pl.BlockSpec((l, tn), lambda j: (0, j)),
        out_shape=jax.ShapeDtypeStruct((l, n), f32),
        compiler_params=_cp("arbitrary"),
        name="hyena_filter",
    )(z, w1, b1.reshape(1, hdim), w2, b2.reshape(1, hdim), w3, window)


def _dft_tables(l):
    n = 2 * l
    n1 = math.isqrt(n)
    assert n == n1 * n1 and n1 % 16 == 0
    na = l // n1
    ncp = -(-(n1 // 2 + 1) // 8) * 8
    a = np.arange(na)
    b = np.arange(n1)
    c = np.arange(ncp)
    th = 2.0 * np.pi * ((n1 * a[None, None, :] + b[:, None, None]) * c[None, :, None]) / n
    t1 = np.concatenate([np.cos(th), -np.sin(th)], axis=1)
    ph = 2.0 * np.pi * (b[:, None] * b[None, :]) / n1
    cs, sn = np.cos(ph), np.sin(ph)
    a3 = np.block([[cs, sn], [-sn, cs]])
    a3i = np.block([[cs, -sn], [sn, cs]])
    a2 = np.arange(na) + na // 2
    th2 = 2.0 * np.pi * ((n1 * a2[None, :, None] + b[:, None, None]) * c[None, None, :]) / n
    wc = np.where((c == 0) | (c == n1 // 2), 1.0, np.where(c < n1 // 2, 2.0, 0.0))[None, None, :]
    t2 = np.concatenate([wc * np.cos(th2), -wc * np.sin(th2)], axis=2)
    return [jnp.asarray(t, f32).astype(bf16) for t in (t1, a3, a3i, t2)]


def _fft_dims(t1):
    n1, ncp2, na = t1.shape
    ncp = ncp2 // 2
    return n1, ncp, na, 2 * n1 + FFT_PAD, 2 * ncp + FFT_PAD, n1 + FFT_PAD


def _ld(ref, rows):
    return jnp.concatenate([ref[j, rows, :] for j in range(ref.shape[0])], axis=1)


def _st(ref, rows, val):
    for j in range(ref.shape[0]):
        ref[j, rows, :] = val[:, j * LANES:(j + 1) * LANES]


def _dft_forward(uf_ref, t1_ref, zs_ref):
    n1, ncp, na, sb, _, su = _fft_dims(t1_ref)
    for b in range(n1):
        ub = _ld(uf_ref, pl.ds(b, na, stride=su)).astype(bf16)
        zb = _dot(t1_ref[b], ub)
        _st(zs_ref, pl.ds(b, ncp, stride=sb), zb[:ncp])
        _st(zs_ref, pl.ds(n1 + b, ncp, stride=sb), zb[ncp:])


def _spectrum_kernel(f_ref, t1_ref, a3_ref, o_ref, uf_ref, zs_ref, *, scale):
    n1, ncp, na, sb, _, su = _fft_dims(t1_ref)
    for a in range(na):
        _st(uf_ref, pl.ds(a * su, n1), f_ref[pl.ds(a * n1, n1), :])
    _dft_forward(uf_ref, t1_ref, zs_ref)
    a3 = a3_ref[...]
    for c in range(ncp):
        zc = _ld(zs_ref, pl.ds(c * sb, 2 * n1)).astype(bf16)
        o_ref[c] = (_dot(a3, zc) * scale).astype(o_ref.dtype)


def _spectrum(filt, tabs, dt=256):
    l, n = filt.shape
    t1, a3, _, _ = tabs
    n1, ncp, na, sb, _, su = _fft_dims(t1)
    nj = dt // LANES
    return pl.pallas_call(
        functools.partial(_spectrum_kernel, scale=1.0 / (2 * l)),
        grid=(n // dt,),
        in_specs=[pl.BlockSpec((l, dt), lambda j: (0, j)),
                  pl.BlockSpec(t1.shape, lambda j: (0, 0, 0)),
                  pl.BlockSpec(a3.shape, lambda j: (0, 0))],
        out_specs=pl.BlockSpec((ncp, 2 * n1, dt), lambda j: (0, 0, j)),
        out_shape=jax.ShapeDtypeStruct((ncp, 2 * n1, n), bf16),
        scratch_shapes=[pltpu.VMEM((nj, na * su, LANES), f32),
                        pltpu.VMEM((nj, ncp * sb, LANES), f32)],
        compiler_params=_cp("arbitrary"),
        name="hyena_spectrum",
    )(filt, t1, a3)


def _long_conv(kf_ref, t1_ref, a3_ref, a3i_ref, t2_ref, uf_ref, zs_ref, qs_ref, y_ref):
    n1, ncp, na, sb, sq, su = _fft_dims(t1_ref)
    _dft_forward(uf_ref, t1_ref, zs_ref)
    a3 = a3_ref[...]
    a3i = a3i_ref[...]
    for c in range(ncp):
        zc = _ld(zs_ref, pl.ds(c * sb, 2 * n1)).astype(bf16)
        xc = _dot(a3, zc)
        kc = kf_ref[c].astype(f32)
        xr, xi = xc[:n1], xc[n1:]
        kr, ki = kc[:n1], kc[n1:]
        pc = jnp.concatenate([xr * kr - xi * ki, xr * ki + xi * kr], axis=0).astype(bf16)
        qc = _dot(a3i, pc)
        _st(qs_ref, pl.ds(c, n1, stride=sq), qc[:n1])
        _st(qs_ref, pl.ds(ncp + c, n1, stride=sq), qc[n1:])
    for b in range(n1):
        qb = _ld(qs_ref, pl.ds(b * sq, 2 * ncp)).astype(bf16)
        _st(y_ref, pl.ds(b, na, stride=su), _dot(t2_ref[b], qb))


def _hyena_kernel(v_ref, x1_ref, x2_ref, kf0_ref, kf1_ref, fb_ref, t1_ref, a3_ref, a3i_ref, t2_ref,
                  o_ref, uf_ref, zs_ref, qs_ref, y_ref):
    n1, _, na, _, _, su = _fft_dims(t1_ref)
    for a in range(na):
        _st(uf_ref, pl.ds(a * su, n1), v_ref[pl.ds(a * n1, n1), :].astype(f32))
    stages = ((kf0_ref, x1_ref, fb_ref[0:1, :]), (kf1_ref, x2_ref, fb_ref[1:2, :]))
    for i, (kf_ref, xg_ref, fb) in enumerate(stages):
        _long_conv(kf_ref, t1_ref, a3_ref, a3i_ref, t2_ref, uf_ref, zs_ref, qs_ref, y_ref)
        for a in range(na):
            rows = pl.ds(a * n1, n1)
            slab = pl.ds(a * su, n1)
            g = (xg_ref[rows, :].astype(f32) * (_ld(y_ref, slab) + _ld(uf_ref, slab) * fb)).astype(bf16)
            if i == 0:
                _st(uf_ref, slab, g.astype(f32))
            else:
                o_ref[rows, :] = g


def _hyena_convs(u, kf, fbias, tabs, d, dt=256):
    bsz, l, _ = u.shape
    t1, a3, a3i, t2 = tabs
    n1, ncp, na, sb, sq, su = _fft_dims(t1)
    nd = d // dt
    nj = dt // LANES
    ucol = lambda k: pl.BlockSpec((None, l, dt), lambda j, b: (b, 0, j + k * nd))
    kcol = lambda k: pl.BlockSpec((ncp, 2 * n1, dt), lambda j, b: (0, 0, j + k * nd))
    return pl.pallas_call(
        _hyena_kernel,
        grid=(nd, bsz),
        in_specs=[ucol(0), ucol(1), ucol(2), kcol(0), kcol(1),
                  pl.BlockSpec((2, dt), lambda j, b: (0, j)),
                  pl.BlockSpec(t1.shape, lambda j, b: (0, 0, 0)),
                  pl.BlockSpec(a3.shape, lambda j, b: (0, 0)),
                  pl.BlockSpec(a3i.shape, lambda j, b: (0, 0)),
                  pl.BlockSpec(t2.shape, lambda j, b: (0, 0, 0))],
        out_specs=pl.BlockSpec((None, l, dt), lambda j, b: (b, 0, j)),
        out_shape=jax.ShapeDtypeStruct((bsz, l, d), bf16),
        scratch_shapes=[pltpu.VMEM((nj, na * su, LANES), f32),
                        pltpu.VMEM((nj, ncp * sb, LANES), f32),
                        pltpu.VMEM((nj, n1 * sq, LANES), f32),
                        pltpu.VMEM((nj, na * su, LANES), f32)],
        compiler_params=_cp("parallel", "arbitrary"),
        name="hyena_convs",
    )(u, u, u, kf, kf, fbias, t1, a3, a3i, t2)


def _rope_tables(l):
    rows = l // GRID_W
    row = jnp.repeat(jnp.arange(rows), GRID_W)
    col = jnp.tile(jnp.arange(GRID_W), rows)
    inv = ROPE_BASE ** (-jnp.arange(ROPE_AXIS_PAIRS, dtype=f32) / ROPE_AXIS_PAIRS)
    ang = jnp.stack([row, col], axis=-1).astype(f32)[..., None] * inv
    ang = jnp.broadcast_to(ang[:, :, None, :], (l, 2, 2, ROPE_AXIS_PAIRS)).reshape(l, A_DQK)
    reps = A_QW // A_DQK
    return jnp.tile(jnp.cos(ang), (1, reps)), jnp.tile(jnp.sin(ang), (1, reps))


def _rotate_cols(w):
    j = np.arange(w.shape[1])
    lo = (j % (2 * ROPE_AXIS_PAIRS)) < ROPE_AXIS_PAIRS
    perm = np.where(lo, j + ROPE_AXIS_PAIRS, j - ROPE_AXIS_PAIRS)
    sign = np.where(lo, -1.0, 1.0).astype(np.float32)
    return w[:, perm] * sign


def _gate_cols(w_g, b_g):
    idx_i = np.array([d * 2 * B_HEADS + hd for d in range(2) for hd in range(B_HEADS)])
    idx_f = idx_i + B_HEADS
    pad = LANES - _NCHAIN
    k = w_g.shape[0]
    w = jnp.concatenate([w_g[:, idx_i], jnp.zeros((k, pad), f32),
                         w_g[:, idx_f], jnp.zeros((k, pad), f32)], axis=1)
    b = jnp.concatenate([b_g[idx_i], jnp.zeros((pad,), f32), b_g[idx_f], jnp.zeros((pad,), f32)])
    return w, b


def _hyena_consts(l, d):
    j = jnp.arange(l, dtype=f32)
    bands = (POS_EMB_DIM - 1) // 2
    freqs = jnp.linspace(1e-4, bands - 1, bands, dtype=f32)
    ang = (2.0 * math.pi / l) * j[:, None] * freqs[None, :]
    z = jnp.concatenate([(j / (l - 1))[:, None], jnp.cos(ang), -jnp.sin(ang)], axis=-1)
    dist = jnp.abs(j - l // 2) / (l // 2)
    max_decay = math.log(DECAY_TARGET) / DECAY_FAST_PCT
    min_decay = math.log(DECAY_TARGET) / DECAY_SLOW_PCT
    deltas = jnp.abs(jnp.linspace(min_decay, max_decay, d, dtype=f32))
    window = jnp.exp(-dist[:, None] * deltas[None, :])
    return z, window


def _ab_mixer(src, norm, hc, w_in, conv_w, conv_b, gate_b, lam_vecs, g_a, g_b, w_out, lam_init):
    s = src.shape[1]
    w = B_WIDTH
    o = 0
    cols = {}
    for name, width in (("aq", A_QW), ("bq", w), ("bo", w), ("ak", A_QW), ("av", A_VW),
                        ("bk", w), ("bv", w), ("g", 4 * B_HEADS)):
        cols[name] = w_in[:, o:o + width]
        o += width
    cos, sin = _rope_tables(s)
    cat = lambda *ws: jnp.concatenate(ws, axis=1).astype(bf16)
    hq, q, k = _qk_rope(src, norm, cat(cols["aq"], _rotate_cols(cols["aq"]), cols["ak"],
                                       _rotate_cols(cols["ak"])), cos, sin, A_DQK ** -0.5)
    h = src if norm is None else hq
    qk = _mm_conv(h, cat(cols["bq"], cols["bk"]), conv_w, conv_b, True)
    wg, bg = _gate_cols(cols["g"], gate_b)
    vvo, gates = _mm_pair(h, cat(cols["av"], cols["bv"], cols["bo"]), wg.astype(bf16), bg)
    ckv, cg = _mm_pair(hc, cat(cols["ak"], cols["av"], cols["bv"]), wg.astype(bf16), bg)
    cbk = _mm_conv(hc, cols["bk"].astype(bf16), conv_w[:, w:], conv_b[w:], True)
    out_a = _attn(lam_vecs, q, k, vvo, ckv, g_a, lam_init)
    out_b = _mlstm(qk, vvo, gates, cbk, ckv, cg, g_b)
    wo = w_out.astype(bf16)
    return [out_a, out_b], [wo[:A_VW], wo[A_VW:]]


def _hyena_mixer(h, w_in, conv_w, conv_b, fw1, fb1, fw2, fb2, fw3, fbias, w_out):
    _, l, d = h.shape
    u = _mm_conv(h, w_in.astype(bf16), conv_w, conv_b, False)
    z, window = _hyena_consts(l, d)
    pz, ph = LANES - z.shape[1], LANES - fw1.shape[1]
    filt = _filters(jnp.pad(z, ((0, 0), (0, pz))), jnp.pad(fw1, ((0, pz), (0, ph))),
                    jnp.pad(fb1, (0, ph)), jnp.pad(fw2, ((0, ph), (0, ph))), jnp.pad(fb2, (0, ph)),
                    jnp.pad(fw3, ((0, ph), (0, 0))), window)
    tabs = _dft_tables(l)
    kf = _spectrum(filt, tabs)
    y = _hyena_convs(u, kf, fbias, tabs, d)
    return [y], [w_out.astype(bf16)]


def kernel(x, c, ctx, c_ctx, w_mod, b_mod, norm_g, w_in_ab, conv_ab_w, conv_ab_b, gate_b_ab, diff_lambda, head_g_a, head_g_b, w_out_ab, w_in_hy, conv_hy_w, conv_hy_b, filt_w1, filt_b1, filt_w2, filt_b2, filt_w3, filt_bias, w_out_hy, router_w, router_b, exp_gu, exp_down, sh_gu, sh_down):
    bsz, s, d = x.shape
    depth = w_mod.shape[0]
    rows = -(-(bsz + 1) // 8) * 8
    cc = jnp.concatenate([c, c_ctx[None, :], jnp.zeros((rows - bsz - 1, d), f32)], axis=0)
    mods = [_mod(cc, w_mod[l], b_mod[l]) for l in range(depth)]
    vec = lambda l, i: mods[l][:bsz, i * d:(i + 1) * d].reshape(bsz, 1, d)
    h = None
    for l in range(depth):
        g_m, sh_f, sc_f, g_f = [vec(l, i) for i in range(2, 6)]
        pre = (norm_g[l, 0], vec(l, 0), vec(l, 1))
        if l % 2 == 0:
            e = l // 2
            lam_init = 0.8 - 0.6 * math.exp(-0.3 * l)
            row_c = lambda i: mods[l][bsz:bsz + 1, i * d:(i + 1) * d].reshape(1, 1, d)
            hc = _norm(ctx, norm_g[l, 0], row_c(0), row_c(1))
            src, norm = (x, pre) if h is None else (h, None)
            acts, ws = _ab_mixer(src, norm, hc, w_in_ab[e], conv_ab_w[e], conv_ab_b[e], gate_b_ab[e],
                                 diff_lambda[e], head_g_a[e], head_g_b[e], w_out_ab[e], lam_init)
        else:
            o = l // 2
            if h is None:
                h = _norm(x, *pre)
            acts, ws = _hyena_mixer(h, w_in_hy[o], conv_hy_w[o], conv_hy_b[o], filt_w1[o], filt_b1[o],
                                    filt_w2[o], filt_b2[o], filt_w3[o], filt_bias[o], w_out_hy[o])
        x, h2, rk, gt, cmax = _out_proj_route(acts, ws, x, norm_g[l, 1], g_m, norm_g[l, 2], sh_f, sc_f,
                                              router_w[l].T, router_b[l])
        nxt = (norm_g[l + 1, 0], vec(l + 1, 0), vec(l + 1, 1)) if l + 1 < depth else None
        routed = _moe_routed(h2, rk, gt, cmax, exp_gu[l].astype(bf16), exp_down[l].astype(bf16))
        outs = _moe_finish(h2, routed, sh_gu[l].astype(bf16), sh_down[l].astype(bf16), x, norm_g[l, 3],
                           g_f, nxt)
        x = outs[0]
        h = outs[1] if nxt is not None else None
    return x
```

```python
import functools
import math

import numpy as np
import jax
import jax.numpy as jnp
from jax import lax
from jax.experimental import pallas as pl
from jax.experimental.pallas import tpu as pltpu

f32 = jnp.float32
bf16 = jnp.bfloat16

RMS_EPS = 1e-6
A_HEADS = 4
A_DQK = 64
A_DV = 128
B_HEADS = 4
B_DH = 128
B_WIDTH = B_HEADS * B_DH
A_QW = A_HEADS * 2 * A_DQK
A_VW = A_HEADS * A_DV
GRID_W = 64
ROPE_BASE = 10000.0
ROPE_AXIS_PAIRS = A_DQK // 4
N_EXPERTS = 64
TOP_K = 8
N_GROUPS = 8
TOPK_GROUPS = 4
ROUTED_SCALE = 2.5
POS_EMB_DIM = 33
FILTER_SIN_W = 1.0
DECAY_FAST_PCT = 0.3
DECAY_SLOW_PCT = 1.5
DECAY_TARGET = 1e-2

LANES = 128
VMEM_LIMIT = 56 * 1024 * 1024
MLSTM_CHUNK = 256
FFT_PAD = 8
MOE_SUB = 256
MOE_WINDOWS = (16, 32, 48, 64)
MOE_GROUP = 4


def _cp(*sem):
    return pltpu.CompilerParams(dimension_semantics=sem, vmem_limit_bytes=VMEM_LIMIT)


def _split_bf16(a):
    hi = a.astype(bf16)
    lo = (a - hi.astype(f32)).astype(bf16)
    return hi, lo


def _dot(a, b, dims=(((1,), (0,)), ((), ()))):
    return lax.dot_general(a, b, dims, preferred_element_type=f32)


_NT = (((1,), (1,)), ((), ()))
_TN = (((0,), (0,)), ((), ()))


def _dot3(a, b, dims=(((1,), (0,)), ((), ()))):
    ah, al = _split_bf16(a)
    bh, bl = _split_bf16(b)
    return _dot(ah, bh, dims) + (_dot(ah, bl, dims) + _dot(al, bh, dims))


def _silu(v):
    return v / (1.0 + jnp.exp(-v))


def _sigmoid(v):
    return 1.0 / (1.0 + jnp.exp(-v))


def _log_sigmoid(v):
    return jnp.minimum(v, 0.0) - jnp.log(1.0 + jnp.exp(-jnp.abs(v)))


def _mod_kernel(c_ref, w_ref, b_ref, o_ref):
    o_ref[...] = _dot3(_silu(c_ref[...]), w_ref[...]) + b_ref[...]


def _mod(cc, w, b):
    rows, d = cc.shape
    n = w.shape[1]
    tn = d
    return pl.pallas_call(
        _mod_kernel,
        grid=(n // tn,),
        in_specs=[pl.BlockSpec((rows, d), lambda j: (0, 0)),
                  pl.BlockSpec((d, tn), lambda j: (0, j)),
                  pl.BlockSpec((1, tn), lambda j: (0, j))],
        out_specs=pl.BlockSpec((rows, tn), lambda j: (0, j)),
        out_shape=jax.ShapeDtypeStruct((rows, n), f32),
        compiler_params=_cp("arbitrary"),
        name="mod",
    )(cc, w, b.reshape(1, n))


def _norm_mod(xv, g, shift, scale):
    y = xv * lax.rsqrt(jnp.mean(xv * xv, axis=-1, keepdims=True) + RMS_EPS)
    return (y * g) * (1.0 + scale) + shift


def _norm_kernel(x_ref, g_ref, sh_ref, sc_ref, o_ref):
    o_ref[...] = _norm_mod(x_ref[...], g_ref[...], sh_ref[...], sc_ref[...]).astype(o_ref.dtype)


def _bidx(arr):
    if arr.shape[0] == 1:
        return lambda b, *_: (0, 0, 0)
    return lambda b, *_: (b, 0, 0)


def _norm(x, g, shift, scale, tl=512):
    bsz, l, d = x.shape
    tl = min(tl, l)
    return pl.pallas_call(
        _norm_kernel,
        grid=(bsz, l // tl),
        in_specs=[pl.BlockSpec((None, tl, d), lambda b, i: (b, i, 0)),
                  pl.BlockSpec((1, d), lambda b, i: (0, 0)),
                  pl.BlockSpec((None, 1, d), _bidx(shift)),
                  pl.BlockSpec((None, 1, d), _bidx(scale))],
        out_specs=pl.BlockSpec((None, tl, d), lambda b, i: (b, i, 0)),
        out_shape=jax.ShapeDtypeStruct((bsz, l, d), bf16),
        compiler_params=_cp("parallel", "parallel"),
        name="norm",
    )(x, g.reshape(1, d), shift, scale)


def _mm_pair_kernel(h_ref, wa_ref, wb_ref, bb_ref, oa_ref, ob_ref):
    hb = h_ref[...]
    oa_ref[...] = _dot(hb, wa_ref[...]).astype(oa_ref.dtype)
    ob_ref[...] = (_dot(hb, wb_ref[...]) + bb_ref[...]).astype(ob_ref.dtype)


def _mm_pair(h, wa, wb, bias_b, tl=512):
    bsz, l, k = h.shape
    na, nb = wa.shape[1], wb.shape[1]
    tl = min(tl, l)
    row = lambda b, i: (b, i, 0)
    fix2 = lambda b, i: (0, 0)
    return pl.pallas_call(
        _mm_pair_kernel,
        grid=(bsz, l // tl),
        in_specs=[pl.BlockSpec((None, tl, k), row), pl.BlockSpec((k, na), fix2),
                  pl.BlockSpec((k, nb), fix2), pl.BlockSpec((1, nb), fix2)],
        out_specs=[pl.BlockSpec((None, tl, na), row), pl.BlockSpec((None, tl, nb), row)],
        out_shape=[jax.ShapeDtypeStruct((bsz, l, na), bf16), jax.ShapeDtypeStruct((bsz, l, nb), f32)],
        compiler_params=_cp("parallel", "parallel"),
        name="mm_pair",
    )(h, wa, wb, bias_b.reshape(1, nb))


def _qk_rope_kernel(*refs, q_scale, normed):
    if normed:
        x_ref, g_ref, sh_ref, sc_ref, w_ref, cos_ref, sin_ref, h_ref, q_ref, k_ref = refs
        hb = _norm_mod(x_ref[...], g_ref[...], sh_ref[...], sc_ref[...]).astype(bf16)
        h_ref[...] = hb
    else:
        x_ref, w_ref, cos_ref, sin_ref, q_ref, k_ref = refs
        hb = x_ref[...]
    p = _dot(hb, w_ref[...])
    n = q_ref.shape[-1]
    cos, sin = cos_ref[...], sin_ref[...]
    q_ref[...] = ((p[:, :n] * cos + p[:, n:2 * n] * sin) * q_scale).astype(q_ref.dtype)
    k_ref[...] = (p[:, 2 * n:3 * n] * cos + p[:, 3 * n:] * sin).astype(k_ref.dtype)


def _qk_rope(src, norm, w, cos, sin, q_scale, tl=512):
    bsz, l, d = src.shape
    n = w.shape[1] // 4
    tl = min(tl, l)
    row = lambda i, b: (b, i, 0)
    fix2 = lambda i, b: (0, 0)
    tab = lambda i, b: (i, 0)
    in_specs = [pl.BlockSpec((None, tl, d), row)]
    args = [src]
    out_specs = [pl.BlockSpec((None, tl, n), row)] * 2
    out_shape = [jax.ShapeDtypeStruct((bsz, l, n), bf16)] * 2
    if norm is not None:
        g, shift, scale = norm
        bvec = lambda a: (lambda i, b: (b if a.shape[0] > 1 else 0, 0, 0))
        in_specs += [pl.BlockSpec((1, d), fix2), pl.BlockSpec((None, 1, d), bvec(shift)),
                     pl.BlockSpec((None, 1, d), bvec(scale))]
        args += [g.reshape(1, d), shift, scale]
        out_specs = [pl.BlockSpec((None, tl, d), row)] + out_specs
        out_shape = [jax.ShapeDtypeStruct((bsz, l, d), bf16)] + out_shape
    in_specs += [pl.BlockSpec(w.shape, fix2), pl.BlockSpec((tl, n), tab), pl.BlockSpec((tl, n), tab)]
    args += [w, cos, sin]
    outs = pl.pallas_call(
        functools.partial(_qk_rope_kernel, q_scale=q_scale, normed=norm is not None),
        grid=(l // tl, bsz),
        in_specs=in_specs,
        out_specs=out_specs,
        out_shape=out_shape,
        compiler_params=_cp("parallel", "parallel"),
        name="qk_rope",
    )(*args)
    return outs if norm is not None else [None] + list(outs)


def _mm_conv_kernel(h_ref, w_ref, cw_ref, cb_ref, o_ref, scr_ref, *, act, rc):
    l = h_ref.shape[0]
    w = w_ref[...]
    w0, w1, w2, cb = cw_ref[0:1, :], cw_ref[1:2, :], cw_ref[2:3, :], cb_ref[...]
    halo = 16
    zrow = jnp.zeros((8, o_ref.shape[1]), f32)
    for c in range(l // rc):
        lo, hi = max(c * rc - halo, 0), min((c + 1) * rc + halo, l)
        n = hi - lo
        scr = scr_ref.at[c % 2]
        scr[8:8 + n, :] = _dot(h_ref[lo:hi, :], w)
        if lo == 0:
            scr[0:8, :] = zrow
        if hi == l:
            scr[8 + n:16 + n, :] = zrow
        off = 8 + c * rc - lo
        y = (scr[off - 1:off - 1 + rc, :] * w0 + scr[off:off + rc, :] * w1
             + scr[off + 1:off + 1 + rc, :] * w2 + cb)
        o_ref[c * rc:(c + 1) * rc, :] = (_silu(y) if act else y).astype(o_ref.dtype)


def _mm_conv(h, w, cw, cb, act, tn=512):
    bsz, l, k = h.shape
    n = w.shape[1]
    tn = min(tn, n)
    rc = min(512, l)
    return pl.pallas_call(
        functools.partial(_mm_conv_kernel, act=act, rc=rc),
        grid=(bsz, n // tn),
        in_specs=[pl.BlockSpec((None, l, k), lambda b, j: (b, 0, 0)),
                  pl.BlockSpec((k, tn), lambda b, j: (0, j)),
                  pl.BlockSpec((3, tn), lambda b, j: (0, j)),
                  pl.BlockSpec((1, tn), lambda b, j: (0, j))],
        out_specs=pl.BlockSpec((None, l, tn), lambda b, j: (b, 0, j)),
        out_shape=jax.ShapeDtypeStruct((bsz, l, n), bf16),
        scratch_shapes=[pltpu.VMEM((2, rc + 48, tn), f32)],
        compiler_params=_cp("parallel", "arbitrary"),
        name="mm_conv",
    )(h, w, cw, cb.reshape(1, n))


def _attn_kernel(lv_ref, q_ref, kc_ref, k_ref, vc_ref, v_ref, g_ref, o_ref, *, lam_init):
    tq = q_ref.shape[0]
    lv = lv_ref[...]
    lam = (jnp.exp(jnp.sum(lv[0:1] * lv[1:2], axis=1, keepdims=True))
           - jnp.exp(jnp.sum(lv[2:3] * lv[3:4], axis=1, keepdims=True)) + lam_init)
    first = lax.broadcasted_iota(jnp.int32, (tq, A_DV), 1) < A_DQK
    ones_c = _ones_block(kc_ref.shape[0])
    ones_l = _ones_block(k_ref.shape[0])
    for hd in range(A_HEADS):
        cs = slice(hd * A_DV, (hd + 1) * A_DV)
        qh = q_ref[:, cs]
        zero = jnp.zeros_like(qh)
        vc_aug = jnp.concatenate([vc_ref[:, cs], ones_c], axis=1)
        v_aug = jnp.concatenate([v_ref[:, cs], ones_l], axis=1)
        ons = []
        for qm in (jnp.where(first, qh, zero), jnp.where(first, zero, qh)):
            s_c = _dot(qm, kc_ref[:, cs], _NT)
            s_l = _dot(qm, k_ref[:, cs], _NT)
            m = jnp.maximum(jnp.max(s_c, axis=1, keepdims=True), jnp.max(s_l, axis=1, keepdims=True))
            p_c = jnp.exp((s_c - m).astype(bf16))
            p_l = jnp.exp((s_l - m).astype(bf16))
            oa = _dot(p_c, vc_aug) + _dot(p_l, v_aug)
            ons.append(oa[:, :A_DV] * (1.0 / oa[:, A_DV:A_DV + 1]))
        o = ons[0] - lam * ons[1]
        o = o * lax.rsqrt(jnp.mean(o * o, axis=1, keepdims=True) + RMS_EPS)
        o_ref[:, cs] = (o * g_ref[:, cs] * (1.0 - lam_init)).astype(o_ref.dtype)


def _attn(lv, q, k, vvo, ckv, g_a, lam_init, tq=256):
    bsz, s, _ = q.shape
    lc = ckv.shape[1]
    tq = min(tq, s)
    w = A_QW
    return pl.pallas_call(
        functools.partial(_attn_kernel, lam_init=lam_init),
        grid=(bsz, s // tq),
        in_specs=[pl.BlockSpec(lv.shape, lambda b, i: (0, 0)),
                  pl.BlockSpec((None, tq, w), lambda b, i: (b, i, 0)),
                  pl.BlockSpec((None, lc, w), lambda b, i: (b, 0, 0)),
                  pl.BlockSpec((None, s, w), lambda b, i: (b, 0, 0)),
                  pl.BlockSpec((None, lc, w), lambda b, i: (b, 0, 1)),
                  pl.BlockSpec((None, s, w), lambda b, i: (b, 0, 0)),
                  pl.BlockSpec((1, w), lambda b, i: (0, 0))],
        out_specs=pl.BlockSpec((None, tq, w), lambda b, i: (b, i, 0)),
        out_shape=jax.ShapeDtypeStruct((bsz, s, w), bf16),
        compiler_params=_cp("parallel", "arbitrary"),
        name="diff_attn",
    )(lv, q, ckv, k, ckv, vvo, g_a.reshape(1, w))


_LN_QSCALE = math.log(B_DH ** -0.5)
_NCHAIN = 2 * B_HEADS


def _chunk_gate_sums(gi, gf, tri):
    lf = _log_sigmoid(gf)
    hi, lo = _split_bf16(lf)
    cum = _dot(tri, hi) + _dot(tri, lo)
    t = gf.shape[0]
    tot = cum[t - 1:t, :]
    rcum = tot - cum + lf
    fwd = lax.broadcasted_iota(jnp.int32, gf.shape, 1) < B_HEADS
    bd = jnp.where(fwd, cum, rcum)
    return bd, tot, (bd - gi).T


def _lower_tri(t):
    r = lax.broadcasted_iota(jnp.int32, (t, t), 0)
    c = lax.broadcasted_iota(jnp.int32, (t, t), 1)
    return r, c


def _ones_block(t):
    one0 = jnp.where(lax.broadcasted_iota(jnp.int32, (1, LANES), 1) == 0, 1.0, 0.0).astype(bf16)
    return jnp.broadcast_to(one0, (t, LANES))


def _absorb(c_ref, m_ref, ch, x_row, tot_c, kb, vaug):
    m_prev = m_ref[ch][:, 0:1]
    g = tot_c - x_row
    m_new = jnp.maximum(tot_c + m_prev, jnp.max(g, axis=1, keepdims=True))
    wgt = jnp.exp(g - m_new)
    decay = jnp.exp(tot_c + m_prev - m_new)
    kw_t = kb.astype(f32).T * wgt
    c_ref[ch] = decay * c_ref[ch] + _dot(kw_t.astype(bf16), vaug)
    m_ref[ch] = jnp.broadcast_to(m_new, m_ref.shape[1:])


def _mlstm_kernel(qk_ref, vvo_ref, g_ref, ck_ref, ckv_ref, cg_ref, gb_ref, o_ref,
                  hf_ref, hb_ref, c_ref, m_ref, *, tc):
    s = o_ref.shape[0]
    lc = ck_ref.shape[0]
    nc = s // tc
    w = B_WIDTH
    dh = B_DH

    c_ref[...] = jnp.zeros_like(c_ref)
    m_ref[...] = jnp.zeros_like(m_ref)

    r, cidx = _lower_tri(lc)
    tri_c = jnp.where(cidx <= r, 1.0, 0.0).astype(bf16)
    cg = cg_ref[...]
    _, tot, xt = _chunk_gate_sums(cg[:, :LANES], cg[:, LANES:], tri_c)
    ones_c = _ones_block(lc)
    for ch in range(_NCHAIN):
        hs = slice((ch % B_HEADS) * dh, (ch % B_HEADS + 1) * dh)
        vs = slice(2 * w + (ch % B_HEADS) * dh, 2 * w + (ch % B_HEADS + 1) * dh)
        _absorb(c_ref, m_ref, ch, xt[ch:ch + 1, :], tot[:, ch:ch + 1], ck_ref[:, hs],
                jnp.concatenate([ckv_ref[:, vs], ones_c], axis=1))
    ones_t = _ones_block(tc)

    r, cidx = _lower_tri(tc)
    tri = jnp.where(cidx <= r, 1.0, 0.0).astype(bf16)
    causal = cidx <= r
    anti = cidx >= r

    def step(i, carry):
        for d in range(2):
            row0 = pl.multiple_of((i if d == 0 else nc - 1 - i) * tc, tc)
            rows = pl.ds(row0, tc)
            gch = g_ref[rows, :]
            gi = gch[:, :LANES]
            bd, tot, xt = _chunk_gate_sums(gi, gch[:, LANES:], tri)
            mask = causal if d == 0 else anti
            dst = hf_ref if d == 0 else hb_ref
            for hd in range(B_HEADS):
                ch = d * B_HEADS + hd
                hs = slice(hd * dh, (hd + 1) * dh)
                qb = qk_ref[rows, hs]
                kb = qk_ref[rows, slice(w + hd * dh, w + (hd + 1) * dh)]
                vaug = jnp.concatenate([vvo_ref[rows, slice(w + hd * dh, w + (hd + 1) * dh)], ones_t],
                                       axis=1)
                bcol = bd[:, ch:ch + 1]
                x_row = xt[ch:ch + 1, :]
                dmat = jnp.where(mask, bcol - x_row, -jnp.inf)
                m_prev = m_ref[ch][:, 0:1]
                inter = bcol + m_prev
                m_t = jnp.maximum(inter, jnp.max(dmat, axis=1, keepdims=True))
                e = jnp.exp(dmat - m_t + _LN_QSCALE)
                smat = _dot(qb, kb, _NT) * e
                sc = jnp.exp(inter - m_t + _LN_QSCALE)
                both = sc * _dot(qb, c_ref[ch].astype(bf16)) + _dot(smat.astype(bf16), vaug)
                den = both[:, dh:dh + 1]
                dst[rows, hs] = both[:, :dh] * (1.0 / jnp.maximum(jnp.abs(den), jnp.exp(-m_t)))
                _absorb(c_ref, m_ref, ch, x_row, tot[:, ch:ch + 1], kb, vaug)
        return carry

    lax.fori_loop(0, nc, step, 0)

    for hd in range(B_HEADS):
        hs = slice(hd * dh, (hd + 1) * dh)
        hsum = hf_ref[:, hs] + hb_ref[:, hs]
        hn = hsum * lax.rsqrt(jnp.mean(hsum * hsum, axis=1, keepdims=True) + RMS_EPS)
        og = _sigmoid(vvo_ref[:, slice(2 * w + hd * dh, 2 * w + (hd + 1) * dh)].astype(f32))
        o_ref[:, hs] = (hn * gb_ref[:, hs] * og).astype(o_ref.dtype)


def _mlstm(qk, vvo, gates, cbk, ckv, cg, g_b):
    bsz, s, _ = qk.shape
    lc = cbk.shape[1]
    w = B_WIDTH
    tc = min(MLSTM_CHUNK, s)
    return pl.pallas_call(
        functools.partial(_mlstm_kernel, tc=tc),
        grid=(bsz,),
        in_specs=[pl.BlockSpec((None, s, 2 * w), lambda b: (b, 0, 0)),
                  pl.BlockSpec((None, s, 3 * w), lambda b: (b, 0, 0)),
                  pl.BlockSpec((None, s, 2 * LANES), lambda b: (b, 0, 0)),
                  pl.BlockSpec((None, lc, w), lambda b: (b, 0, 0)),
                  pl.BlockSpec((None, lc, 3 * w), lambda b: (b, 0, 0)),
                  pl.BlockSpec((None, lc, 2 * LANES), lambda b: (b, 0, 0)),
                  pl.BlockSpec((1, w), lambda b: (0, 0))],
        out_specs=pl.BlockSpec((None, s, w), lambda b: (b, 0, 0)),
        out_shape=jax.ShapeDtypeStruct((bsz, s, w), bf16),
        scratch_shapes=[pltpu.VMEM((s, w), f32), pltpu.VMEM((s, w), f32),
                        pltpu.VMEM((_NCHAIN, B_DH, 2 * B_DH), f32),
                        pltpu.VMEM((_NCHAIN, 1, LANES), f32)],
        compiler_params=_cp("arbitrary"),
        name="mlstm",
    )(qk, vvo, gates, cbk, ckv, cg, g_b.reshape(1, w))


def _out_kernel(*refs, n_act):
    acts = refs[:n_act]
    ws = refs[n_act:2 * n_act]
    (x_ref, g_ref, gate_ref, g2_ref, sh_ref, sc_ref, rw_ref, rb_ref,
     o_ref, h_ref, rk_ref, gt_ref, cm_ref) = refs[2 * n_act:]
    mix = _dot(acts[0][...], ws[0][...])
    for a, wr in zip(acts[1:], ws[1:]):
        mix = mix + _dot(a[...], wr[...])
    y = mix * lax.rsqrt(jnp.mean(mix * mix, axis=-1, keepdims=True) + RMS_EPS) * g_ref[...]
    xn = x_ref[...] + gate_ref[...] * y
    o_ref[...] = xn
    _route(xn, g2_ref[...], sh_ref[...], sc_ref[...], rw_ref, rb_ref, h_ref, rk_ref, gt_ref, cm_ref)


def _out_proj_route(acts, ws, x, g, gate, g2, shift, scale, rw_t, rb, tl=512):
    bsz, l, d = x.shape
    tl = min(tl, l)
    nl = l // tl
    n_act = len(acts)
    row = lambda b, i: (b, i, 0)
    fix2 = lambda b, i: (0, 0)
    in_specs = [pl.BlockSpec((None, tl, a.shape[2]), row) for a in acts]
    in_specs += [pl.BlockSpec(wm.shape, fix2) for wm in ws]
    in_specs += [pl.BlockSpec((None, tl, d), row),
                 pl.BlockSpec((1, d), fix2),
                 pl.BlockSpec((None, 1, d), _bidx(gate)),
                 pl.BlockSpec((1, d), fix2),
                 pl.BlockSpec((None, 1, d), _bidx(shift)),
                 pl.BlockSpec((None, 1, d), _bidx(scale)),
                 pl.BlockSpec((N_EXPERTS, d), fix2),
                 pl.BlockSpec((N_EXPERTS, 1), fix2)]
    return pl.pallas_call(
        functools.partial(_out_kernel, n_act=n_act),
        grid=(bsz, nl),
        in_specs=in_specs,
        out_specs=[pl.BlockSpec((None, tl, d), row),
                   pl.BlockSpec((None, tl, d), row),
                   pl.BlockSpec((N_EXPERTS, tl), lambda b, i: (0, b * nl + i)),
                   pl.BlockSpec((N_EXPERTS, tl), lambda b, i: (0, b * nl + i)),
                   pl.BlockSpec((None, N_EXPERTS, LANES), lambda b, i: (b * nl + i, 0, 0))],
        out_shape=[jax.ShapeDtypeStruct((bsz, l, d), f32),
                   jax.ShapeDtypeStruct((bsz, l, d), bf16),
                   jax.ShapeDtypeStruct((N_EXPERTS, bsz * l), f32),
                   jax.ShapeDtypeStruct((N_EXPERTS, bsz * l), f32),
                   jax.ShapeDtypeStruct((bsz * nl, N_EXPERTS, LANES), f32)],
        compiler_params=_cp("parallel", "parallel"),
        name="out_proj_route",
    )(*acts, *ws, x, g.reshape(1, d), gate, g2.reshape(1, d), shift, scale, rw_t,
      rb.reshape(N_EXPERTS, 1))


def _route(xv, g, shift, scale, rw_ref, rb_ref, h_ref, rk_ref, gt_ref, cm_ref):
    hf = _norm_mod(xv, g, shift, scale)
    tl = hf.shape[0]
    h_ref[...] = hf.astype(h_ref.dtype)
    per = N_EXPERTS // N_GROUPS
    logits = _dot3(rw_ref[...], hf, _NT)
    s3 = _sigmoid(logits).reshape(N_GROUPS, per, tl)
    b3 = s3 + rb_ref[...].reshape(N_GROUPS, per, 1)
    neg = -jnp.inf
    jdx = lax.broadcasted_iota(jnp.int32, b3.shape, 1)
    gdx = lax.broadcasted_iota(jnp.int32, b3.shape, 0)
    m1 = jnp.max(b3, axis=1, keepdims=True)
    f1 = jnp.min(jnp.where(b3 == m1, jdx, per), axis=1, keepdims=True)
    m2 = jnp.max(jnp.where(jdx == f1, neg, b3), axis=1, keepdims=True)
    grp = m1 + m2
    g1 = lax.broadcasted_iota(jnp.int32, grp.shape, 0)
    cnt = jnp.zeros(grp.shape, jnp.int32)
    for gp in range(N_GROUPS):
        rv = grp[gp:gp + 1]
        ahead = jnp.where(rv > grp, 1, jnp.where(rv == grp, jnp.where(g1 > gp, 1, 0), 0))
        cnt = cnt + ahead
    v = jnp.where(cnt < TOPK_GROUPS, b3, neg)
    eidx = gdx * per + jdx
    sel = jnp.zeros(b3.shape, f32)
    for _ in range(TOP_K):
        m = jnp.max(jnp.max(v, axis=1, keepdims=True), axis=0, keepdims=True)
        cand = jnp.where(v == m, eidx, N_EXPERTS)
        fi = jnp.min(jnp.min(cand, axis=1, keepdims=True), axis=0, keepdims=True)
        hit = eidx == fi
        sel = jnp.where(hit, 1.0, sel)
        v = jnp.where(hit, neg, v)
    ssel = sel * s3
    den = jnp.sum(jnp.sum(ssel, axis=1, keepdims=True), axis=0, keepdims=True)
    gt_ref[...] = ((ROUTED_SCALE * ssel) / den).reshape(N_EXPERTS, tl)
    sel2 = sel.reshape(N_EXPERTS, tl)
    r = lax.broadcasted_iota(jnp.int32, (MOE_SUB, MOE_SUB), 0)
    c = lax.broadcasted_iota(jnp.int32, (MOE_SUB, MOE_SUB), 1)
    before = jnp.where(r < c, 1.0, 0.0).astype(bf16)
    cmax = jnp.zeros((N_EXPERTS, 1), f32)
    for j in range(tl // MOE_SUB):
        sub = sel2[:, j * MOE_SUB:(j + 1) * MOE_SUB]
        rank = _dot(sub.astype(bf16), before)
        rk_ref[:, j * MOE_SUB:(j + 1) * MOE_SUB] = jnp.where(sub > 0.0, rank, -1.0)
        cmax = jnp.maximum(cmax, jnp.sum(sub, axis=1, keepdims=True))
    cm_ref[...] = jnp.broadcast_to(cmax, cm_ref.shape)


def _swiglu_act(hh):
    half = hh.shape[1] // 2
    return _silu(hh[:, :half]) * hh[:, half:]


def _moe_kernel(cnt_ref, ord_ref, h_ref, rk_ref, gt_ref, *refs):
    ng = MOE_GROUP
    gu_refs, dn_refs = refs[:ng], refs[ng:2 * ng]
    o_ref, acc_ref, xg_ref, ys_ref, p_ref, gr_ref, fill_ref = refs[2 * ng:]
    tile = pl.program_id(0)
    grp = pl.program_id(1)
    tm, d = acc_ref.shape
    ns = tm // MOE_SUB
    ktot = p_ref.shape[1]
    eids = [ord_ref[tile, grp * ng + el] for el in range(ng)]

    @pl.when(grp == 0)
    def _():
        acc_ref[...] = jnp.zeros_like(acc_ref)
        fill_ref[0] = 0

    @pl.when(jnp.logical_and(tile == 0, grp == 0))
    def _():
        ys_ref[...] = jnp.zeros_like(ys_ref)

    def combine(rows):
        for s in range(ns):
            acc_ref[s * MOE_SUB:(s + 1) * MOE_SUB, :] += _dot(
                p_ref[s, 0:rows, :], ys_ref[s, 0:rows, :], _TN)

    def flush():
        combine(ktot)
        fill_ref[0] = 0

    def expert_ffn(el, win, off):
        hh = _dot(xg_ref[el, 0:ns * win, :], gu_refs[el][...])
        gr = gr_ref[el, 0:ns * win, :]
        act = _swiglu_act(hh) * jnp.concatenate([gr] * (hh.shape[1] // (2 * LANES)), axis=1)
        y = _dot(act.astype(bf16), dn_refs[el][...]).astype(bf16)
        start = off + el * win
        if not isinstance(start, int):
            start = pl.multiple_of(start, 16)
        for s in range(ns):
            ys_ref[s, pl.ds(start, win), :] = y[s * win:(s + 1) * win]

    def one_pass(p, win, off=0, defer=False):
        base = p * win
        riota = lax.broadcasted_iota(jnp.int32, (win, MOE_SUB), 0).astype(f32)
        for s in range(ns):
            cols = slice(s * MOE_SUB, (s + 1) * MOE_SUB)
            onehots = []
            for el in range(ng):
                row = pl.ds(eids[el], 1)
                hit = (rk_ref[row, cols] - base) == riota
                onehots.append(jnp.where(hit, 1.0, 0.0).astype(bf16))
                gsel = jnp.sum(jnp.where(hit, gt_ref[row, cols], 0.0), axis=1, keepdims=True)
                gr_ref[el, s * win:(s + 1) * win, :] = jnp.broadcast_to(gsel, (win, LANES))
            pm = jnp.concatenate(onehots, axis=0)
            p_ref[s, pl.ds(off, ng * win), :] = pm
            gx = _dot(pm, h_ref[cols, :])
            for el in range(ng):
                xg_ref[el, s * win:(s + 1) * win, :] = gx[el * win:(el + 1) * win].astype(bf16)
        for el in range(ng):
            expert_ffn(el, win, off)
        if not defer:
            combine(ng * win)

    most = cnt_ref[tile, eids[0]]
    for el in range(1, ng):
        most = jnp.maximum(most, cnt_ref[tile, eids[el]])

    thin = [w for w in MOE_WINDOWS if 2 * ng * w <= ktot]

    def batched_pass(win):
        rows = ng * win
        pl.when(fill_ref[0] + rows > ktot)(flush)
        off = pl.multiple_of(fill_ref[0], 16)

        @pl.when(off == 0)
        def _():
            p_ref[...] = jnp.zeros_like(p_ref)

        one_pass(0, win, off, defer=True)
        fill_ref[0] = off + rows

    pl.when(jnp.logical_and(most > (thin[-1] if thin else 0), fill_ref[0] > 0))(flush)
    lo = 0
    for win in MOE_WINDOWS[:-1]:
        body = functools.partial(batched_pass, win) if win in thin else functools.partial(one_pass, 0, win)
        pl.when(jnp.logical_and(most > lo, most <= win))(body)
        lo = win
    big = MOE_WINDOWS[-1]

    def big_pass(p, carry):
        one_pass(p, big)
        return carry

    lax.fori_loop(0, jnp.where(most > lo, (most + big - 1) // big, 0), big_pass, 0)

    @pl.when(grp == pl.num_programs(1) - 1)
    def _():
        pl.when(fill_ref[0] > 0)(flush)
        o_ref[...] = acc_ref[...].astype(o_ref.dtype)


def _moe_routed(h2, rk, gt, cmax, gu, dn, tm=2048):
    bsz, l, d = h2.shape
    tm = min(tm, l)
    nt = bsz * (l // tm)
    ne = gu.shape[0]
    ng = MOE_GROUP
    ns = tm // MOE_SUB
    wmax = MOE_WINDOWS[-1]
    counts = jnp.max(cmax[:, :, 0].reshape(nt, -1, ne), axis=1).astype(jnp.int32)
    order = jnp.argsort(-counts, axis=1).astype(jnp.int32)

    def expert_spec(arr, k):
        return pl.BlockSpec((None,) + arr.shape[1:], lambda t, e, cnt, order_ref: (order_ref[t, e * ng + k], 0, 0))

    tile_spec = pl.BlockSpec((tm, d), lambda t, e, *_: (t, 0))
    grid_spec = pltpu.PrefetchScalarGridSpec(
        num_scalar_prefetch=2,
        grid=(nt, ne // ng),
        in_specs=([tile_spec,
                   pl.BlockSpec((ne, tm), lambda t, e, *_: (0, t)),
                   pl.BlockSpec((ne, tm), lambda t, e, *_: (0, t))]
                  + [expert_spec(gu, k) for k in range(ng)]
                  + [expert_spec(dn, k) for k in range(ng)]),
        out_specs=tile_spec,
        scratch_shapes=[pltpu.VMEM((tm, d), f32),
                        pltpu.VMEM((ng, ns * wmax, d), bf16),
                        pltpu.VMEM((ns, ng * wmax, d), bf16),
                        pltpu.VMEM((ns, ng * wmax, MOE_SUB), bf16),
                        pltpu.VMEM((ng, ns * wmax, LANES), f32),
                        pltpu.SMEM((1,), jnp.int32)])
    return pl.pallas_call(
        _moe_kernel,
        grid_spec=grid_spec,
        out_shape=jax.ShapeDtypeStruct((bsz * l, d), bf16),
        compiler_params=_cp("arbitrary", "arbitrary"),
        name="moe",
    )(counts, order, h2.reshape(bsz * l, d), rk, gt, *([gu] * ng), *([dn] * ng))


def _moe_finish_kernel(h_ref, r_ref, sgu_ref, sdn_ref, x_ref, g_ref, gate_ref, *refs, has_next):
    act = _swiglu_act(_dot(h_ref[...], sgu_ref[...]))
    mo = _dot(act.astype(bf16), sdn_ref[...]) + r_ref[...]
    y = mo * lax.rsqrt(jnp.mean(mo * mo, axis=-1, keepdims=True) + RMS_EPS) * g_ref[...]
    xn = x_ref[...] + gate_ref[...] * y
    if has_next:
        gn_ref, shn_ref, scn_ref, o_ref, hn_ref = refs
        hn_ref[...] = _norm_mod(xn, gn_ref[...], shn_ref[...], scn_ref[...]).astype(hn_ref.dtype)
    else:
        (o_ref,) = refs
    o_ref[...] = xn


def _moe_finish(h2, routed, sgu, sdn, x, g, gate, nxt=None, tl=512):
    bsz, l, d = x.shape
    tl = min(tl, l)
    row = lambda b, i: (b, i, 0)
    fix2 = lambda b, i: (0, 0)
    in_specs = [pl.BlockSpec((None, tl, d), row), pl.BlockSpec((None, tl, d), row),
                pl.BlockSpec(sgu.shape, fix2), pl.BlockSpec(sdn.shape, fix2),
                pl.BlockSpec((None, tl, d), row), pl.BlockSpec((1, d), fix2),
                pl.BlockSpec((None, 1, d), _bidx(gate))]
    args = [h2, routed.reshape(bsz, l, d), sgu, sdn, x, g.reshape(1, d), gate]
    out_specs, out_shape = [pl.BlockSpec((None, tl, d), row)], [jax.ShapeDtypeStruct((bsz, l, d), f32)]
    if nxt is not None:
        in_specs += [pl.BlockSpec((1, d), fix2), pl.BlockSpec((None, 1, d), _bidx(nxt[1])),
                     pl.BlockSpec((None, 1, d), _bidx(nxt[2]))]
        args += [nxt[0].reshape(1, d), nxt[1], nxt[2]]
        out_specs.append(pl.BlockSpec((None, tl, d), row))
        out_shape.append(jax.ShapeDtypeStruct((bsz, l, d), bf16))
    return pl.pallas_call(
        functools.partial(_moe_finish_kernel, has_next=nxt is not None),
        grid=(bsz, l // tl),
        in_specs=in_specs,
        out_specs=out_specs,
        out_shape=out_shape,
        compiler_params=_cp("parallel", "parallel"),
        name="moe_finish",
    )(*args)


def _filter_kernel(z_ref, w1_ref, b1_ref, w2_ref, b2_ref, w3_ref, win_ref, o_ref):
    hid = jnp.sin(FILTER_SIN_W * (_dot3(z_ref[...], w1_ref[...]) + b1_ref[...]))
    hid = jnp.sin(FILTER_SIN_W * (_dot3(hid, w2_ref[...]) + b2_ref[...]))
    o_ref[...] = _dot3(hid, w3_ref[...]) * win_ref[...]


def _filters(z, w1, b1, w2, b2, w3, window, tn=512):
    l, p = z.shape
    hdim = w1.shape[1]
    n = w3.shape[1]
    d = window.shape[1]
    nd = d // tn
    return pl.pallas_call(
        _filter_kernel,
        grid=(n // tn,),
        in_specs=[pl.BlockSpec((l, p), lambda j: (0, 0)),
                  pl.BlockSpec((p, hdim), lambda j: (0, 0)),
                  pl.BlockSpec((1, hdim), lambda j: (0, 0)),
                  pl.BlockSpec((hdim, hdim), lambda j: (0, 0)),
                  pl.BlockSpec((1, hdim), lambda j: (0, 0)),
                  pl.BlockSpec((hdim, tn), lambda j: (0, j)),
                  pl.BlockSpec((l, tn), lambda j: (0, j % nd))],
        out_specs=pl.BlockSpec((l, tn), lambda j: (0, j)),
        out_shape=jax.ShapeDtypeStruct((l, n), f32),
        compiler_params=_cp("arbitrary"),
        name="hyena_filter",
    )(z, w1, b1.reshape(1, hdim), w2, b2.reshape(1, hdim), w3, window)


def _dft_tables(l):
    n = 2 * l
    n1 = math.isqrt(n)
    assert n == n1 * n1 and n1 % 16 == 0
    na = l // n1
    ncp = -(-(n1 // 2 + 1) // 8) * 8
    a = np.arange(na)
    b = np.arange(n1)
    c = np.arange(ncp)
    th = 2.0 * np.pi * ((n1 * a[None, None, :] + b[:, None, None]) * c[None, :, None]) / n
    t1 = np.concatenate([np.cos(th), -np.sin(th)], axis=1)
    ph = 2.0 * np.pi * (b[:, None] * b[None, :]) / n1
    cs, sn = np.cos(ph), np.sin(ph)
    a3 = np.block([[cs, sn], [-sn, cs]])
    a3i = np.block([[cs, -sn], [sn, cs]])
    a2 = np.arange(na) + na // 2
    th2 = 2.0 * np.pi * ((n1 * a2[None, :, None] + b[:, None, None]) * c[None, None, :]) / n
    wc = np.where((c == 0) | (c == n1 // 2), 1.0, np.where(c < n1 // 2, 2.0, 0.0))[None, None, :]
    t2 = np.concatenate([wc * np.cos(th2), -wc * np.sin(th2)], axis=2)
    return [jnp.asarray(t, f32).astype(bf16) for t in (t1, a3, a3i, t2)]


def _fft_dims(t1):
    n1, ncp2, na = t1.shape
    ncp = ncp2 // 2
    return n1, ncp, na, 2 * n1 + FFT_PAD, 2 * ncp + FFT_PAD, n1 + FFT_PAD


def _ld(ref, rows):
    return jnp.concatenate([ref[j, rows, :] for j in range(ref.shape[0])], axis=1)


def _st(ref, rows, val):
    for j in range(ref.shape[0]):
        ref[j, rows, :] = val[:, j * LANES:(j + 1) * LANES]


def _dft_forward(uf_ref, t1_ref, zs_ref):
    n1, ncp, na, sb, _, su = _fft_dims(t1_ref)
    for b in range(n1):
        ub = _ld(uf_ref, pl.ds(b, na, stride=su)).astype(bf16)
        zb = _dot(t1_ref[b], ub)
        _st(zs_ref, pl.ds(b, ncp, stride=sb), zb[:ncp])
        _st(zs_ref, pl.ds(n1 + b, ncp, stride=sb), zb[ncp:])


def _spectrum_kernel(f_ref, t1_ref, a3_ref, o_ref, uf_ref, zs_ref, *, scale):
    n1, ncp, na, sb, _, su = _fft_dims(t1_ref)
    for a in range(na):
        _st(uf_ref, pl.ds(a * su, n1), f_ref[pl.ds(a * n1, n1), :])
    _dft_forward(uf_ref, t1_ref, zs_ref)
    a3 = a3_ref[...]
    for c in range(ncp):
        zc = _ld(zs_ref, pl.ds(c * sb, 2 * n1)).astype(bf16)
        o_ref[c] = (_dot(a3, zc) * scale).astype(o_ref.dtype)


def _spectrum(filt, tabs, dt=256):
    l, n = filt.shape
    t1, a3, _, _ = tabs
    n1, ncp, na, sb, _, su = _fft_dims(t1)
    nj = dt // LANES
    return pl.pallas_call(
        functools.partial(_spectrum_kernel, scale=1.0 / (2 * l)),
        grid=(n // dt,),
        in_specs=[pl.BlockSpec((l, dt), lambda j: (0, j)),
                  pl.BlockSpec(t1.shape, lambda j: (0, 0, 0)),
                  pl.BlockSpec(a3.shape, lambda j: (0, 0))],
        out_specs=pl.BlockSpec((ncp, 2 * n1, dt), lambda j: (0, 0, j)),
        out_shape=jax.ShapeDtypeStruct((ncp, 2 * n1, n), bf16),
        scratch_shapes=[pltpu.VMEM((nj, na * su, LANES), f32),
                        pltpu.VMEM((nj, ncp * sb, LANES), f32)],
        compiler_params=_cp("arbitrary"),
        name="hyena_spectrum",
    )(filt, t1, a3)


def _long_conv(kf_ref, t1_ref, a3_ref, a3i_ref, t2_ref, uf_ref, zs_ref, qs_ref, y_ref):
    n1, ncp, na, sb, sq, su = _fft_dims(t1_ref)
    _dft_forward(uf_ref, t1_ref, zs_ref)
    a3 = a3_ref[...]
    a3i = a3i_ref[...]
    for c in range(ncp):
        zc = _ld(zs_ref, pl.ds(c * sb, 2 * n1)).astype(bf16)
        xc = _dot(a3, zc)
        kc = kf_ref[c].astype(f32)
        xr, xi = xc[:n1], xc[n1:]
        kr, ki = kc[:n1], kc[n1:]
        pc = jnp.concatenate([xr * kr - xi * ki, xr * ki + xi * kr], axis=0).astype(bf16)
        qc = _dot(a3i, pc)
        _st(qs_ref, pl.ds(c, n1, stride=sq), qc[:n1])
        _st(qs_ref, pl.ds(ncp + c, n1, stride=sq), qc[n1:])
    for b in range(n1):
        qb = _ld(qs_ref, pl.ds(b * sq, 2 * ncp)).astype(bf16)
        _st(y_ref, pl.ds(b, na, stride=su), _dot(t2_ref[b], qb))


def _hyena_kernel(v_ref, x1_ref, x2_ref, kf0_ref, kf1_ref, fb_ref, t1_ref, a3_ref, a3i_ref, t2_ref,
                  o_ref, uf_ref, zs_ref, qs_ref, y_ref):
    n1, _, na, _, _, su = _fft_dims(t1_ref)
    for a in range(na):
        _st(uf_ref, pl.ds(a * su, n1), v_ref[pl.ds(a * n1, n1), :].astype(f32))
    stages = ((kf0_ref, x1_ref, fb_ref[0:1, :]), (kf1_ref, x2_ref, fb_ref[1:2, :]))
    for i, (kf_ref, xg_ref, fb) in enumerate(stages):
        _long_conv(kf_ref, t1_ref, a3_ref, a3i_ref, t2_ref, uf_ref, zs_ref, qs_ref, y_ref)
        for a in range(na):
            rows = pl.ds(a * n1, n1)
            slab = pl.ds(a * su, n1)
            g = (xg_ref[rows, :].astype(f32) * (_ld(y_ref, slab) + _ld(uf_ref, slab) * fb)).astype(bf16)
            if i == 0:
                _st(uf_ref, slab, g.astype(f32))
            else:
                o_ref[rows, :] = g


def _hyena_convs(u, kf, fbias, tabs, d, dt=256):
    bsz, l, _ = u.shape
    t1, a3, a3i, t2 = tabs
    n1, ncp, na, sb, sq, su = _fft_dims(t1)
    nd = d // dt
    nj = dt // LANES
    ucol = lambda k: pl.BlockSpec((None, l, dt), lambda j, b: (b, 0, j + k * nd))
    kcol = lambda k: pl.BlockSpec((ncp, 2 * n1, dt), lambda j, b: (0, 0, j + k * nd))
    return pl.pallas_call(
        _hyena_kernel,
        grid=(nd, bsz),
        in_specs=[ucol(0), ucol(1), ucol(2), kcol(0), kcol(1),
                  pl.BlockSpec((2, dt), lambda j, b: (0, j)),
                  pl.BlockSpec(t1.shape, lambda j, b: (0, 0, 0)),
                  pl.BlockSpec(a3.shape, lambda j, b: (0, 0)),
                  pl.BlockSpec(a3i.shape, lambda j, b: (0, 0)),
                  pl.BlockSpec(t2.shape, lambda j, b: (0, 0, 0))],
        out_specs=pl.BlockSpec((None, l, dt), lambda j, b: (b, 0, j)),
        out_shape=jax.ShapeDtypeStruct((bsz, l, d), bf16),
        scratch_shapes=[pltpu.VMEM((nj, na * su, LANES), f32),
                        pltpu.VMEM((nj, ncp * sb, LANES), f32),
                        pltpu.VMEM((nj, n1 * sq, LANES), f32),
                        pltpu.VMEM((nj, na * su, LANES), f32)],
        compiler_params=_cp("parallel", "arbitrary"),
        name="hyena_convs",
    )(u, u, u, kf, kf, fbias, t1, a3, a3i, t2)


def _rope_tables(l):
    rows = l // GRID_W
    row = jnp.repeat(jnp.arange(rows), GRID_W)
    col = jnp.tile(jnp.arange(GRID_W), rows)
    inv = ROPE_BASE ** (-jnp.arange(ROPE_AXIS_PAIRS, dtype=f32) / ROPE_AXIS_PAIRS)
    ang = jnp.stack([row, col], axis=-1).astype(f32)[..., None] * inv
    ang = jnp.broadcast_to(ang[:, :, None, :], (l, 2, 2, ROPE_AXIS_PAIRS)).reshape(l, A_DQK)
    reps = A_QW // A_DQK
    return jnp.tile(jnp.cos(ang), (1, reps)), jnp.tile(jnp.sin(ang), (1, reps))


def _rotate_cols(w):
    j = np.arange(w.shape[1])
    lo = (j % (2 * ROPE_AXIS_PAIRS)) < ROPE_AXIS_PAIRS
    perm = np.where(lo, j + ROPE_AXIS_PAIRS, j - ROPE_AXIS_PAIRS)
    sign = np.where(lo, -1.0, 1.0).astype(np.float32)
    return w[:, perm] * sign


def _gate_cols(w_g, b_g):
    idx_i = np.array([d * 2 * B_HEADS + hd for d in range(2) for hd in range(B_HEADS)])
    idx_f = idx_i + B_HEADS
    pad = LANES - _NCHAIN
    k = w_g.shape[0]
    w = jnp.concatenate([w_g[:, idx_i], jnp.zeros((k, pad), f32),
                         w_g[:, idx_f], jnp.zeros((k, pad), f32)], axis=1)
    b = jnp.concatenate([b_g[idx_i], jnp.zeros((pad,), f32), b_g[idx_f], jnp.zeros((pad,), f32)])
    return w, b


def _hyena_consts(l, d):
    j = jnp.arange(l, dtype=f32)
    bands = (POS_EMB_DIM - 1) // 2
    freqs = jnp.linspace(1e-4, bands - 1, bands, dtype=f32)
    ang = (2.0 * math.pi / l) * j[:, None] * freqs[None, :]
    z = jnp.concatenate([(j / (l - 1))[:, None], jnp.cos(ang), -jnp.sin(ang)], axis=-1)
    dist = jnp.abs(j - l // 2) / (l // 2)
    max_decay = math.log(DECAY_TARGET) / DECAY_FAST_PCT
    min_decay = math.log(DECAY_TARGET) / DECAY_SLOW_PCT
    deltas = jnp.abs(jnp.linspace(min_decay, max_decay, d, dtype=f32))
    window = jnp.exp(-dist[:, None] * deltas[None, :])
    return z, window


def _ab_mixer(src, norm, hc, w_in, conv_w, conv_b, gate_b, lam_vecs, g_a, g_b, w_out, lam_init):
    s = src.shape[1]
    w = B_WIDTH
    o = 0
    cols = {}
    for name, width in (("aq", A_QW), ("bq", w), ("bo", w), ("ak", A_QW), ("av", A_VW),
                        ("bk", w), ("bv", w), ("g", 4 * B_HEADS)):
        cols[name] = w_in[:, o:o + width]
        o += width
    cos, sin = _rope_tables(s)
    cat = lambda *ws: jnp.concatenate(ws, axis=1).astype(bf16)
    hq, q, k = _qk_rope(src, norm, cat(cols["aq"], _rotate_cols(cols["aq"]), cols["ak"],
                                       _rotate_cols(cols["ak"])), cos, sin, A_DQK ** -0.5)
    h = src if norm is None else hq
    qk = _mm_conv(h, cat(cols["bq"], cols["bk"]), conv_w, conv_b, True)
    wg, bg = _gate_cols(cols["g"], gate_b)
    vvo, gates = _mm_pair(h, cat(cols["av"], cols["bv"], cols["bo"]), wg.astype(bf16), bg)
    ckv, cg = _mm_pair(hc, cat(cols["ak"], cols["av"], cols["bv"]), wg.astype(bf16), bg)
    cbk = _mm_conv(hc, cols["bk"].astype(bf16), conv_w[:, w:], conv_b[w:], True)
    out_a = _attn(lam_vecs, q, k, vvo, ckv, g_a, lam_init)
    out_b = _mlstm(qk, vvo, gates, cbk, ckv, cg, g_b)
    wo = w_out.astype(bf16)
    return [out_a, out_b], [wo[:A_VW], wo[A_VW:]]


def _hyena_mixer(h, w_in, conv_w, conv_b, fw1, fb1, fw2, fb2, fw3, fbias, w_out):
    _, l, d = h.shape
    u = _mm_conv(h, w_in.astype(bf16), conv_w, conv_b, False)
    z, window = _hyena_consts(l, d)
    pz, ph = LANES - z.shape[1], LANES - fw1.shape[1]
    filt = _filters(jnp.pad(z, ((0, 0), (0, pz))), jnp.pad(fw1, ((0, pz), (0, ph))),
                    jnp.pad(fb1, (0, ph)), jnp.pad(fw2, ((0, ph), (0, ph))), jnp.pad(fb2, (0, ph)),
                    jnp.pad(fw3, ((0, ph), (0, 0))), window)
    tabs = _dft_tables(l)
    kf = _spectrum(filt, tabs)
    y = _hyena_convs(u, kf, fbias, tabs, d)
    return [y], [w_out.astype(bf16)]


def kernel(x, c, ctx, c_ctx, w_mod, b_mod, norm_g, w_in_ab, conv_ab_w, conv_ab_b, gate_b_ab, diff_lambda, head_g_a, head_g_b, w_out_ab, w_in_hy, conv_hy_w, conv_hy_b, filt_w1, filt_b1, filt_w2, filt_b2, filt_w3, filt_bias, w_out_hy, router_w, router_b, exp_gu, exp_down, sh_gu, sh_down):
    bsz, s, d = x.shape
    depth = w_mod.shape[0]
    rows = -(-(bsz + 1) // 8) * 8
    cc = jnp.concatenate([c, c_ctx[None, :], jnp.zeros((rows - bsz - 1, d), f32)], axis=0)
    mods = [_mod(cc, w_mod[l], b_mod[l]) for l in range(depth)]
    vec = lambda l, i: mods[l][:bsz, i * d:(i + 1) * d].reshape(bsz, 1, d)
    h = None
    for l in range(depth):
        g_m, sh_f, sc_f, g_f = [vec(l, i) for i in range(2, 6)]
        pre = (norm_g[l, 0], vec(l, 0), vec(l, 1))
        if l % 2 == 0:
            e = l // 2
            lam_init = 0.8 - 0.6 * math.exp(-0.3 * l)
            row_c = lambda i: mods[l][bsz:bsz + 1, i * d:(i + 1) * d].reshape(1, 1, d)
            hc = _norm(ctx, norm_g[l, 0], row_c(0), row_c(1))
            src, norm = (x, pre) if h is None else (h, None)
            acts, ws = _ab_mixer(src, norm, hc, w_in_ab[e], conv_ab_w[e], conv_ab_b[e], gate_b_ab[e],
                                 diff_lambda[e], head_g_a[e], head_g_b[e], w_out_ab[e], lam_init)
        else:
            o = l // 2
            if h is None:
                h = _norm(x, *pre)
            acts, ws = _hyena_mixer(h, w_in_hy[o], conv_hy_w[o], conv_hy_b[o], filt_w1[o], filt_b1[o],
                                    filt_w2[o], filt_b2[o], filt_w3[o], filt_bias[o], w_out_hy[o])
        x, h2, rk, gt, cmax = _out_proj_route(acts, ws, x, norm_g[l, 1], g_m, norm_g[l, 2], sh_f, sc_f,
                                              router_w[l].T, router_b[l])
        nxt = (norm_g[l + 1, 0], vec(l + 1, 0), vec(l + 1, 1)) if l + 1 < depth else None
        routed = _moe_routed(h2, rk, gt, cmax, exp_gu[l].astype(bf16), exp_down[l].astype(bf16))
        outs = _moe_finish(h2, routed, sh_gu[l].astype(bf16), sh_down[l].astype(bf16), x, norm_g[l, 3],
                           g_f, nxt)
        x = outs[0]
        h = outs[1] if nxt is not None else None
    return x
```
